```python
import math, functools
import jax, jax.numpy as jnp
from jax import lax
import numpy as np

D_MODEL = 2048
BATCH = 2
SEQ = 16384
DEPTH = 1
DEC_BATCH = 8
DEC_SEQ = 16
PAST_LEN = 4096

CHUNK = 64
N_HEADS = 16
N_KV = 4
HEAD_DIM = 64
GQA = N_HEADS // N_KV
ATTN_DIM = N_HEADS * HEAD_DIM
KV_DIM = N_KV * HEAD_DIM
WINDOW = 128
WIN_CHUNKS = WINDOW // CHUNK
WIN_CACHE = min(WINDOW, PAST_LEN)
ROT_DIM = HEAD_DIM // 4
ROPE_THETA = 500000.0
SSM_HEADS = 16
SSM_HEAD_DIM = 64
SSM_DIM = SSM_HEADS * SSM_HEAD_DIM
SSM_GROUPS = 2
SSM_HPG = SSM_HEADS // SSM_GROUPS
SSM_STATE = 128
CONV_W = 4
CONV_DIM = SSM_DIM + 2 * SSM_GROUPS * SSM_STATE
SSD_CHUNK = 64
MIX_DIM = ATTN_DIM + SSM_DIM
IN_DIM = ATTN_DIM + 2 * KV_DIM + SSM_DIM + CONV_DIM + SSM_HEADS
IN_SPLITS = (ATTN_DIM, ATTN_DIM + KV_DIM, ATTN_DIM + 2 * KV_DIM,
             ATTN_DIM + 2 * KV_DIM + SSM_DIM, ATTN_DIM + 2 * KV_DIM + SSM_DIM + CONV_DIM)
N_EXPERTS = 32
TOP_K = 4
D_FF = 2048
SWIGLU_ALPHA = 1.702
SWIGLU_LIMIT = 7.0
MOE_BLOCK = 128
NORM_EPS = 1e-6

kernel_name = "hybrid_swa_ssd_moe_stream_step"


def rmsnorm(x, g):
    xf = x.astype(jnp.float32)
    y = xf * lax.rsqrt(jnp.mean(xf * xf, axis=-1, keepdims=True) + NORM_EPS)
    return (y * g.astype(jnp.float32)).astype(x.dtype)


def rope_partial(x, pos):
    inv_freq = ROPE_THETA ** (-jnp.arange(0, ROT_DIM, 2, dtype=jnp.float32) / ROT_DIM)
    ang = pos.astype(jnp.float32)[:, None] * inv_freq[None, :]
    cos = jnp.cos(ang)[None, :, None, :]
    sin = jnp.sin(ang)[None, :, None, :]
    half = ROT_DIM // 2
    xr = x[..., :ROT_DIM].astype(jnp.float32)
    x1, x2 = xr[..., :half], xr[..., half:]
    rot = jnp.concatenate([x1 * cos - x2 * sin, x2 * cos + x1 * sin], axis=-1).astype(x.dtype)
    return jnp.concatenate([rot, x[..., ROT_DIM:]], axis=-1)


def sink_attention(q, k, v, sinks, mask):
    s = jnp.einsum('...qkgd,...skd->...kgqs', q, k).astype(jnp.float32) * (HEAD_DIM ** -0.5)
    if mask is not None:
        s = jnp.where(mask, s, -jnp.inf)
    sink = jnp.broadcast_to(sinks.astype(jnp.float32).reshape(N_KV, GQA, 1, 1), s.shape[:-1] + (1,))
    p = jax.nn.softmax(jnp.concatenate([s, sink], axis=-1), axis=-1)[..., :-1]
    return jnp.einsum('...kgqs,...skd->...qkgd', p.astype(v.dtype), v)


def attend_prompt(q, k, v, sinks):
    b, s = q.shape[:2]
    nc = s // CHUNK
    qc = q.reshape(b, nc, CHUNK, N_KV, GQA, HEAD_DIM)

    def band(t):
        tc = t.reshape(b, nc, CHUNK, N_KV, HEAD_DIM)
        tp = jnp.pad(tc, ((0, 0), (WIN_CHUNKS, 0), (0, 0), (0, 0), (0, 0)))
        return jnp.concatenate([tp[:, i:i + nc] for i in range(WIN_CHUNKS + 1)], axis=2)

    kb, vb = band(k), band(v)
    key_chunk = (jnp.arange(nc)[:, None] - WIN_CHUNKS
                 + (jnp.arange((WIN_CHUNKS + 1) * CHUNK) // CHUNK)[None, :])
    mask = (key_chunk >= 0)[None, :, None, None, None, :]
    o = sink_attention(qc, kb, vb, sinks, mask)
    return o.reshape(b, s, ATTN_DIM), k[:, s - WIN_CACHE:], v[:, s - WIN_CACHE:]


def attend_sample(q, k, v, sinks, k_hist, v_hist):
    b, s = q.shape[:2]
    kf = jnp.concatenate([k_hist.astype(k.dtype), k], axis=1)
    vf = jnp.concatenate([v_hist.astype(v.dtype), v], axis=1)
    o = sink_attention(q.reshape(b, s, N_KV, GQA, HEAD_DIM), kf, vf, sinks, None)
    n = kf.shape[1]
    return o.reshape(b, s, ATTN_DIM), kf[:, n - WIN_CACHE:], vf[:, n - WIN_CACHE:]


def ssd_scan(x, dt, a, bm, cm, h0, q_len):
    b, s = x.shape[:2]
    nc = s // q_len
    xd = (x * dt[..., None]).reshape(b, nc, q_len, SSM_GROUPS, SSM_HPG, SSM_HEAD_DIM)
    ad = (dt * a).reshape(b, nc, q_len, SSM_GROUPS, SSM_HPG).transpose(0, 3, 4, 1, 2)
    bc = bm.reshape(b, nc, q_len, SSM_GROUPS, SSM_STATE)
    cc = cm.reshape(b, nc, q_len, SSM_GROUPS, SSM_STATE)
    a_cs = jnp.cumsum(ad, axis=-1)
    causal = jnp.tril(jnp.ones((q_len, q_len), dtype=bool))
    seg = a_cs[..., :, None] - a_cs[..., None, :]
    lmat = jnp.exp(jnp.where(causal, seg, -jnp.inf))
    cb = jnp.einsum('bclgn,bcsgn->bcgls', cc, bc)
    y_diag = jnp.einsum('bcgls,bgrcls,bcsgrp->bclgrp', cb, lmat, xd)
    decay_states = jnp.exp(a_cs[..., -1:] - a_cs)
    states = jnp.einsum('bclgn,bgrcl,bclgrp->bcgrpn', bc, decay_states, xd)
    chunk_decay = jnp.exp(a_cs[..., -1])

    def step(h, inp):
        st, dec = inp
        return h * dec[..., None, None] + st, h

    h_last, h_prev = lax.scan(step, h0, (jnp.moveaxis(states, 1, 0), jnp.moveaxis(chunk_decay, -1, 0)))
    h_prev = jnp.moveaxis(h_prev, 0, 1)
    y_off = jnp.einsum('bclgn,bcgrpn,bgrcl->bclgrp', cc, h_prev, jnp.exp(a_cs))
    y = (y_diag + y_off).reshape(b, s, SSM_GROUPS, SSM_HPG, SSM_HEAD_DIM)
    return y, h_last


def ssm_mixer(z, xbc, dt_raw, conv_hist, ssm_h0, q_len, conv_w, conv_b, dt_bias, a_log, d_skip, g_ssm):
    b, s, _ = xbc.shape
    xpad = jnp.concatenate([conv_hist.astype(xbc.dtype), xbc], axis=1)
    conv = conv_b
    for i in range(CONV_W):
        conv = conv + xpad[:, i:i + s] * conv_w[i]
    new_conv = xpad[:, xpad.shape[1] - (CONV_W - 1):]
    xs, bm, cm = jnp.split(jax.nn.silu(conv), (SSM_DIM, SSM_DIM + SSM_GROUPS * SSM_STATE), axis=-1)
    xs = xs.reshape(b, s, SSM_GROUPS, SSM_HPG, SSM_HEAD_DIM).astype(jnp.float32)
    bm = bm.reshape(b, s, SSM_GROUPS, SSM_STATE).astype(jnp.float32)
    cm = cm.reshape(b, s, SSM_GROUPS, SSM_STATE).astype(jnp.float32)
    dt = jax.nn.softplus(dt_raw.astype(jnp.float32) + dt_bias.astype(jnp.float32))
    dt = dt.reshape(b, s, SSM_GROUPS, SSM_HPG)
    a = -jnp.exp(a_log.astype(jnp.float32)).reshape(SSM_GROUPS, SSM_HPG)
    h0 = ssm_h0.astype(jnp.float32).reshape(b, SSM_GROUPS, SSM_HPG, SSM_HEAD_DIM, SSM_STATE)
    y, h_last = ssd_scan(xs, dt, a, bm, cm, h0, q_len)
    y = y + d_skip.astype(jnp.float32).reshape(SSM_GROUPS, SSM_HPG)[..., None] * xs
    y = y.reshape(b, s, SSM_DIM).astype(z.dtype) * jax.nn.silu(z)
    y = rmsnorm(y.reshape(b, s, SSM_GROUPS, SSM_DIM // SSM_GROUPS),
                g_ssm.reshape(SSM_GROUPS, SSM_DIM // SSM_GROUPS)).reshape(b, s, SSM_DIM)
    h_last = h_last.reshape(b, SSM_HEADS, SSM_HEAD_DIM, SSM_STATE).astype(ssm_h0.dtype)
    return y, new_conv, h_last


def clamped_swiglu(u):
    glu, lin = u[..., :D_FF], u[..., D_FF:]
    glu = jnp.minimum(glu, SWIGLU_LIMIT)
    lin = jnp.clip(lin, -SWIGLU_LIMIT, SWIGLU_LIMIT)
    return glu * jax.nn.sigmoid(SWIGLU_ALPHA * glu) * (lin + 1.0)


def routed_experts(h, w_router, b_router, w_up, b_up, w_down, b_down):
    n = h.shape[0]
    nk = n * TOP_K
    logits = (h @ w_router + b_router).astype(jnp.float32)
    top_val, top_idx = lax.top_k(logits, TOP_K)
    gates = jax.nn.softmax(top_val, axis=-1).reshape(nk)
    expert = top_idx.reshape(nk).astype(jnp.int32)
    token = jnp.arange(nk, dtype=jnp.int32) // TOP_K
    order = jnp.argsort(expert)
    e_sorted = expert[order]
    counts = jnp.bincount(expert, length=N_EXPERTS)
    start = jnp.cumsum(counts) - counts
    padded = (counts + MOE_BLOCK - 1) // MOE_BLOCK * MOE_BLOCK
    pend = jnp.cumsum(padded)
    dest = (pend - padded)[e_sorted] + jnp.arange(nk, dtype=jnp.int32) - start[e_sorted]
    n_blocks = -(-nk // MOE_BLOCK) + N_EXPERTS
    n_slots = n_blocks * MOE_BLOCK
    slot_tok = jnp.zeros((n_slots,), jnp.int32).at[dest].set(token[order])
    slot_gate = jnp.zeros((n_slots,), jnp.float32).at[dest].set(gates[order])
    block_expert = jnp.minimum(
        jnp.searchsorted(pend, jnp.arange(n_blocks, dtype=jnp.int32) * MOE_BLOCK, side='right'),
        N_EXPERTS - 1)

    def expert_block(args):
        tok, gate, e = args
        up = h[tok] @ w_up[e] + b_up[e]
        out = clamped_swiglu(up) @ w_down[e] + b_down[e]
        return out * gate[:, None].astype(out.dtype)

    outs = lax.map(expert_block, (slot_tok.reshape(n_blocks, MOE_BLOCK),
                                  slot_gate.reshape(n_blocks, MOE_BLOCK), block_expert))
    return jax.ops.segment_sum(outs.reshape(n_slots, D_MODEL), slot_tok, num_segments=n)


def trunk_layer(x, c, pos, attend, conv_hist, ssm_h0, q_len,
                w_mod, b_mod, g_mix_pre, g_mix_post, g_ffn_pre, g_ffn_post, w_in,
                conv_w, conv_b, dt_bias, a_log, d_skip, g_ssm, sinks, w_out,
                w_router, b_router, w_up, b_up, w_down, b_down):
    b, s, _ = x.shape
    mod = (jax.nn.silu(c) @ w_mod + b_mod)[:, None, :]
    sh_m, sc_m, gt_m, sh_f, sc_f, gt_f = jnp.split(mod, 6, axis=-1)
    h = rmsnorm(x, g_mix_pre) * (1.0 + sc_m) + sh_m
    q, k, v, z, xbc, dt_raw = jnp.split(h @ w_in, IN_SPLITS, axis=-1)
    q = rope_partial(q.reshape(b, s, N_HEADS, HEAD_DIM), pos)
    k = rope_partial(k.reshape(b, s, N_KV, HEAD_DIM), pos)
    v = v.reshape(b, s, N_KV, HEAD_DIM)
    attn_out, k_win, v_win = attend(q, k, v, sinks)
    ssm_out, conv_state, ssm_state = ssm_mixer(z, xbc, dt_raw, conv_hist, ssm_h0, q_len, conv_w, conv_b,
                                               dt_bias, a_log, d_skip, g_ssm)
    mix = jnp.concatenate([attn_out, ssm_out], axis=-1) @ w_out
    x = x + gt_m * rmsnorm(mix, g_mix_post)
    h = rmsnorm(x, g_ffn_pre) * (1.0 + sc_f) + sh_f
    f = routed_experts(h.reshape(b * s, D_MODEL), w_router, b_router, w_up, b_up, w_down, b_down)
    x = x + gt_f * rmsnorm(f.reshape(b, s, D_MODEL), g_ffn_post)
    return x, k_win, v_win, conv_state, ssm_state


def setup_inputs(seed: int = 0) -> dict:
    key = jax.random.key(seed)
    ks = jax.random.split(key, 32)
    f32 = jnp.float32

    def nrm(k, shape, scale):
        return jax.random.normal(k, shape, f32) * scale

    L = DEPTH
    dt0 = jnp.exp(jax.random.uniform(ks[15], (L, SSM_HEADS), f32, math.log(1e-3), math.log(1e-1)))
    return {
        "x_prompt": nrm(ks[0], (BATCH, SEQ, D_MODEL), 1.0),
        "x_sample": nrm(ks[1], (DEC_BATCH, DEC_SEQ, D_MODEL), 1.0),
        "c_prompt": nrm(ks[2], (BATCH, D_MODEL), 1.0),
        "c_sample": nrm(ks[3], (DEC_BATCH, D_MODEL), 1.0),
        "cache_k": nrm(ks[4], (L, DEC_BATCH, WIN_CACHE, N_KV, HEAD_DIM), 1.0),
        "cache_v": nrm(ks[5], (L, DEC_BATCH, WIN_CACHE, N_KV, HEAD_DIM), 1.0),
        "state_conv": nrm(ks[6], (L, DEC_BATCH, CONV_W - 1, CONV_DIM), 1.0),
        "state_ssm": nrm(ks[7], (L, DEC_BATCH, SSM_HEADS, SSM_HEAD_DIM, SSM_STATE), 0.1),
        "w_mod": nrm(ks[8], (L, D_MODEL, 6 * D_MODEL), 0.5 * D_MODEL ** -0.5),
        "b_mod": nrm(ks[9], (L, 6 * D_MODEL), 0.02),
        "g_mix_pre": 1.0 + nrm(ks[10], (L, D_MODEL), 0.05),
        "g_mix_post": 1.0 + nrm(ks[11], (L, D_MODEL), 0.05),
        "g_ffn_pre": 1.0 + nrm(ks[12], (L, D_MODEL), 0.05),
        "g_ffn_post": 1.0 + nrm(ks[13], (L, D_MODEL), 0.05),
        "w_in": nrm(ks[14], (L, D_MODEL, IN_DIM), D_MODEL ** -0.5),
        "conv_w": nrm(ks[16], (L, CONV_W, CONV_DIM), CONV_W ** -0.5),
        "conv_b": nrm(ks[17], (L, CONV_DIM), 0.02),
        "dt_bias": dt0 + jnp.log(-jnp.expm1(-dt0)),
        "a_log": jnp.log(jax.random.uniform(ks[18], (L, SSM_HEADS), f32, 1.0, 16.0)),
        "d_skip": 1.0 + nrm(ks[19], (L, SSM_HEADS), 0.1),
        "g_ssm": 1.0 + nrm(ks[20], (L, SSM_DIM), 0.05),
        "sinks": nrm(ks[21], (L, N_HEADS), 0.5),
        "w_out": nrm(ks[22], (L, MIX_DIM, D_MODEL), MIX_DIM ** -0.5),
        "w_router": nrm(ks[23], (L, D_MODEL, N_EXPERTS), D_MODEL ** -0.5),
        "b_router": nrm(ks[24], (L, N_EXPERTS), 0.01),
        "w_up": nrm(ks[25], (L, N_EXPERTS, D_MODEL, 2 * D_FF), D_MODEL ** -0.5),
        "b_up": nrm(ks[26], (L, N_EXPERTS, 2 * D_FF), 0.01),
        "w_down": nrm(ks[27], (L, N_EXPERTS, D_FF, D_MODEL), D_FF ** -0.5),
        "b_down": nrm(ks[28], (L, N_EXPERTS, D_MODEL), 0.01),
    }


def reference(x_prompt, x_sample, c_prompt, c_sample, cache_k, cache_v, state_conv, state_ssm,
              w_mod, b_mod, g_mix_pre, g_mix_post, g_ffn_pre, g_ffn_post, w_in, conv_w, conv_b,
              dt_bias, a_log, d_skip, g_ssm, sinks, w_out, w_router, b_router, w_up, b_up,
              w_down, b_down):
    bp, sp, _ = x_prompt.shape
    ss = x_sample.shape[1]
    pos_prompt = jnp.arange(sp, dtype=jnp.int32)
    pos_sample = PAST_LEN + jnp.arange(ss, dtype=jnp.int32)
    zero_conv = jnp.zeros((bp, CONV_W - 1, CONV_DIM), x_prompt.dtype)
    zero_ssm = jnp.zeros((bp, SSM_HEADS, SSM_HEAD_DIM, SSM_STATE), state_ssm.dtype)
    y_prompt, y_sample = x_prompt, x_sample
    new_p = ([], [], [], [])
    new_s = ([], [], [], [])
    for l in range(DEPTH):
        lw = (w_mod[l], b_mod[l], g_mix_pre[l], g_mix_post[l], g_ffn_pre[l], g_ffn_post[l], w_in[l],
              conv_w[l], conv_b[l], dt_bias[l], a_log[l], d_skip[l], g_ssm[l], sinks[l], w_out[l],
              w_router[l], b_router[l], w_up[l], b_up[l], w_down[l], b_down[l])
        y_prompt, *st_p = trunk_layer(y_prompt, c_prompt, pos_prompt, attend_prompt,
                                      zero_conv, zero_ssm, SSD_CHUNK, *lw)
        attend_s = functools.partial(attend_sample, k_hist=cache_k[l], v_hist=cache_v[l])
        y_sample, *st_s = trunk_layer(y_sample, c_sample, pos_sample, attend_s,
                                      state_conv[l], state_ssm[l], ss, *lw)
        for lst, val in zip(new_p, st_p):
            lst.append(val)
        for lst, val in zip(new_s, st_s):
            lst.append(val)
    return (y_prompt, y_sample,
            jnp.stack(new_p[0]), jnp.stack(new_p[1]), jnp.stack(new_p[2]), jnp.stack(new_p[3]),
            jnp.stack(new_s[0]), jnp.stack(new_s[1]), jnp.stack(new_s[2]), jnp.stack(new_s[3]))
```

```python
import functools

import numpy as np
import jax
import jax.numpy as jnp
from jax import lax
from jax.experimental import pallas as pl
from jax.experimental.pallas import tpu as pltpu

F32 = jnp.float32
BF16 = jnp.bfloat16

D_MODEL = 2048
CHUNK = 64
N_HEADS = 16
N_KV = 4
HEAD_DIM = 64
GQA = N_HEADS // N_KV
ATTN_DIM = N_HEADS * HEAD_DIM
KV_DIM = N_KV * HEAD_DIM
WINDOW = 128
WIN_CHUNKS = WINDOW // CHUNK
PAST_LEN = 4096
WIN_CACHE = min(WINDOW, PAST_LEN)
ROT_DIM = HEAD_DIM // 4
ROPE_THETA = 500000.0
SSM_HEADS = 16
SSM_HEAD_DIM = 64
SSM_DIM = SSM_HEADS * SSM_HEAD_DIM
SSM_GROUPS = 2
SSM_HPG = SSM_HEADS // SSM_GROUPS
SSM_STATE = 128
CONV_W = 4
CONV_DIM = SSM_DIM + 2 * SSM_GROUPS * SSM_STATE
SSD_CHUNK = 64
N_EXPERTS = 32
TOP_K = 4
D_FF = 2048
SWIGLU_ALPHA = 1.702
SWIGLU_LIMIT = 7.0
NORM_EPS = 1e-6

LANES = 128
SUBLANES = 8
MAIN_DIM = ATTN_DIM + 2 * KV_DIM + SSM_DIM + CONV_DIM
MOE_TM = 256
VMEM_LIMIT = 56 * 1024 * 1024


def _sigmoid(x):
    return 1.0 / (1.0 + jnp.exp(-x))


def _silu(x):
    return x * _sigmoid(x)


def _rms(x, g):
    return x * lax.rsqrt(jnp.mean(x * x, axis=-1, keepdims=True) + NORM_EPS) * g


def _split3(x):
    hi = x.astype(BF16)
    r1 = x - hi.astype(F32)
    mid = r1.astype(BF16)
    lo = (r1 - mid.astype(F32)).astype(BF16)
    return hi, mid, lo


def _dot_exact_rhs(x, m):
    hi, mid, lo = _split3(x)
    d = functools.partial(jnp.dot, preferred_element_type=F32)
    return d(hi, m) + d(mid, m) + d(lo, m)


_NN = (((1,), (0,)), ((), ()))
_NT = (((1,), (1,)), ((), ()))
_TN = (((0,), (0,)), ((), ()))


def _dot(a, b, dims=_NN):
    assert a.dtype == b.dtype, (a.dtype, b.dtype)
    prec = lax.Precision.HIGHEST if a.dtype == F32 else None
    return lax.dot_general(a, b, dims, preferred_element_type=F32, precision=prec)


def _params(sem, vmem=VMEM_LIMIT):
    return pltpu.CompilerParams(dimension_semantics=sem, vmem_limit_bytes=vmem)


def _mod_kernel(c_ref, w_ref, b_ref, o_ref):
    o_ref[...] = _dot(_silu(c_ref[...]).astype(BF16), w_ref[...].astype(BF16)) + b_ref[...]


def _modulation(c_all, w_mod, b_mod):
    rows, d = c_all.shape
    n = w_mod.shape[1]
    tn = 1536
    return pl.pallas_call(
        _mod_kernel,
        grid=(n // tn,),
        in_specs=[pl.BlockSpec((rows, d), lambda j: (0, 0)),
                  pl.BlockSpec((d, tn), lambda j: (0, j)),
                  pl.BlockSpec((1, tn), lambda j: (0, j))],
        out_specs=pl.BlockSpec((rows, tn), lambda j: (0, j)),
        out_shape=jax.ShapeDtypeStruct((rows, n), F32),
        compiler_params=_params(("arbitrary",)),
        name="modulation",
    )(c_all, w_mod, b_mod.reshape(1, n))


def _rope(t, cos, s1, s2):
    outs = []
    for j in range(t.shape[1] // LANES):
        tj = t[:, j * LANES:(j + 1) * LANES]
        up = pltpu.roll(tj, LANES - ROT_DIM // 2, 1)
        dn = pltpu.roll(tj, ROT_DIM // 2, 1)
        outs.append(tj * cos + up * s1 + dn * s2)
    return jnp.concatenate(outs, axis=1)


def _inproj_kernel(x_ref, sc_ref, sh_ref, g_ref, w_ref, wdt_ref, cos_ref, s1_ref, s2_ref,
                   q_ref, k_ref, v_ref, z_ref, xbc_ref, dt_ref):
    h = _rms(x_ref[0], g_ref[...]) * sc_ref[0] + sh_ref[0]
    hb = h.astype(w_ref.dtype)
    cos, s1, s2 = cos_ref[...], s1_ref[...], s2_ref[...]
    step = 512

    def mm(lo, hi):
        return jnp.concatenate([_dot(hb, w_ref[:, c:min(c + step, hi)]) for c in range(lo, hi, step)], axis=1)

    o = 0
    q_ref[0] = _rope(mm(o, o + ATTN_DIM), cos, s1, s2).astype(q_ref.dtype)
    o += ATTN_DIM
    k_ref[0] = _rope(mm(o, o + KV_DIM), cos, s1, s2)
    o += KV_DIM
    v_ref[0] = mm(o, o + KV_DIM)
    o += KV_DIM
    z_ref[0] = mm(o, o + SSM_DIM)
    o += SSM_DIM
    xbc_ref[0] = mm(o, o + CONV_DIM)
    dt_ref[0] = _dot(hb, wdt_ref[...])


def _mod_spec(m, tm):
    if m.shape[1] == 1:
        return pl.BlockSpec((1, 1, m.shape[2]), lambda bi, i: (bi, 0, 0))
    return pl.BlockSpec((1, tm, m.shape[2]), lambda bi, i: (bi, i, 0))


def _in_projection(x, sc1p, sh, g, w_main, w_dt, rope_tabs, tm):
    b, s, d = x.shape
    cos, s1, s2 = rope_tabs
    row = lambda bi, i: (bi, i, 0)
    const2 = lambda bi, i: (0, 0)
    tab = lambda bi, i: (i, 0)
    widths = (ATTN_DIM, KV_DIM, KV_DIM, SSM_DIM, CONV_DIM, LANES)
    dtypes = (w_main.dtype, F32, F32, F32, F32, F32)
    return pl.pallas_call(
        _inproj_kernel,
        grid=(b, s // tm),
        in_specs=[pl.BlockSpec((1, tm, d), row),
                  _mod_spec(sc1p, tm),
                  _mod_spec(sh, tm),
                  pl.BlockSpec((1, d), const2),
                  pl.BlockSpec((d, MAIN_DIM), const2, pipeline_mode=pl.Buffered(1)),
                  pl.BlockSpec((d, LANES), const2, pipeline_mode=pl.Buffered(1)),
                  pl.BlockSpec((tm, LANES), tab),
                  pl.BlockSpec((tm, LANES), tab),
                  pl.BlockSpec((tm, LANES), tab)],
        out_specs=[pl.BlockSpec((1, tm, w), row) for w in widths],
        out_shape=[jax.ShapeDtypeStruct((b, s, w), dt) for w, dt in zip(widths, dtypes)],
        compiler_params=_params(("arbitrary", "arbitrary")),
        name="in_projection",
    )(x, sc1p, sh, g, w_main, w_dt, cos, s1, s2)


def _rope_tables(pos):
    inv_freq = ROPE_THETA ** (-jnp.arange(0, ROT_DIM, 2, dtype=F32) / ROT_DIM)
    ang = pos.astype(F32)[:, None] * inv_freq[None, :]
    cos, sin = jnp.cos(ang), jnp.sin(ang)
    half = ROT_DIM // 2
    n = pos.shape[0]
    ones = jnp.ones((n, HEAD_DIM - ROT_DIM), F32)
    zeros_h = jnp.zeros((n, half), F32)
    zeros_r = jnp.zeros((n, HEAD_DIM - ROT_DIM), F32)
    c = jnp.concatenate([cos, cos, ones], axis=1)
    a = jnp.concatenate([-sin, zeros_h, zeros_r], axis=1)
    b = jnp.concatenate([zeros_h, sin, zeros_r], axis=1)
    rep = LANES // HEAD_DIM
    return tuple(jnp.tile(t, (1, rep)) for t in (c, a, b))


def _attend(qs, kc, vc, sink_col, valid):
    s = _dot(qs, kc, _NT) * (HEAD_DIM ** -0.5)
    if valid is not None:
        s = jnp.where(valid, s, -1e30)
    m = jnp.maximum(jnp.max(s, axis=-1, keepdims=True), sink_col)
    p = jnp.exp(s - m)
    den = jnp.sum(p, axis=-1, keepdims=True) + jnp.exp(sink_col - m)
    return _dot(p.astype(vc.dtype), vc) / den


def _sink_col(sinks_ref, g, rows):
    return jnp.concatenate([jnp.full((rows, 1), sinks_ref[g * GQA + a], F32) for a in range(GQA)], axis=0)


def _attn_prompt_kernel(sinks_ref, q_ref, km_ref, kh_ref, vm_ref, vh_ref, o_ref):
    i = pl.program_id(1)
    tq = q_ref.shape[1]
    chunks = tq // CHUNK
    kfull = jnp.concatenate([kh_ref[0], km_ref[0]], axis=0).astype(q_ref.dtype)
    vfull = jnp.concatenate([vh_ref[0], vm_ref[0]], axis=0).astype(q_ref.dtype)
    span = (WIN_CHUNKS + 1) * CHUNK
    col_chunk = lax.broadcasted_iota(jnp.int32, (GQA * CHUNK, span), 1) // CHUNK
    for g in range(N_KV):
        kg = kfull[:, g * HEAD_DIM:(g + 1) * HEAD_DIM]
        vg = vfull[:, g * HEAD_DIM:(g + 1) * HEAD_DIM]
        sink = _sink_col(sinks_ref, g, CHUNK)
        for c in range(chunks):
            r0 = c * CHUNK
            qs = jnp.concatenate(
                [q_ref[0, r0:r0 + CHUNK, (g * GQA + a) * HEAD_DIM:(g * GQA + a + 1) * HEAD_DIM] for a in range(GQA)],
                axis=0)
            valid = None
            if c < WIN_CHUNKS:
                valid = (i * chunks + c - WIN_CHUNKS + col_chunk) >= 0
            o = _attend(qs, kg[r0:r0 + span], vg[r0:r0 + span], sink, valid)
            for a in range(GQA):
                hd = (g * GQA + a) * HEAD_DIM
                o_ref[0, r0:r0 + CHUNK, hd:hd + HEAD_DIM] = o[a * CHUNK:(a + 1) * CHUNK].astype(o_ref.dtype)


def _attention_prompt(q, k, v, sinks, tq):
    b, s, _ = q.shape
    halo = WIN_CHUNKS * CHUNK
    ratio = tq // halo
    main = lambda bi, i: (bi, i, 0)
    prev = lambda bi, i: (bi, jnp.maximum(i * ratio - 1, 0), 0)
    return pl.pallas_call(
        _attn_prompt_kernel,
        grid=(b, s // tq),
        in_specs=[pl.BlockSpec(memory_space=pltpu.SMEM),
                  pl.BlockSpec((1, tq, ATTN_DIM), main),
                  pl.BlockSpec((1, tq, KV_DIM), main),
                  pl.BlockSpec((1, halo, KV_DIM), prev),
                  pl.BlockSpec((1, tq, KV_DIM), main),
                  pl.BlockSpec((1, halo, KV_DIM), prev)],
        out_specs=pl.BlockSpec((1, tq, ATTN_DIM), main),
        out_shape=jax.ShapeDtypeStruct((b, s, ATTN_DIM), q.dtype),
        compiler_params=_params(("arbitrary", "arbitrary")),
        name="attention_prompt",
    )(sinks, q, k, k, v, v)


def _attn_sample_kernel(sinks_ref, q_ref, k_ref, v_ref, ck_ref, cv_ref, o_ref, kw_ref, vw_ref):
    s = q_ref.shape[1]
    kf = jnp.concatenate([ck_ref[0], k_ref[0]], axis=0)
    vf = jnp.concatenate([cv_ref[0], v_ref[0]], axis=0)
    n = kf.shape[0]
    kw_ref[0] = kf[n - WIN_CACHE:]
    vw_ref[0] = vf[n - WIN_CACHE:]
    kb, vb = kf.astype(q_ref.dtype), vf.astype(q_ref.dtype)
    for g in range(N_KV):
        kg = kb[:, g * HEAD_DIM:(g + 1) * HEAD_DIM]
        vg = vb[:, g * HEAD_DIM:(g + 1) * HEAD_DIM]
        qs = jnp.concatenate(
            [q_ref[0, :, (g * GQA + a) * HEAD_DIM:(g * GQA + a + 1) * HEAD_DIM] for a in range(GQA)], axis=0)
        o = _attend(qs, kg, vg, _sink_col(sinks_ref, g, s), None)
        for a in range(GQA):
            hd = (g * GQA + a) * HEAD_DIM
            o_ref[0, :, hd:hd + HEAD_DIM] = o[a * s:(a + 1) * s].astype(o_ref.dtype)


def _attention_sample(q, k, v, sinks, cache_k, cache_v):
    b, s, _ = q.shape
    blk = lambda bi: (bi, 0, 0)
    return pl.pallas_call(
        _attn_sample_kernel,
        grid=(b,),
        in_specs=[pl.BlockSpec(memory_space=pltpu.SMEM),
                  pl.BlockSpec((1, s, ATTN_DIM), blk),
                  pl.BlockSpec((1, s, KV_DIM), blk),
                  pl.BlockSpec((1, s, KV_DIM), blk),
                  pl.BlockSpec((1, WIN_CACHE, KV_DIM), blk),
                  pl.BlockSpec((1, WIN_CACHE, KV_DIM), blk)],
        out_specs=[pl.BlockSpec((1, s, ATTN_DIM), blk),
                   pl.BlockSpec((1, WIN_CACHE, KV_DIM), blk),
                   pl.BlockSpec((1, WIN_CACHE, KV_DIM), blk)],
        out_shape=[jax.ShapeDtypeStruct((b, s, ATTN_DIM), q.dtype),
                   jax.ShapeDtypeStruct((b, WIN_CACHE, KV_DIM), F32),
                   jax.ShapeDtypeStruct((b, WIN_CACHE, KV_DIM), F32)],
        compiler_params=_params(("arbitrary",)),
        name="attention_sample",
    )(sinks, q, k, v, cache_k, cache_v)


def _ssd_kernel(xbc_ref, z_ref, dt_ref, hist_ref, h0_ref, cw_ref, cb_ref, dtb_ref, alog_ref, dsk_ref, gs_ref,
                ep_ref, el_ref, dmask_ref, causal_ref, tril_ref,
                y_ref, conv_ref, hfin_ref, prev, ht):
    c = pl.program_id(1)
    last = pl.num_programs(1) - 1
    ln = xbc_ref.shape[1]
    gw = SSM_HPG * SSM_HEAD_DIM

    @pl.when(c == 0)
    def _():
        prev[...] = hist_ref[0]
        ht[...] = h0_ref[0]

    xr = xbc_ref[0]
    ext = jnp.concatenate([prev[...], xr], axis=0)
    conv = cb_ref[...]
    for i in range(CONV_W):
        sh = CONV_W - 1 - i
        tap = xr if sh == 0 else pltpu.roll(ext, sh, 0)[SUBLANES:]
        conv = conv + tap * cw_ref[i:i + 1, :]
    prev[...] = xr[ln - SUBLANES:]

    @pl.when(c == last)
    def _():
        conv_ref[0] = xr[ln - (CONV_W - 1):]

    act = _silu(conv)
    xs = act[:, :SSM_DIM]
    bm = act[:, SSM_DIM:SSM_DIM + SSM_GROUPS * SSM_STATE]
    cm = act[:, SSM_DIM + SSM_GROUPS * SSM_STATE:]

    dtv = dt_ref[0] + dtb_ref[...]
    dt = jnp.maximum(dtv, 0.0) + jnp.log1p(jnp.exp(-jnp.abs(dtv)))
    ad = dt * (-jnp.exp(alog_ref[...]))
    hi, mid, lo = _split3(ad)
    tril = tril_ref[...]
    d = functools.partial(jnp.dot, preferred_element_type=F32)
    a_cs = d(tril, hi) + d(tril, mid) + d(tril, lo)
    a_last = a_cs[ln - 1:ln, :]
    stacked = jnp.concatenate([dt, jnp.exp(a_cs), jnp.exp(a_last - a_cs)], axis=0)
    wide = _dot_exact_rhs(stacked, ep_ref[...])
    dt_x, ea_x, ds_x = wide[:ln], wide[ln:2 * ln], wide[2 * ln:]
    cd_x = ea_x[ln - 1:ln, :]

    a_l = _dot_exact_rhs(a_cs, el_ref[...])
    a_s = jnp.sum(a_l * dmask_ref[...], axis=0, keepdims=True)
    lmat = jnp.exp(jnp.where(causal_ref[...] > 0.0, a_l - a_s, -1e30))

    cdt = y_ref.dtype
    xd = xs * dt_x
    xds = (xd * ds_x).astype(cdt)
    xdb = xd.astype(cdt)
    ys = []
    for g in range(SSM_GROUPS):
        bg = bm[:, g * SSM_STATE:(g + 1) * SSM_STATE].astype(cdt)
        cg = cm[:, g * SSM_STATE:(g + 1) * SSM_STATE].astype(cdt)
        cbm = _dot(cg, bg, _NT)
        yd = []
        for r in range(SSM_HPG):
            hd = g * SSM_HPG + r
            w = (cbm * lmat[:, hd * ln:(hd + 1) * ln]).astype(cdt)
            yd.append(_dot(w, xdb[:, hd * SSM_HEAD_DIM:(hd + 1) * SSM_HEAD_DIM]))
        htg = ht[g]
        y_off = _dot(cg, htg.astype(cdt)) * ea_x[:, g * gw:(g + 1) * gw]
        st = _dot(bg, xds[:, g * gw:(g + 1) * gw], _TN)
        ht[g] = htg * cd_x[:, g * gw:(g + 1) * gw] + st
        ys.append(jnp.concatenate(yd, axis=1) + y_off)
    y = jnp.concatenate(ys, axis=1) + dsk_ref[...] * xs
    y = y * _silu(z_ref[0])
    outs = [_rms(y[:, g * gw:(g + 1) * gw], gs_ref[:, g * gw:(g + 1) * gw]) for g in range(SSM_GROUPS)]
    y_ref[0] = jnp.concatenate(outs, axis=1).astype(y_ref.dtype)

    @pl.when(c == last)
    def _():
        hfin_ref[0] = ht[...]


def _ssd_constants(ln):
    heads = SSM_HEADS
    ep = np.zeros((LANES, heads * SSM_HEAD_DIM), np.float32)
    el = np.zeros((LANES, heads * ln), np.float32)
    for r in range(heads):
        ep[r, r * SSM_HEAD_DIM:(r + 1) * SSM_HEAD_DIM] = 1.0
        el[r, r * ln:(r + 1) * ln] = 1.0
    eye = np.tile(np.eye(ln, dtype=np.float32), (1, heads))
    causal = np.tile(np.tril(np.ones((ln, ln), np.float32)), (1, heads))
    tril = np.tril(np.ones((ln, ln), np.float32))
    return (jnp.asarray(ep, BF16), jnp.asarray(el, BF16), jnp.asarray(eye), jnp.asarray(causal),
            jnp.asarray(tril, BF16))


def _ssd_mixer(xbc, z, dt_raw, hist8, h0t, conv_w8, conv_b, dt_bias, a_log, dskip_x, g_ssm, ln, out_dtype):
    b, s, _ = xbc.shape
    consts = _ssd_constants(ln)
    row = lambda bi, c: (bi, c, 0)
    per_b3 = lambda bi, c: (bi, 0, 0)
    per_b4 = lambda bi, c: (bi, 0, 0, 0)
    const2 = lambda bi, c: (0, 0)
    full = lambda a: pl.BlockSpec(a.shape, const2)
    gw = SSM_HPG * SSM_HEAD_DIM
    return pl.pallas_call(
        _ssd_kernel,
        grid=(b, s // ln),
        in_specs=[pl.BlockSpec((1, ln, CONV_DIM), row),
                  pl.BlockSpec((1, ln, SSM_DIM), row),
                  pl.BlockSpec((1, ln, LANES), row),
                  pl.BlockSpec((1, SUBLANES, CONV_DIM), per_b3),
                  pl.BlockSpec((1, SSM_GROUPS, SSM_STATE, gw), per_b4),
                  full(conv_w8), full(conv_b), full(dt_bias), full(a_log), full(dskip_x), full(g_ssm)]
                 + [full(a) for a in consts],
        out_specs=[pl.BlockSpec((1, ln, SSM_DIM), row),
                   pl.BlockSpec((1, CONV_W - 1, CONV_DIM), per_b3),
                   pl.BlockSpec((1, SSM_GROUPS, SSM_STATE, gw), per_b4)],
        out_shape=[jax.ShapeDtypeStruct((b, s, SSM_DIM), out_dtype),
                   jax.ShapeDtypeStruct((b, CONV_W - 1, CONV_DIM), F32),
                   jax.ShapeDtypeStruct((b, SSM_GROUPS, SSM_STATE, gw), F32)],
        scratch_shapes=[pltpu.VMEM((SUBLANES, CONV_DIM), F32),
                        pltpu.VMEM((SSM_GROUPS, SSM_STATE, gw), F32)],
        compiler_params=_params(("arbitrary", "arbitrary")),
        name="ssd_mixer",
    )(xbc, z, dt_raw, hist8, h0t, conv_w8, conv_b, dt_bias, a_log, dskip_x, g_ssm, *consts)


def _outproj_kernel(a_ref, s_ref, x_ref, w_ref, gpost_ref, gate_ref, gpre_ref, sc_ref, sh_ref, x1_ref, h2_ref):
    mix = _dot(a_ref[0], w_ref[:ATTN_DIM, :]) + _dot(s_ref[0], w_ref[ATTN_DIM:, :])
    x1 = x_ref[0] + gate_ref[0] * _rms(mix, gpost_ref[...])
    x1_ref[0] = x1
    h2_ref[...] = _rms(x1, gpre_ref[...]) * sc_ref[0] + sh_ref[0]


def _out_projection(attn, ssm, x, w_out, g_post, gate, g_pre, sc1p, sh, tm, h2_rows):
    b, s, d = x.shape
    nt = s // tm
    row = lambda bi, i: (bi, i, 0)
    const2 = lambda bi, i: (0, 0)
    return pl.pallas_call(
        _outproj_kernel,
        grid=(b, nt),
        in_specs=[pl.BlockSpec((1, tm, ATTN_DIM), row),
                  pl.BlockSpec((1, tm, SSM_DIM), row),
                  pl.BlockSpec((1, tm, d), row),
                  pl.BlockSpec((ATTN_DIM + SSM_DIM, d), const2, pipeline_mode=pl.Buffered(1)),
                  pl.BlockSpec((1, d), const2),
                  _mod_spec(gate, tm),
                  pl.BlockSpec((1, d), const2),
                  _mod_spec(sc1p, tm),
                  _mod_spec(sh, tm)],
        out_specs=[pl.BlockSpec((1, tm, d), row),
                   pl.BlockSpec((tm, d), lambda bi, i: (bi * nt + i, 0))],
        out_shape=[jax.ShapeDtypeStruct((b, s, d), F32),
                   jax.ShapeDtypeStruct((h2_rows, d), F32)],
        compiler_params=_params(("arbitrary", "arbitrary")),
        name="out_projection",
    )(attn, ssm, x, w_out, g_post, gate, g_pre, sc1p, sh)


LANE_IDX, LANE_RANK, LANE_GATE = 0, TOP_K, 2 * TOP_K


def _router_kernel(hp_ref, hs_ref, w_ref, b_ref, ltri_ref, meta_ref, cnt_ref, *, p_tiles):
    i = pl.program_id(0)
    tt = hp_ref.shape[0]

    @pl.when(i == 0)
    def _():
        cnt_ref[...] = jnp.zeros_like(cnt_ref)

    h = jnp.where(i < p_tiles, hp_ref[...], hs_ref[...])
    logits = _dot(h.astype(w_ref.dtype), w_ref[...]) + b_ref[...]
    lane = lax.broadcasted_iota(jnp.int32, (tt, LANES), 1)
    lane_f = lane.astype(F32)
    work = jnp.where(lane < N_EXPERTS, logits, -jnp.inf)
    vals, hots, idxs = [], [], []
    for _ in range(TOP_K):
        m = jnp.max(work, axis=-1, keepdims=True)
        idx = jnp.min(jnp.where(work == m, lane_f, float(LANES)), axis=-1, keepdims=True)
        hot = lane_f == idx
        vals.append(m)
        hots.append(hot)
        idxs.append(idx)
        work = jnp.where(hot, -jnp.inf, work)
    es = [jnp.exp(v - vals[0]) for v in vals]
    den = es[0] + es[1] + es[2] + es[3]
    onehot = jnp.zeros((tt, LANES), F32)
    for hot in hots:
        onehot = jnp.where(hot, 1.0, onehot)
    before = jnp.dot(ltri_ref[...], onehot.astype(BF16), preferred_element_type=F32) + cnt_ref[0:1, :]
    meta = jnp.zeros((tt, LANES), F32)
    for k in range(TOP_K):
        rank_k = jnp.sum(jnp.where(hots[k], before, 0.0), axis=-1, keepdims=True)
        meta = jnp.where(lane == LANE_IDX + k, idxs[k], meta)
        meta = jnp.where(lane == LANE_RANK + k, rank_k, meta)
        meta = jnp.where(lane == LANE_GATE + k, es[k] / den, meta)
    meta_ref[...] = meta
    cnt_ref[...] = cnt_ref[...] + jnp.sum(onehot, axis=0, keepdims=True)


def _token_specs(p_tiles, tt, d):
    return [pl.BlockSpec((tt, d), lambda i, *_: (jnp.minimum(i, p_tiles - 1), 0)),
            pl.BlockSpec((tt, d), lambda i, *_: (jnp.maximum(i - p_tiles, 0), 0))]


def _router(h_p, h_s, w_router_pad, b_router_pad, tt):
    d = h_p.shape[1]
    n = h_p.shape[0] + h_s.shape[0]
    p_tiles = h_p.shape[0] // tt
    ltri = jnp.asarray(np.tril(np.ones((tt, tt), np.float32), -1), BF16)
    return pl.pallas_call(
        functools.partial(_router_kernel, p_tiles=p_tiles),
        grid=(n // tt,),
        in_specs=_token_specs(p_tiles, tt, d)
                 + [pl.BlockSpec((d, LANES), lambda i: (0, 0)),
                    pl.BlockSpec((1, LANES), lambda i: (0, 0)),
                    pl.BlockSpec((tt, tt), lambda i: (0, 0))],
        out_specs=[pl.BlockSpec((tt, LANES), lambda i: (i, 0)),
                   pl.BlockSpec((SUBLANES, LANES), lambda i: (0, 0))],
        out_shape=[jax.ShapeDtypeStruct((n, LANES), F32),
                   jax.ShapeDtypeStruct((SUBLANES, LANES), F32)],
        compiler_params=_params(("arbitrary",)),
        name="router",
    )(h_p, h_s, w_router_pad, b_router_pad, ltri)


def _dispatch_kernel(zstart_ref, nu_ref, pos_ref, hp_ref, hs_ref, xs_ref, zeros, sem_z, sem_r, *, p_tiles):
    i = pl.program_id(0)
    tt = hp_ref.shape[0]
    tm = zeros.shape[0]
    n_blocks = xs_ref.shape[0] // tm

    def zero_copy(row):
        return pltpu.make_async_copy(zeros, xs_ref.at[pl.ds(pl.multiple_of(row, tm), tm)], sem_z)

    @pl.when(i == 0)
    def _():
        zeros[...] = jnp.zeros_like(zeros)

        def per_expert(act):
            def body(e, carry):
                @pl.when(zstart_ref[e] >= 0)
                def _():
                    act(zero_copy(zstart_ref[e]))
                return carry
            lax.fori_loop(0, N_EXPERTS, body, 0)

        def per_tail(act):
            def body(blk, carry):
                act(zero_copy(blk * tm))
                return carry
            lax.fori_loop(nu_ref[0], n_blocks, body, 0)

        per_expert(lambda cp: cp.start())
        per_tail(lambda cp: cp.start())
        per_expert(lambda cp: cp.wait())
        per_tail(lambda cp: cp.wait())

    def scatter_rows(h_ref):
        def row_copy(t, k):
            return pltpu.make_async_copy(h_ref.at[pl.ds(t, 1)],
                                         xs_ref.at[pl.ds(pos_ref[0, 0, t * TOP_K + k], 1)], sem_r)

        def start_rows(t, carry):
            for k in range(TOP_K):
                row_copy(t, k).start()
            return carry

        def wait_rows(t, carry):
            for k in range(TOP_K):
                row_copy(t, k).wait()
            return carry

        lax.fori_loop(0, tt, start_rows, 0)
        lax.fori_loop(0, tt, wait_rows, 0)

    @pl.when(i < p_tiles)
    def _():
        scatter_rows(hp_ref)

    @pl.when(i >= p_tiles)
    def _():
        scatter_rows(hs_ref)


def _dispatch(h_p, h_s, pos3, zstart, n_used, n_slots, tt):
    d = h_p.shape[1]
    n = h_p.shape[0] + h_s.shape[0]
    p_tiles = h_p.shape[0] // tt
    return pl.pallas_call(
        functools.partial(_dispatch_kernel, p_tiles=p_tiles),
        grid_spec=pltpu.PrefetchScalarGridSpec(
            num_scalar_prefetch=2,
            grid=(n // tt,),
            in_specs=[pl.BlockSpec((1, 1, tt * TOP_K), lambda i, *_: (i, 0, 0), memory_space=pltpu.SMEM)]
                     + _token_specs(p_tiles, tt, d),
            out_specs=pl.BlockSpec(memory_space=pl.ANY),
            scratch_shapes=[pltpu.VMEM((MOE_TM, d), F32),
                            pltpu.SemaphoreType.DMA(()),
                            pltpu.SemaphoreType.DMA(())]),
        out_shape=jax.ShapeDtypeStruct((n_slots, d), F32),
        compiler_params=_params(("arbitrary",)),
        name="dispatch",
    )(zstart, n_used, pos3, h_p, h_s)


def _up_kernel(be_ref, nu_ref, x_ref, w_ref, b_ref, act_ref):
    i = pl.program_id(0)
    fc = 512

    @pl.when(i < nu_ref[0])
    def _():
        xb = x_ref[...].astype(BF16)
        for c in range(D_FF // fc):
            glu = jnp.dot(xb, w_ref[:, c * fc:(c + 1) * fc], preferred_element_type=F32) + b_ref[:, c * fc:(c + 1) * fc]
            lin = (jnp.dot(xb, w_ref[:, D_FF + c * fc:D_FF + (c + 1) * fc], preferred_element_type=F32)
                   + b_ref[:, D_FF + c * fc:D_FF + (c + 1) * fc])
            glu = jnp.minimum(glu, SWIGLU_LIMIT)
            lin = jnp.clip(lin, -SWIGLU_LIMIT, SWIGLU_LIMIT)
            act_ref[:, c * fc:(c + 1) * fc] = (glu * _sigmoid(SWIGLU_ALPHA * glu) * (lin + 1.0)).astype(act_ref.dtype)

    @pl.when(i >= nu_ref[0])
    def _():
        act_ref[...] = jnp.zeros_like(act_ref)


def _down_kernel(be_ref, nu_ref, a_ref, w_ref, b_ref, y_ref):
    i = pl.program_id(0)

    @pl.when(i < nu_ref[0])
    def _():
        y_ref[...] = jnp.dot(a_ref[...], w_ref[...], preferred_element_type=F32) + b_ref[...]

    @pl.when(i >= nu_ref[0])
    def _():
        y_ref[...] = jnp.zeros_like(y_ref)


def _grouped(kernel, name, x, w, b, block_expert, n_used, out_dtype):
    n_slots, kdim = x.shape
    _, _, ndim = w.shape
    nb = n_slots // MOE_TM
    used = lambda i, be, nu: (jnp.minimum(i, nu[0] - 1), 0)
    expert = lambda i, be, nu: (be[i], 0, 0)
    out_w = D_FF if kernel is _up_kernel else ndim
    return pl.pallas_call(
        kernel,
        grid_spec=pltpu.PrefetchScalarGridSpec(
            num_scalar_prefetch=2,
            grid=(nb,),
            in_specs=[pl.BlockSpec((MOE_TM, kdim), used),
                      pl.BlockSpec((None, kdim, ndim), expert),
                      pl.BlockSpec((None, 1, ndim), expert)],
            out_specs=pl.BlockSpec((MOE_TM, out_w), lambda i, be, nu: (i, 0))),
        out_shape=jax.ShapeDtypeStruct((n_slots, out_w), out_dtype),
        compiler_params=_params(("arbitrary",)),
        name=name,
    )(block_expert, n_used, x, w, b)


def _combine_kernel(pos_ref, y_ref, meta_ref, x1_ref, gate_ref, g_ref, o_ref, rows, sem):
    tt = x1_ref.shape[0]

    def row_copy(t, k):
        return pltpu.make_async_copy(y_ref.at[pl.ds(pos_ref[0, 0, t * TOP_K + k], 1)],
                                     rows.at[k, pl.ds(t, 1)], sem)

    def start_rows(t, carry):
        for k in range(TOP_K):
            row_copy(t, k).start()
        return carry

    def wait_rows(t, carry):
        for k in range(TOP_K):
            row_copy(t, k).wait()
        return carry

    lax.fori_loop(0, tt, start_rows, 0)
    lax.fori_loop(0, tt, wait_rows, 0)
    meta = meta_ref[...]
    f = rows[0] * meta[:, LANE_GATE:LANE_GATE + 1]
    for k in range(1, TOP_K):
        f = f + rows[k] * meta[:, LANE_GATE + k:LANE_GATE + k + 1]
    o_ref[...] = x1_ref[...] + gate_ref[0] * _rms(f, g_ref[...])


def _combine(y, pos3, meta, x1, gate, g_post, tt):
    n, d = x1.shape
    nb = gate.shape[0]
    tiles_per_b = n // tt // nb
    return pl.pallas_call(
        _combine_kernel,
        grid=(n // tt,),
        in_specs=[pl.BlockSpec((1, 1, tt * TOP_K), lambda i: (i, 0, 0), memory_space=pltpu.SMEM),
                  pl.BlockSpec(memory_space=pl.ANY),
                  pl.BlockSpec((tt, LANES), lambda i: (i, 0)),
                  pl.BlockSpec((tt, d), lambda i: (i, 0)),
                  pl.BlockSpec((1, 1, d), lambda i: (i // tiles_per_b, 0, 0)),
                  pl.BlockSpec((1, d), lambda i: (0, 0))],
        out_specs=pl.BlockSpec((tt, d), lambda i: (i, 0)),
        out_shape=jax.ShapeDtypeStruct((n, d), F32),
        scratch_shapes=[pltpu.VMEM((TOP_K, tt, d), F32), pltpu.SemaphoreType.DMA(())],
        compiler_params=_params(("arbitrary",)),
        name="combine",
    )(pos3, y, meta, x1, gate, g_post)


def _mixer(x, mod, pos, hist8, h0t, ln, tm, tq, wts, cache, h2_rows):
    b, s, d = x.shape
    mods = [m.reshape(b, 1, d) for m in jnp.split(mod, 6, axis=-1)]
    gt_f = mods[5]
    if cache is None:
        xf, fb, fs = x, b, s
    else:
        fb, fs = 1, b * s
        xf = x.reshape(fb, fs, d)
        pos = jnp.tile(pos, b)
        mods = [jnp.broadcast_to(m, (b, s, d)).reshape(fb, fs, d) for m in mods]
    sh_m, sc_m, gt_m, sh_f, sc_f, _ = mods
    proj = _in_projection(xf, 1.0 + sc_m, sh_m, wts["g_mix_pre"], wts["w_main"], wts["w_dt"], _rope_tables(pos), tm)
    q, k, v, z, xbc, dt_raw = [t.reshape(b, s, t.shape[-1]) for t in proj]
    if cache is None:
        attn = _attention_prompt(q, k, v, wts["sinks"], tq)
        k_win, v_win = k[:, s - WIN_CACHE:], v[:, s - WIN_CACHE:]
    else:
        attn, k_win, v_win = _attention_sample(q, k, v, wts["sinks"], *cache)
    ssm, conv_state, hfin = _ssd_mixer(xbc, z, dt_raw, hist8, h0t, wts["conv_w8"], wts["conv_b"], wts["dt_bias"],
                                       wts["a_log"], wts["dskip_x"], wts["g_ssm"], ln, q.dtype)
    x1, h2 = _out_projection(attn.reshape(fb, fs, -1), ssm.reshape(fb, fs, -1), xf, wts["w_out"], wts["g_mix_post"],
                             gt_m, wts["g_ffn_pre"], 1.0 + sc_f, sh_f, tm, h2_rows)
    ssm_state = hfin.reshape(b, SSM_GROUPS, SSM_STATE, SSM_HPG, SSM_HEAD_DIM)
    ssm_state = ssm_state.transpose(0, 1, 3, 4, 2).reshape(b, SSM_HEADS, SSM_HEAD_DIM, SSM_STATE)
    states = (k_win.reshape(b, WIN_CACHE, N_KV, HEAD_DIM), v_win.reshape(b, WIN_CACHE, N_KV, HEAD_DIM),
              conv_state, ssm_state)
    return x1.reshape(b * s, d), h2, gt_f, states


def _pad_lanes(a, width=LANES):
    return jnp.pad(a, [(0, 0)] * (a.ndim - 1) + [(0, width - a.shape[-1])])


def _largest_tile(n, cap):
    t = cap
    while n % t:
        t //= 2
    return t


def kernel(x_prompt, x_sample, c_prompt, c_sample, cache_k, cache_v, state_conv, state_ssm, w_mod, b_mod, g_mix_pre, g_mix_post, g_ffn_pre, g_ffn_post, w_in, conv_w, conv_b, dt_bias, a_log, d_skip, g_ssm, sinks, w_out, w_router, b_router, w_up, b_up, w_down, b_down):
    depth = w_mod.shape[0]
    assert depth == 1, "single-layer step"
    l = 0
    bp, sp, d = x_prompt.shape
    bs, ss, _ = x_sample.shape
    n_p, n_s = bp * sp, bs * ss
    n_tok = n_p + n_s

    c_all = jnp.concatenate([c_prompt, c_sample], axis=0)
    c_rows = -(-c_all.shape[0] // SUBLANES) * SUBLANES
    mod_all = _modulation(jnp.pad(c_all, ((0, c_rows - c_all.shape[0]), (0, 0))), w_mod[l], b_mod[l])

    row2 = lambda a: a.reshape(1, -1)
    wts = {
        "g_mix_pre": row2(g_mix_pre[l]), "g_mix_post": row2(g_mix_post[l]), "g_ffn_pre": row2(g_ffn_pre[l]),
        "w_main": w_in[l][:, :MAIN_DIM].astype(BF16),
        "w_dt": _pad_lanes(w_in[l][:, MAIN_DIM:]).astype(BF16),
        "sinks": sinks[l],
        "conv_w8": jnp.pad(conv_w[l], ((0, SUBLANES - CONV_W), (0, 0))), "conv_b": row2(conv_b[l]),
        "dt_bias": _pad_lanes(row2(dt_bias[l])), "a_log": _pad_lanes(row2(a_log[l])),
        "dskip_x": row2(jnp.repeat(d_skip[l], SSM_HEAD_DIM)), "g_ssm": row2(g_ssm[l]),
        "w_out": w_out[l].astype(BF16),
    }
    gw = SSM_HPG * SSM_HEAD_DIM

    hist_p = jnp.zeros((bp, SUBLANES, CONV_DIM), F32)
    h0_p = jnp.zeros((bp, SSM_GROUPS, SSM_STATE, gw), F32)
    x1_p, h2_p, gtf_p, st_p = _mixer(x_prompt, mod_all[:bp], jnp.arange(sp, dtype=jnp.int32), hist_p, h0_p,
                                     SSD_CHUNK, _largest_tile(sp, 256), _largest_tile(sp, 256), wts, None, n_p)

    hist_s = jnp.pad(state_conv[l], ((0, 0), (SUBLANES - (CONV_W - 1), 0), (0, 0)))
    h0_s = state_ssm[l].astype(F32).reshape(bs, SSM_GROUPS, SSM_HPG, SSM_HEAD_DIM, SSM_STATE)
    h0_s = h0_s.transpose(0, 1, 4, 2, 3).reshape(bs, SSM_GROUPS, SSM_STATE, gw)
    cache = (cache_k[l].reshape(bs, WIN_CACHE, KV_DIM), cache_v[l].reshape(bs, WIN_CACHE, KV_DIM))
    x1_s, h2_s, gtf_s, st_s = _mixer(x_sample, mod_all[bp:bp + bs], PAST_LEN + jnp.arange(ss, dtype=jnp.int32),
                                     hist_s, h0_s, ss, n_s, ss, wts, cache, n_s)

    tt = _largest_tile(n_s, 128)
    assert n_p % tt == 0
    meta, counts = _router(h2_p, h2_s, _pad_lanes(w_router[l]).astype(BF16), _pad_lanes(row2(b_router[l])), tt)
    top_idx = meta[:, LANE_IDX:LANE_IDX + TOP_K].astype(jnp.int32)
    rank = meta[:, LANE_RANK:LANE_RANK + TOP_K].astype(jnp.int32)
    cnt = counts[0, :N_EXPERTS].astype(jnp.int32)
    padded = (cnt + MOE_TM - 1) // MOE_TM * MOE_TM
    pend = jnp.cumsum(padded)
    offs = pend - padded
    pos = offs[top_idx] + rank
    n_blocks = -(-n_tok * TOP_K // MOE_TM) + N_EXPERTS
    n_slots = n_blocks * MOE_TM
    n_used = (pend[-1] // MOE_TM).astype(jnp.int32)
    blk = jnp.arange(n_blocks, dtype=jnp.int32)
    block_expert = jnp.minimum(jnp.searchsorted(pend, jnp.minimum(blk, n_used - 1) * MOE_TM, side="right"),
                               N_EXPERTS - 1).astype(jnp.int32)
    zstart = jnp.where(cnt > 0, pend - MOE_TM, -1).astype(jnp.int32)
    pos3 = pos.reshape(n_tok // tt, 1, tt * TOP_K)
    nu = n_used.reshape(1)
    xs = _dispatch(h2_p, h2_s, pos3, zstart, nu, n_slots, tt)
    act = _grouped(_up_kernel, "expert_up", xs, w_up[l].astype(BF16), b_up[l].reshape(N_EXPERTS, 1, -1),
                   block_expert, nu, BF16)
    y = _grouped(_down_kernel, "expert_down", act, w_down[l].astype(BF16), b_down[l].reshape(N_EXPERTS, 1, -1),
                 block_expert, nu, F32)

    g_post = row2(g_ffn_post[l])
    tt_p = _largest_tile(sp, 128)
    y_p = _combine(y, pos[:n_p].reshape(n_p // tt_p, 1, tt_p * TOP_K), meta[:n_p], x1_p,
                   gtf_p, g_post, tt_p).reshape(bp, sp, d)
    y_s = _combine(y, pos[n_p:].reshape(bs, 1, ss * TOP_K), meta[n_p:], x1_s,
                   gtf_s, g_post, ss).reshape(bs, ss, d)

    stack = lambda a: a[None]
    return (y_p, y_s, stack(st_p[0]), stack(st_p[1]), stack(st_p[2]), stack(st_p[3]),
            stack(st_s[0]), stack(st_s[1]), stack(st_s[2]), stack(st_s[3]))
```

```python
import functools

import numpy as np
import jax
import jax.numpy as jnp
from jax import lax
from jax.experimental import pallas as pl
from jax.experimental.pallas import tpu as pltpu

F32 = jnp.float32
BF16 = jnp.bfloat16

D_MODEL = 2048
CHUNK = 64
N_HEADS = 16
N_KV = 4
HEAD_DIM = 64
GQA = N_HEADS // N_KV
ATTN_DIM = N_HEADS * HEAD_DIM
KV_DIM = N_KV * HEAD_DIM
WINDOW = 128
WIN_CHUNKS = WINDOW // CHUNK
PAST_LEN = 4096
WIN_CACHE = min(WINDOW, PAST_LEN)
ROT_DIM = HEAD_DIM // 4
ROPE_THETA = 500000.0
SSM_HEADS = 16
SSM_HEAD_DIM = 64
SSM_DIM = SSM_HEADS * SSM_HEAD_DIM
SSM_GROUPS = 2
SSM_HPG = SSM_HEADS // SSM_GROUPS
SSM_STATE = 128
CONV_W = 4
CONV_DIM = SSM_DIM + 2 * SSM_GROUPS * SSM_STATE
SSD_CHUNK = 64
N_EXPERTS = 32
TOP_K = 4
D_FF = 2048
SWIGLU_ALPHA = 1.702
SWIGLU_LIMIT = 7.0
NORM_EPS = 1e-6

LANES = 128
SUBLANES = 8
MAIN_DIM = ATTN_DIM + 2 * KV_DIM + SSM_DIM + CONV_DIM
MOE_TM = 256
VMEM_LIMIT = 56 * 1024 * 1024


def _sigmoid(x):
    return 1.0 / (1.0 + jnp.exp(-x))


def _silu(x):
    return x * _sigmoid(x)


def _rms(x, g):
    return x * lax.rsqrt(jnp.mean(x * x, axis=-1, keepdims=True) + NORM_EPS) * g


def _split3(x):
    hi = x.astype(BF16)
    r1 = x - hi.astype(F32)
    mid = r1.astype(BF16)
    lo = (r1 - mid.astype(F32)).astype(BF16)
    return hi, mid, lo


def _dot_exact_rhs(x, m):
    hi, mid, lo = _split3(x)
    d = functools.partial(jnp.dot, preferred_element_type=F32)
    return d(hi, m) + d(mid, m) + d(lo, m)


_NN = (((1,), (0,)), ((), ()))
_NT = (((1,), (1,)), ((), ()))
_TN = (((0,), (0,)), ((), ()))


def _dot(a, b, dims=_NN):
    assert a.dtype == b.dtype, (a.dtype, b.dtype)
    prec = lax.Precision.HIGHEST if a.dtype == F32 else None
    return lax.dot_general(a, b, dims, preferred_element_type=F32, precision=prec)


def _params(sem, vmem=VMEM_LIMIT):
    return pltpu.CompilerParams(dimension_semantics=sem, vmem_limit_bytes=vmem)


def _mod_kernel(c_ref, w_ref, b_ref, o_ref):
    o_ref[...] = _dot(_silu(c_ref[...]).astype(BF16), w_ref[...].astype(BF16)) + b_ref[...]


def _modulation(c_all, w_mod, b_mod):
    rows, d = c_all.shape
    n = w_mod.shape[1]
    tn = 1536
    return pl.pallas_call(
        _mod_kernel,
        grid=(n // tn,),
        in_specs=[pl.BlockSpec((rows, d), lambda j: (0, 0)),
                  pl.BlockSpec((d, tn), lambda j: (0, j)),
                  pl.BlockSpec((1, tn), lambda j: (0, j))],
        out_specs=pl.BlockSpec((rows, tn), lambda j: (0, j)),
        out_shape=jax.ShapeDtypeStruct((rows, n), F32),
        compiler_params=_params(("arbitrary",)),
        name="modulation",
    )(c_all, w_mod, b_mod.reshape(1, n))


def _rope(t, cos, s1, s2):
    outs = []
    for j in range(t.shape[1] // LANES):
        tj = t[:, j * LANES:(j + 1) * LANES]
        up = pltpu.roll(tj, LANES - ROT_DIM // 2, 1)
        dn = pltpu.roll(tj, ROT_DIM // 2, 1)
        outs.append(tj * cos + up * s1 + dn * s2)
    return jnp.concatenate(outs, axis=1)


def _inproj_kernel(x_ref, sc_ref, sh_ref, g_ref, w_ref, wdt_ref, cos_ref, s1_ref, s2_ref,
                   q_ref, k_ref, v_ref, z_ref, xbc_ref, dt_ref):
    h = _rms(x_ref[0], g_ref[...]) * sc_ref[0] + sh_ref[0]
    hb = h.astype(w_ref.dtype)
    cos, s1, s2 = cos_ref[...], s1_ref[...], s2_ref[...]
    step = 512

    def mm(lo, hi):
        return jnp.concatenate([_dot(hb, w_ref[:, c:min(c + step, hi)]) for c in range(lo, hi, step)], axis=1)

    o = 0
    q_ref[0] = (_rope(mm(o, o + ATTN_DIM), cos, s1, s2) * (HEAD_DIM ** -0.5)).astype(q_ref.dtype)
    o += ATTN_DIM
    k_ref[0] = _rope(mm(o, o + KV_DIM), cos, s1, s2)
    o += KV_DIM
    v_ref[0] = mm(o, o + KV_DIM)
    o += KV_DIM
    z_ref[0] = mm(o, o + SSM_DIM)
    o += SSM_DIM
    xbc_ref[0] = mm(o, o + CONV_DIM)
    dt_ref[0] = _dot(hb, wdt_ref[...])


def _mod_spec(m, tm):
    if m.shape[1] == 1:
        return pl.BlockSpec((1, 1, m.shape[2]), lambda bi, i: (bi, 0, 0))
    return pl.BlockSpec((1, tm, m.shape[2]), lambda bi, i: (bi, i, 0))


def _in_projection(x, sc1p, sh, g, w_main, w_dt, rope_tabs, tm):
    b, s, d = x.shape
    cos, s1, s2 = rope_tabs
    row = lambda bi, i: (bi, i, 0)
    const2 = lambda bi, i: (0, 0)
    tab = lambda bi, i: (i, 0)
    widths = (ATTN_DIM, KV_DIM, KV_DIM, SSM_DIM, CONV_DIM, LANES)
    dtypes = (w_main.dtype, F32, F32, F32, F32, F32)
    return pl.pallas_call(
        _inproj_kernel,
        grid=(b, s // tm),
        in_specs=[pl.BlockSpec((1, tm, d), row),
                  _mod_spec(sc1p, tm),
                  _mod_spec(sh, tm),
                  pl.BlockSpec((1, d), const2),
                  pl.BlockSpec((d, MAIN_DIM), const2, pipeline_mode=pl.Buffered(1)),
                  pl.BlockSpec((d, LANES), const2, pipeline_mode=pl.Buffered(1)),
                  pl.BlockSpec((tm, LANES), tab),
                  pl.BlockSpec((tm, LANES), tab),
                  pl.BlockSpec((tm, LANES), tab)],
        out_specs=[pl.BlockSpec((1, tm, w), row) for w in widths],
        out_shape=[jax.ShapeDtypeStruct((b, s, w), dt) for w, dt in zip(widths, dtypes)],
        compiler_params=_params(("arbitrary", "arbitrary")),
        name="in_projection",
    )(x, sc1p, sh, g, w_main, w_dt, cos, s1, s2)


def _rope_tables(pos):
    inv_freq = ROPE_THETA ** (-jnp.arange(0, ROT_DIM, 2, dtype=F32) / ROT_DIM)
    ang = pos.astype(F32)[:, None] * inv_freq[None, :]
    cos, sin = jnp.cos(ang), jnp.sin(ang)
    half = ROT_DIM // 2
    n = pos.shape[0]
    ones = jnp.ones((n, HEAD_DIM - ROT_DIM), F32)
    zeros_h = jnp.zeros((n, half), F32)
    zeros_r = jnp.zeros((n, HEAD_DIM - ROT_DIM), F32)
    c = jnp.concatenate([cos, cos, ones], axis=1)
    a = jnp.concatenate([-sin, zeros_h, zeros_r], axis=1)
    b = jnp.concatenate([zeros_h, sin, zeros_r], axis=1)
    rep = LANES // HEAD_DIM
    return tuple(jnp.tile(t, (1, rep)) for t in (c, a, b))


KV_PAIRS = N_KV // 2


def _paired_head_columns():
    cols = []
    for j in range(KV_PAIRS):
        for a in range(GQA):
            for g in (2 * j, 2 * j + 1):
                h = g * GQA + a
                cols.extend(range(h * HEAD_DIM, (h + 1) * HEAD_DIM))
    return np.asarray(cols, np.int32)


def _attend_pair(qcols, k2, v2, sink_col, valid):
    rows = qcols[0].shape[0]
    lo = lax.broadcasted_iota(jnp.int32, (rows, LANES), 1) < HEAD_DIM
    zero = jnp.zeros_like(qcols[0])
    lhs = jnp.concatenate([jnp.where(lo, qc, zero) for qc in qcols] + [jnp.where(lo, zero, qc) for qc in qcols], axis=0)
    s = _dot(lhs, k2, _NT)
    if valid is not None:
        s = jnp.where(valid, s, -1e30)
    m = jnp.maximum(jnp.max(s, axis=-1, keepdims=True), sink_col)
    p = jnp.exp(s - m)
    den = jnp.sum(p, axis=-1, keepdims=True) + jnp.exp(sink_col - m)
    o = _dot((p * (1.0 / den)).astype(v2.dtype), v2)
    half = GQA * rows
    return [jnp.where(lo, o[a * rows:(a + 1) * rows], o[half + a * rows:half + (a + 1) * rows]) for a in range(GQA)]


def _sink_col(sinks_ref, j, rows):
    heads = [(2 * j) * GQA + a for a in range(GQA)] + [(2 * j + 1) * GQA + a for a in range(GQA)]
    return jnp.concatenate([jnp.full((rows, 1), sinks_ref[h], F32) for h in heads], axis=0)


def _attn_prompt_kernel(sinks_ref, q_ref, km_ref, kh_ref, vm_ref, vh_ref, o_ref):
    i = pl.program_id(1)
    tq = q_ref.shape[1]
    chunks = tq // CHUNK
    kfull = jnp.concatenate([kh_ref[0], km_ref[0]], axis=0).astype(q_ref.dtype)
    vfull = jnp.concatenate([vh_ref[0], vm_ref[0]], axis=0).astype(q_ref.dtype)
    span = (WIN_CHUNKS + 1) * CHUNK
    col_chunk = lax.broadcasted_iota(jnp.int32, (2 * GQA * CHUNK, span), 1) // CHUNK
    for j in range(KV_PAIRS):
        k2 = kfull[:, j * LANES:(j + 1) * LANES]
        v2 = vfull[:, j * LANES:(j + 1) * LANES]
        sink = _sink_col(sinks_ref, j, CHUNK)
        for c in range(chunks):
            r0 = c * CHUNK
            slabs = [(j * GQA + a) * LANES for a in range(GQA)]
            qcols = [q_ref[0, r0:r0 + CHUNK, sl:sl + LANES] for sl in slabs]
            valid = None
            if c < WIN_CHUNKS:
                valid = (i * chunks + c - WIN_CHUNKS + col_chunk) >= 0
            outs = _attend_pair(qcols, k2[r0:r0 + span], v2[r0:r0 + span], sink, valid)
            for sl, o in zip(slabs, outs):
                o_ref[0, r0:r0 + CHUNK, sl:sl + LANES] = o.astype(o_ref.dtype)


def _attention_prompt(q, k, v, sinks, tq):
    b, s, _ = q.shape
    halo = WIN_CHUNKS * CHUNK
    ratio = tq // halo
    main = lambda bi, i: (bi, i, 0)
    prev = lambda bi, i: (bi, jnp.maximum(i * ratio - 1, 0), 0)
    return pl.pallas_call(
        _attn_prompt_kernel,
        grid=(b, s // tq),
        in_specs=[pl.BlockSpec(memory_space=pltpu.SMEM),
                  pl.BlockSpec((1, tq, ATTN_DIM), main),
                  pl.BlockSpec((1, tq, KV_DIM), main),
                  pl.BlockSpec((1, halo, KV_DIM), prev),
                  pl.BlockSpec((1, tq, KV_DIM), main),
                  pl.BlockSpec((1, halo, KV_DIM), prev)],
        out_specs=pl.BlockSpec((1, tq, ATTN_DIM), main),
        out_shape=jax.ShapeDtypeStruct((b, s, ATTN_DIM), q.dtype),
        compiler_params=_params(("arbitrary", "arbitrary")),
        name="attention_prompt",
    )(sinks, q, k, k, v, v)


def _attn_sample_kernel(sinks_ref, q_ref, k_ref, v_ref, ck_ref, cv_ref, o_ref, kw_ref, vw_ref):
    s = q_ref.shape[1]
    kf = jnp.concatenate([ck_ref[0], k_ref[0]], axis=0)
    vf = jnp.concatenate([cv_ref[0], v_ref[0]], axis=0)
    n = kf.shape[0]
    kw_ref[0] = kf[n - WIN_CACHE:]
    vw_ref[0] = vf[n - WIN_CACHE:]
    kb, vb = kf.astype(q_ref.dtype), vf.astype(q_ref.dtype)
    for j in range(KV_PAIRS):
        slabs = [(j * GQA + a) * LANES for a in range(GQA)]
        qcols = [q_ref[0, :, sl:sl + LANES] for sl in slabs]
        outs = _attend_pair(qcols, kb[:, j * LANES:(j + 1) * LANES], vb[:, j * LANES:(j + 1) * LANES],
                            _sink_col(sinks_ref, j, s), None)
        for sl, o in zip(slabs, outs):
            o_ref[0, :, sl:sl + LANES] = o.astype(o_ref.dtype)


def _attention_sample(q, k, v, sinks, cache_k, cache_v):
    b, s, _ = q.shape
    blk = lambda bi: (bi, 0, 0)
    return pl.pallas_call(
        _attn_sample_kernel,
        grid=(b,),
        in_specs=[pl.BlockSpec(memory_space=pltpu.SMEM),
                  pl.BlockSpec((1, s, ATTN_DIM), blk),
                  pl.BlockSpec((1, s, KV_DIM), blk),
                  pl.BlockSpec((1, s, KV_DIM), blk),
                  pl.BlockSpec((1, WIN_CACHE, KV_DIM), blk),
                  pl.BlockSpec((1, WIN_CACHE, KV_DIM), blk)],
        out_specs=[pl.BlockSpec((1, s, ATTN_DIM), blk),
                   pl.BlockSpec((1, WIN_CACHE, KV_DIM), blk),
                   pl.BlockSpec((1, WIN_CACHE, KV_DIM), blk)],
        out_shape=[jax.ShapeDtypeStruct((b, s, ATTN_DIM), q.dtype),
                   jax.ShapeDtypeStruct((b, WIN_CACHE, KV_DIM), F32),
                   jax.ShapeDtypeStruct((b, WIN_CACHE, KV_DIM), F32)],
        compiler_params=_params(("arbitrary",)),
        name="attention_sample",
    )(sinks, q, k, v, cache_k, cache_v)


def _ssd_kernel(xbc_ref, z_ref, dt_ref, hist_ref, h0_ref, cw_ref, cb_ref, dtb_ref, alog_ref, dsk_ref, gs_ref,
                ep_ref, el_ref, dmask_ref, causal_ref, tril_ref,
                y_ref, conv_ref, hfin_ref, prev, ht):
    c = pl.program_id(1)
    last = pl.num_programs(1) - 1
    ln = xbc_ref.shape[1]
    gw = SSM_HPG * SSM_HEAD_DIM

    @pl.when(c == 0)
    def _():
        prev[...] = hist_ref[0]
        ht[...] = h0_ref[0]

    xr = xbc_ref[0]
    ext = jnp.concatenate([prev[...], xr], axis=0)
    conv = cb_ref[...]
    for i in range(CONV_W):
        sh = CONV_W - 1 - i
        tap = xr if sh == 0 else pltpu.roll(ext, sh, 0)[SUBLANES:]
        conv = conv + tap * cw_ref[i:i + 1, :]
    prev[...] = xr[ln - SUBLANES:]

    @pl.when(c == last)
    def _():
        conv_ref[0] = xr[ln - (CONV_W - 1):]

    act = _silu(conv)
    xs = act[:, :SSM_DIM]
    bm = act[:, SSM_DIM:SSM_DIM + SSM_GROUPS * SSM_STATE]
    cm = act[:, SSM_DIM + SSM_GROUPS * SSM_STATE:]

    dtv = dt_ref[0] + dtb_ref[...]
    dt = jnp.maximum(dtv, 0.0) + jnp.log1p(jnp.exp(-jnp.abs(dtv)))
    ad = dt * (-jnp.exp(alog_ref[...]))
    hi, mid, lo = _split3(ad)
    tril = tril_ref[...]
    d = functools.partial(jnp.dot, preferred_element_type=F32)
    a_cs = d(tril, hi) + d(tril, mid) + d(tril, lo)
    a_last = a_cs[ln - 1:ln, :]
    stacked = jnp.concatenate([dt, jnp.exp(a_cs), jnp.exp(a_last - a_cs)], axis=0)
    wide = _dot_exact_rhs(stacked, ep_ref[...])
    dt_x, ea_x, ds_x = wide[:ln], wide[ln:2 * ln], wide[2 * ln:]
    cd_x = ea_x[ln - 1:ln, :]

    a_l = _dot_exact_rhs(a_cs, el_ref[...])
    a_s = jnp.sum(a_l * dmask_ref[...], axis=0, keepdims=True)
    lmat = jnp.exp(jnp.where(causal_ref[...] > 0.0, a_l - a_s, -1e30))

    cdt = y_ref.dtype
    xd = xs * dt_x
    xds = (xd * ds_x).astype(cdt)
    xdb = xd.astype(cdt)
    ys = []
    for g in range(SSM_GROUPS):
        bg = bm[:, g * SSM_STATE:(g + 1) * SSM_STATE].astype(cdt)
        cg = cm[:, g * SSM_STATE:(g + 1) * SSM_STATE].astype(cdt)
        cbm = _dot(cg, bg, _NT)
        yd = []
        for r in range(SSM_HPG):
            hd = g * SSM_HPG + r
            w = (cbm * lmat[:, hd * ln:(hd + 1) * ln]).astype(cdt)
            yd.append(_dot(w, xdb[:, hd * SSM_HEAD_DIM:(hd + 1) * SSM_HEAD_DIM]))
        htg = ht[g]
        y_off = _dot(cg, htg.astype(cdt)) * ea_x[:, g * gw:(g + 1) * gw]
        st = _dot(bg, xds[:, g * gw:(g + 1) * gw], _TN)
        ht[g] = htg * cd_x[:, g * gw:(g + 1) * gw] + st
        ys.append(jnp.concatenate(yd, axis=1) + y_off)
    y = jnp.concatenate(ys, axis=1) + dsk_ref[...] * xs
    y = y * _silu(z_ref[0])
    outs = [_rms(y[:, g * gw:(g + 1) * gw], gs_ref[:, g * gw:(g + 1) * gw]) for g in range(SSM_GROUPS)]
    y_ref[0] = jnp.concatenate(outs, axis=1).astype(y_ref.dtype)

    @pl.when(c == last)
    def _():
        hfin_ref[0] = ht[...]


def _ssd_constants(ln):
    heads = SSM_HEADS
    ep = np.zeros((LANES, heads * SSM_HEAD_DIM), np.float32)
    el = np.zeros((LANES, heads * ln), np.float32)
    for r in range(heads):
        ep[r, r * SSM_HEAD_DIM:(r + 1) * SSM_HEAD_DIM] = 1.0
        el[r, r * ln:(r + 1) * ln] = 1.0
    eye = np.tile(np.eye(ln, dtype=np.float32), (1, heads))
    causal = np.tile(np.tril(np.ones((ln, ln), np.float32)), (1, heads))
    tril = np.tril(np.ones((ln, ln), np.float32))
    return (jnp.asarray(ep, BF16), jnp.asarray(el, BF16), jnp.asarray(eye), jnp.asarray(causal),
            jnp.asarray(tril, BF16))


def _ssd_mixer(xbc, z, dt_raw, hist8, h0t, conv_w8, conv_b, dt_bias, a_log, dskip_x, g_ssm, ln, out_dtype):
    b, s, _ = xbc.shape
    consts = _ssd_constants(ln)
    row = lambda bi, c: (bi, c, 0)
    per_b3 = lambda bi, c: (bi, 0, 0)
    per_b4 = lambda bi, c: (bi, 0, 0, 0)
    const2 = lambda bi, c: (0, 0)
    full = lambda a: pl.BlockSpec(a.shape, const2)
    gw = SSM_HPG * SSM_HEAD_DIM
    return pl.pallas_call(
        _ssd_kernel,
        grid=(b, s // ln),
        in_specs=[pl.BlockSpec((1, ln, CONV_DIM), row),
                  pl.BlockSpec((1, ln, SSM_DIM), row),
                  pl.BlockSpec((1, ln, LANES), row),
                  pl.BlockSpec((1, SUBLANES, CONV_DIM), per_b3),
                  pl.BlockSpec((1, SSM_GROUPS, SSM_STATE, gw), per_b4),
                  full(conv_w8), full(conv_b), full(dt_bias), full(a_log), full(dskip_x), full(g_ssm)]
                 + [full(a) for a in consts],
        out_specs=[pl.BlockSpec((1, ln, SSM_DIM), row),
                   pl.BlockSpec((1, CONV_W - 1, CONV_DIM), per_b3),
                   pl.BlockSpec((1, SSM_GROUPS, SSM_STATE, gw), per_b4)],
        out_shape=[jax.ShapeDtypeStruct((b, s, SSM_DIM), out_dtype),
                   jax.ShapeDtypeStruct((b, CONV_W - 1, CONV_DIM), F32),
                   jax.ShapeDtypeStruct((b, SSM_GROUPS, SSM_STATE, gw), F32)],
        scratch_shapes=[pltpu.VMEM((SUBLANES, CONV_DIM), F32),
                        pltpu.VMEM((SSM_GROUPS, SSM_STATE, gw), F32)],
        compiler_params=_params(("arbitrary", "arbitrary")),
        name="ssd_mixer",
    )(xbc, z, dt_raw, hist8, h0t, conv_w8, conv_b, dt_bias, a_log, dskip_x, g_ssm, *consts)


def _outproj_kernel(a_ref, s_ref, x_ref, w_ref, gpost_ref, gate_ref, gpre_ref, sc_ref, sh_ref, x1_ref, h2_ref):
    mix = _dot(a_ref[0], w_ref[:ATTN_DIM, :]) + _dot(s_ref[0], w_ref[ATTN_DIM:, :])
    x1 = x_ref[0] + gate_ref[0] * _rms(mix, gpost_ref[...])
    x1_ref[0] = x1
    h2_ref[...] = _rms(x1, gpre_ref[...]) * sc_ref[0] + sh_ref[0]


def _out_projection(attn, ssm, x, w_out, g_post, gate, g_pre, sc1p, sh, tm, h2_rows):
    b, s, d = x.shape
    nt = s // tm
    row = lambda bi, i: (bi, i, 0)
    const2 = lambda bi, i: (0, 0)
    return pl.pallas_call(
        _outproj_kernel,
        grid=(b, nt),
        in_specs=[pl.BlockSpec((1, tm, ATTN_DIM), row),
                  pl.BlockSpec((1, tm, SSM_DIM), row),
                  pl.BlockSpec((1, tm, d), row),
                  pl.BlockSpec((ATTN_DIM + SSM_DIM, d), const2, pipeline_mode=pl.Buffered(1)),
                  pl.BlockSpec((1, d), const2),
                  _mod_spec(gate, tm),
                  pl.BlockSpec((1, d), const2),
                  _mod_spec(sc1p, tm),
                  _mod_spec(sh, tm)],
        out_specs=[pl.BlockSpec((1, tm, d), row),
                   pl.BlockSpec((tm, d), lambda bi, i: (bi * nt + i, 0))],
        out_shape=[jax.ShapeDtypeStruct((b, s, d), F32),
                   jax.ShapeDtypeStruct((h2_rows, d), F32)],
        compiler_params=_params(("arbitrary", "arbitrary")),
        name="out_projection",
    )(attn, ssm, x, w_out, g_post, gate, g_pre, sc1p, sh)


LANE_IDX, LANE_RANK, LANE_GATE = 0, TOP_K, 2 * TOP_K


def _router_kernel(hp_ref, hs_ref, w_ref, b_ref, ltri_ref, meta_ref, cnt_ref, *, p_tiles):
    i = pl.program_id(0)
    tt = hp_ref.shape[0]

    @pl.when(i == 0)
    def _():
        cnt_ref[...] = jnp.zeros_like(cnt_ref)

    h = jnp.where(i < p_tiles, hp_ref[...], hs_ref[...])
    logits = _dot(h.astype(w_ref.dtype), w_ref[...]) + b_ref[...]
    lane = lax.broadcasted_iota(jnp.int32, (tt, LANES), 1)
    lane_f = lane.astype(F32)
    work = jnp.where(lane < N_EXPERTS, logits, -jnp.inf)
    vals, hots, idxs = [], [], []
    for _ in range(TOP_K):
        m = jnp.max(work, axis=-1, keepdims=True)
        idx = jnp.min(jnp.where(work == m, lane_f, float(LANES)), axis=-1, keepdims=True)
        hot = lane_f == idx
        vals.append(m)
        hots.append(hot)
        idxs.append(idx)
        work = jnp.where(hot, -jnp.inf, work)
    es = [jnp.exp(v - vals[0]) for v in vals]
    den = es[0] + es[1] + es[2] + es[3]
    onehot = jnp.zeros((tt, LANES), F32)
    for hot in hots:
        onehot = jnp.where(hot, 1.0, onehot)
    before = jnp.dot(ltri_ref[...], onehot.astype(BF16), preferred_element_type=F32) + cnt_ref[0:1, :]
    meta = jnp.zeros((tt, LANES), F32)
    for k in range(TOP_K):
        rank_k = jnp.sum(jnp.where(hots[k], before, 0.0), axis=-1, keepdims=True)
        meta = jnp.where(lane == LANE_IDX + k, idxs[k], meta)
        meta = jnp.where(lane == LANE_RANK + k, rank_k, meta)
        meta = jnp.where(lane == LANE_GATE + k, es[k] / den, meta)
    meta_ref[...] = meta
    cnt_ref[...] = cnt_ref[...] + jnp.sum(onehot, axis=0, keepdims=True)


def _token_specs(p_tiles, tt, d):
    return [pl.BlockSpec((tt, d), lambda i, *_: (jnp.minimum(i, p_tiles - 1), 0)),
            pl.BlockSpec((tt, d), lambda i, *_: (jnp.maximum(i - p_tiles, 0), 0))]


def _router(h_p, h_s, w_router_pad, b_router_pad, tt):
    d = h_p.shape[1]
    n = h_p.shape[0] + h_s.shape[0]
    p_tiles = h_p.shape[0] // tt
    ltri = jnp.asarray(np.tril(np.ones((tt, tt), np.float32), -1), BF16)
    return pl.pallas_call(
        functools.partial(_router_kernel, p_tiles=p_tiles),
        grid=(n // tt,),
        in_specs=_token_specs(p_tiles, tt, d)
                 + [pl.BlockSpec((d, LANES), lambda i: (0, 0)),
                    pl.BlockSpec((1, LANES), lambda i: (0, 0)),
                    pl.BlockSpec((tt, tt), lambda i: (0, 0))],
        out_specs=[pl.BlockSpec((tt, LANES), lambda i: (i, 0)),
                   pl.BlockSpec((SUBLANES, LANES), lambda i: (0, 0))],
        out_shape=[jax.ShapeDtypeStruct((n, LANES), F32),
                   jax.ShapeDtypeStruct((SUBLANES, LANES), F32)],
        compiler_params=_params(("arbitrary",)),
        name="router",
    )(h_p, h_s, w_router_pad, b_router_pad, ltri)


def _dispatch_kernel(zstart_ref, nu_ref, pos_ref, hp_ref, hs_ref, xs_ref, zeros, sem_z, sem_r, *, p_tiles):
    i = pl.program_id(0)
    tt = hp_ref.shape[0]
    tm = zeros.shape[0]
    n_blocks = xs_ref.shape[0] // tm

    def zero_copy(row):
        return pltpu.make_async_copy(zeros, xs_ref.at[pl.ds(pl.multiple_of(row, tm), tm)], sem_z)

    @pl.when(i == 0)
    def _():
        zeros[...] = jnp.zeros_like(zeros)

        def per_expert(act):
            def body(e, carry):
                @pl.when(zstart_ref[e] >= 0)
                def _():
                    act(zero_copy(zstart_ref[e]))
                return carry
            lax.fori_loop(0, N_EXPERTS, body, 0)

        def per_tail(act):
            def body(blk, carry):
                act(zero_copy(blk * tm))
                return carry
            lax.fori_loop(nu_ref[0], n_blocks, body, 0)

        per_expert(lambda cp: cp.start())
        per_tail(lambda cp: cp.start())
        per_expert(lambda cp: cp.wait())
        per_tail(lambda cp: cp.wait())

    def scatter_rows(h_ref):
        def row_copy(t, k):
            return pltpu.make_async_copy(h_ref.at[pl.ds(t, 1)],
                                         xs_ref.at[pl.ds(pos_ref[0, 0, t * TOP_K + k], 1)], sem_r)

        def start_rows(t, carry):
            for k in range(TOP_K):
                row_copy(t, k).start()
            return carry

        def wait_rows(t, carry):
            for k in range(TOP_K):
                row_copy(t, k).wait()
            return carry

        lax.fori_loop(0, tt, start_rows, 0)
        lax.fori_loop(0, tt, wait_rows, 0)

    @pl.when(i < p_tiles)
    def _():
        scatter_rows(hp_ref)

    @pl.when(i >= p_tiles)
    def _():
        scatter_rows(hs_ref)


def _dispatch(h_p, h_s, pos3, zstart, n_used, n_slots, tt):
    d = h_p.shape[1]
    n = h_p.shape[0] + h_s.shape[0]
    p_tiles = h_p.shape[0] // tt
    return pl.pallas_call(
        functools.partial(_dispatch_kernel, p_tiles=p_tiles),
        grid_spec=pltpu.PrefetchScalarGridSpec(
            num_scalar_prefetch=2,
            grid=(n // tt,),
            in_specs=[pl.BlockSpec((1, 1, tt * TOP_K), lambda i, *_: (i, 0, 0), memory_space=pltpu.SMEM)]
                     + _token_specs(p_tiles, tt, d),
            out_specs=pl.BlockSpec(memory_space=pl.ANY),
            scratch_shapes=[pltpu.VMEM((MOE_TM, d), F32),
                            pltpu.SemaphoreType.DMA(()),
                            pltpu.SemaphoreType.DMA(())]),
        out_shape=jax.ShapeDtypeStruct((n_slots, d), F32),
        compiler_params=_params(("arbitrary",)),
        name="dispatch",
    )(zstart, n_used, pos3, h_p, h_s)


def _up_kernel(be_ref, nu_ref, x_ref, w_ref, b_ref, act_ref):
    i = pl.program_id(0)
    fc = 512

    @pl.when(i < nu_ref[0])
    def _():
        xb = x_ref[...].astype(BF16)
        for c in range(D_FF // fc):
            glu = jnp.dot(xb, w_ref[:, c * fc:(c + 1) * fc], preferred_element_type=F32) + b_ref[:, c * fc:(c + 1) * fc]
            lin = (jnp.dot(xb, w_ref[:, D_FF + c * fc:D_FF + (c + 1) * fc], preferred_element_type=F32)
                   + b_ref[:, D_FF + c * fc:D_FF + (c + 1) * fc])
            glu = jnp.minimum(glu, SWIGLU_LIMIT)
            lin = jnp.clip(lin, -SWIGLU_LIMIT, SWIGLU_LIMIT)
            act_ref[:, c * fc:(c + 1) * fc] = (glu * _sigmoid(SWIGLU_ALPHA * glu) * (lin + 1.0)).astype(act_ref.dtype)

    @pl.when(i >= nu_ref[0])
    def _():
        act_ref[...] = jnp.zeros_like(act_ref)


def _down_kernel(be_ref, nu_ref, a_ref, w_ref, b_ref, y_ref):
    i = pl.program_id(0)

    @pl.when(i < nu_ref[0])
    def _():
        y_ref[...] = jnp.dot(a_ref[...], w_ref[...], preferred_element_type=F32) + b_ref[...]

    @pl.when(i >= nu_ref[0])
    def _():
        y_ref[...] = jnp.zeros_like(y_ref)


def _grouped(kernel, name, x, w, b, block_expert, n_used, out_dtype):
    n_slots, kdim = x.shape
    _, _, ndim = w.shape
    nb = n_slots // MOE_TM
    used = lambda i, be, nu: (jnp.minimum(i, nu[0] - 1), 0)
    expert = lambda i, be, nu: (be[i], 0, 0)
    out_w = D_FF if kernel is _up_kernel else ndim
    return pl.pallas_call(
        kernel,
        grid_spec=pltpu.PrefetchScalarGridSpec(
            num_scalar_prefetch=2,
            grid=(nb,),
            in_specs=[pl.BlockSpec((MOE_TM, kdim), used),
                      pl.BlockSpec((None, kdim, ndim), expert),
                      pl.BlockSpec((None, 1, ndim), expert)],
            out_specs=pl.BlockSpec((MOE_TM, out_w), lambda i, be, nu: (i, 0))),
        out_shape=jax.ShapeDtypeStruct((n_slots, out_w), out_dtype),
        compiler_params=_params(("arbitrary",)),
        name=name,
    )(block_expert, n_used, x, w, b)


def _combine_kernel(pos_ref, y_ref, meta_ref, x1_ref, gate_ref, g_ref, o_ref, rows, sem):
    tt = x1_ref.shape[0]

    def row_copy(t, k):
        return pltpu.make_async_copy(y_ref.at[pl.ds(pos_ref[0, 0, t * TOP_K + k], 1)],
                                     rows.at[k, pl.ds(t, 1)], sem)

    def start_rows(t, carry):
        for k in range(TOP_K):
            row_copy(t, k).start()
        return carry

    def wait_rows(t, carry):
        for k in range(TOP_K):
            row_copy(t, k).wait()
        return carry

    lax.fori_loop(0, tt, start_rows, 0)
    lax.fori_loop(0, tt, wait_rows, 0)
    meta = meta_ref[...]
    f = rows[0] * meta[:, LANE_GATE:LANE_GATE + 1]
    for k in range(1, TOP_K):
        f = f + rows[k] * meta[:, LANE_GATE + k:LANE_GATE + k + 1]
    o_ref[...] = x1_ref[...] + gate_ref[0] * _rms(f, g_ref[...])


def _combine(y, pos3, meta, x1, gate, g_post, tt):
    n, d = x1.shape
    nb = gate.shape[0]
    tiles_per_b = n // tt // nb
    return pl.pallas_call(
        _combine_kernel,
        grid=(n // tt,),
        in_specs=[pl.BlockSpec((1, 1, tt * TOP_K), lambda i: (i, 0, 0), memory_space=pltpu.SMEM),
                  pl.BlockSpec(memory_space=pl.ANY),
                  pl.BlockSpec((tt, LANES), lambda i: (i, 0)),
                  pl.BlockSpec((tt, d), lambda i: (i, 0)),
                  pl.BlockSpec((1, 1, d), lambda i: (i // tiles_per_b, 0, 0)),
                  pl.BlockSpec((1, d), lambda i: (0, 0))],
        out_specs=pl.BlockSpec((tt, d), lambda i: (i, 0)),
        out_shape=jax.ShapeDtypeStruct((n, d), F32),
        scratch_shapes=[pltpu.VMEM((TOP_K, tt, d), F32), pltpu.SemaphoreType.DMA(())],
        compiler_params=_params(("arbitrary",)),
        name="combine",
    )(pos3, y, meta, x1, gate, g_post)


def _mixer(x, mod, pos, hist8, h0t, ln, tm, tq, wts, cache, h2_rows):
    b, s, d = x.shape
    mods = [m.reshape(b, 1, d) for m in jnp.split(mod, 6, axis=-1)]
    gt_f = mods[5]
    if cache is None:
        xf, fb, fs = x, b, s
    else:
        fb, fs = 1, b * s
        xf = x.reshape(fb, fs, d)
        pos = jnp.tile(pos, b)
        mods = [jnp.broadcast_to(m, (b, s, d)).reshape(fb, fs, d) for m in mods]
    sh_m, sc_m, gt_m, sh_f, sc_f, _ = mods
    proj = _in_projection(xf, 1.0 + sc_m, sh_m, wts["g_mix_pre"], wts["w_main"], wts["w_dt"], _rope_tables(pos), tm)
    q, k, v, z, xbc, dt_raw = [t.reshape(b, s, t.shape[-1]) for t in proj]
    if cache is None:
        attn = _attention_prompt(q, k, v, wts["sinks"], tq)
        k_win, v_win = k[:, s - WIN_CACHE:], v[:, s - WIN_CACHE:]
    else:
        attn, k_win, v_win = _attention_sample(q, k, v, wts["sinks"], *cache)
    ssm, conv_state, hfin = _ssd_mixer(xbc, z, dt_raw, hist8, h0t, wts["conv_w8"], wts["conv_b"], wts["dt_bias"],
                                       wts["a_log"], wts["dskip_x"], wts["g_ssm"], ln, q.dtype)
    x1, h2 = _out_projection(attn.reshape(fb, fs, -1), ssm.reshape(fb, fs, -1), xf, wts["w_out"], wts["g_mix_post"],
                             gt_m, wts["g_ffn_pre"], 1.0 + sc_f, sh_f, tm, h2_rows)
    ssm_state = hfin.reshape(b, SSM_GROUPS, SSM_STATE, SSM_HPG, SSM_HEAD_DIM)
    ssm_state = ssm_state.transpose(0, 1, 3, 4, 2).reshape(b, SSM_HEADS, SSM_HEAD_DIM, SSM_STATE)
    states = (k_win.reshape(b, WIN_CACHE, N_KV, HEAD_DIM), v_win.reshape(b, WIN_CACHE, N_KV, HEAD_DIM),
              conv_state, ssm_state)
    return x1.reshape(b * s, d), h2, gt_f, states


def _pad_lanes(a, width=LANES):
    return jnp.pad(a, [(0, 0)] * (a.ndim - 1) + [(0, width - a.shape[-1])])


def _largest_tile(n, cap):
    t = cap
    while n % t:
        t //= 2
    return t


def kernel(x_prompt, x_sample, c_prompt, c_sample, cache_k, cache_v, state_conv, state_ssm, w_mod, b_mod, g_mix_pre, g_mix_post, g_ffn_pre, g_ffn_post, w_in, conv_w, conv_b, dt_bias, a_log, d_skip, g_ssm, sinks, w_out, w_router, b_router, w_up, b_up, w_down, b_down):
    depth = w_mod.shape[0]
    assert depth == 1, "single-layer step"
    l = 0
    bp, sp, d = x_prompt.shape
    bs, ss, _ = x_sample.shape
    n_p, n_s = bp * sp, bs * ss
    n_tok = n_p + n_s

    c_all = jnp.concatenate([c_prompt, c_sample], axis=0)
    c_rows = -(-c_all.shape[0] // SUBLANES) * SUBLANES
    mod_all = _modulation(jnp.pad(c_all, ((0, c_rows - c_all.shape[0]), (0, 0))), w_mod[l], b_mod[l])

    row2 = lambda a: a.reshape(1, -1)
    paired = _paired_head_columns()
    wts = {
        "g_mix_pre": row2(g_mix_pre[l]), "g_mix_post": row2(g_mix_post[l]), "g_ffn_pre": row2(g_ffn_pre[l]),
        "w_main": jnp.concatenate([w_in[l][:, :ATTN_DIM][:, paired], w_in[l][:, ATTN_DIM:MAIN_DIM]],
                                  axis=1).astype(BF16),
        "w_dt": _pad_lanes(w_in[l][:, MAIN_DIM:]).astype(BF16),
        "sinks": sinks[l],
        "conv_w8": jnp.pad(conv_w[l], ((0, SUBLANES - CONV_W), (0, 0))), "conv_b": row2(conv_b[l]),
        "dt_bias": _pad_lanes(row2(dt_bias[l])), "a_log": _pad_lanes(row2(a_log[l])),
        "dskip_x": row2(jnp.repeat(d_skip[l], SSM_HEAD_DIM)), "g_ssm": row2(g_ssm[l]),
        "w_out": jnp.concatenate([w_out[l][:ATTN_DIM][paired], w_out[l][ATTN_DIM:]], axis=0).astype(BF16),
    }
    gw = SSM_HPG * SSM_HEAD_DIM

    hist_p = jnp.zeros((bp, SUBLANES, CONV_DIM), F32)
    h0_p = jnp.zeros((bp, SSM_GROUPS, SSM_STATE, gw), F32)
    x1_p, h2_p, gtf_p, st_p = _mixer(x_prompt, mod_all[:bp], jnp.arange(sp, dtype=jnp.int32), hist_p, h0_p,
                                     SSD_CHUNK, _largest_tile(sp, 256), _largest_tile(sp, 256), wts, None, n_p)

    hist_s = jnp.pad(state_conv[l], ((0, 0), (SUBLANES - (CONV_W - 1), 0), (0, 0)))
    h0_s = state_ssm[l].astype(F32).reshape(bs, SSM_GROUPS, SSM_HPG, SSM_HEAD_DIM, SSM_STATE)
    h0_s = h0_s.transpose(0, 1, 4, 2, 3).reshape(bs, SSM_GROUPS, SSM_STATE, gw)
    cache = (cache_k[l].reshape(bs, WIN_CACHE, KV_DIM), cache_v[l].reshape(bs, WIN_CACHE, KV_DIM))
    x1_s, h2_s, gtf_s, st_s = _mixer(x_sample, mod_all[bp:bp + bs], PAST_LEN + jnp.arange(ss, dtype=jnp.int32),
                                     hist_s, h0_s, ss, n_s, ss, wts, cache, n_s)

    tt = _largest_tile(n_s, 128)
    assert n_p % tt == 0
    meta, counts = _router(h2_p, h2_s, _pad_lanes(w_router[l]).astype(BF16), _pad_lanes(row2(b_router[l])), tt)
    top_idx = meta[:, LANE_IDX:LANE_IDX + TOP_K].astype(jnp.int32)
    rank = meta[:, LANE_RANK:LANE_RANK + TOP_K].astype(jnp.int32)
    cnt = counts[0, :N_EXPERTS].astype(jnp.int32)
    padded = (cnt + MOE_TM - 1) // MOE_TM * MOE_TM
    pend = jnp.cumsum(padded)
    offs = pend - padded
    pos = offs[top_idx] + rank
    n_blocks = -(-n_tok * TOP_K // MOE_TM) + N_EXPERTS
    n_slots = n_blocks * MOE_TM
    n_used = (pend[-1] // MOE_TM).astype(jnp.int32)
    blk = jnp.arange(n_blocks, dtype=jnp.int32)
    blk_row = jnp.minimum(blk, n_used - 1) * MOE_TM
    block_expert = jnp.minimum(jnp.sum(pend[None, :] <= blk_row[:, None], axis=1), N_EXPERTS - 1).astype(jnp.int32)
    zstart = jnp.where(cnt > 0, pend - MOE_TM, -1).astype(jnp.int32)
    pos3 = pos.reshape(n_tok // tt, 1, tt * TOP_K)
    nu = n_used.reshape(1)
    xs = _dispatch(h2_p, h2_s, pos3, zstart, nu, n_slots, tt)
    act = _grouped(_up_kernel, "expert_up", xs, w_up[l].astype(BF16), b_up[l].reshape(N_EXPERTS, 1, -1),
                   block_expert, nu, BF16)
    y = _grouped(_down_kernel, "expert_down", act, w_down[l].astype(BF16), b_down[l].reshape(N_EXPERTS, 1, -1),
                 block_expert, nu, F32)

    g_post = row2(g_ffn_post[l])
    tt_p = _largest_tile(sp, 128)
    y_p = _combine(y, pos[:n_p].reshape(n_p // tt_p, 1, tt_p * TOP_K), meta[:n_p], x1_p,
                   gtf_p, g_post, tt_p).reshape(bp, sp, d)
    y_s = _combine(y, pos[n_p:].reshape(bs, 1, ss * TOP_K), meta[n_p:], x1_s,
                   gtf_s, g_post, ss).reshape(bs, ss, d)

    stack = lambda a: a[None]
    return (y_p, y_s, stack(st_p[0]), stack(st_p[1]), stack(st_p[2]), stack(st_p[3]),
            stack(st_s[0]), stack(st_s[1]), stack(st_s[2]), stack(st_s[3]))
```

```python
import functools

import numpy as np
import jax
import jax.numpy as jnp
from jax import lax
from jax.experimental import pallas as pl
from jax.experimental.pallas import tpu as pltpu

F32 = jnp.float32
BF16 = jnp.bfloat16

D_MODEL = 2048
CHUNK = 64
N_HEADS = 16
N_KV = 4
HEAD_DIM = 64
GQA = N_HEADS // N_KV
ATTN_DIM = N_HEADS * HEAD_DIM
KV_DIM = N_KV * HEAD_DIM
WINDOW = 128
WIN_CHUNKS = WINDOW // CHUNK
PAST_LEN = 4096
WIN_CACHE = min(WINDOW, PAST_LEN)
ROT_DIM = HEAD_DIM // 4
ROPE_THETA = 500000.0
SSM_HEADS = 16
SSM_HEAD_DIM = 64
SSM_DIM = SSM_HEADS * SSM_HEAD_DIM
SSM_GROUPS = 2
SSM_HPG = SSM_HEADS // SSM_GROUPS
SSM_STATE = 128
CONV_W = 4
CONV_DIM = SSM_DIM + 2 * SSM_GROUPS * SSM_STATE
SSD_CHUNK = 64
N_EXPERTS = 32
TOP_K = 4
D_FF = 2048
SWIGLU_ALPHA = 1.702
SWIGLU_LIMIT = 7.0
NORM_EPS = 1e-6

LANES = 128
SUBLANES = 8
MAIN_DIM = ATTN_DIM + 2 * KV_DIM + SSM_DIM + CONV_DIM
MOE_TM = 256
VMEM_LIMIT = 56 * 1024 * 1024


def _sigmoid(x):
    return 1.0 / (1.0 + jnp.exp(-x))


def _silu(x):
    return x * _sigmoid(x)


def _rms(x, g):
    return x * lax.rsqrt(jnp.mean(x * x, axis=-1, keepdims=True) + NORM_EPS) * g


def _split3(x):
    hi = x.astype(BF16)
    r1 = x - hi.astype(F32)
    mid = r1.astype(BF16)
    lo = (r1 - mid.astype(F32)).astype(BF16)
    return hi, mid, lo


def _dot_exact_rhs(x, m):
    hi, mid, lo = _split3(x)
    d = functools.partial(jnp.dot, preferred_element_type=F32)
    return d(hi, m) + d(mid, m) + d(lo, m)


_NN = (((1,), (0,)), ((), ()))
_NT = (((1,), (1,)), ((), ()))
_TN = (((0,), (0,)), ((), ()))


def _dot(a, b, dims=_NN):
    assert a.dtype == b.dtype, (a.dtype, b.dtype)
    prec = lax.Precision.HIGHEST if a.dtype == F32 else None
    return lax.dot_general(a, b, dims, preferred_element_type=F32, precision=prec)


def _params(sem, vmem=VMEM_LIMIT):
    return pltpu.CompilerParams(dimension_semantics=sem, vmem_limit_bytes=vmem)


def _mod_kernel(c_ref, w_ref, b_ref, o_ref):
    o_ref[...] = _dot(_silu(c_ref[...]).astype(BF16), w_ref[...].astype(BF16)) + b_ref[...]


def _modulation(c_all, w_mod, b_mod):
    rows, d = c_all.shape
    n = w_mod.shape[1]
    tn = 1536
    return pl.pallas_call(
        _mod_kernel,
        grid=(n // tn,),
        in_specs=[pl.BlockSpec((rows, d), lambda j: (0, 0)),
                  pl.BlockSpec((d, tn), lambda j: (0, j)),
                  pl.BlockSpec((1, tn), lambda j: (0, j))],
        out_specs=pl.BlockSpec((rows, tn), lambda j: (0, j)),
        out_shape=jax.ShapeDtypeStruct((rows, n), F32),
        compiler_params=_params(("arbitrary",)),
        name="modulation",
    )(c_all, w_mod, b_mod.reshape(1, n))


def _rope(t, cos, s1, s2):
    outs = []
    for j in range(t.shape[1] // LANES):
        tj = t[:, j * LANES:(j + 1) * LANES]
        up = pltpu.roll(tj, LANES - ROT_DIM // 2, 1)
        dn = pltpu.roll(tj, ROT_DIM // 2, 1)
        outs.append(tj * cos + up * s1 + dn * s2)
    return jnp.concatenate(outs, axis=1)


def _inproj_kernel(x_ref, sc_ref, sh_ref, g_ref, w_ref, wdt_ref, cos_ref, s1_ref, s2_ref,
                   q_ref, k_ref, v_ref, z_ref, xbc_ref, dt_ref):
    h = _rms(x_ref[0], g_ref[...]) * sc_ref[0] + sh_ref[0]
    hb = h.astype(w_ref.dtype)
    cos, s1, s2 = cos_ref[...], s1_ref[...], s2_ref[...]
    step = 512

    def mm(lo, hi):
        return jnp.concatenate([_dot(hb, w_ref[:, c:min(c + step, hi)]) for c in range(lo, hi, step)], axis=1)

    o = 0
    q_ref[0] = (_rope(mm(o, o + ATTN_DIM), cos, s1, s2) * (HEAD_DIM ** -0.5)).astype(q_ref.dtype)
    o += ATTN_DIM
    k_ref[0] = _rope(mm(o, o + KV_DIM), cos, s1, s2)
    o += KV_DIM
    v_ref[0] = mm(o, o + KV_DIM)
    o += KV_DIM
    z_ref[0] = mm(o, o + SSM_DIM)
    o += SSM_DIM
    xbc_ref[0] = mm(o, o + CONV_DIM)
    dt_ref[0] = _dot(hb, wdt_ref[...])


def _mod_spec(m, tm):
    if m.shape[1] == 1:
        return pl.BlockSpec((1, 1, m.shape[2]), lambda bi, i: (bi, 0, 0))
    return pl.BlockSpec((1, tm, m.shape[2]), lambda bi, i: (bi, i, 0))


def _in_projection(x, sc1p, sh, g, w_main, w_dt, rope_tabs, tm):
    b, s, d = x.shape
    cos, s1, s2 = rope_tabs
    row = lambda bi, i: (bi, i, 0)
    const2 = lambda bi, i: (0, 0)
    tab = lambda bi, i: (i, 0)
    widths = (ATTN_DIM, KV_DIM, KV_DIM, SSM_DIM, CONV_DIM, LANES)
    dtypes = (w_main.dtype, F32, F32, F32, F32, F32)
    return pl.pallas_call(
        _inproj_kernel,
        grid=(b, s // tm),
        in_specs=[pl.BlockSpec((1, tm, d), row),
                  _mod_spec(sc1p, tm),
                  _mod_spec(sh, tm),
                  pl.BlockSpec((1, d), const2),
                  pl.BlockSpec((d, MAIN_DIM), const2, pipeline_mode=pl.Buffered(1)),
                  pl.BlockSpec((d, LANES), const2, pipeline_mode=pl.Buffered(1)),
                  pl.BlockSpec((tm, LANES), tab),
                  pl.BlockSpec((tm, LANES), tab),
                  pl.BlockSpec((tm, LANES), tab)],
        out_specs=[pl.BlockSpec((1, tm, w), row) for w in widths],
        out_shape=[jax.ShapeDtypeStruct((b, s, w), dt) for w, dt in zip(widths, dtypes)],
        compiler_params=_params(("arbitrary", "arbitrary")),
        name="in_projection",
    )(x, sc1p, sh, g, w_main, w_dt, cos, s1, s2)


def _rope_tables(pos):
    inv_freq = ROPE_THETA ** (-jnp.arange(0, ROT_DIM, 2, dtype=F32) / ROT_DIM)
    ang = pos.astype(F32)[:, None] * inv_freq[None, :]
    cos, sin = jnp.cos(ang), jnp.sin(ang)
    half = ROT_DIM // 2
    n = pos.shape[0]
    ones = jnp.ones((n, HEAD_DIM - ROT_DIM), F32)
    zeros_h = jnp.zeros((n, half), F32)
    zeros_r = jnp.zeros((n, HEAD_DIM - ROT_DIM), F32)
    c = jnp.concatenate([cos, cos, ones], axis=1)
    a = jnp.concatenate([-sin, zeros_h, zeros_r], axis=1)
    b = jnp.concatenate([zeros_h, sin, zeros_r], axis=1)
    rep = LANES // HEAD_DIM
    return tuple(jnp.tile(t, (1, rep)) for t in (c, a, b))


KV_PAIRS = N_KV // 2


def _paired_head_columns():
    cols = []
    for j in range(KV_PAIRS):
        for a in range(GQA):
            for g in (2 * j, 2 * j + 1):
                h = g * GQA + a
                cols.extend(range(h * HEAD_DIM, (h + 1) * HEAD_DIM))
    return np.asarray(cols, np.int32)


def _attend_pair(qcols, k2, v2, sink_col, valid):
    rows = qcols[0].shape[0]
    lo = lax.broadcasted_iota(jnp.int32, (rows, LANES), 1) < HEAD_DIM
    zero = jnp.zeros_like(qcols[0])
    lhs = jnp.concatenate([jnp.where(lo, qc, zero) for qc in qcols] + [jnp.where(lo, zero, qc) for qc in qcols], axis=0)
    s = _dot(lhs, k2, _NT)
    if valid is not None:
        s = jnp.where(valid, s, -1e30)
    m = jnp.maximum(jnp.max(s, axis=-1, keepdims=True), sink_col)
    p = jnp.exp(s - m)
    den = jnp.sum(p, axis=-1, keepdims=True) + jnp.exp(sink_col - m)
    o = _dot((p * (1.0 / den)).astype(v2.dtype), v2)
    half = GQA * rows
    return [jnp.where(lo, o[a * rows:(a + 1) * rows], o[half + a * rows:half + (a + 1) * rows]) for a in range(GQA)]


def _sink_col(sinks_ref, j, rows):
    heads = [(2 * j) * GQA + a for a in range(GQA)] + [(2 * j + 1) * GQA + a for a in range(GQA)]
    return jnp.concatenate([jnp.full((rows, 1), sinks_ref[h], F32) for h in heads], axis=0)


def _attn_prompt_kernel(sinks_ref, q_ref, km_ref, kh_ref, vm_ref, vh_ref, o_ref):
    i = pl.program_id(1)
    tq = q_ref.shape[1]
    chunks = tq // CHUNK
    kfull = jnp.concatenate([kh_ref[0], km_ref[0]], axis=0).astype(q_ref.dtype)
    vfull = jnp.concatenate([vh_ref[0], vm_ref[0]], axis=0).astype(q_ref.dtype)
    span = (WIN_CHUNKS + 1) * CHUNK
    col_chunk = lax.broadcasted_iota(jnp.int32, (2 * GQA * CHUNK, span), 1) // CHUNK
    for j in range(KV_PAIRS):
        k2 = kfull[:, j * LANES:(j + 1) * LANES]
        v2 = vfull[:, j * LANES:(j + 1) * LANES]
        sink = _sink_col(sinks_ref, j, CHUNK)
        for c in range(chunks):
            r0 = c * CHUNK
            slabs = [(j * GQA + a) * LANES for a in range(GQA)]
            qcols = [q_ref[0, r0:r0 + CHUNK, sl:sl + LANES] for sl in slabs]
            valid = None
            if c < WIN_CHUNKS:
                valid = (i * chunks + c - WIN_CHUNKS + col_chunk) >= 0
            outs = _attend_pair(qcols, k2[r0:r0 + span], v2[r0:r0 + span], sink, valid)
            for sl, o in zip(slabs, outs):
                o_ref[0, r0:r0 + CHUNK, sl:sl + LANES] = o.astype(o_ref.dtype)


def _attention_prompt(q, k, v, sinks, tq):
    b, s, _ = q.shape
    halo = WIN_CHUNKS * CHUNK
    ratio = tq // halo
    main = lambda bi, i: (bi, i, 0)
    prev = lambda bi, i: (bi, jnp.maximum(i * ratio - 1, 0), 0)
    return pl.pallas_call(
        _attn_prompt_kernel,
        grid=(b, s // tq),
        in_specs=[pl.BlockSpec(memory_space=pltpu.SMEM),
                  pl.BlockSpec((1, tq, ATTN_DIM), main),
                  pl.BlockSpec((1, tq, KV_DIM), main),
                  pl.BlockSpec((1, halo, KV_DIM), prev),
                  pl.BlockSpec((1, tq, KV_DIM), main),
                  pl.BlockSpec((1, halo, KV_DIM), prev)],
        out_specs=pl.BlockSpec((1, tq, ATTN_DIM), main),
        out_shape=jax.ShapeDtypeStruct((b, s, ATTN_DIM), q.dtype),
        compiler_params=_params(("arbitrary", "arbitrary")),
        name="attention_prompt",
    )(sinks, q, k, k, v, v)


def _attn_sample_kernel(sinks_ref, q_ref, k_ref, v_ref, ck_ref, cv_ref, o_ref, kw_ref, vw_ref):
    s = q_ref.shape[1]
    kf = jnp.concatenate([ck_ref[0], k_ref[0]], axis=0)
    vf = jnp.concatenate([cv_ref[0], v_ref[0]], axis=0)
    n = kf.shape[0]
    kw_ref[0] = kf[n - WIN_CACHE:]
    vw_ref[0] = vf[n - WIN_CACHE:]
    kb, vb = kf.astype(q_ref.dtype), vf.astype(q_ref.dtype)
    for j in range(KV_PAIRS):
        slabs = [(j * GQA + a) * LANES for a in range(GQA)]
        qcols = [q_ref[0, :, sl:sl + LANES] for sl in slabs]
        outs = _attend_pair(qcols, kb[:, j * LANES:(j + 1) * LANES], vb[:, j * LANES:(j + 1) * LANES],
                            _sink_col(sinks_ref, j, s), None)
        for sl, o in zip(slabs, outs):
            o_ref[0, :, sl:sl + LANES] = o.astype(o_ref.dtype)


def _attention_sample(q, k, v, sinks, cache_k, cache_v):
    b, s, _ = q.shape
    blk = lambda bi: (bi, 0, 0)
    return pl.pallas_call(
        _attn_sample_kernel,
        grid=(b,),
        in_specs=[pl.BlockSpec(memory_space=pltpu.SMEM),
                  pl.BlockSpec((1, s, ATTN_DIM), blk),
                  pl.BlockSpec((1, s, KV_DIM), blk),
                  pl.BlockSpec((1, s, KV_DIM), blk),
                  pl.BlockSpec((1, WIN_CACHE, KV_DIM), blk),
                  pl.BlockSpec((1, WIN_CACHE, KV_DIM), blk)],
        out_specs=[pl.BlockSpec((1, s, ATTN_DIM), blk),
                   pl.BlockSpec((1, WIN_CACHE, KV_DIM), blk),
                   pl.BlockSpec((1, WIN_CACHE, KV_DIM), blk)],
        out_shape=[jax.ShapeDtypeStruct((b, s, ATTN_DIM), q.dtype),
                   jax.ShapeDtypeStruct((b, WIN_CACHE, KV_DIM), F32),
                   jax.ShapeDtypeStruct((b, WIN_CACHE, KV_DIM), F32)],
        compiler_params=_params(("arbitrary",)),
        name="attention_sample",
    )(sinks, q, k, v, cache_k, cache_v)


def _ssd_kernel(xbc_ref, z_ref, dt_ref, hist_ref, h0_ref, cw_ref, cb_ref, dtb_ref, alog_ref, dsk_ref, gs_ref,
                ep_ref, el_ref, dmask_ref, causal_ref, tril_ref,
                y_ref, conv_ref, hfin_ref, prev, ht):
    c = pl.program_id(1)
    last = pl.num_programs(1) - 1
    ln = xbc_ref.shape[1]
    gw = SSM_HPG * SSM_HEAD_DIM

    @pl.when(c == 0)
    def _():
        prev[...] = hist_ref[0]
        ht[...] = h0_ref[0]

    xr = xbc_ref[0]
    ext = jnp.concatenate([prev[...], xr], axis=0)
    conv = cb_ref[...]
    for i in range(CONV_W):
        sh = CONV_W - 1 - i
        tap = xr if sh == 0 else pltpu.roll(ext, sh, 0)[SUBLANES:]
        conv = conv + tap * cw_ref[i:i + 1, :]
    prev[...] = xr[ln - SUBLANES:]

    @pl.when(c == last)
    def _():
        conv_ref[0] = xr[ln - (CONV_W - 1):]

    act = _silu(conv)
    xs = act[:, :SSM_DIM]
    bm = act[:, SSM_DIM:SSM_DIM + SSM_GROUPS * SSM_STATE]
    cm = act[:, SSM_DIM + SSM_GROUPS * SSM_STATE:]

    dtv = dt_ref[0] + dtb_ref[...]
    dt = jnp.maximum(dtv, 0.0) + jnp.log1p(jnp.exp(-jnp.abs(dtv)))
    ad = dt * (-jnp.exp(alog_ref[...]))
    hi, mid, lo = _split3(ad)
    tril = tril_ref[...]
    d = functools.partial(jnp.dot, preferred_element_type=F32)
    a_cs = d(tril, hi) + d(tril, mid) + d(tril, lo)
    a_last = a_cs[ln - 1:ln, :]
    stacked = jnp.concatenate([dt, jnp.exp(a_cs), jnp.exp(a_last - a_cs)], axis=0)
    wide = _dot_exact_rhs(stacked, ep_ref[...])
    dt_x, ea_x, ds_x = wide[:ln], wide[ln:2 * ln], wide[2 * ln:]
    cd_x = ea_x[ln - 1:ln, :]

    a_l = _dot_exact_rhs(a_cs, el_ref[...])
    a_s = jnp.sum(a_l * dmask_ref[...], axis=0, keepdims=True)
    lmat = jnp.exp(jnp.where(causal_ref[...] > 0.0, a_l - a_s, -1e30))

    cdt = y_ref.dtype
    xd = xs * dt_x
    xds = (xd * ds_x).astype(cdt)
    xdb = xd.astype(cdt)
    ys = []
    for g in range(SSM_GROUPS):
        bg = bm[:, g * SSM_STATE:(g + 1) * SSM_STATE].astype(cdt)
        cg = cm[:, g * SSM_STATE:(g + 1) * SSM_STATE].astype(cdt)
        cbm = _dot(cg, bg, _NT)
        yd = []
        for r in range(SSM_HPG):
            hd = g * SSM_HPG + r
            w = (cbm * lmat[:, hd * ln:(hd + 1) * ln]).astype(cdt)
            yd.append(_dot(w, xdb[:, hd * SSM_HEAD_DIM:(hd + 1) * SSM_HEAD_DIM]))
        htg = ht[g]
        y_off = _dot(cg, htg.astype(cdt)) * ea_x[:, g * gw:(g + 1) * gw]
        st = _dot(bg, xds[:, g * gw:(g + 1) * gw], _TN)
        ht[g] = htg * cd_x[:, g * gw:(g + 1) * gw] + st
        ys.append(jnp.concatenate(yd, axis=1) + y_off)
    y = jnp.concatenate(ys, axis=1) + dsk_ref[...] * xs
    y = y * _silu(z_ref[0])
    outs = [_rms(y[:, g * gw:(g + 1) * gw], gs_ref[:, g * gw:(g + 1) * gw]) for g in range(SSM_GROUPS)]
    y_ref[0] = jnp.concatenate(outs, axis=1).astype(y_ref.dtype)

    @pl.when(c == last)
    def _():
        hfin_ref[0] = ht[...]


def _ssd_constants(ln):
    heads = SSM_HEADS
    ep = np.zeros((LANES, heads * SSM_HEAD_DIM), np.float32)
    el = np.zeros((LANES, heads * ln), np.float32)
    for r in range(heads):
        ep[r, r * SSM_HEAD_DIM:(r + 1) * SSM_HEAD_DIM] = 1.0
        el[r, r * ln:(r + 1) * ln] = 1.0
    eye = np.tile(np.eye(ln, dtype=np.float32), (1, heads))
    causal = np.tile(np.tril(np.ones((ln, ln), np.float32)), (1, heads))
    tril = np.tril(np.ones((ln, ln), np.float32))
    return (jnp.asarray(ep, BF16), jnp.asarray(el, BF16), jnp.asarray(eye), jnp.asarray(causal),
            jnp.asarray(tril, BF16))


def _ssd_mixer(xbc, z, dt_raw, hist8, h0t, conv_w8, conv_b, dt_bias, a_log, dskip_x, g_ssm, ln, out_dtype):
    b, s, _ = xbc.shape
    consts = _ssd_constants(ln)
    row = lambda bi, c: (bi, c, 0)
    per_b3 = lambda bi, c: (bi, 0, 0)
    per_b4 = lambda bi, c: (bi, 0, 0, 0)
    const2 = lambda bi, c: (0, 0)
    full = lambda a: pl.BlockSpec(a.shape, const2)
    gw = SSM_HPG * SSM_HEAD_DIM
    return pl.pallas_call(
        _ssd_kernel,
        grid=(b, s // ln),
        in_specs=[pl.BlockSpec((1, ln, CONV_DIM), row),
                  pl.BlockSpec((1, ln, SSM_DIM), row),
                  pl.BlockSpec((1, ln, LANES), row),
                  pl.BlockSpec((1, SUBLANES, CONV_DIM), per_b3),
                  pl.BlockSpec((1, SSM_GROUPS, SSM_STATE, gw), per_b4),
                  full(conv_w8), full(conv_b), full(dt_bias), full(a_log), full(dskip_x), full(g_ssm)]
                 + [full(a) for a in consts],
        out_specs=[pl.BlockSpec((1, ln, SSM_DIM), row),
                   pl.BlockSpec((1, CONV_W - 1, CONV_DIM), per_b3),
                   pl.BlockSpec((1, SSM_GROUPS, SSM_STATE, gw), per_b4)],
        out_shape=[jax.ShapeDtypeStruct((b, s, SSM_DIM), out_dtype),
                   jax.ShapeDtypeStruct((b, CONV_W - 1, CONV_DIM), F32),
                   jax.ShapeDtypeStruct((b, SSM_GROUPS, SSM_STATE, gw), F32)],
        scratch_shapes=[pltpu.VMEM((SUBLANES, CONV_DIM), F32),
                        pltpu.VMEM((SSM_GROUPS, SSM_STATE, gw), F32)],
        compiler_params=_params(("arbitrary", "arbitrary")),
        name="ssd_mixer",
    )(xbc, z, dt_raw, hist8, h0t, conv_w8, conv_b, dt_bias, a_log, dskip_x, g_ssm, *consts)


def _to_row_tiles(dst_ref, val):
    for s in range(val.shape[1] // LANES):
        dst_ref[:, s, :] = val[:, s * LANES:(s + 1) * LANES]


def _outproj_kernel(a_ref, s_ref, x_ref, w_ref, gpost_ref, gate_ref, gpre_ref, sc_ref, sh_ref, wr_ref, br_ref,
                    x1_ref, h2_ref, lg_ref):
    mix = _dot(a_ref[0], w_ref[:ATTN_DIM, :]) + _dot(s_ref[0], w_ref[ATTN_DIM:, :])
    x1 = x_ref[0] + gate_ref[0] * _rms(mix, gpost_ref[...])
    x1_ref[0] = x1
    h2 = _rms(x1, gpre_ref[...]) * sc_ref[0] + sh_ref[0]
    _to_row_tiles(h2_ref, h2)
    lg_ref[...] = _dot(h2.astype(wr_ref.dtype), wr_ref[...]) + br_ref[...]


def _out_projection(attn, ssm, x, w_out, g_post, gate, g_pre, sc1p, sh, w_router, b_router, tm):
    b, s, d = x.shape
    nt = s // tm
    row = lambda bi, i: (bi, i, 0)
    const2 = lambda bi, i: (0, 0)
    flat = lambda bi, i: (bi * nt + i, 0)
    return pl.pallas_call(
        _outproj_kernel,
        grid=(b, nt),
        in_specs=[pl.BlockSpec((1, tm, ATTN_DIM), row),
                  pl.BlockSpec((1, tm, SSM_DIM), row),
                  pl.BlockSpec((1, tm, d), row),
                  pl.BlockSpec((ATTN_DIM + SSM_DIM, d), const2, pipeline_mode=pl.Buffered(1)),
                  pl.BlockSpec((1, d), const2),
                  _mod_spec(gate, tm),
                  pl.BlockSpec((1, d), const2),
                  _mod_spec(sc1p, tm),
                  _mod_spec(sh, tm),
                  pl.BlockSpec((d, LANES), const2),
                  pl.BlockSpec((1, LANES), const2)],
        out_specs=[pl.BlockSpec((1, tm, d), row),
                   pl.BlockSpec((tm, d // LANES, LANES), lambda bi, i: (bi * nt + i, 0, 0)),
                   pl.BlockSpec((tm, LANES), flat)],
        out_shape=[jax.ShapeDtypeStruct((b, s, d), F32),
                   jax.ShapeDtypeStruct((b * s, d // LANES, LANES), F32),
                   jax.ShapeDtypeStruct((b * s, LANES), F32)],
        compiler_params=_params(("arbitrary", "arbitrary")),
        name="out_projection",
    )(attn, ssm, x, w_out, g_post, gate, g_pre, sc1p, sh, w_router, b_router)


LANE_IDX, LANE_RANK, LANE_GATE = 0, TOP_K, 2 * TOP_K


def _router_kernel(lp_ref, ls_ref, ltri_ref, meta_ref, cnt_ref, *, p_tiles):
    i = pl.program_id(0)
    tt = lp_ref.shape[0]

    @pl.when(i == 0)
    def _():
        cnt_ref[...] = jnp.zeros_like(cnt_ref)

    logits = jnp.where(i < p_tiles, lp_ref[...], ls_ref[...])
    lane = lax.broadcasted_iota(jnp.int32, (tt, LANES), 1)
    lane_f = lane.astype(F32)
    work = jnp.where(lane < N_EXPERTS, logits, -jnp.inf)
    vals, hots, idxs = [], [], []
    for _ in range(TOP_K):
        m = jnp.max(work, axis=-1, keepdims=True)
        idx = jnp.min(jnp.where(work == m, lane_f, float(LANES)), axis=-1, keepdims=True)
        hot = lane_f == idx
        vals.append(m)
        hots.append(hot)
        idxs.append(idx)
        work = jnp.where(hot, -jnp.inf, work)
    es = [jnp.exp(v - vals[0]) for v in vals]
    den = es[0] + es[1] + es[2] + es[3]
    onehot = jnp.zeros((tt, LANES), F32)
    for hot in hots:
        onehot = jnp.where(hot, 1.0, onehot)
    before = jnp.dot(ltri_ref[...], onehot.astype(BF16), preferred_element_type=F32) + cnt_ref[0:1, :]
    meta = jnp.zeros((tt, LANES), F32)
    for k in range(TOP_K):
        rank_k = jnp.sum(jnp.where(hots[k], before, 0.0), axis=-1, keepdims=True)
        meta = jnp.where(lane == LANE_IDX + k, idxs[k], meta)
        meta = jnp.where(lane == LANE_RANK + k, rank_k, meta)
        meta = jnp.where(lane == LANE_GATE + k, es[k] / den, meta)
    meta_ref[...] = meta
    cnt_ref[...] = cnt_ref[...] + jnp.sum(onehot, axis=0, keepdims=True)


def _token_specs(p_tiles, tt, trailing):
    zeros = (0,) * len(trailing)
    return [pl.BlockSpec((tt,) + trailing, lambda i, *_: (jnp.minimum(i, p_tiles - 1),) + zeros),
            pl.BlockSpec((tt,) + trailing, lambda i, *_: (jnp.maximum(i - p_tiles, 0),) + zeros)]


def _router(logits_p, logits_s, tt):
    n = logits_p.shape[0] + logits_s.shape[0]
    p_tiles = logits_p.shape[0] // tt
    ltri = jnp.asarray(np.tril(np.ones((tt, tt), np.float32), -1), BF16)
    return pl.pallas_call(
        functools.partial(_router_kernel, p_tiles=p_tiles),
        grid=(n // tt,),
        in_specs=_token_specs(p_tiles, tt, (LANES,)) + [pl.BlockSpec((tt, tt), lambda i: (0, 0))],
        out_specs=[pl.BlockSpec((tt, LANES), lambda i: (i, 0)),
                   pl.BlockSpec((SUBLANES, LANES), lambda i: (0, 0))],
        out_shape=[jax.ShapeDtypeStruct((n, LANES), F32),
                   jax.ShapeDtypeStruct((SUBLANES, LANES), F32)],
        compiler_params=_params(("arbitrary",)),
        name="router",
    )(logits_p, logits_s, ltri)


def _dispatch_kernel(zstart_ref, nu_ref, pos_ref, hp_ref, hs_ref, xs_ref, zeros, sem_z, sem_r, *, p_tiles):
    i = pl.program_id(0)
    tt = hp_ref.shape[0]
    tm = zeros.shape[0]
    n_blocks = xs_ref.shape[0] // tm

    def zero_copy(row):
        return pltpu.make_async_copy(zeros, xs_ref.at[pl.ds(pl.multiple_of(row, tm), tm)], sem_z)

    @pl.when(i == 0)
    def _():
        zeros[...] = jnp.zeros_like(zeros)

        def per_expert(act):
            def body(e, carry):
                @pl.when(zstart_ref[e] >= 0)
                def _():
                    act(zero_copy(zstart_ref[e]))
                return carry
            lax.fori_loop(0, N_EXPERTS, body, 0)

        def per_tail(act):
            def body(blk, carry):
                act(zero_copy(blk * tm))
                return carry
            lax.fori_loop(nu_ref[0], n_blocks, body, 0)

        per_expert(lambda cp: cp.start())
        per_tail(lambda cp: cp.start())
        per_expert(lambda cp: cp.wait())
        per_tail(lambda cp: cp.wait())

    def scatter_rows(h_ref):
        def row_copy(t, k):
            return pltpu.make_async_copy(h_ref.at[pl.ds(t, 1)],
                                         xs_ref.at[pl.ds(pos_ref[0, 0, t * TOP_K + k], 1)], sem_r)

        def start_rows(t, carry):
            for k in range(TOP_K):
                row_copy(t, k).start()
            return carry

        def wait_rows(t, carry):
            for k in range(TOP_K):
                row_copy(t, k).wait()
            return carry

        lax.fori_loop(0, tt, start_rows, 0)
        lax.fori_loop(0, tt, wait_rows, 0)

    @pl.when(i < p_tiles)
    def _():
        scatter_rows(hp_ref)

    @pl.when(i >= p_tiles)
    def _():
        scatter_rows(hs_ref)


def _dispatch(h_p, h_s, pos3, zstart, n_used, n_slots, tt):
    row_shape = h_p.shape[1:]
    n = h_p.shape[0] + h_s.shape[0]
    p_tiles = h_p.shape[0] // tt
    return pl.pallas_call(
        functools.partial(_dispatch_kernel, p_tiles=p_tiles),
        grid_spec=pltpu.PrefetchScalarGridSpec(
            num_scalar_prefetch=2,
            grid=(n // tt,),
            in_specs=[pl.BlockSpec((1, 1, tt * TOP_K), lambda i, *_: (i, 0, 0), memory_space=pltpu.SMEM)]
                     + _token_specs(p_tiles, tt, row_shape),
            out_specs=pl.BlockSpec(memory_space=pl.ANY),
            scratch_shapes=[pltpu.VMEM((MOE_TM,) + row_shape, F32),
                            pltpu.SemaphoreType.DMA(()),
                            pltpu.SemaphoreType.DMA(())]),
        out_shape=jax.ShapeDtypeStruct((n_slots,) + row_shape, F32),
        compiler_params=_params(("arbitrary",)),
        name="dispatch",
    )(zstart, n_used, pos3, h_p, h_s)


W_CHUNK_ROWS = 256


def _stage_expert_weights(i, be_ref, nxt_ref, slot_ref, w_hbm, wbuf, stg, sem, done):
    rows = W_CHUNK_ROWS
    chunks = wbuf.shape[1] // rows
    e, e_next, slot = be_ref[i], nxt_ref[i], slot_ref[i]
    first = jnp.logical_or(i == 0, be_ref[jnp.maximum(i - 1, 0)] != e)

    def chunk_copy(expert, c, buf):
        return pltpu.make_async_copy(w_hbm.at[expert, pl.ds(pl.multiple_of(c * rows, rows), rows)],
                                     stg.at[buf], sem.at[buf])

    def convert_next_chunk(expert, into):
        c = done[0]
        buf = c % 2
        chunk_copy(expert, c, buf).wait()

        @pl.when(c + 1 < chunks)
        def _():
            chunk_copy(expert, c + 1, 1 - buf).start()

        wbuf[into, pl.ds(pl.multiple_of(c * rows, rows), rows), :] = stg[buf].astype(wbuf.dtype)
        done[0] = c + 1

    @pl.when(i == 0)
    def _():
        done[0] = 0
        chunk_copy(e, 0, 0).start()

    @pl.when(first)
    def _():
        def body(_, carry):
            convert_next_chunk(e, slot)
            return carry

        lax.fori_loop(done[0], chunks, body, 0)

        @pl.when(e_next >= 0)
        def _():
            done[0] = 0
            chunk_copy(e_next, 0, 0).start()

    @pl.when(jnp.logical_and(jnp.logical_not(first), jnp.logical_and(e_next >= 0, done[0] < chunks)))
    def _():
        convert_next_chunk(e_next, 1 - slot)

    return slot


def _up_kernel(be_ref, nxt_ref, slot_ref, nu_ref, x_ref, w_hbm, b_ref, act_ref, wbuf, stg, sem, done, xb):
    i = pl.program_id(0)
    fc = 512

    @pl.when(i < nu_ref[0])
    def _():
        slot = _stage_expert_weights(i, be_ref, nxt_ref, slot_ref, w_hbm, wbuf, stg, sem, done)
        for s in range(x_ref.shape[1]):
            xb[:, s * LANES:(s + 1) * LANES] = x_ref[:, s, :].astype(xb.dtype)
        x = xb[...]
        for c in range(D_FF // fc):
            glu = _dot(x, wbuf[slot, :, c * fc:(c + 1) * fc]) + b_ref[:, c * fc:(c + 1) * fc]
            lin = (_dot(x, wbuf[slot, :, D_FF + c * fc:D_FF + (c + 1) * fc])
                   + b_ref[:, D_FF + c * fc:D_FF + (c + 1) * fc])
            glu = jnp.minimum(glu, SWIGLU_LIMIT)
            lin = jnp.clip(lin, -SWIGLU_LIMIT, SWIGLU_LIMIT)
            act_ref[:, c * fc:(c + 1) * fc] = (glu * _sigmoid(SWIGLU_ALPHA * glu) * (lin + 1.0)).astype(act_ref.dtype)

    @pl.when(i >= nu_ref[0])
    def _():
        act_ref[...] = jnp.zeros_like(act_ref)


def _down_kernel(be_ref, nxt_ref, slot_ref, nu_ref, a_ref, w_hbm, b_ref, y_ref, wbuf, stg, sem, done):
    i = pl.program_id(0)

    @pl.when(i < nu_ref[0])
    def _():
        slot = _stage_expert_weights(i, be_ref, nxt_ref, slot_ref, w_hbm, wbuf, stg, sem, done)
        _to_row_tiles(y_ref, _dot(a_ref[...], wbuf[slot]) + b_ref[...])

    @pl.when(i >= nu_ref[0])
    def _():
        y_ref[...] = jnp.zeros_like(y_ref)


def _grouped(kernel, name, x, w, b, plan, out_tail, out_dtype, extra_scratch=()):
    block_expert, next_expert, slot, n_used = plan
    n_slots = x.shape[0]
    _, kdim, ndim = w.shape
    nb = n_slots // MOE_TM
    zeros = lambda t: (0,) * len(t)
    x_tail, o_tail = x.shape[1:], tuple(out_tail)
    return pl.pallas_call(
        kernel,
        grid_spec=pltpu.PrefetchScalarGridSpec(
            num_scalar_prefetch=4,
            grid=(nb,),
            in_specs=[pl.BlockSpec((MOE_TM,) + x_tail, lambda i, be, nx, sl, nu: (jnp.minimum(i, nu[0] - 1),) + zeros(x_tail)),
                      pl.BlockSpec(memory_space=pl.ANY),
                      pl.BlockSpec((None, 1, ndim), lambda i, be, nx, sl, nu: (be[i], 0, 0))],
            out_specs=pl.BlockSpec((MOE_TM,) + o_tail, lambda i, be, nx, sl, nu: (i,) + zeros(o_tail)),
            scratch_shapes=[pltpu.VMEM((2, kdim, ndim), BF16),
                            pltpu.VMEM((2, W_CHUNK_ROWS, ndim), F32),
                            pltpu.SemaphoreType.DMA((2,)),
                            pltpu.SMEM((1,), jnp.int32)] + list(extra_scratch)),
        out_shape=jax.ShapeDtypeStruct((n_slots,) + o_tail, out_dtype),
        compiler_params=_params(("arbitrary",), 60 * 1024 * 1024),
        name=name,
    )(block_expert, next_expert, slot, n_used, x, w, b)


def _combine_kernel(pos_ref, y_ref, meta_ref, x1_ref, gate_ref, g_ref, o_ref, rows, sem):
    tt = x1_ref.shape[0]

    def row_copy(t, k):
        return pltpu.make_async_copy(y_ref.at[pl.ds(pos_ref[0, 0, t * TOP_K + k], 1)],
                                     rows.at[k, pl.ds(t, 1)], sem)

    def start_rows(t, carry):
        for k in range(TOP_K):
            row_copy(t, k).start()
        return carry

    def wait_rows(t, carry):
        for k in range(TOP_K):
            row_copy(t, k).wait()
        return carry

    lax.fori_loop(0, tt, start_rows, 0)
    lax.fori_loop(0, tt, wait_rows, 0)
    meta = meta_ref[...]
    gates = [jnp.broadcast_to(meta[:, LANE_GATE + k:LANE_GATE + k + 1], (tt, LANES)) for k in range(TOP_K)]
    pieces = []
    for s in range(rows.shape[2]):
        piece = rows[0, :, s, :] * gates[0]
        for k in range(1, TOP_K):
            piece = piece + rows[k, :, s, :] * gates[k]
        pieces.append(piece)
    f = jnp.concatenate(pieces, axis=1)
    o_ref[...] = x1_ref[...] + gate_ref[0] * _rms(f, g_ref[...])


def _combine(y, pos3, meta, x1, gate, g_post, tt):
    n, d = x1.shape
    nb = gate.shape[0]
    tiles_per_b = n // tt // nb
    return pl.pallas_call(
        _combine_kernel,
        grid=(n // tt,),
        in_specs=[pl.BlockSpec((1, 1, tt * TOP_K), lambda i: (i, 0, 0), memory_space=pltpu.SMEM),
                  pl.BlockSpec(memory_space=pl.ANY),
                  pl.BlockSpec((tt, LANES), lambda i: (i, 0)),
                  pl.BlockSpec((tt, d), lambda i: (i, 0)),
                  pl.BlockSpec((1, 1, d), lambda i: (i // tiles_per_b, 0, 0)),
                  pl.BlockSpec((1, d), lambda i: (0, 0))],
        out_specs=pl.BlockSpec((tt, d), lambda i: (i, 0)),
        out_shape=jax.ShapeDtypeStruct((n, d), F32),
        scratch_shapes=[pltpu.VMEM((TOP_K, tt) + y.shape[1:], F32), pltpu.SemaphoreType.DMA(())],
        compiler_params=_params(("arbitrary",)),
        name="combine",
    )(pos3, y, meta, x1, gate, g_post)


def _mixer(x, mod, pos, hist8, h0t, ln, tm, tq, wts, cache):
    b, s, d = x.shape
    mods = [m.reshape(b, 1, d) for m in jnp.split(mod, 6, axis=-1)]
    gt_f = mods[5]
    if cache is None:
        xf, fb, fs = x, b, s
    else:
        fb, fs = 1, b * s
        xf = x.reshape(fb, fs, d)
        pos = jnp.tile(pos, b)
        mods = [jnp.broadcast_to(m, (b, s, d)).reshape(fb, fs, d) for m in mods]
    sh_m, sc_m, gt_m, sh_f, sc_f, _ = mods
    proj = _in_projection(xf, 1.0 + sc_m, sh_m, wts["g_mix_pre"], wts["w_main"], wts["w_dt"], _rope_tables(pos), tm)
    q, k, v, z, xbc, dt_raw = [t.reshape(b, s, t.shape[-1]) for t in proj]
    if cache is None:
        attn = _attention_prompt(q, k, v, wts["sinks"], tq)
        k_win, v_win = k[:, s - WIN_CACHE:], v[:, s - WIN_CACHE:]
    else:
        attn, k_win, v_win = _attention_sample(q, k, v, wts["sinks"], *cache)
    ssm, conv_state, hfin = _ssd_mixer(xbc, z, dt_raw, hist8, h0t, wts["conv_w8"], wts["conv_b"], wts["dt_bias"],
                                       wts["a_log"], wts["dskip_x"], wts["g_ssm"], ln, q.dtype)
    x1, h2, logits = _out_projection(attn.reshape(fb, fs, -1), ssm.reshape(fb, fs, -1), xf, wts["w_out"],
                                     wts["g_mix_post"], gt_m, wts["g_ffn_pre"], 1.0 + sc_f, sh_f,
                                     wts["w_router"], wts["b_router"], tm)
    ssm_state = hfin.reshape(b, SSM_GROUPS, SSM_STATE, SSM_HPG, SSM_HEAD_DIM)
    ssm_state = ssm_state.transpose(0, 1, 3, 4, 2).reshape(b, SSM_HEADS, SSM_HEAD_DIM, SSM_STATE)
    states = (k_win.reshape(b, WIN_CACHE, N_KV, HEAD_DIM), v_win.reshape(b, WIN_CACHE, N_KV, HEAD_DIM),
              conv_state, ssm_state)
    return x1.reshape(b * s, d), h2, logits, gt_f, states


def _pad_lanes(a, width=LANES):
    return jnp.pad(a, [(0, 0)] * (a.ndim - 1) + [(0, width - a.shape[-1])])


def _largest_tile(n, cap):
    t = cap
    while n % t:
        t //= 2
    return t


def kernel(x_prompt, x_sample, c_prompt, c_sample, cache_k, cache_v, state_conv, state_ssm, w_mod, b_mod, g_mix_pre, g_mix_post, g_ffn_pre, g_ffn_post, w_in, conv_w, conv_b, dt_bias, a_log, d_skip, g_ssm, sinks, w_out, w_router, b_router, w_up, b_up, w_down, b_down):
    depth = w_mod.shape[0]
    assert depth == 1, "single-layer step"
    l = 0
    bp, sp, d = x_prompt.shape
    bs, ss, _ = x_sample.shape
    n_p, n_s = bp * sp, bs * ss
    n_tok = n_p + n_s

    c_all = jnp.concatenate([c_prompt, c_sample], axis=0)
    c_rows = -(-c_all.shape[0] // SUBLANES) * SUBLANES
    mod_all = _modulation(jnp.pad(c_all, ((0, c_rows - c_all.shape[0]), (0, 0))), w_mod[l], b_mod[l])

    row2 = lambda a: a.reshape(1, -1)
    paired = _paired_head_columns()
    wts = {
        "g_mix_pre": row2(g_mix_pre[l]), "g_mix_post": row2(g_mix_post[l]), "g_ffn_pre": row2(g_ffn_pre[l]),
        "w_main": jnp.concatenate([w_in[l][:, :ATTN_DIM][:, paired], w_in[l][:, ATTN_DIM:MAIN_DIM]],
                                  axis=1).astype(BF16),
        "w_dt": _pad_lanes(w_in[l][:, MAIN_DIM:]).astype(BF16),
        "sinks": sinks[l],
        "conv_w8": jnp.pad(conv_w[l], ((0, SUBLANES - CONV_W), (0, 0))), "conv_b": row2(conv_b[l]),
        "dt_bias": _pad_lanes(row2(dt_bias[l])), "a_log": _pad_lanes(row2(a_log[l])),
        "dskip_x": row2(jnp.repeat(d_skip[l], SSM_HEAD_DIM)), "g_ssm": row2(g_ssm[l]),
        "w_out": jnp.concatenate([w_out[l][:ATTN_DIM][paired], w_out[l][ATTN_DIM:]], axis=0).astype(BF16),
        "w_router": _pad_lanes(w_router[l]).astype(BF16), "b_router": _pad_lanes(row2(b_router[l])),
    }
    gw = SSM_HPG * SSM_HEAD_DIM

    hist_p = jnp.zeros((bp, SUBLANES, CONV_DIM), F32)
    h0_p = jnp.zeros((bp, SSM_GROUPS, SSM_STATE, gw), F32)
    x1_p, h2_p, lg_p, gtf_p, st_p = _mixer(x_prompt, mod_all[:bp], jnp.arange(sp, dtype=jnp.int32), hist_p, h0_p,
                                           SSD_CHUNK, _largest_tile(sp, 256), _largest_tile(sp, 256), wts, None)

    hist_s = jnp.pad(state_conv[l], ((0, 0), (SUBLANES - (CONV_W - 1), 0), (0, 0)))
    h0_s = state_ssm[l].astype(F32).reshape(bs, SSM_GROUPS, SSM_HPG, SSM_HEAD_DIM, SSM_STATE)
    h0_s = h0_s.transpose(0, 1, 4, 2, 3).reshape(bs, SSM_GROUPS, SSM_STATE, gw)
    cache = (cache_k[l].reshape(bs, WIN_CACHE, KV_DIM), cache_v[l].reshape(bs, WIN_CACHE, KV_DIM))
    x1_s, h2_s, lg_s, gtf_s, st_s = _mixer(x_sample, mod_all[bp:bp + bs], PAST_LEN + jnp.arange(ss, dtype=jnp.int32),
                                           hist_s, h0_s, ss, n_s, ss, wts, cache)

    tt = _largest_tile(n_s, 128)
    assert n_p % tt == 0
    meta, counts = _router(lg_p, lg_s, tt)
    top_idx = meta[:, LANE_IDX:LANE_IDX + TOP_K].astype(jnp.int32)
    rank = meta[:, LANE_RANK:LANE_RANK + TOP_K].astype(jnp.int32)
    cnt = counts[0, :N_EXPERTS].astype(jnp.int32)
    padded = (cnt + MOE_TM - 1) // MOE_TM * MOE_TM
    pend = jnp.cumsum(padded)
    offs = pend - padded
    pos = offs[top_idx] + rank
    n_blocks = -(-n_tok * TOP_K // MOE_TM) + N_EXPERTS
    n_slots = n_blocks * MOE_TM
    n_used = (pend[-1] // MOE_TM).astype(jnp.int32)
    blk = jnp.arange(n_blocks, dtype=jnp.int32)
    blk_row = jnp.minimum(blk, n_used - 1) * MOE_TM
    block_expert = jnp.minimum(jnp.sum(pend[None, :] <= blk_row[:, None], axis=1), N_EXPERTS - 1).astype(jnp.int32)
    zstart = jnp.where(cnt > 0, pend - MOE_TM, -1).astype(jnp.int32)
    pos3 = pos.reshape(n_tok // tt, 1, tt * TOP_K)
    nu = n_used.reshape(1)
    xs = _dispatch(h2_p, h2_s, pos3, zstart, nu, n_slots, tt)
    e_ids = jnp.arange(N_EXPERTS, dtype=jnp.int32)
    live = jnp.where(cnt > 0, e_ids, N_EXPERTS)
    later = jnp.concatenate([lax.cummin(live, reverse=True)[1:], jnp.full((1,), N_EXPERTS, jnp.int32)])
    next_live = jnp.where(later < N_EXPERTS, later, -1).astype(jnp.int32)
    visit = (jnp.cumsum((cnt > 0).astype(jnp.int32)) - 1) % 2
    plan = (block_expert, next_live[block_expert], visit[block_expert].astype(jnp.int32), nu)
    act = _grouped(_up_kernel, "expert_up", xs, w_up[l], b_up[l].reshape(N_EXPERTS, 1, -1), plan,
                   (D_FF,), BF16, extra_scratch=[pltpu.VMEM((MOE_TM, d), BF16)])
    y = _grouped(_down_kernel, "expert_down", act, w_down[l], b_down[l].reshape(N_EXPERTS, 1, -1), plan,
                 xs.shape[1:], F32)

    g_post = row2(g_ffn_post[l])
    tt_p = _largest_tile(sp, 128)
    y_p = _combine(y, pos[:n_p].reshape(n_p // tt_p, 1, tt_p * TOP_K), meta[:n_p], x1_p,
                   gtf_p, g_post, tt_p).reshape(bp, sp, d)
    y_s = _combine(y, pos[n_p:].reshape(bs, 1, ss * TOP_K), meta[n_p:], x1_s,
                   gtf_s, g_post, ss).reshape(bs, ss, d)

    stack = lambda a: a[None]
    return (y_p, y_s, stack(st_p[0]), stack(st_p[1]), stack(st_p[2]), stack(st_p[3]),
            stack(st_s[0]), stack(st_s[1]), stack(st_s[2]), stack(st_s[3]))
```

```python
import functools

import numpy as np
import jax
import jax.numpy as jnp
from jax import lax
from jax.experimental import pallas as pl
from jax.experimental.pallas import tpu as pltpu

F32 = jnp.float32
BF16 = jnp.bfloat16

D_MODEL = 2048
CHUNK = 64
N_HEADS = 16
N_KV = 4
HEAD_DIM = 64
GQA = N_HEADS // N_KV
ATTN_DIM = N_HEADS * HEAD_DIM
KV_DIM = N_KV * HEAD_DIM
WINDOW = 128
WIN_CHUNKS = WINDOW // CHUNK
PAST_LEN = 4096
WIN_CACHE = min(WINDOW, PAST_LEN)
ROT_DIM = HEAD_DIM // 4
ROPE_THETA = 500000.0
SSM_HEADS = 16
SSM_HEAD_DIM = 64
SSM_DIM = SSM_HEADS * SSM_HEAD_DIM
SSM_GROUPS = 2
SSM_HPG = SSM_HEADS // SSM_GROUPS
SSM_STATE = 128
CONV_W = 4
CONV_DIM = SSM_DIM + 2 * SSM_GROUPS * SSM_STATE
SSD_CHUNK = 64
N_EXPERTS = 32
TOP_K = 4
D_FF = 2048
SWIGLU_ALPHA = 1.702
SWIGLU_LIMIT = 7.0
NORM_EPS = 1e-6

LANES = 128
SUBLANES = 8
MAIN_DIM = ATTN_DIM + 2 * KV_DIM + SSM_DIM + CONV_DIM
MOE_TM = 256
VMEM_LIMIT = 56 * 1024 * 1024


def _sigmoid(x):
    return 1.0 / (1.0 + jnp.exp(-x))


def _silu(x):
    return x * _sigmoid(x)


def _rms(x, g):
    return x * lax.rsqrt(jnp.mean(x * x, axis=-1, keepdims=True) + NORM_EPS) * g


def _split3(x):
    hi = x.astype(BF16)
    r1 = x - hi.astype(F32)
    mid = r1.astype(BF16)
    lo = (r1 - mid.astype(F32)).astype(BF16)
    return hi, mid, lo


def _dot_exact_rhs(x, m):
    hi, mid, lo = _split3(x)
    d = functools.partial(jnp.dot, preferred_element_type=F32)
    return d(hi, m) + d(mid, m) + d(lo, m)


_NN = (((1,), (0,)), ((), ()))
_NT = (((1,), (1,)), ((), ()))
_TN = (((0,), (0,)), ((), ()))


def _dot(a, b, dims=_NN):
    assert a.dtype == b.dtype, (a.dtype, b.dtype)
    prec = lax.Precision.HIGHEST if a.dtype == F32 else None
    return lax.dot_general(a, b, dims, preferred_element_type=F32, precision=prec)


def _params(sem, vmem=VMEM_LIMIT):
    return pltpu.CompilerParams(dimension_semantics=sem, vmem_limit_bytes=vmem)


def _mod_kernel(c_ref, w_ref, b_ref, o_ref):
    o_ref[...] = _dot(_silu(c_ref[...]).astype(BF16), w_ref[...].astype(BF16)) + b_ref[...]


def _modulation(c_all, w_mod, b_mod):
    rows, d = c_all.shape
    n = w_mod.shape[1]
    tn = 1536
    return pl.pallas_call(
        _mod_kernel,
        grid=(n // tn,),
        in_specs=[pl.BlockSpec((rows, d), lambda j: (0, 0)),
                  pl.BlockSpec((d, tn), lambda j: (0, j)),
                  pl.BlockSpec((1, tn), lambda j: (0, j))],
        out_specs=pl.BlockSpec((rows, tn), lambda j: (0, j)),
        out_shape=jax.ShapeDtypeStruct((rows, n), F32),
        compiler_params=_params(("arbitrary",)),
        name="modulation",
    )(c_all, w_mod, b_mod.reshape(1, n))


def _rope(t, cos, s1, s2):
    outs = []
    for j in range(t.shape[1] // LANES):
        tj = t[:, j * LANES:(j + 1) * LANES]
        up = pltpu.roll(tj, LANES - ROT_DIM // 2, 1)
        dn = pltpu.roll(tj, ROT_DIM // 2, 1)
        outs.append(tj * cos + up * s1 + dn * s2)
    return jnp.concatenate(outs, axis=1)


def _inproj_kernel(x_ref, sc_ref, sh_ref, g_ref, w_ref, wdt_ref, cos_ref, s1_ref, s2_ref,
                   q_ref, k_ref, v_ref, z_ref, xbc_ref, dt_ref):
    h = _rms(x_ref[0], g_ref[...]) * sc_ref[0] + sh_ref[0]
    hb = h.astype(w_ref.dtype)
    cos, s1, s2 = cos_ref[...], s1_ref[...], s2_ref[...]
    step = 512

    def mm(lo, hi):
        return jnp.concatenate([_dot(hb, w_ref[:, c:min(c + step, hi)]) for c in range(lo, hi, step)], axis=1)

    o = 0
    q_ref[0] = (_rope(mm(o, o + ATTN_DIM), cos, s1, s2) * (HEAD_DIM ** -0.5)).astype(q_ref.dtype)
    o += ATTN_DIM
    k_ref[0] = _rope(mm(o, o + KV_DIM), cos, s1, s2)
    o += KV_DIM
    v_ref[0] = mm(o, o + KV_DIM)
    o += KV_DIM
    z_ref[0] = mm(o, o + SSM_DIM)
    o += SSM_DIM
    xbc_ref[0] = mm(o, o + CONV_DIM)
    dt_ref[0] = _dot(hb, wdt_ref[...])


def _mod_spec(m, tm):
    if m.shape[1] == 1:
        return pl.BlockSpec((1, 1, m.shape[2]), lambda bi, i: (bi, 0, 0))
    return pl.BlockSpec((1, tm, m.shape[2]), lambda bi, i: (bi, i, 0))


def _in_projection(x, sc1p, sh, g, w_main, w_dt, rope_tabs, tm):
    b, s, d = x.shape
    cos, s1, s2 = rope_tabs
    row = lambda bi, i: (bi, i, 0)
    const2 = lambda bi, i: (0, 0)
    tab = lambda bi, i: (i, 0)
    widths = (ATTN_DIM, KV_DIM, KV_DIM, SSM_DIM, CONV_DIM, LANES)
    dtypes = (w_main.dtype, F32, F32, F32, F32, F32)
    return pl.pallas_call(
        _inproj_kernel,
        grid=(b, s // tm),
        in_specs=[pl.BlockSpec((1, tm, d), row),
                  _mod_spec(sc1p, tm),
                  _mod_spec(sh, tm),
                  pl.BlockSpec((1, d), const2),
                  pl.BlockSpec((d, MAIN_DIM), const2, pipeline_mode=pl.Buffered(1)),
                  pl.BlockSpec((d, LANES), const2, pipeline_mode=pl.Buffered(1)),
                  pl.BlockSpec((tm, LANES), tab),
                  pl.BlockSpec((tm, LANES), tab),
                  pl.BlockSpec((tm, LANES), tab)],
        out_specs=[pl.BlockSpec((1, tm, w), row) for w in widths],
        out_shape=[jax.ShapeDtypeStruct((b, s, w), dt) for w, dt in zip(widths, dtypes)],
        compiler_params=_params(("arbitrary", "arbitrary")),
        name="in_projection",
    )(x, sc1p, sh, g, w_main, w_dt, cos, s1, s2)


def _rope_tables(pos):
    inv_freq = ROPE_THETA ** (-jnp.arange(0, ROT_DIM, 2, dtype=F32) / ROT_DIM)
    ang = pos.astype(F32)[:, None] * inv_freq[None, :]
    cos, sin = jnp.cos(ang), jnp.sin(ang)
    half = ROT_DIM // 2
    n = pos.shape[0]
    ones = jnp.ones((n, HEAD_DIM - ROT_DIM), F32)
    zeros_h = jnp.zeros((n, half), F32)
    zeros_r = jnp.zeros((n, HEAD_DIM - ROT_DIM), F32)
    c = jnp.concatenate([cos, cos, ones], axis=1)
    a = jnp.concatenate([-sin, zeros_h, zeros_r], axis=1)
    b = jnp.concatenate([zeros_h, sin, zeros_r], axis=1)
    rep = LANES // HEAD_DIM
    return tuple(jnp.tile(t, (1, rep)) for t in (c, a, b))


KV_PAIRS = N_KV // 2


def _paired_head_columns():
    cols = []
    for j in range(KV_PAIRS):
        for a in range(GQA):
            for g in (2 * j, 2 * j + 1):
                h = g * GQA + a
                cols.extend(range(h * HEAD_DIM, (h + 1) * HEAD_DIM))
    return np.asarray(cols, np.int32)


def _attend_pair(qcols, k2, v2, sink_col, valid):
    rows = qcols[0].shape[0]
    lo = lax.broadcasted_iota(jnp.int32, (rows, LANES), 1) < HEAD_DIM
    zero = jnp.zeros_like(qcols[0])
    lhs = jnp.concatenate([jnp.where(lo, qc, zero) for qc in qcols] + [jnp.where(lo, zero, qc) for qc in qcols], axis=0)
    s = _dot(lhs, k2, _NT)
    if valid is not None:
        s = jnp.where(valid, s, -1e30)
    m = jnp.maximum(jnp.max(s, axis=-1, keepdims=True), sink_col)
    p = jnp.exp(s - m)
    den = jnp.sum(p, axis=-1, keepdims=True) + jnp.exp(sink_col - m)
    o = _dot((p * (1.0 / den)).astype(v2.dtype), v2)
    half = GQA * rows
    return [jnp.where(lo, o[a * rows:(a + 1) * rows], o[half + a * rows:half + (a + 1) * rows]) for a in range(GQA)]


def _sink_col(sinks_ref, j, rows):
    heads = [(2 * j) * GQA + a for a in range(GQA)] + [(2 * j + 1) * GQA + a for a in range(GQA)]
    return jnp.concatenate([jnp.full((rows, 1), sinks_ref[h], F32) for h in heads], axis=0)


def _attn_prompt_kernel(sinks_ref, q_ref, km_ref, kh_ref, vm_ref, vh_ref, o_ref):
    i = pl.program_id(1)
    tq = q_ref.shape[1]
    chunks = tq // CHUNK
    kfull = jnp.concatenate([kh_ref[0], km_ref[0]], axis=0).astype(q_ref.dtype)
    vfull = jnp.concatenate([vh_ref[0], vm_ref[0]], axis=0).astype(q_ref.dtype)
    span = (WIN_CHUNKS + 1) * CHUNK
    col_chunk = lax.broadcasted_iota(jnp.int32, (2 * GQA * CHUNK, span), 1) // CHUNK
    for j in range(KV_PAIRS):
        k2 = kfull[:, j * LANES:(j + 1) * LANES]
        v2 = vfull[:, j * LANES:(j + 1) * LANES]
        sink = _sink_col(sinks_ref, j, CHUNK)
        for c in range(chunks):
            r0 = c * CHUNK
            slabs = [(j * GQA + a) * LANES for a in range(GQA)]
            qcols = [q_ref[0, r0:r0 + CHUNK, sl:sl + LANES] for sl in slabs]
            valid = None
            if c < WIN_CHUNKS:
                valid = (i * chunks + c - WIN_CHUNKS + col_chunk) >= 0
            outs = _attend_pair(qcols, k2[r0:r0 + span], v2[r0:r0 + span], sink, valid)
            for sl, o in zip(slabs, outs):
                o_ref[0, r0:r0 + CHUNK, sl:sl + LANES] = o.astype(o_ref.dtype)


def _attention_prompt(q, k, v, sinks, tq):
    b, s, _ = q.shape
    halo = WIN_CHUNKS * CHUNK
    ratio = tq // halo
    main = lambda bi, i: (bi, i, 0)
    prev = lambda bi, i: (bi, jnp.maximum(i * ratio - 1, 0), 0)
    return pl.pallas_call(
        _attn_prompt_kernel,
        grid=(b, s // tq),
        in_specs=[pl.BlockSpec(memory_space=pltpu.SMEM),
                  pl.BlockSpec((1, tq, ATTN_DIM), main),
                  pl.BlockSpec((1, tq, KV_DIM), main),
                  pl.BlockSpec((1, halo, KV_DIM), prev),
                  pl.BlockSpec((1, tq, KV_DIM), main),
                  pl.BlockSpec((1, halo, KV_DIM), prev)],
        out_specs=pl.BlockSpec((1, tq, ATTN_DIM), main),
        out_shape=jax.ShapeDtypeStruct((b, s, ATTN_DIM), q.dtype),
        compiler_params=_params(("arbitrary", "arbitrary")),
        name="attention_prompt",
    )(sinks, q, k, k, v, v)


def _attn_sample_kernel(sinks_ref, q_ref, k_ref, v_ref, ck_ref, cv_ref, o_ref, kw_ref, vw_ref):
    s = q_ref.shape[1]
    kf = jnp.concatenate([ck_ref[0], k_ref[0]], axis=0)
    vf = jnp.concatenate([cv_ref[0], v_ref[0]], axis=0)
    n = kf.shape[0]
    kw_ref[0] = kf[n - WIN_CACHE:]
    vw_ref[0] = vf[n - WIN_CACHE:]
    kb, vb = kf.astype(q_ref.dtype), vf.astype(q_ref.dtype)
    for j in range(KV_PAIRS):
        slabs = [(j * GQA + a) * LANES for a in range(GQA)]
        qcols = [q_ref[0, :, sl:sl + LANES] for sl in slabs]
        outs = _attend_pair(qcols, kb[:, j * LANES:(j + 1) * LANES], vb[:, j * LANES:(j + 1) * LANES],
                            _sink_col(sinks_ref, j, s), None)
        for sl, o in zip(slabs, outs):
            o_ref[0, :, sl:sl + LANES] = o.astype(o_ref.dtype)


def _attention_sample(q, k, v, sinks, cache_k, cache_v):
    b, s, _ = q.shape
    blk = lambda bi: (bi, 0, 0)
    return pl.pallas_call(
        _attn_sample_kernel,
        grid=(b,),
        in_specs=[pl.BlockSpec(memory_space=pltpu.SMEM),
                  pl.BlockSpec((1, s, ATTN_DIM), blk),
                  pl.BlockSpec((1, s, KV_DIM), blk),
                  pl.BlockSpec((1, s, KV_DIM), blk),
                  pl.BlockSpec((1, WIN_CACHE, KV_DIM), blk),
                  pl.BlockSpec((1, WIN_CACHE, KV_DIM), blk)],
        out_specs=[pl.BlockSpec((1, s, ATTN_DIM), blk),
                   pl.BlockSpec((1, WIN_CACHE, KV_DIM), blk),
                   pl.BlockSpec((1, WIN_CACHE, KV_DIM), blk)],
        out_shape=[jax.ShapeDtypeStruct((b, s, ATTN_DIM), q.dtype),
                   jax.ShapeDtypeStruct((b, WIN_CACHE, KV_DIM), F32),
                   jax.ShapeDtypeStruct((b, WIN_CACHE, KV_DIM), F32)],
        compiler_params=_params(("arbitrary",)),
        name="attention_sample",
    )(sinks, q, k, v, cache_k, cache_v)


def _ssd_kernel(xbc_ref, z_ref, dt_ref, hist_ref, h0_ref, cw_ref, cb_ref, dtb_ref, alog_ref, dsk_ref, gs_ref,
                ep_ref, el_ref, dmask_ref, causal_ref, tril_ref,
                y_ref, conv_ref, hfin_ref, prev, ht):
    c = pl.program_id(1)
    last = pl.num_programs(1) - 1
    ln = xbc_ref.shape[1]
    gw = SSM_HPG * SSM_HEAD_DIM

    @pl.when(c == 0)
    def _():
        prev[...] = hist_ref[0]
        ht[...] = h0_ref[0]

    xr = xbc_ref[0]
    ext = jnp.concatenate([prev[...], xr], axis=0)
    conv = cb_ref[...]
    for i in range(CONV_W):
        sh = CONV_W - 1 - i
        tap = xr if sh == 0 else pltpu.roll(ext, sh, 0)[SUBLANES:]
        conv = conv + tap * cw_ref[i:i + 1, :]
    prev[...] = xr[ln - SUBLANES:]

    @pl.when(c == last)
    def _():
        conv_ref[0] = xr[ln - (CONV_W - 1):]

    act = _silu(conv)
    xs = act[:, :SSM_DIM]
    bm = act[:, SSM_DIM:SSM_DIM + SSM_GROUPS * SSM_STATE]
    cm = act[:, SSM_DIM + SSM_GROUPS * SSM_STATE:]

    dtv = dt_ref[0] + dtb_ref[...]
    dt = jnp.maximum(dtv, 0.0) + jnp.log1p(jnp.exp(-jnp.abs(dtv)))
    ad = dt * (-jnp.exp(alog_ref[...]))
    hi, mid, lo = _split3(ad)
    tril = tril_ref[...]
    d = functools.partial(jnp.dot, preferred_element_type=F32)
    a_cs = d(tril, hi) + d(tril, mid) + d(tril, lo)
    a_last = a_cs[ln - 1:ln, :]
    stacked = jnp.concatenate([dt, jnp.exp(a_cs), jnp.exp(a_last - a_cs)], axis=0)
    wide = _dot_exact_rhs(stacked, ep_ref[...])
    dt_x, ea_x, ds_x = wide[:ln], wide[ln:2 * ln], wide[2 * ln:]
    cd_x = ea_x[ln - 1:ln, :]

    a_l = _dot_exact_rhs(a_cs, el_ref[...])
    a_s = jnp.sum(a_l * dmask_ref[...], axis=0, keepdims=True)
    lmat = jnp.exp(jnp.where(causal_ref[...] > 0.0, a_l - a_s, -1e30))

    cdt = y_ref.dtype
    xd = xs * dt_x
    xds = (xd * ds_x).astype(cdt)
    xdb = xd.astype(cdt)
    ys = []
    for g in range(SSM_GROUPS):
        bg = bm[:, g * SSM_STATE:(g + 1) * SSM_STATE].astype(cdt)
        cg = cm[:, g * SSM_STATE:(g + 1) * SSM_STATE].astype(cdt)
        cbm = _dot(cg, bg, _NT)
        yd = []
        for r in range(SSM_HPG):
            hd = g * SSM_HPG + r
            w = (cbm * lmat[:, hd * ln:(hd + 1) * ln]).astype(cdt)
            yd.append(_dot(w, xdb[:, hd * SSM_HEAD_DIM:(hd + 1) * SSM_HEAD_DIM]))
        htg = ht[g]
        y_off = _dot(cg, htg.astype(cdt)) * ea_x[:, g * gw:(g + 1) * gw]
        st = _dot(bg, xds[:, g * gw:(g + 1) * gw], _TN)
        ht[g] = htg * cd_x[:, g * gw:(g + 1) * gw] + st
        ys.append(jnp.concatenate(yd, axis=1) + y_off)
    y = jnp.concatenate(ys, axis=1) + dsk_ref[...] * xs
    y = y * _silu(z_ref[0])
    outs = [_rms(y[:, g * gw:(g + 1) * gw], gs_ref[:, g * gw:(g + 1) * gw]) for g in range(SSM_GROUPS)]
    y_ref[0] = jnp.concatenate(outs, axis=1).astype(y_ref.dtype)

    @pl.when(c == last)
    def _():
        hfin_ref[0] = ht[...]


def _ssd_constants(ln):
    heads = SSM_HEADS
    ep = np.zeros((LANES, heads * SSM_HEAD_DIM), np.float32)
    el = np.zeros((LANES, heads * ln), np.float32)
    for r in range(heads):
        ep[r, r * SSM_HEAD_DIM:(r + 1) * SSM_HEAD_DIM] = 1.0
        el[r, r * ln:(r + 1) * ln] = 1.0
    eye = np.tile(np.eye(ln, dtype=np.float32), (1, heads))
    causal = np.tile(np.tril(np.ones((ln, ln), np.float32)), (1, heads))
    tril = np.tril(np.ones((ln, ln), np.float32))
    return (jnp.asarray(ep, BF16), jnp.asarray(el, BF16), jnp.asarray(eye), jnp.asarray(causal),
            jnp.asarray(tril, BF16))


def _ssd_mixer(xbc, z, dt_raw, hist8, h0t, conv_w8, conv_b, dt_bias, a_log, dskip_x, g_ssm, ln, out_dtype):
    b, s, _ = xbc.shape
    consts = _ssd_constants(ln)
    row = lambda bi, c: (bi, c, 0)
    per_b3 = lambda bi, c: (bi, 0, 0)
    per_b4 = lambda bi, c: (bi, 0, 0, 0)
    const2 = lambda bi, c: (0, 0)
    full = lambda a: pl.BlockSpec(a.shape, const2)
    gw = SSM_HPG * SSM_HEAD_DIM
    return pl.pallas_call(
        _ssd_kernel,
        grid=(b, s // ln),
        in_specs=[pl.BlockSpec((1, ln, CONV_DIM), row),
                  pl.BlockSpec((1, ln, SSM_DIM), row),
                  pl.BlockSpec((1, ln, LANES), row),
                  pl.BlockSpec((1, SUBLANES, CONV_DIM), per_b3),
                  pl.BlockSpec((1, SSM_GROUPS, SSM_STATE, gw), per_b4),
                  full(conv_w8), full(conv_b), full(dt_bias), full(a_log), full(dskip_x), full(g_ssm)]
                 + [full(a) for a in consts],
        out_specs=[pl.BlockSpec((1, ln, SSM_DIM), row),
                   pl.BlockSpec((1, CONV_W - 1, CONV_DIM), per_b3),
                   pl.BlockSpec((1, SSM_GROUPS, SSM_STATE, gw), per_b4)],
        out_shape=[jax.ShapeDtypeStruct((b, s, SSM_DIM), out_dtype),
                   jax.ShapeDtypeStruct((b, CONV_W - 1, CONV_DIM), F32),
                   jax.ShapeDtypeStruct((b, SSM_GROUPS, SSM_STATE, gw), F32)],
        scratch_shapes=[pltpu.VMEM((SUBLANES, CONV_DIM), F32),
                        pltpu.VMEM((SSM_GROUPS, SSM_STATE, gw), F32)],
        compiler_params=_params(("arbitrary", "arbitrary")),
        name="ssd_mixer",
    )(xbc, z, dt_raw, hist8, h0t, conv_w8, conv_b, dt_bias, a_log, dskip_x, g_ssm, *consts)


def _outproj_kernel(a_ref, s_ref, x_ref, w_ref, gpost_ref, gate_ref, gpre_ref, sc_ref, sh_ref, wr_ref, br_ref,
                    x1_ref, h2_ref, lg_ref):
    mix = _dot(a_ref[0], w_ref[:ATTN_DIM, :]) + _dot(s_ref[0], w_ref[ATTN_DIM:, :])
    x1 = x_ref[0] + gate_ref[0] * _rms(mix, gpost_ref[...])
    x1_ref[0] = x1
    h2 = _rms(x1, gpre_ref[...]) * sc_ref[0] + sh_ref[0]
    h2_ref[...] = h2
    lg_ref[...] = _dot(h2.astype(wr_ref.dtype), wr_ref[...]) + br_ref[...]


def _out_projection(attn, ssm, x, w_out, g_post, gate, g_pre, sc1p, sh, w_router, b_router, tm):
    b, s, d = x.shape
    nt = s // tm
    row = lambda bi, i: (bi, i, 0)
    const2 = lambda bi, i: (0, 0)
    flat = lambda bi, i: (bi * nt + i, 0)
    return pl.pallas_call(
        _outproj_kernel,
        grid=(b, nt),
        in_specs=[pl.BlockSpec((1, tm, ATTN_DIM), row),
                  pl.BlockSpec((1, tm, SSM_DIM), row),
                  pl.BlockSpec((1, tm, d), row),
                  pl.BlockSpec((ATTN_DIM + SSM_DIM, d), const2, pipeline_mode=pl.Buffered(1)),
                  pl.BlockSpec((1, d), const2),
                  _mod_spec(gate, tm),
                  pl.BlockSpec((1, d), const2),
                  _mod_spec(sc1p, tm),
                  _mod_spec(sh, tm),
                  pl.BlockSpec((d, LANES), const2),
                  pl.BlockSpec((1, LANES), const2)],
        out_specs=[pl.BlockSpec((1, tm, d), row),
                   pl.BlockSpec((tm, d), flat),
                   pl.BlockSpec((tm, LANES), flat)],
        out_shape=[jax.ShapeDtypeStruct((b, s, d), F32),
                   jax.ShapeDtypeStruct((b * s, d), F32),
                   jax.ShapeDtypeStruct((b * s, LANES), F32)],
        compiler_params=_params(("arbitrary", "arbitrary")),
        name="out_projection",
    )(attn, ssm, x, w_out, g_post, gate, g_pre, sc1p, sh, w_router, b_router)


LANE_IDX, LANE_RANK, LANE_GATE = 0, TOP_K, 2 * TOP_K


def _router_kernel(lp_ref, ls_ref, ltri_ref, meta_ref, cnt_ref, *, p_tiles):
    i = pl.program_id(0)
    tt = lp_ref.shape[0]

    @pl.when(i == 0)
    def _():
        cnt_ref[...] = jnp.zeros_like(cnt_ref)

    logits = jnp.where(i < p_tiles, lp_ref[...], ls_ref[...])
    lane = lax.broadcasted_iota(jnp.int32, (tt, LANES), 1)
    lane_f = lane.astype(F32)
    work = jnp.where(lane < N_EXPERTS, logits, -jnp.inf)
    vals, hots, idxs = [], [], []
    for _ in range(TOP_K):
        m = jnp.max(work, axis=-1, keepdims=True)
        idx = jnp.min(jnp.where(work == m, lane_f, float(LANES)), axis=-1, keepdims=True)
        hot = lane_f == idx
        vals.append(m)
        hots.append(hot)
        idxs.append(idx)
        work = jnp.where(hot, -jnp.inf, work)
    es = [jnp.exp(v - vals[0]) for v in vals]
    den = es[0] + es[1] + es[2] + es[3]
    onehot = jnp.zeros((tt, LANES), F32)
    for hot in hots:
        onehot = jnp.where(hot, 1.0, onehot)
    before = jnp.dot(ltri_ref[...], onehot.astype(BF16), preferred_element_type=F32) + cnt_ref[0:1, :]
    meta = jnp.zeros((tt, LANES), F32)
    for k in range(TOP_K):
        rank_k = jnp.sum(jnp.where(hots[k], before, 0.0), axis=-1, keepdims=True)
        meta = jnp.where(lane == LANE_IDX + k, idxs[k], meta)
        meta = jnp.where(lane == LANE_RANK + k, rank_k, meta)
        meta = jnp.where(lane == LANE_GATE + k, es[k] / den, meta)
    meta_ref[...] = meta
    cnt_ref[...] = cnt_ref[...] + jnp.sum(onehot, axis=0, keepdims=True)


def _token_specs(p_tiles, tt, trailing):
    zeros = (0,) * len(trailing)
    return [pl.BlockSpec((tt,) + trailing, lambda i, *_: (jnp.minimum(i, p_tiles - 1),) + zeros),
            pl.BlockSpec((tt,) + trailing, lambda i, *_: (jnp.maximum(i - p_tiles, 0),) + zeros)]


def _router(logits_p, logits_s, tt):
    n = logits_p.shape[0] + logits_s.shape[0]
    p_tiles = logits_p.shape[0] // tt
    ltri = jnp.asarray(np.tril(np.ones((tt, tt), np.float32), -1), BF16)
    return pl.pallas_call(
        functools.partial(_router_kernel, p_tiles=p_tiles),
        grid=(n // tt,),
        in_specs=_token_specs(p_tiles, tt, (LANES,)) + [pl.BlockSpec((tt, tt), lambda i: (0, 0))],
        out_specs=[pl.BlockSpec((tt, LANES), lambda i: (i, 0)),
                   pl.BlockSpec((SUBLANES, LANES), lambda i: (0, 0))],
        out_shape=[jax.ShapeDtypeStruct((n, LANES), F32),
                   jax.ShapeDtypeStruct((SUBLANES, LANES), F32)],
        compiler_params=_params(("arbitrary",)),
        name="router",
    )(logits_p, logits_s, ltri)


def _dispatch_kernel(zstart_ref, nu_ref, pos_ref, hp_ref, hs_ref, xs_ref, zeros, sem_z, sem_r, *, p_tiles):
    i = pl.program_id(0)
    tt = hp_ref.shape[0]
    tm = zeros.shape[0]
    n_blocks = xs_ref.shape[0] // tm

    def zero_copy(row):
        return pltpu.make_async_copy(zeros, xs_ref.at[pl.ds(pl.multiple_of(row, tm), tm)], sem_z)

    @pl.when(i == 0)
    def _():
        zeros[...] = jnp.zeros_like(zeros)

        def per_expert(act):
            def body(e, carry):
                @pl.when(zstart_ref[e] >= 0)
                def _():
                    act(zero_copy(zstart_ref[e]))
                return carry
            lax.fori_loop(0, N_EXPERTS, body, 0)

        def per_tail(act):
            def body(blk, carry):
                act(zero_copy(blk * tm))
                return carry
            lax.fori_loop(nu_ref[0], n_blocks, body, 0)

        per_expert(lambda cp: cp.start())
        per_tail(lambda cp: cp.start())
        per_expert(lambda cp: cp.wait())
        per_tail(lambda cp: cp.wait())

    def scatter_rows(h_ref):
        def row_copy(t, k):
            return pltpu.make_async_copy(h_ref.at[pl.ds(t, 1)],
                                         xs_ref.at[pl.ds(pos_ref[0, 0, t * TOP_K + k], 1)], sem_r)

        def start_rows(t, carry):
            for k in range(TOP_K):
                row_copy(t, k).start()
            return carry

        def wait_rows(t, carry):
            for k in range(TOP_K):
                row_copy(t, k).wait()
            return carry

        lax.fori_loop(0, tt, start_rows, 0)
        lax.fori_loop(0, tt, wait_rows, 0)

    @pl.when(i < p_tiles)
    def _():
        scatter_rows(hp_ref)

    @pl.when(i >= p_tiles)
    def _():
        scatter_rows(hs_ref)


def _dispatch(h_p, h_s, pos3, zstart, n_used, n_slots, tt):
    row_shape = h_p.shape[1:]
    n = h_p.shape[0] + h_s.shape[0]
    p_tiles = h_p.shape[0] // tt
    return pl.pallas_call(
        functools.partial(_dispatch_kernel, p_tiles=p_tiles),
        grid_spec=pltpu.PrefetchScalarGridSpec(
            num_scalar_prefetch=2,
            grid=(n // tt,),
            in_specs=[pl.BlockSpec((1, 1, tt * TOP_K), lambda i, *_: (i, 0, 0), memory_space=pltpu.SMEM)]
                     + _token_specs(p_tiles, tt, row_shape),
            out_specs=pl.BlockSpec(memory_space=pl.ANY),
            scratch_shapes=[pltpu.VMEM((MOE_TM,) + row_shape, F32),
                            pltpu.SemaphoreType.DMA(()),
                            pltpu.SemaphoreType.DMA(())]),
        out_shape=jax.ShapeDtypeStruct((n_slots,) + row_shape, F32),
        compiler_params=_params(("arbitrary",)),
        name="dispatch",
    )(zstart, n_used, pos3, h_p, h_s)


W_CHUNK_ROWS = 256


def _stage_expert_weights(i, be_ref, nxt_ref, slot_ref, w_hbm, wbuf, stg, sem, done):
    rows = W_CHUNK_ROWS
    chunks = wbuf.shape[1] // rows
    e, e_next, slot = be_ref[i], nxt_ref[i], slot_ref[i]
    first = jnp.logical_or(i == 0, be_ref[jnp.maximum(i - 1, 0)] != e)

    def chunk_copy(expert, c, buf):
        return pltpu.make_async_copy(w_hbm.at[expert, pl.ds(pl.multiple_of(c * rows, rows), rows)],
                                     stg.at[buf], sem.at[buf])

    def convert_next_chunk(expert, into):
        c = done[0]
        buf = c % 2
        chunk_copy(expert, c, buf).wait()

        @pl.when(c + 1 < chunks)
        def _():
            chunk_copy(expert, c + 1, 1 - buf).start()

        wbuf[into, pl.ds(pl.multiple_of(c * rows, rows), rows), :] = stg[buf].astype(wbuf.dtype)
        done[0] = c + 1

    @pl.when(i == 0)
    def _():
        done[0] = 0
        chunk_copy(e, 0, 0).start()

    @pl.when(first)
    def _():
        def body(_, carry):
            convert_next_chunk(e, slot)
            return carry

        lax.fori_loop(done[0], chunks, body, 0)

        @pl.when(e_next >= 0)
        def _():
            done[0] = 0
            chunk_copy(e_next, 0, 0).start()

    @pl.when(jnp.logical_and(jnp.logical_not(first), jnp.logical_and(e_next >= 0, done[0] < chunks)))
    def _():
        convert_next_chunk(e_next, 1 - slot)

    return slot


def _up_kernel(be_ref, nxt_ref, slot_ref, nu_ref, x_ref, w_hbm, b_ref, act_ref, wbuf, stg, sem, done):
    i = pl.program_id(0)
    fc = 512

    @pl.when(i < nu_ref[0])
    def _():
        slot = _stage_expert_weights(i, be_ref, nxt_ref, slot_ref, w_hbm, wbuf, stg, sem, done)
        x = x_ref[...].astype(wbuf.dtype)
        for c in range(D_FF // fc):
            glu = _dot(x, wbuf[slot, :, c * fc:(c + 1) * fc]) + b_ref[:, c * fc:(c + 1) * fc]
            lin = (_dot(x, wbuf[slot, :, D_FF + c * fc:D_FF + (c + 1) * fc])
                   + b_ref[:, D_FF + c * fc:D_FF + (c + 1) * fc])
            glu = jnp.minimum(glu, SWIGLU_LIMIT)
            lin = jnp.clip(lin, -SWIGLU_LIMIT, SWIGLU_LIMIT)
            act_ref[:, c * fc:(c + 1) * fc] = (glu * _sigmoid(SWIGLU_ALPHA * glu) * (lin + 1.0)).astype(act_ref.dtype)

    @pl.when(i >= nu_ref[0])
    def _():
        act_ref[...] = jnp.zeros_like(act_ref)


def _down_kernel(be_ref, nxt_ref, slot_ref, nu_ref, a_ref, w_hbm, b_ref, y_ref, wbuf, stg, sem, done):
    i = pl.program_id(0)

    @pl.when(i < nu_ref[0])
    def _():
        slot = _stage_expert_weights(i, be_ref, nxt_ref, slot_ref, w_hbm, wbuf, stg, sem, done)
        y_ref[...] = _dot(a_ref[...], wbuf[slot]) + b_ref[...]

    @pl.when(i >= nu_ref[0])
    def _():
        y_ref[...] = jnp.zeros_like(y_ref)


def _grouped(kernel, name, x, w, b, plan, out_tail, out_dtype, extra_scratch=()):
    block_expert, next_expert, slot, n_used = plan
    n_slots = x.shape[0]
    _, kdim, ndim = w.shape
    nb = n_slots // MOE_TM
    zeros = lambda t: (0,) * len(t)
    x_tail, o_tail = x.shape[1:], tuple(out_tail)
    return pl.pallas_call(
        kernel,
        grid_spec=pltpu.PrefetchScalarGridSpec(
            num_scalar_prefetch=4,
            grid=(nb,),
            in_specs=[pl.BlockSpec((MOE_TM,) + x_tail, lambda i, be, nx, sl, nu: (jnp.minimum(i, nu[0] - 1),) + zeros(x_tail)),
                      pl.BlockSpec(memory_space=pl.ANY),
                      pl.BlockSpec((None, 1, ndim), lambda i, be, nx, sl, nu: (be[i], 0, 0))],
            out_specs=pl.BlockSpec((MOE_TM,) + o_tail, lambda i, be, nx, sl, nu: (i,) + zeros(o_tail)),
            scratch_shapes=[pltpu.VMEM((2, kdim, ndim), BF16),
                            pltpu.VMEM((2, W_CHUNK_ROWS, ndim), F32),
                            pltpu.SemaphoreType.DMA((2,)),
                            pltpu.SMEM((1,), jnp.int32)] + list(extra_scratch)),
        out_shape=jax.ShapeDtypeStruct((n_slots,) + o_tail, out_dtype),
        compiler_params=_params(("arbitrary",), 60 * 1024 * 1024),
        name=name,
    )(block_expert, next_expert, slot, n_used, x, w, b)


def _combine_kernel(pos_ref, y_ref, meta_ref, x1_ref, gate_ref, g_ref, o_ref, rows, sem):
    tt = x1_ref.shape[0]

    def row_copy(t, k):
        return pltpu.make_async_copy(y_ref.at[pl.ds(pos_ref[0, 0, t * TOP_K + k], 1)],
                                     rows.at[k, pl.ds(t, 1)], sem)

    def start_rows(t, carry):
        for k in range(TOP_K):
            row_copy(t, k).start()
        return carry

    def wait_rows(t, carry):
        for k in range(TOP_K):
            row_copy(t, k).wait()
        return carry

    lax.fori_loop(0, tt, start_rows, 0)
    lax.fori_loop(0, tt, wait_rows, 0)
    meta = meta_ref[...]
    f = rows[0] * meta[:, LANE_GATE:LANE_GATE + 1]
    for k in range(1, TOP_K):
        f = f + rows[k] * meta[:, LANE_GATE + k:LANE_GATE + k + 1]
    o_ref[...] = x1_ref[...] + gate_ref[0] * _rms(f, g_ref[...])


def _combine(y, pos3, meta, x1, gate, g_post, tt):
    n, d = x1.shape
    nb = gate.shape[0]
    tiles_per_b = n // tt // nb
    return pl.pallas_call(
        _combine_kernel,
        grid=(n // tt,),
        in_specs=[pl.BlockSpec((1, 1, tt * TOP_K), lambda i: (i, 0, 0), memory_space=pltpu.SMEM),
                  pl.BlockSpec(memory_space=pl.ANY),
                  pl.BlockSpec((tt, LANES), lambda i: (i, 0)),
                  pl.BlockSpec((tt, d), lambda i: (i, 0)),
                  pl.BlockSpec((1, 1, d), lambda i: (i // tiles_per_b, 0, 0)),
                  pl.BlockSpec((1, d), lambda i: (0, 0))],
        out_specs=pl.BlockSpec((tt, d), lambda i: (i, 0)),
        out_shape=jax.ShapeDtypeStruct((n, d), F32),
        scratch_shapes=[pltpu.VMEM((TOP_K, tt) + y.shape[1:], F32), pltpu.SemaphoreType.DMA(())],
        compiler_params=_params(("arbitrary",)),
        name="combine",
    )(pos3, y, meta, x1, gate, g_post)


def _mixer(x, mod, pos, hist8, h0t, ln, tm, tq, wts, cache):
    b, s, d = x.shape
    mods = [m.reshape(b, 1, d) for m in jnp.split(mod, 6, axis=-1)]
    gt_f = mods[5]
    if cache is None:
        xf, fb, fs = x, b, s
    else:
        fb, fs = 1, b * s
        xf = x.reshape(fb, fs, d)
        pos = jnp.tile(pos, b)
        mods = [jnp.broadcast_to(m, (b, s, d)).reshape(fb, fs, d) for m in mods]
    sh_m, sc_m, gt_m, sh_f, sc_f, _ = mods
    proj = _in_projection(xf, 1.0 + sc_m, sh_m, wts["g_mix_pre"], wts["w_main"], wts["w_dt"], _rope_tables(pos), tm)
    q, k, v, z, xbc, dt_raw = [t.reshape(b, s, t.shape[-1]) for t in proj]
    if cache is None:
        attn = _attention_prompt(q, k, v, wts["sinks"], tq)
        k_win, v_win = k[:, s - WIN_CACHE:], v[:, s - WIN_CACHE:]
    else:
        attn, k_win, v_win = _attention_sample(q, k, v, wts["sinks"], *cache)
    ssm, conv_state, hfin = _ssd_mixer(xbc, z, dt_raw, hist8, h0t, wts["conv_w8"], wts["conv_b"], wts["dt_bias"],
                                       wts["a_log"], wts["dskip_x"], wts["g_ssm"], ln, q.dtype)
    x1, h2, logits = _out_projection(attn.reshape(fb, fs, -1), ssm.reshape(fb, fs, -1), xf, wts["w_out"],
                                     wts["g_mix_post"], gt_m, wts["g_ffn_pre"], 1.0 + sc_f, sh_f,
                                     wts["w_router"], wts["b_router"], tm)
    ssm_state = hfin.reshape(b, SSM_GROUPS, SSM_STATE, SSM_HPG, SSM_HEAD_DIM)
    ssm_state = ssm_state.transpose(0, 1, 3, 4, 2).reshape(b, SSM_HEADS, SSM_HEAD_DIM, SSM_STATE)
    states = (k_win.reshape(b, WIN_CACHE, N_KV, HEAD_DIM), v_win.reshape(b, WIN_CACHE, N_KV, HEAD_DIM),
              conv_state, ssm_state)
    return x1.reshape(b * s, d), h2, logits, gt_f, states


def _pad_lanes(a, width=LANES):
    return jnp.pad(a, [(0, 0)] * (a.ndim - 1) + [(0, width - a.shape[-1])])


def _largest_tile(n, cap):
    t = cap
    while n % t:
        t //= 2
    return t


def kernel(x_prompt, x_sample, c_prompt, c_sample, cache_k, cache_v, state_conv, state_ssm, w_mod, b_mod, g_mix_pre, g_mix_post, g_ffn_pre, g_ffn_post, w_in, conv_w, conv_b, dt_bias, a_log, d_skip, g_ssm, sinks, w_out, w_router, b_router, w_up, b_up, w_down, b_down):
    depth = w_mod.shape[0]
    assert depth == 1, "single-layer step"
    l = 0
    bp, sp, d = x_prompt.shape
    bs, ss, _ = x_sample.shape
    n_p, n_s = bp * sp, bs * ss
    n_tok = n_p + n_s

    c_all = jnp.concatenate([c_prompt, c_sample], axis=0)
    c_rows = -(-c_all.shape[0] // SUBLANES) * SUBLANES
    mod_all = _modulation(jnp.pad(c_all, ((0, c_rows - c_all.shape[0]), (0, 0))), w_mod[l], b_mod[l])

    row2 = lambda a: a.reshape(1, -1)
    paired = _paired_head_columns()
    wts = {
        "g_mix_pre": row2(g_mix_pre[l]), "g_mix_post": row2(g_mix_post[l]), "g_ffn_pre": row2(g_ffn_pre[l]),
        "w_main": jnp.concatenate([w_in[l][:, :ATTN_DIM][:, paired], w_in[l][:, ATTN_DIM:MAIN_DIM]],
                                  axis=1).astype(BF16),
        "w_dt": _pad_lanes(w_in[l][:, MAIN_DIM:]).astype(BF16),
        "sinks": sinks[l],
        "conv_w8": jnp.pad(conv_w[l], ((0, SUBLANES - CONV_W), (0, 0))), "conv_b": row2(conv_b[l]),
        "dt_bias": _pad_lanes(row2(dt_bias[l])), "a_log": _pad_lanes(row2(a_log[l])),
        "dskip_x": row2(jnp.repeat(d_skip[l], SSM_HEAD_DIM)), "g_ssm": row2(g_ssm[l]),
        "w_out": jnp.concatenate([w_out[l][:ATTN_DIM][paired], w_out[l][ATTN_DIM:]], axis=0).astype(BF16),
        "w_router": _pad_lanes(w_router[l]).astype(BF16), "b_router": _pad_lanes(row2(b_router[l])),
    }
    gw = SSM_HPG * SSM_HEAD_DIM

    hist_p = jnp.zeros((bp, SUBLANES, CONV_DIM), F32)
    h0_p = jnp.zeros((bp, SSM_GROUPS, SSM_STATE, gw), F32)
    x1_p, h2_p, lg_p, gtf_p, st_p = _mixer(x_prompt, mod_all[:bp], jnp.arange(sp, dtype=jnp.int32), hist_p, h0_p,
                                           SSD_CHUNK, _largest_tile(sp, 256), _largest_tile(sp, 256), wts, None)

    hist_s = jnp.pad(state_conv[l], ((0, 0), (SUBLANES - (CONV_W - 1), 0), (0, 0)))
    h0_s = state_ssm[l].astype(F32).reshape(bs, SSM_GROUPS, SSM_HPG, SSM_HEAD_DIM, SSM_STATE)
    h0_s = h0_s.transpose(0, 1, 4, 2, 3).reshape(bs, SSM_GROUPS, SSM_STATE, gw)
    cache = (cache_k[l].reshape(bs, WIN_CACHE, KV_DIM), cache_v[l].reshape(bs, WIN_CACHE, KV_DIM))
    x1_s, h2_s, lg_s, gtf_s, st_s = _mixer(x_sample, mod_all[bp:bp + bs], PAST_LEN + jnp.arange(ss, dtype=jnp.int32),
                                           hist_s, h0_s, ss, n_s, ss, wts, cache)

    tt = _largest_tile(n_s, 128)
    assert n_p % tt == 0
    meta, counts = _router(lg_p, lg_s, tt)
    top_idx = meta[:, LANE_IDX:LANE_IDX + TOP_K].astype(jnp.int32)
    rank = meta[:, LANE_RANK:LANE_RANK + TOP_K].astype(jnp.int32)
    cnt = counts[0, :N_EXPERTS].astype(jnp.int32)
    padded = (cnt + MOE_TM - 1) // MOE_TM * MOE_TM
    pend = jnp.cumsum(padded)
    offs = pend - padded
    pos = offs[top_idx] + rank
    n_blocks = -(-n_tok * TOP_K // MOE_TM) + N_EXPERTS
    n_slots = n_blocks * MOE_TM
    n_used = (pend[-1] // MOE_TM).astype(jnp.int32)
    blk = jnp.arange(n_blocks, dtype=jnp.int32)
    blk_row = jnp.minimum(blk, n_used - 1) * MOE_TM
    block_expert = jnp.minimum(jnp.sum(pend[None, :] <= blk_row[:, None], axis=1), N_EXPERTS - 1).astype(jnp.int32)
    zstart = jnp.where(cnt > 0, pend - MOE_TM, -1).astype(jnp.int32)
    pos3 = pos.reshape(n_tok // tt, 1, tt * TOP_K)
    nu = n_used.reshape(1)
    xs = _dispatch(h2_p, h2_s, pos3, zstart, nu, n_slots, tt)
    e_ids = jnp.arange(N_EXPERTS, dtype=jnp.int32)
    live = jnp.where(cnt > 0, e_ids, N_EXPERTS)
    later = jnp.concatenate([lax.cummin(live, reverse=True)[1:], jnp.full((1,), N_EXPERTS, jnp.int32)])
    next_live = jnp.where(later < N_EXPERTS, later, -1).astype(jnp.int32)
    visit = (jnp.cumsum((cnt > 0).astype(jnp.int32)) - 1) % 2
    plan = (block_expert, next_live[block_expert], visit[block_expert].astype(jnp.int32), nu)
    act = _grouped(_up_kernel, "expert_up", xs, w_up[l], b_up[l].reshape(N_EXPERTS, 1, -1), plan, (D_FF,), BF16)
    y = _grouped(_down_kernel, "expert_down", act, w_down[l], b_down[l].reshape(N_EXPERTS, 1, -1), plan,
                 xs.shape[1:], F32)

    g_post = row2(g_ffn_post[l])
    tt_p = _largest_tile(sp, 128)
    y_p = _combine(y, pos[:n_p].reshape(n_p // tt_p, 1, tt_p * TOP_K), meta[:n_p], x1_p,
                   gtf_p, g_post, tt_p).reshape(bp, sp, d)
    y_s = _combine(y, pos[n_p:].reshape(bs, 1, ss * TOP_K), meta[n_p:], x1_s,
                   gtf_s, g_post, ss).reshape(bs, ss, d)

    stack = lambda a: a[None]
    return (y_p, y_s, stack(st_p[0]), stack(st_p[1]), stack(st_p[2]), stack(st_p[3]),
            stack(st_s[0]), stack(st_s[1]), stack(st_s[2]), stack(st_s[3]))
```

```python
import functools

import numpy as np
import jax
import jax.numpy as jnp
from jax import lax
from jax.experimental import pallas as pl
from jax.experimental.pallas import tpu as pltpu

F32 = jnp.float32
BF16 = jnp.bfloat16

D_MODEL = 2048
CHUNK = 64
N_HEADS = 16
N_KV = 4
HEAD_DIM = 64
GQA = N_HEADS // N_KV
ATTN_DIM = N_HEADS * HEAD_DIM
KV_DIM = N_KV * HEAD_DIM
WINDOW = 128
WIN_CHUNKS = WINDOW // CHUNK
PAST_LEN = 4096
WIN_CACHE = min(WINDOW, PAST_LEN)
ROT_DIM = HEAD_DIM // 4
ROPE_THETA = 500000.0
SSM_HEADS = 16
SSM_HEAD_DIM = 64
SSM_DIM = SSM_HEADS * SSM_HEAD_DIM
SSM_GROUPS = 2
SSM_HPG = SSM_HEADS // SSM_GROUPS
SSM_STATE = 128
CONV_W = 4
CONV_DIM = SSM_DIM + 2 * SSM_GROUPS * SSM_STATE
SSD_CHUNK = 64
N_EXPERTS = 32
TOP_K = 4
D_FF = 2048
SWIGLU_ALPHA = 1.702
SWIGLU_LIMIT = 7.0
NORM_EPS = 1e-6

LANES = 128
SUBLANES = 8
MAIN_DIM = ATTN_DIM + 2 * KV_DIM + SSM_DIM + CONV_DIM
MOE_TM = 256
VMEM_LIMIT = 56 * 1024 * 1024


def _sigmoid(x):
    return 1.0 / (1.0 + jnp.exp(-x))


def _silu(x):
    return x * _sigmoid(x)


def _rms(x, g):
    return x * lax.rsqrt(jnp.mean(x * x, axis=-1, keepdims=True) + NORM_EPS) * g


def _split3(x):
    hi = x.astype(BF16)
    r1 = x - hi.astype(F32)
    mid = r1.astype(BF16)
    lo = (r1 - mid.astype(F32)).astype(BF16)
    return hi, mid, lo


def _dot_exact_rhs(x, m):
    hi, mid, lo = _split3(x)
    d = functools.partial(jnp.dot, preferred_element_type=F32)
    return d(hi, m) + d(mid, m) + d(lo, m)


_NN = (((1,), (0,)), ((), ()))
_NT = (((1,), (1,)), ((), ()))
_TN = (((0,), (0,)), ((), ()))


def _dot(a, b, dims=_NN):
    assert a.dtype == b.dtype, (a.dtype, b.dtype)
    prec = lax.Precision.HIGHEST if a.dtype == F32 else None
    return lax.dot_general(a, b, dims, preferred_element_type=F32, precision=prec)


def _params(sem, vmem=VMEM_LIMIT):
    return pltpu.CompilerParams(dimension_semantics=sem, vmem_limit_bytes=vmem)


def _mod_kernel(c_ref, w_ref, b_ref, o_ref):
    o_ref[...] = _dot(_silu(c_ref[...]).astype(BF16), w_ref[...].astype(BF16)) + b_ref[...]


def _modulation(c_all, w_mod, b_mod):
    rows, d = c_all.shape
    n = w_mod.shape[1]
    tn = 1536
    return pl.pallas_call(
        _mod_kernel,
        grid=(n // tn,),
        in_specs=[pl.BlockSpec((rows, d), lambda j: (0, 0)),
                  pl.BlockSpec((d, tn), lambda j: (0, j)),
                  pl.BlockSpec((1, tn), lambda j: (0, j))],
        out_specs=pl.BlockSpec((rows, tn), lambda j: (0, j)),
        out_shape=jax.ShapeDtypeStruct((rows, n), F32),
        compiler_params=_params(("arbitrary",)),
        name="modulation",
    )(c_all, w_mod, b_mod.reshape(1, n))


def _rope(t, cos, s1, s2):
    outs = []
    for j in range(t.shape[1] // LANES):
        tj = t[:, j * LANES:(j + 1) * LANES]
        up = pltpu.roll(tj, LANES - ROT_DIM // 2, 1)
        dn = pltpu.roll(tj, ROT_DIM // 2, 1)
        outs.append(tj * cos + up * s1 + dn * s2)
    return jnp.concatenate(outs, axis=1)


def _inproj_kernel(x_ref, sc_ref, sh_ref, g_ref, w_ref, wdt_ref, cos_ref, s1_ref, s2_ref,
                   q_ref, k_ref, v_ref, z_ref, xbc_ref, dt_ref):
    h = _rms(x_ref[0], g_ref[...]) * sc_ref[0] + sh_ref[0]
    hb = h.astype(w_ref.dtype)
    cos, s1, s2 = cos_ref[...], s1_ref[...], s2_ref[...]
    step = 512

    def mm(lo, hi):
        return jnp.concatenate([_dot(hb, w_ref[:, c:min(c + step, hi)]) for c in range(lo, hi, step)], axis=1)

    o = 0
    q_ref[0] = (_rope(mm(o, o + ATTN_DIM), cos, s1, s2) * (HEAD_DIM ** -0.5)).astype(q_ref.dtype)
    o += ATTN_DIM
    k_ref[0] = _rope(mm(o, o + KV_DIM), cos, s1, s2)
    o += KV_DIM
    v_ref[0] = mm(o, o + KV_DIM)
    o += KV_DIM
    z_ref[0] = mm(o, o + SSM_DIM)
    o += SSM_DIM
    xbc_ref[0] = mm(o, o + CONV_DIM)
    dt_ref[0] = _dot(hb, wdt_ref[...])


def _mod_spec(m, tm):
    if m.shape[1] == 1:
        return pl.BlockSpec((1, 1, m.shape[2]), lambda bi, i: (bi, 0, 0))
    return pl.BlockSpec((1, tm, m.shape[2]), lambda bi, i: (bi, i, 0))


def _in_projection(x, sc1p, sh, g, w_main, w_dt, rope_tabs, tm):
    b, s, d = x.shape
    cos, s1, s2 = rope_tabs
    row = lambda bi, i: (bi, i, 0)
    const2 = lambda bi, i: (0, 0)
    tab = lambda bi, i: (i, 0)
    widths = (ATTN_DIM, KV_DIM, KV_DIM, SSM_DIM, CONV_DIM, LANES)
    dtypes = (w_main.dtype, F32, F32, F32, F32, F32)
    return pl.pallas_call(
        _inproj_kernel,
        grid=(b, s // tm),
        in_specs=[pl.BlockSpec((1, tm, d), row),
                  _mod_spec(sc1p, tm),
                  _mod_spec(sh, tm),
                  pl.BlockSpec((1, d), const2),
                  pl.BlockSpec((d, MAIN_DIM), const2, pipeline_mode=pl.Buffered(1)),
                  pl.BlockSpec((d, LANES), const2, pipeline_mode=pl.Buffered(1)),
                  pl.BlockSpec((tm, LANES), tab),
                  pl.BlockSpec((tm, LANES), tab),
                  pl.BlockSpec((tm, LANES), tab)],
        out_specs=[pl.BlockSpec((1, tm, w), row) for w in widths],
        out_shape=[jax.ShapeDtypeStruct((b, s, w), dt) for w, dt in zip(widths, dtypes)],
        compiler_params=_params(("arbitrary", "arbitrary")),
        name="in_projection",
    )(x, sc1p, sh, g, w_main, w_dt, cos, s1, s2)


def _rope_tables(pos):
    inv_freq = ROPE_THETA ** (-jnp.arange(0, ROT_DIM, 2, dtype=F32) / ROT_DIM)
    ang = pos.astype(F32)[:, None] * inv_freq[None, :]
    cos, sin = jnp.cos(ang), jnp.sin(ang)
    half = ROT_DIM // 2
    n = pos.shape[0]
    ones = jnp.ones((n, HEAD_DIM - ROT_DIM), F32)
    zeros_h = jnp.zeros((n, half), F32)
    zeros_r = jnp.zeros((n, HEAD_DIM - ROT_DIM), F32)
    c = jnp.concatenate([cos, cos, ones], axis=1)
    a = jnp.concatenate([-sin, zeros_h, zeros_r], axis=1)
    b = jnp.concatenate([zeros_h, sin, zeros_r], axis=1)
    rep = LANES // HEAD_DIM
    return tuple(jnp.tile(t, (1, rep)) for t in (c, a, b))


KV_PAIRS = N_KV // 2


def _paired_head_columns():
    cols = []
    for j in range(KV_PAIRS):
        for a in range(GQA):
            for g in (2 * j, 2 * j + 1):
                h = g * GQA + a
                cols.extend(range(h * HEAD_DIM, (h + 1) * HEAD_DIM))
    return np.asarray(cols, np.int32)


def _attend_pair(qcols, k2, v2, sink_row, valid):
    rows = qcols[0].shape[0]
    lo = lax.broadcasted_iota(jnp.int32, (rows, LANES), 1) < HEAD_DIM
    zero = jnp.zeros_like(qcols[0])
    lhs = jnp.concatenate([jnp.where(lo, qc, zero) for qc in qcols] + [jnp.where(lo, zero, qc) for qc in qcols], axis=0)
    s = _dot(k2, lhs, _NT)
    if valid is not None:
        s = jnp.where(valid, s, -1e30)
    m = jnp.maximum(jnp.max(s, axis=0, keepdims=True), sink_row)
    p = jnp.exp(s - m)
    den = jnp.sum(p, axis=0, keepdims=True) + jnp.exp(sink_row - m)
    o = _dot((p * (1.0 / den)).astype(v2.dtype), v2, _TN)
    half = GQA * rows
    return [jnp.where(lo, o[a * rows:(a + 1) * rows], o[half + a * rows:half + (a + 1) * rows]) for a in range(GQA)]


def _sink_row(sinks_ref, j, rows):
    heads = [(2 * j) * GQA + a for a in range(GQA)] + [(2 * j + 1) * GQA + a for a in range(GQA)]
    col = lax.broadcasted_iota(jnp.int32, (1, len(heads) * rows), 1) // rows
    out = jnp.zeros((1, len(heads) * rows), F32)
    for n, h in enumerate(heads):
        out = jnp.where(col == n, sinks_ref[h], out)
    return out


def _attn_prompt_kernel(sinks_ref, q_ref, km_ref, kh_ref, vm_ref, vh_ref, o_ref):
    i = pl.program_id(1)
    tq = q_ref.shape[1]
    chunks = tq // CHUNK
    kfull = jnp.concatenate([kh_ref[0], km_ref[0]], axis=0).astype(q_ref.dtype)
    vfull = jnp.concatenate([vh_ref[0], vm_ref[0]], axis=0).astype(q_ref.dtype)
    span = (WIN_CHUNKS + 1) * CHUNK
    col_chunk = lax.broadcasted_iota(jnp.int32, (span, 2 * GQA * CHUNK), 0) // CHUNK
    for j in range(KV_PAIRS):
        k2 = kfull[:, j * LANES:(j + 1) * LANES]
        v2 = vfull[:, j * LANES:(j + 1) * LANES]
        sink = _sink_row(sinks_ref, j, CHUNK)
        for c in range(chunks):
            r0 = c * CHUNK
            slabs = [(j * GQA + a) * LANES for a in range(GQA)]
            qcols = [q_ref[0, r0:r0 + CHUNK, sl:sl + LANES] for sl in slabs]
            valid = None
            if c < WIN_CHUNKS:
                valid = (i * chunks + c - WIN_CHUNKS + col_chunk) >= 0
            outs = _attend_pair(qcols, k2[r0:r0 + span], v2[r0:r0 + span], sink, valid)
            for sl, o in zip(slabs, outs):
                o_ref[0, r0:r0 + CHUNK, sl:sl + LANES] = o.astype(o_ref.dtype)


def _attention_prompt(q, k, v, sinks, tq):
    b, s, _ = q.shape
    halo = WIN_CHUNKS * CHUNK
    ratio = tq // halo
    main = lambda bi, i: (bi, i, 0)
    prev = lambda bi, i: (bi, jnp.maximum(i * ratio - 1, 0), 0)
    return pl.pallas_call(
        _attn_prompt_kernel,
        grid=(b, s // tq),
        in_specs=[pl.BlockSpec(memory_space=pltpu.SMEM),
                  pl.BlockSpec((1, tq, ATTN_DIM), main),
                  pl.BlockSpec((1, tq, KV_DIM), main),
                  pl.BlockSpec((1, halo, KV_DIM), prev),
                  pl.BlockSpec((1, tq, KV_DIM), main),
                  pl.BlockSpec((1, halo, KV_DIM), prev)],
        out_specs=pl.BlockSpec((1, tq, ATTN_DIM), main),
        out_shape=jax.ShapeDtypeStruct((b, s, ATTN_DIM), q.dtype),
        compiler_params=_params(("arbitrary", "arbitrary")),
        name="attention_prompt",
    )(sinks, q, k, k, v, v)


def _attn_sample_kernel(sinks_ref, q_ref, k_ref, v_ref, ck_ref, cv_ref, o_ref, kw_ref, vw_ref):
    s = q_ref.shape[1]
    kf = jnp.concatenate([ck_ref[0], k_ref[0]], axis=0)
    vf = jnp.concatenate([cv_ref[0], v_ref[0]], axis=0)
    n = kf.shape[0]
    kw_ref[0] = kf[n - WIN_CACHE:]
    vw_ref[0] = vf[n - WIN_CACHE:]
    kb, vb = kf.astype(q_ref.dtype), vf.astype(q_ref.dtype)
    for j in range(KV_PAIRS):
        slabs = [(j * GQA + a) * LANES for a in range(GQA)]
        qcols = [q_ref[0, :, sl:sl + LANES] for sl in slabs]
        outs = _attend_pair(qcols, kb[:, j * LANES:(j + 1) * LANES], vb[:, j * LANES:(j + 1) * LANES],
                            _sink_row(sinks_ref, j, s), None)
        for sl, o in zip(slabs, outs):
            o_ref[0, :, sl:sl + LANES] = o.astype(o_ref.dtype)


def _attention_sample(q, k, v, sinks, cache_k, cache_v):
    b, s, _ = q.shape
    blk = lambda bi: (bi, 0, 0)
    return pl.pallas_call(
        _attn_sample_kernel,
        grid=(b,),
        in_specs=[pl.BlockSpec(memory_space=pltpu.SMEM),
                  pl.BlockSpec((1, s, ATTN_DIM), blk),
                  pl.BlockSpec((1, s, KV_DIM), blk),
                  pl.BlockSpec((1, s, KV_DIM), blk),
                  pl.BlockSpec((1, WIN_CACHE, KV_DIM), blk),
                  pl.BlockSpec((1, WIN_CACHE, KV_DIM), blk)],
        out_specs=[pl.BlockSpec((1, s, ATTN_DIM), blk),
                   pl.BlockSpec((1, WIN_CACHE, KV_DIM), blk),
                   pl.BlockSpec((1, WIN_CACHE, KV_DIM), blk)],
        out_shape=[jax.ShapeDtypeStruct((b, s, ATTN_DIM), q.dtype),
                   jax.ShapeDtypeStruct((b, WIN_CACHE, KV_DIM), F32),
                   jax.ShapeDtypeStruct((b, WIN_CACHE, KV_DIM), F32)],
        compiler_params=_params(("arbitrary",)),
        name="attention_sample",
    )(sinks, q, k, v, cache_k, cache_v)


def _ssd_kernel(xbc_ref, z_ref, dt_ref, hist_ref, h0_ref, cw_ref, cb_ref, dtb_ref, alog_ref, dsk_ref, gs_ref,
                ep_ref, el_ref, dmask_ref, causal_ref, tril_ref,
                y_ref, conv_ref, hfin_ref, prev, ht):
    c = pl.program_id(1)
    last = pl.num_programs(1) - 1
    ln = xbc_ref.shape[1]

    @pl.when(c == 0)
    def _():
        prev[...] = hist_ref[...]
        ht[...] = h0_ref[...]

    for bi in range(xbc_ref.shape[0]):
        _ssd_chunk(bi, xbc_ref, z_ref, dt_ref, cw_ref, cb_ref, dtb_ref, alog_ref, dsk_ref, gs_ref,
                   ep_ref, el_ref, dmask_ref, causal_ref, tril_ref, y_ref, prev, ht)

    @pl.when(c == last)
    def _():
        hfin_ref[...] = ht[...]
        conv_ref[...] = xbc_ref[:, ln - (CONV_W - 1):, :]


def _ssd_chunk(bi, xbc_ref, z_ref, dt_ref, cw_ref, cb_ref, dtb_ref, alog_ref, dsk_ref, gs_ref,
               ep_ref, el_ref, dmask_ref, causal_ref, tril_ref, y_ref, prev, ht):
    ln = xbc_ref.shape[1]
    gw = SSM_HPG * SSM_HEAD_DIM
    xr = xbc_ref[bi]
    ext = jnp.concatenate([prev[bi], xr], axis=0)
    conv = cb_ref[...]
    for i in range(CONV_W):
        sh = CONV_W - 1 - i
        tap = xr if sh == 0 else pltpu.roll(ext, sh, 0)[SUBLANES:]
        conv = conv + tap * cw_ref[i:i + 1, :]
    prev[bi] = xr[ln - SUBLANES:]

    act = _silu(conv)
    xs = act[:, :SSM_DIM]
    bm = act[:, SSM_DIM:SSM_DIM + SSM_GROUPS * SSM_STATE]
    cm = act[:, SSM_DIM + SSM_GROUPS * SSM_STATE:]

    dtv = dt_ref[bi] + dtb_ref[...]
    dt = jnp.maximum(dtv, 0.0) + jnp.log1p(jnp.exp(-jnp.abs(dtv)))
    ad = dt * (-jnp.exp(alog_ref[...]))
    hi, mid, lo = _split3(ad)
    tril = tril_ref[...]
    d = functools.partial(jnp.dot, preferred_element_type=F32)
    a_cs = d(tril, hi) + d(tril, mid) + d(tril, lo)
    a_last = a_cs[ln - 1:ln, :]
    stacked = jnp.concatenate([dt, jnp.exp(a_cs), jnp.exp(a_last - a_cs)], axis=0)
    wide = _dot_exact_rhs(stacked, ep_ref[...])
    dt_x, ea_x, ds_x = wide[:ln], wide[ln:2 * ln], wide[2 * ln:]
    cd_x = ea_x[ln - 1:ln, :]

    a_l = _dot_exact_rhs(a_cs, el_ref[...])
    a_s = jnp.sum(a_l * dmask_ref[...], axis=0, keepdims=True)
    lmat = jnp.exp(jnp.where(causal_ref[...] > 0.0, a_l - a_s, -1e30))

    cdt = y_ref.dtype
    xd = xs * dt_x
    xds = (xd * ds_x).astype(cdt)
    xdb = xd.astype(cdt)
    ys = []
    for g in range(SSM_GROUPS):
        bg = bm[:, g * SSM_STATE:(g + 1) * SSM_STATE].astype(cdt)
        cg = cm[:, g * SSM_STATE:(g + 1) * SSM_STATE].astype(cdt)
        cbm = _dot(cg, bg, _NT)
        yd = []
        for r in range(SSM_HPG):
            hd = g * SSM_HPG + r
            w = (cbm * lmat[:, hd * ln:(hd + 1) * ln]).astype(cdt)
            yd.append(_dot(w, xdb[:, hd * SSM_HEAD_DIM:(hd + 1) * SSM_HEAD_DIM]))
        htg = ht[bi, g]
        y_off = _dot(cg, htg.astype(cdt)) * ea_x[:, g * gw:(g + 1) * gw]
        st = _dot(bg, xds[:, g * gw:(g + 1) * gw], _TN)
        ht[bi, g] = htg * cd_x[:, g * gw:(g + 1) * gw] + st
        ys.append(jnp.concatenate(yd, axis=1) + y_off)
    y = jnp.concatenate(ys, axis=1) + dsk_ref[...] * xs
    y = y * _silu(z_ref[bi])
    outs = [_rms(y[:, g * gw:(g + 1) * gw], gs_ref[:, g * gw:(g + 1) * gw]) for g in range(SSM_GROUPS)]
    y_ref[bi] = jnp.concatenate(outs, axis=1).astype(y_ref.dtype)


def _ssd_constants(ln):
    heads = SSM_HEADS
    ep = np.zeros((LANES, heads * SSM_HEAD_DIM), np.float32)
    el = np.zeros((LANES, heads * ln), np.float32)
    for r in range(heads):
        ep[r, r * SSM_HEAD_DIM:(r + 1) * SSM_HEAD_DIM] = 1.0
        el[r, r * ln:(r + 1) * ln] = 1.0
    eye = np.tile(np.eye(ln, dtype=np.float32), (1, heads))
    causal = np.tile(np.tril(np.ones((ln, ln), np.float32)), (1, heads))
    tril = np.tril(np.ones((ln, ln), np.float32))
    return (jnp.asarray(ep, BF16), jnp.asarray(el, BF16), jnp.asarray(eye), jnp.asarray(causal),
            jnp.asarray(tril, BF16))


def _ssd_mixer(xbc, z, dt_raw, hist8, h0t, conv_w8, conv_b, dt_bias, a_log, dskip_x, g_ssm, ln, out_dtype):
    b, s, _ = xbc.shape
    bb = 2 if b % 2 == 0 else 1
    consts = _ssd_constants(ln)
    row = lambda bi, c: (bi, c, 0)
    per_b3 = lambda bi, c: (bi, 0, 0)
    per_b4 = lambda bi, c: (bi, 0, 0, 0)
    const2 = lambda bi, c: (0, 0)
    full = lambda a: pl.BlockSpec(a.shape, const2)
    gw = SSM_HPG * SSM_HEAD_DIM
    return pl.pallas_call(
        _ssd_kernel,
        grid=(b // bb, s // ln),
        in_specs=[pl.BlockSpec((bb, ln, CONV_DIM), row),
                  pl.BlockSpec((bb, ln, SSM_DIM), row),
                  pl.BlockSpec((bb, ln, LANES), row),
                  pl.BlockSpec((bb, SUBLANES, CONV_DIM), per_b3),
                  pl.BlockSpec((bb, SSM_GROUPS, SSM_STATE, gw), per_b4),
                  full(conv_w8), full(conv_b), full(dt_bias), full(a_log), full(dskip_x), full(g_ssm)]
                 + [full(a) for a in consts],
        out_specs=[pl.BlockSpec((bb, ln, SSM_DIM), row),
                   pl.BlockSpec((bb, CONV_W - 1, CONV_DIM), per_b3),
                   pl.BlockSpec((bb, SSM_GROUPS, SSM_STATE, gw), per_b4)],
        out_shape=[jax.ShapeDtypeStruct((b, s, SSM_DIM), out_dtype),
                   jax.ShapeDtypeStruct((b, CONV_W - 1, CONV_DIM), F32),
                   jax.ShapeDtypeStruct((b, SSM_GROUPS, SSM_STATE, gw), F32)],
        scratch_shapes=[pltpu.VMEM((bb, SUBLANES, CONV_DIM), F32),
                        pltpu.VMEM((bb, SSM_GROUPS, SSM_STATE, gw), F32)],
        compiler_params=_params(("arbitrary", "arbitrary")),
        name="ssd_mixer",
    )(xbc, z, dt_raw, hist8, h0t, conv_w8, conv_b, dt_bias, a_log, dskip_x, g_ssm, *consts)


def _outproj_kernel(a_ref, s_ref, x_ref, w_ref, gpost_ref, gate_ref, gpre_ref, sc_ref, sh_ref, wr_ref, br_ref,
                    x1_ref, h2_ref, lg_ref):
    mix = _dot(a_ref[0], w_ref[:ATTN_DIM, :]) + _dot(s_ref[0], w_ref[ATTN_DIM:, :])
    x1 = x_ref[0] + gate_ref[0] * _rms(mix, gpost_ref[...])
    x1_ref[0] = x1
    h2 = _rms(x1, gpre_ref[...]) * sc_ref[0] + sh_ref[0]
    h2_ref[...] = h2
    lg_ref[...] = _dot(h2.astype(wr_ref.dtype), wr_ref[...]) + br_ref[...]


def _out_projection(attn, ssm, x, w_out, g_post, gate, g_pre, sc1p, sh, w_router, b_router, tm):
    b, s, d = x.shape
    nt = s // tm
    row = lambda bi, i: (bi, i, 0)
    const2 = lambda bi, i: (0, 0)
    flat = lambda bi, i: (bi * nt + i, 0)
    return pl.pallas_call(
        _outproj_kernel,
        grid=(b, nt),
        in_specs=[pl.BlockSpec((1, tm, ATTN_DIM), row),
                  pl.BlockSpec((1, tm, SSM_DIM), row),
                  pl.BlockSpec((1, tm, d), row),
                  pl.BlockSpec((ATTN_DIM + SSM_DIM, d), const2, pipeline_mode=pl.Buffered(1)),
                  pl.BlockSpec((1, d), const2),
                  _mod_spec(gate, tm),
                  pl.BlockSpec((1, d), const2),
                  _mod_spec(sc1p, tm),
                  _mod_spec(sh, tm),
                  pl.BlockSpec((d, LANES), const2),
                  pl.BlockSpec((1, LANES), const2)],
        out_specs=[pl.BlockSpec((1, tm, d), row),
                   pl.BlockSpec((tm, d), flat),
                   pl.BlockSpec((tm, LANES), flat)],
        out_shape=[jax.ShapeDtypeStruct((b, s, d), F32),
                   jax.ShapeDtypeStruct((b * s, d), F32),
                   jax.ShapeDtypeStruct((b * s, LANES), F32)],
        compiler_params=_params(("arbitrary", "arbitrary")),
        name="out_projection",
    )(attn, ssm, x, w_out, g_post, gate, g_pre, sc1p, sh, w_router, b_router)


LANE_IDX, LANE_RANK, LANE_GATE = 0, TOP_K, 2 * TOP_K


def _router_kernel(lp_ref, ls_ref, ltri_ref, meta_ref, cnt_ref, *, p_tiles):
    i = pl.program_id(0)
    tt = lp_ref.shape[0]

    @pl.when(i == 0)
    def _():
        cnt_ref[...] = jnp.zeros_like(cnt_ref)

    logits = jnp.where(i < p_tiles, lp_ref[...], ls_ref[...])
    lane = lax.broadcasted_iota(jnp.int32, (tt, LANES), 1)
    lane_f = lane.astype(F32)
    work = jnp.where(lane < N_EXPERTS, logits, -jnp.inf)
    vals, hots, idxs = [], [], []
    for _ in range(TOP_K):
        m = jnp.max(work, axis=-1, keepdims=True)
        idx = jnp.min(jnp.where(work == m, lane_f, float(LANES)), axis=-1, keepdims=True)
        hot = lane_f == idx
        vals.append(m)
        hots.append(hot)
        idxs.append(idx)
        work = jnp.where(hot, -jnp.inf, work)
    es = [jnp.exp(v - vals[0]) for v in vals]
    den = es[0] + es[1] + es[2] + es[3]
    onehot = jnp.zeros((tt, LANES), F32)
    for hot in hots:
        onehot = jnp.where(hot, 1.0, onehot)
    before = jnp.dot(ltri_ref[...], onehot.astype(BF16), preferred_element_type=F32) + cnt_ref[0:1, :]
    meta = jnp.zeros((tt, LANES), F32)
    for k in range(TOP_K):
        rank_k = jnp.sum(jnp.where(hots[k], before, 0.0), axis=-1, keepdims=True)
        meta = jnp.where(lane == LANE_IDX + k, idxs[k], meta)
        meta = jnp.where(lane == LANE_RANK + k, rank_k, meta)
        meta = jnp.where(lane == LANE_GATE + k, es[k] / den, meta)
    meta_ref[...] = meta
    cnt_ref[...] = cnt_ref[...] + jnp.sum(onehot, axis=0, keepdims=True)


def _token_specs(p_tiles, tt, trailing):
    zeros = (0,) * len(trailing)
    return [pl.BlockSpec((tt,) + trailing, lambda i, *_: (jnp.minimum(i, p_tiles - 1),) + zeros),
            pl.BlockSpec((tt,) + trailing, lambda i, *_: (jnp.maximum(i - p_tiles, 0),) + zeros)]


def _router(logits_p, logits_s, tt):
    n = logits_p.shape[0] + logits_s.shape[0]
    p_tiles = logits_p.shape[0] // tt
    ltri = jnp.asarray(np.tril(np.ones((tt, tt), np.float32), -1), BF16)
    return pl.pallas_call(
        functools.partial(_router_kernel, p_tiles=p_tiles),
        grid=(n // tt,),
        in_specs=_token_specs(p_tiles, tt, (LANES,)) + [pl.BlockSpec((tt, tt), lambda i: (0, 0))],
        out_specs=[pl.BlockSpec((tt, LANES), lambda i: (i, 0)),
                   pl.BlockSpec((SUBLANES, LANES), lambda i: (0, 0))],
        out_shape=[jax.ShapeDtypeStruct((n, LANES), F32),
                   jax.ShapeDtypeStruct((SUBLANES, LANES), F32)],
        compiler_params=_params(("arbitrary",)),
        name="router",
    )(logits_p, logits_s, ltri)


def _dispatch_kernel(zstart_ref, nu_ref, pos_ref, hp_ref, hs_ref, xs_ref, zeros, sem_z, sem_r, *, p_tiles):
    i = pl.program_id(0)
    tt = hp_ref.shape[0]
    tm = zeros.shape[0]
    n_blocks = xs_ref.shape[0] // tm

    def zero_copy(row):
        return pltpu.make_async_copy(zeros, xs_ref.at[pl.ds(pl.multiple_of(row, tm), tm)], sem_z)

    @pl.when(i == 0)
    def _():
        zeros[...] = jnp.zeros_like(zeros)

        def per_expert(act):
            def body(e, carry):
                @pl.when(zstart_ref[e] >= 0)
                def _():
                    act(zero_copy(zstart_ref[e]))
                return carry
            lax.fori_loop(0, N_EXPERTS, body, 0)

        def per_tail(act):
            def body(blk, carry):
                act(zero_copy(blk * tm))
                return carry
            lax.fori_loop(nu_ref[0], n_blocks, body, 0)

        per_expert(lambda cp: cp.start())
        per_tail(lambda cp: cp.start())
        per_expert(lambda cp: cp.wait())
        per_tail(lambda cp: cp.wait())

    def scatter_rows(h_ref):
        def row_copy(t, k):
            return pltpu.make_async_copy(h_ref.at[pl.ds(t, 1)],
                                         xs_ref.at[pl.ds(pos_ref[0, 0, t * TOP_K + k], 1)], sem_r)

        def start_rows(t, carry):
            for k in range(TOP_K):
                row_copy(t, k).start()
            return carry

        def wait_rows(t, carry):
            for k in range(TOP_K):
                row_copy(t, k).wait()
            return carry

        lax.fori_loop(0, tt, start_rows, 0)
        lax.fori_loop(0, tt, wait_rows, 0)

    @pl.when(i < p_tiles)
    def _():
        scatter_rows(hp_ref)

    @pl.when(i >= p_tiles)
    def _():
        scatter_rows(hs_ref)


def _dispatch(h_p, h_s, pos3, zstart, n_used, n_slots, tt):
    row_shape = h_p.shape[1:]
    n = h_p.shape[0] + h_s.shape[0]
    p_tiles = h_p.shape[0] // tt
    return pl.pallas_call(
        functools.partial(_dispatch_kernel, p_tiles=p_tiles),
        grid_spec=pltpu.PrefetchScalarGridSpec(
            num_scalar_prefetch=2,
            grid=(n // tt,),
            in_specs=[pl.BlockSpec((1, 1, tt * TOP_K), lambda i, *_: (i, 0, 0), memory_space=pltpu.SMEM)]
                     + _token_specs(p_tiles, tt, row_shape),
            out_specs=pl.BlockSpec(memory_space=pl.ANY),
            scratch_shapes=[pltpu.VMEM((MOE_TM,) + row_shape, F32),
                            pltpu.SemaphoreType.DMA(()),
                            pltpu.SemaphoreType.DMA(())]),
        out_shape=jax.ShapeDtypeStruct((n_slots,) + row_shape, F32),
        compiler_params=_params(("arbitrary",)),
        name="dispatch",
    )(zstart, n_used, pos3, h_p, h_s)


W_CHUNK_ROWS = 256


def _stage_expert_weights(i, be_ref, nxt_ref, slot_ref, w_hbm, wbuf, stg, sem, done):
    rows = W_CHUNK_ROWS
    chunks = wbuf.shape[1] // rows
    e, e_next, slot = be_ref[i], nxt_ref[i], slot_ref[i]
    first = jnp.logical_or(i == 0, be_ref[jnp.maximum(i - 1, 0)] != e)

    def chunk_copy(expert, c, buf):
        return pltpu.make_async_copy(w_hbm.at[expert, pl.ds(pl.multiple_of(c * rows, rows), rows)],
                                     stg.at[buf], sem.at[buf])

    def convert_next_chunk(expert, into):
        c = done[0]
        buf = c % 2
        chunk_copy(expert, c, buf).wait()

        @pl.when(c + 1 < chunks)
        def _():
            chunk_copy(expert, c + 1, 1 - buf).start()

        wbuf[into, pl.ds(pl.multiple_of(c * rows, rows), rows), :] = stg[buf].astype(wbuf.dtype)
        done[0] = c + 1

    @pl.when(i == 0)
    def _():
        done[0] = 0
        chunk_copy(e, 0, 0).start()

    @pl.when(first)
    def _():
        def body(_, carry):
            convert_next_chunk(e, slot)
            return carry

        lax.fori_loop(done[0], chunks, body, 0)

        @pl.when(e_next >= 0)
        def _():
            done[0] = 0
            chunk_copy(e_next, 0, 0).start()

    @pl.when(jnp.logical_and(jnp.logical_not(first), jnp.logical_and(e_next >= 0, done[0] < chunks)))
    def _():
        convert_next_chunk(e_next, 1 - slot)

    return slot


def _up_kernel(be_ref, nxt_ref, slot_ref, nu_ref, x_ref, w_hbm, b_ref, act_ref, wbuf, stg, sem, done):
    i = pl.program_id(0)
    fc = 512

    @pl.when(i < nu_ref[0])
    def _():
        slot = _stage_expert_weights(i, be_ref, nxt_ref, slot_ref, w_hbm, wbuf, stg, sem, done)
        x = x_ref[...].astype(wbuf.dtype)
        for c in range(D_FF // fc):
            glu = _dot(x, wbuf[slot, :, c * fc:(c + 1) * fc]) + b_ref[:, c * fc:(c + 1) * fc]
            lin = (_dot(x, wbuf[slot, :, D_FF + c * fc:D_FF + (c + 1) * fc])
                   + b_ref[:, D_FF + c * fc:D_FF + (c + 1) * fc])
            glu = jnp.minimum(glu, SWIGLU_LIMIT)
            lin = jnp.clip(lin, -SWIGLU_LIMIT, SWIGLU_LIMIT)
            act_ref[:, c * fc:(c + 1) * fc] = (glu * _sigmoid(SWIGLU_ALPHA * glu) * (lin + 1.0)).astype(act_ref.dtype)

    @pl.when(i >= nu_ref[0])
    def _():
        act_ref[...] = jnp.zeros_like(act_ref)


def _down_kernel(be_ref, nxt_ref, slot_ref, nu_ref, a_ref, w_hbm, b_ref, y_ref, wbuf, stg, sem, done):
    i = pl.program_id(0)

    @pl.when(i < nu_ref[0])
    def _():
        slot = _stage_expert_weights(i, be_ref, nxt_ref, slot_ref, w_hbm, wbuf, stg, sem, done)
        y_ref[...] = _dot(a_ref[...], wbuf[slot]) + b_ref[...]

    @pl.when(i >= nu_ref[0])
    def _():
        y_ref[...] = jnp.zeros_like(y_ref)


def _grouped(kernel, name, x, w, b, plan, out_tail, out_dtype, extra_scratch=()):
    block_expert, next_expert, slot, n_used = plan
    n_slots = x.shape[0]
    _, kdim, ndim = w.shape
    nb = n_slots // MOE_TM
    zeros = lambda t: (0,) * len(t)
    x_tail, o_tail = x.shape[1:], tuple(out_tail)
    return pl.pallas_call(
        kernel,
        grid_spec=pltpu.PrefetchScalarGridSpec(
            num_scalar_prefetch=4,
            grid=(nb,),
            in_specs=[pl.BlockSpec((MOE_TM,) + x_tail, lambda i, be, nx, sl, nu: (jnp.minimum(i, nu[0] - 1),) + zeros(x_tail)),
                      pl.BlockSpec(memory_space=pl.ANY),
                      pl.BlockSpec((None, 1, ndim), lambda i, be, nx, sl, nu: (be[i], 0, 0))],
            out_specs=pl.BlockSpec((MOE_TM,) + o_tail, lambda i, be, nx, sl, nu: (i,) + zeros(o_tail)),
            scratch_shapes=[pltpu.VMEM((2, kdim, ndim), BF16),
                            pltpu.VMEM((2, W_CHUNK_ROWS, ndim), F32),
                            pltpu.SemaphoreType.DMA((2,)),
                            pltpu.SMEM((1,), jnp.int32)] + list(extra_scratch)),
        out_shape=jax.ShapeDtypeStruct((n_slots,) + o_tail, out_dtype),
        compiler_params=_params(("arbitrary",), 60 * 1024 * 1024),
        name=name,
    )(block_expert, next_expert, slot, n_used, x, w, b)


def _combine_kernel(pos_ref, y_ref, meta_ref, x1_ref, gate_ref, g_ref, o_ref, rows, sem):
    tt = x1_ref.shape[0]

    def row_copy(t, k):
        return pltpu.make_async_copy(y_ref.at[pl.ds(pos_ref[0, 0, t * TOP_K + k], 1)],
                                     rows.at[k, pl.ds(t, 1)], sem)

    def start_rows(t, carry):
        for k in range(TOP_K):
            row_copy(t, k).start()
        return carry

    def wait_rows(t, carry):
        for k in range(TOP_K):
            row_copy(t, k).wait()
        return carry

    lax.fori_loop(0, tt, start_rows, 0)
    lax.fori_loop(0, tt, wait_rows, 0)
    meta = meta_ref[...]
    f = rows[0] * meta[:, LANE_GATE:LANE_GATE + 1]
    for k in range(1, TOP_K):
        f = f + rows[k] * meta[:, LANE_GATE + k:LANE_GATE + k + 1]
    o_ref[...] = x1_ref[...] + gate_ref[0] * _rms(f, g_ref[...])


def _combine(y, pos3, meta, x1, gate, g_post, tt):
    n, d = x1.shape
    nb = gate.shape[0]
    tiles_per_b = n // tt // nb
    return pl.pallas_call(
        _combine_kernel,
        grid=(n // tt,),
        in_specs=[pl.BlockSpec((1, 1, tt * TOP_K), lambda i: (i, 0, 0), memory_space=pltpu.SMEM),
                  pl.BlockSpec(memory_space=pl.ANY),
                  pl.BlockSpec((tt, LANES), lambda i: (i, 0)),
                  pl.BlockSpec((tt, d), lambda i: (i, 0)),
                  pl.BlockSpec((1, 1, d), lambda i: (i // tiles_per_b, 0, 0)),
                  pl.BlockSpec((1, d), lambda i: (0, 0))],
        out_specs=pl.BlockSpec((tt, d), lambda i: (i, 0)),
        out_shape=jax.ShapeDtypeStruct((n, d), F32),
        scratch_shapes=[pltpu.VMEM((TOP_K, tt) + y.shape[1:], F32), pltpu.SemaphoreType.DMA(())],
        compiler_params=_params(("arbitrary",)),
        name="combine",
    )(pos3, y, meta, x1, gate, g_post)


def _mixer(x, mod, pos, hist8, h0t, ln, tm, tq, wts, cache):
    b, s, d = x.shape
    mods = [m.reshape(b, 1, d) for m in jnp.split(mod, 6, axis=-1)]
    gt_f = mods[5]
    if cache is None:
        xf, fb, fs = x, b, s
    else:
        fb, fs = 1, b * s
        xf = x.reshape(fb, fs, d)
        pos = jnp.tile(pos, b)
        mods = [jnp.broadcast_to(m, (b, s, d)).reshape(fb, fs, d) for m in mods]
    sh_m, sc_m, gt_m, sh_f, sc_f, _ = mods
    proj = _in_projection(xf, 1.0 + sc_m, sh_m, wts["g_mix_pre"], wts["w_main"], wts["w_dt"], _rope_tables(pos), tm)
    q, k, v, z, xbc, dt_raw = [t.reshape(b, s, t.shape[-1]) for t in proj]
    if cache is None:
        attn = _attention_prompt(q, k, v, wts["sinks"], tq)
        k_win, v_win = k[:, s - WIN_CACHE:], v[:, s - WIN_CACHE:]
    else:
        attn, k_win, v_win = _attention_sample(q, k, v, wts["sinks"], *cache)
    ssm, conv_state, hfin = _ssd_mixer(xbc, z, dt_raw, hist8, h0t, wts["conv_w8"], wts["conv_b"], wts["dt_bias"],
                                       wts["a_log"], wts["dskip_x"], wts["g_ssm"], ln, q.dtype)
    x1, h2, logits = _out_projection(attn.reshape(fb, fs, -1), ssm.reshape(fb, fs, -1), xf, wts["w_out"],
                                     wts["g_mix_post"], gt_m, wts["g_ffn_pre"], 1.0 + sc_f, sh_f,
                                     wts["w_router"], wts["b_router"], tm)
    ssm_state = hfin.reshape(b, SSM_GROUPS, SSM_STATE, SSM_HPG, SSM_HEAD_DIM)
    ssm_state = ssm_state.transpose(0, 1, 3, 4, 2).reshape(b, SSM_HEADS, SSM_HEAD_DIM, SSM_STATE)
    states = (k_win.reshape(b, WIN_CACHE, N_KV, HEAD_DIM), v_win.reshape(b, WIN_CACHE, N_KV, HEAD_DIM),
              conv_state, ssm_state)
    return x1.reshape(b * s, d), h2, logits, gt_f, states


def _pad_lanes(a, width=LANES):
    return jnp.pad(a, [(0, 0)] * (a.ndim - 1) + [(0, width - a.shape[-1])])


def _largest_tile(n, cap):
    t = cap
    while n % t:
        t //= 2
    return t


def kernel(x_prompt, x_sample, c_prompt, c_sample, cache_k, cache_v, state_conv, state_ssm, w_mod, b_mod, g_mix_pre, g_mix_post, g_ffn_pre, g_ffn_post, w_in, conv_w, conv_b, dt_bias, a_log, d_skip, g_ssm, sinks, w_out, w_router, b_router, w_up, b_up, w_down, b_down):
    depth = w_mod.shape[0]
    assert depth == 1, "single-layer step"
    l = 0
    bp, sp, d = x_prompt.shape
    bs, ss, _ = x_sample.shape
    n_p, n_s = bp * sp, bs * ss
    n_tok = n_p + n_s

    c_all = jnp.concatenate([c_prompt, c_sample], axis=0)
    c_rows = -(-c_all.shape[0] // SUBLANES) * SUBLANES
    mod_all = _modulation(jnp.pad(c_all, ((0, c_rows - c_all.shape[0]), (0, 0))), w_mod[l], b_mod[l])

    row2 = lambda a: a.reshape(1, -1)
    paired = _paired_head_columns()
    wts = {
        "g_mix_pre": row2(g_mix_pre[l]), "g_mix_post": row2(g_mix_post[l]), "g_ffn_pre": row2(g_ffn_pre[l]),
        "w_main": jnp.concatenate([w_in[l][:, :ATTN_DIM][:, paired], w_in[l][:, ATTN_DIM:MAIN_DIM]],
                                  axis=1).astype(BF16),
        "w_dt": _pad_lanes(w_in[l][:, MAIN_DIM:]).astype(BF16),
        "sinks": sinks[l],
        "conv_w8": jnp.pad(conv_w[l], ((0, SUBLANES - CONV_W), (0, 0))), "conv_b": row2(conv_b[l]),
        "dt_bias": _pad_lanes(row2(dt_bias[l])), "a_log": _pad_lanes(row2(a_log[l])),
        "dskip_x": row2(jnp.repeat(d_skip[l], SSM_HEAD_DIM)), "g_ssm": row2(g_ssm[l]),
        "w_out": jnp.concatenate([w_out[l][:ATTN_DIM][paired], w_out[l][ATTN_DIM:]], axis=0).astype(BF16),
        "w_router": _pad_lanes(w_router[l]).astype(BF16), "b_router": _pad_lanes(row2(b_router[l])),
    }
    gw = SSM_HPG * SSM_HEAD_DIM

    hist_p = jnp.zeros((bp, SUBLANES, CONV_DIM), F32)
    h0_p = jnp.zeros((bp, SSM_GROUPS, SSM_STATE, gw), F32)
    x1_p, h2_p, lg_p, gtf_p, st_p = _mixer(x_prompt, mod_all[:bp], jnp.arange(sp, dtype=jnp.int32), hist_p, h0_p,
                                           SSD_CHUNK, _largest_tile(sp, 256), _largest_tile(sp, 256), wts, None)

    hist_s = jnp.pad(state_conv[l], ((0, 0), (SUBLANES - (CONV_W - 1), 0), (0, 0)))
    h0_s = state_ssm[l].astype(F32).reshape(bs, SSM_GROUPS, SSM_HPG, SSM_HEAD_DIM, SSM_STATE)
    h0_s = h0_s.transpose(0, 1, 4, 2, 3).reshape(bs, SSM_GROUPS, SSM_STATE, gw)
    cache = (cache_k[l].reshape(bs, WIN_CACHE, KV_DIM), cache_v[l].reshape(bs, WIN_CACHE, KV_DIM))
    x1_s, h2_s, lg_s, gtf_s, st_s = _mixer(x_sample, mod_all[bp:bp + bs], PAST_LEN + jnp.arange(ss, dtype=jnp.int32),
                                           hist_s, h0_s, ss, n_s, ss, wts, cache)

    tt = _largest_tile(n_s, 128)
    assert n_p % tt == 0
    meta, counts = _router(lg_p, lg_s, tt)
    top_idx = meta[:, LANE_IDX:LANE_IDX + TOP_K].astype(jnp.int32)
    rank = meta[:, LANE_RANK:LANE_RANK + TOP_K].astype(jnp.int32)
    cnt = counts[0, :N_EXPERTS].astype(jnp.int32)
    padded = (cnt + MOE_TM - 1) // MOE_TM * MOE_TM
    pend = jnp.cumsum(padded)
    offs = pend - padded
    pos = offs[top_idx] + rank
    n_blocks = -(-n_tok * TOP_K // MOE_TM) + N_EXPERTS
    n_slots = n_blocks * MOE_TM
    n_used = (pend[-1] // MOE_TM).astype(jnp.int32)
    blk = jnp.arange(n_blocks, dtype=jnp.int32)
    blk_row = jnp.minimum(blk, n_used - 1) * MOE_TM
    block_expert = jnp.minimum(jnp.sum(pend[None, :] <= blk_row[:, None], axis=1), N_EXPERTS - 1).astype(jnp.int32)
    zstart = jnp.where(cnt > 0, pend - MOE_TM, -1).astype(jnp.int32)
    pos3 = pos.reshape(n_tok // tt, 1, tt * TOP_K)
    nu = n_used.reshape(1)
    xs = _dispatch(h2_p, h2_s, pos3, zstart, nu, n_slots, tt)
    e_ids = jnp.arange(N_EXPERTS, dtype=jnp.int32)
    live = jnp.where(cnt > 0, e_ids, N_EXPERTS)
    later = jnp.concatenate([lax.cummin(live, reverse=True)[1:], jnp.full((1,), N_EXPERTS, jnp.int32)])
    next_live = jnp.where(later < N_EXPERTS, later, -1).astype(jnp.int32)
    visit = (jnp.cumsum((cnt > 0).astype(jnp.int32)) - 1) % 2
    plan = (block_expert, next_live[block_expert], visit[block_expert].astype(jnp.int32), nu)
    act = _grouped(_up_kernel, "expert_up", xs, w_up[l], b_up[l].reshape(N_EXPERTS, 1, -1), plan, (D_FF,), BF16)
    y = _grouped(_down_kernel, "expert_down", act, w_down[l], b_down[l].reshape(N_EXPERTS, 1, -1), plan,
                 xs.shape[1:], F32)

    g_post = row2(g_ffn_post[l])
    tt_p = _largest_tile(sp, 128)
    y_p = _combine(y, pos[:n_p].reshape(n_p // tt_p, 1, tt_p * TOP_K), meta[:n_p], x1_p,
                   gtf_p, g_post, tt_p).reshape(bp, sp, d)
    y_s = _combine(y, pos[n_p:].reshape(bs, 1, ss * TOP_K), meta[n_p:], x1_s,
                   gtf_s, g_post, ss).reshape(bs, ss, d)

    stack = lambda a: a[None]
    return (y_p, y_s, stack(st_p[0]), stack(st_p[1]), stack(st_p[2]), stack(st_p[3]),
            stack(st_s[0]), stack(st_s[1]), stack(st_s[2]), stack(st_s[3]))
```

```python
import functools

import numpy as np
import jax
import jax.numpy as jnp
from jax import lax
from jax.experimental import pallas as pl
from jax.experimental.pallas import tpu as pltpu

F32 = jnp.float32
BF16 = jnp.bfloat16

D_MODEL = 2048
CHUNK = 64
N_HEADS = 16
N_KV = 4
HEAD_DIM = 64
GQA = N_HEADS // N_KV
ATTN_DIM = N_HEADS * HEAD_DIM
KV_DIM = N_KV * HEAD_DIM
WINDOW = 128
WIN_CHUNKS = WINDOW // CHUNK
PAST_LEN = 4096
WIN_CACHE = min(WINDOW, PAST_LEN)
ROT_DIM = HEAD_DIM // 4
ROPE_THETA = 500000.0
SSM_HEADS = 16
SSM_HEAD_DIM = 64
SSM_DIM = SSM_HEADS * SSM_HEAD_DIM
SSM_GROUPS = 2
SSM_HPG = SSM_HEADS // SSM_GROUPS
SSM_STATE = 128
CONV_W = 4
CONV_DIM = SSM_DIM + 2 * SSM_GROUPS * SSM_STATE
SSD_CHUNK = 64
N_EXPERTS = 32
TOP_K = 4
D_FF = 2048
SWIGLU_ALPHA = 1.702
SWIGLU_LIMIT = 7.0
NORM_EPS = 1e-6

LANES = 128
SUBLANES = 8
MAIN_DIM = ATTN_DIM + 2 * KV_DIM + SSM_DIM + CONV_DIM
MOE_TM = 256
VMEM_LIMIT = 56 * 1024 * 1024


def _sigmoid(x):
    return 1.0 / (1.0 + jnp.exp(-x))


def _silu(x):
    return x * _sigmoid(x)


def _rms(x, g):
    return x * lax.rsqrt(jnp.mean(x * x, axis=-1, keepdims=True) + NORM_EPS) * g


def _split3(x):
    hi = x.astype(BF16)
    r1 = x - hi.astype(F32)
    mid = r1.astype(BF16)
    lo = (r1 - mid.astype(F32)).astype(BF16)
    return hi, mid, lo


def _dot_exact_rhs(x, m):
    hi, mid, lo = _split3(x)
    d = functools.partial(jnp.dot, preferred_element_type=F32)
    return d(hi, m) + d(mid, m) + d(lo, m)


_NN = (((1,), (0,)), ((), ()))
_NT = (((1,), (1,)), ((), ()))
_TN = (((0,), (0,)), ((), ()))


def _dot(a, b, dims=_NN):
    assert a.dtype == b.dtype, (a.dtype, b.dtype)
    prec = lax.Precision.HIGHEST if a.dtype == F32 else None
    return lax.dot_general(a, b, dims, preferred_element_type=F32, precision=prec)


def _params(sem, vmem=VMEM_LIMIT):
    return pltpu.CompilerParams(dimension_semantics=sem, vmem_limit_bytes=vmem)


def _mod_kernel(c_ref, w_ref, b_ref, o_ref):
    o_ref[...] = _dot(_silu(c_ref[...]).astype(BF16), w_ref[...].astype(BF16)) + b_ref[...]


def _modulation(c_all, w_mod, b_mod):
    rows, d = c_all.shape
    n = w_mod.shape[1]
    tn = 1536
    return pl.pallas_call(
        _mod_kernel,
        grid=(n // tn,),
        in_specs=[pl.BlockSpec((rows, d), lambda j: (0, 0)),
                  pl.BlockSpec((d, tn), lambda j: (0, j)),
                  pl.BlockSpec((1, tn), lambda j: (0, j))],
        out_specs=pl.BlockSpec((rows, tn), lambda j: (0, j)),
        out_shape=jax.ShapeDtypeStruct((rows, n), F32),
        compiler_params=_params(("arbitrary",)),
        name="modulation",
    )(c_all, w_mod, b_mod.reshape(1, n))


def _rope(t, cos, s1, s2):
    outs = []
    for j in range(t.shape[1] // LANES):
        tj = t[:, j * LANES:(j + 1) * LANES]
        up = pltpu.roll(tj, LANES - ROT_DIM // 2, 1)
        dn = pltpu.roll(tj, ROT_DIM // 2, 1)
        outs.append(tj * cos + up * s1 + dn * s2)
    return jnp.concatenate(outs, axis=1)


def _inproj_kernel(x_ref, sc_ref, sh_ref, g_ref, w_ref, wdt_ref, cos_ref, s1_ref, s2_ref,
                   q_ref, k_ref, v_ref, z_ref, xbc_ref, dt_ref):
    h = _rms(x_ref[0], g_ref[...]) * sc_ref[0] + sh_ref[0]
    hb = h.astype(w_ref.dtype)
    cos, s1, s2 = cos_ref[...], s1_ref[...], s2_ref[...]
    step = 512

    def mm(lo, hi):
        return jnp.concatenate([_dot(hb, w_ref[:, c:min(c + step, hi)]) for c in range(lo, hi, step)], axis=1)

    o = 0
    q_ref[0] = (_rope(mm(o, o + ATTN_DIM), cos, s1, s2) * (HEAD_DIM ** -0.5)).astype(q_ref.dtype)
    o += ATTN_DIM
    k_ref[0] = _rope(mm(o, o + KV_DIM), cos, s1, s2)
    o += KV_DIM
    v_ref[0] = mm(o, o + KV_DIM)
    o += KV_DIM
    z_ref[0] = mm(o, o + SSM_DIM)
    o += SSM_DIM
    xbc_ref[0] = mm(o, o + CONV_DIM)
    dt_ref[0] = _dot(hb, wdt_ref[...])


def _mod_spec(m, tm):
    if m.shape[1] == 1:
        return pl.BlockSpec((1, 1, m.shape[2]), lambda bi, i: (bi, 0, 0))
    return pl.BlockSpec((1, tm, m.shape[2]), lambda bi, i: (bi, i, 0))


def _in_projection(x, sc1p, sh, g, w_main, w_dt, rope_tabs, tm):
    b, s, d = x.shape
    cos, s1, s2 = rope_tabs
    row = lambda bi, i: (bi, i, 0)
    const2 = lambda bi, i: (0, 0)
    tab = lambda bi, i: (i, 0)
    widths = (ATTN_DIM, KV_DIM, KV_DIM, SSM_DIM, CONV_DIM, LANES)
    dtypes = (w_main.dtype, F32, F32, F32, F32, F32)
    return pl.pallas_call(
        _inproj_kernel,
        grid=(b, s // tm),
        in_specs=[pl.BlockSpec((1, tm, d), row),
                  _mod_spec(sc1p, tm),
                  _mod_spec(sh, tm),
                  pl.BlockSpec((1, d), const2),
                  pl.BlockSpec((d, MAIN_DIM), const2, pipeline_mode=pl.Buffered(1)),
                  pl.BlockSpec((d, LANES), const2, pipeline_mode=pl.Buffered(1)),
                  pl.BlockSpec((tm, LANES), tab),
                  pl.BlockSpec((tm, LANES), tab),
                  pl.BlockSpec((tm, LANES), tab)],
        out_specs=[pl.BlockSpec((1, tm, w), row) for w in widths],
        out_shape=[jax.ShapeDtypeStruct((b, s, w), dt) for w, dt in zip(widths, dtypes)],
        compiler_params=_params(("arbitrary", "arbitrary")),
        name="in_projection",
    )(x, sc1p, sh, g, w_main, w_dt, cos, s1, s2)


def _rope_tables(pos):
    inv_freq = ROPE_THETA ** (-jnp.arange(0, ROT_DIM, 2, dtype=F32) / ROT_DIM)
    ang = pos.astype(F32)[:, None] * inv_freq[None, :]
    cos, sin = jnp.cos(ang), jnp.sin(ang)
    half = ROT_DIM // 2
    n = pos.shape[0]
    ones = jnp.ones((n, HEAD_DIM - ROT_DIM), F32)
    zeros_h = jnp.zeros((n, half), F32)
    zeros_r = jnp.zeros((n, HEAD_DIM - ROT_DIM), F32)
    c = jnp.concatenate([cos, cos, ones], axis=1)
    a = jnp.concatenate([-sin, zeros_h, zeros_r], axis=1)
    b = jnp.concatenate([zeros_h, sin, zeros_r], axis=1)
    rep = LANES // HEAD_DIM
    return tuple(jnp.tile(t, (1, rep)) for t in (c, a, b))


KV_PAIRS = N_KV // 2


def _paired_head_columns():
    cols = []
    for j in range(KV_PAIRS):
        for a in range(GQA):
            for g in (2 * j, 2 * j + 1):
                h = g * GQA + a
                cols.extend(range(h * HEAD_DIM, (h + 1) * HEAD_DIM))
    return np.asarray(cols, np.int32)


def _attend_pair(qcols, k2, v2, sink_row, valid):
    rows = qcols[0].shape[0]
    lo = lax.broadcasted_iota(jnp.int32, (rows, LANES), 1) < HEAD_DIM
    zero = jnp.zeros_like(qcols[0])
    lhs = jnp.concatenate([jnp.where(lo, qc, zero) for qc in qcols] + [jnp.where(lo, zero, qc) for qc in qcols], axis=0)
    s = _dot(k2, lhs, _NT)
    if valid is not None:
        s = jnp.where(valid, s, -1e30)
    m = jnp.maximum(jnp.max(s, axis=0, keepdims=True), sink_row)
    p = jnp.exp(s - m)
    den = jnp.sum(p, axis=0, keepdims=True) + jnp.exp(sink_row - m)
    o = _dot((p * (1.0 / den)).astype(v2.dtype), v2, _TN)
    half = GQA * rows
    return [jnp.where(lo, o[a * rows:(a + 1) * rows], o[half + a * rows:half + (a + 1) * rows]) for a in range(GQA)]


def _sink_row(sinks_ref, j, rows):
    heads = [(2 * j) * GQA + a for a in range(GQA)] + [(2 * j + 1) * GQA + a for a in range(GQA)]
    col = lax.broadcasted_iota(jnp.int32, (1, len(heads) * rows), 1) // rows
    out = jnp.zeros((1, len(heads) * rows), F32)
    for n, h in enumerate(heads):
        out = jnp.where(col == n, sinks_ref[h], out)
    return out


def _attn_prompt_kernel(sinks_ref, q_ref, km_ref, kh_ref, vm_ref, vh_ref, o_ref):
    i = pl.program_id(1)
    tq = q_ref.shape[1]
    chunks = tq // CHUNK
    kfull = jnp.concatenate([kh_ref[0], km_ref[0]], axis=0).astype(q_ref.dtype)
    vfull = jnp.concatenate([vh_ref[0], vm_ref[0]], axis=0).astype(q_ref.dtype)
    span = (WIN_CHUNKS + 1) * CHUNK
    col_chunk = lax.broadcasted_iota(jnp.int32, (span, 2 * GQA * CHUNK), 0) // CHUNK
    for j in range(KV_PAIRS):
        k2 = kfull[:, j * LANES:(j + 1) * LANES]
        v2 = vfull[:, j * LANES:(j + 1) * LANES]
        sink = _sink_row(sinks_ref, j, CHUNK)
        for c in range(chunks):
            r0 = c * CHUNK
            slabs = [(j * GQA + a) * LANES for a in range(GQA)]
            qcols = [q_ref[0, r0:r0 + CHUNK, sl:sl + LANES] for sl in slabs]
            valid = None
            if c < WIN_CHUNKS:
                valid = (i * chunks + c - WIN_CHUNKS + col_chunk) >= 0
            outs = _attend_pair(qcols, k2[r0:r0 + span], v2[r0:r0 + span], sink, valid)
            for sl, o in zip(slabs, outs):
                o_ref[0, r0:r0 + CHUNK, sl:sl + LANES] = o.astype(o_ref.dtype)


def _attention_prompt(q, k, v, sinks, tq):
    b, s, _ = q.shape
    halo = WIN_CHUNKS * CHUNK
    ratio = tq // halo
    main = lambda bi, i: (bi, i, 0)
    prev = lambda bi, i: (bi, jnp.maximum(i * ratio - 1, 0), 0)
    return pl.pallas_call(
        _attn_prompt_kernel,
        grid=(b, s // tq),
        in_specs=[pl.BlockSpec(memory_space=pltpu.SMEM),
                  pl.BlockSpec((1, tq, ATTN_DIM), main),
                  pl.BlockSpec((1, tq, KV_DIM), main),
                  pl.BlockSpec((1, halo, KV_DIM), prev),
                  pl.BlockSpec((1, tq, KV_DIM), main),
                  pl.BlockSpec((1, halo, KV_DIM), prev)],
        out_specs=pl.BlockSpec((1, tq, ATTN_DIM), main),
        out_shape=jax.ShapeDtypeStruct((b, s, ATTN_DIM), q.dtype),
        compiler_params=_params(("arbitrary", "arbitrary")),
        name="attention_prompt",
    )(sinks, q, k, k, v, v)


def _attn_sample_kernel(sinks_ref, q_ref, k_ref, v_ref, ck_ref, cv_ref, o_ref, kw_ref, vw_ref):
    s = q_ref.shape[1]
    kf = jnp.concatenate([ck_ref[0], k_ref[0]], axis=0)
    vf = jnp.concatenate([cv_ref[0], v_ref[0]], axis=0)
    n = kf.shape[0]
    kw_ref[0] = kf[n - WIN_CACHE:]
    vw_ref[0] = vf[n - WIN_CACHE:]
    kb, vb = kf.astype(q_ref.dtype), vf.astype(q_ref.dtype)
    for j in range(KV_PAIRS):
        slabs = [(j * GQA + a) * LANES for a in range(GQA)]
        qcols = [q_ref[0, :, sl:sl + LANES] for sl in slabs]
        outs = _attend_pair(qcols, kb[:, j * LANES:(j + 1) * LANES], vb[:, j * LANES:(j + 1) * LANES],
                            _sink_row(sinks_ref, j, s), None)
        for sl, o in zip(slabs, outs):
            o_ref[0, :, sl:sl + LANES] = o.astype(o_ref.dtype)


def _attention_sample(q, k, v, sinks, cache_k, cache_v):
    b, s, _ = q.shape
    blk = lambda bi: (bi, 0, 0)
    return pl.pallas_call(
        _attn_sample_kernel,
        grid=(b,),
        in_specs=[pl.BlockSpec(memory_space=pltpu.SMEM),
                  pl.BlockSpec((1, s, ATTN_DIM), blk),
                  pl.BlockSpec((1, s, KV_DIM), blk),
                  pl.BlockSpec((1, s, KV_DIM), blk),
                  pl.BlockSpec((1, WIN_CACHE, KV_DIM), blk),
                  pl.BlockSpec((1, WIN_CACHE, KV_DIM), blk)],
        out_specs=[pl.BlockSpec((1, s, ATTN_DIM), blk),
                   pl.BlockSpec((1, WIN_CACHE, KV_DIM), blk),
                   pl.BlockSpec((1, WIN_CACHE, KV_DIM), blk)],
        out_shape=[jax.ShapeDtypeStruct((b, s, ATTN_DIM), q.dtype),
                   jax.ShapeDtypeStruct((b, WIN_CACHE, KV_DIM), F32),
                   jax.ShapeDtypeStruct((b, WIN_CACHE, KV_DIM), F32)],
        compiler_params=_params(("arbitrary",)),
        name="attention_sample",
    )(sinks, q, k, v, cache_k, cache_v)


def _ssd_kernel(xbc_ref, z_ref, dt_ref, hist_ref, h0_ref, cw_ref, cb_ref, dtb_ref, alog_ref, dsk_ref, gs_ref,
                ep_ref, el_ref, dmask_ref, causal_ref, tril_ref,
                y_ref, conv_ref, hfin_ref, prev, ht):
    c = pl.program_id(1)
    last = pl.num_programs(1) - 1
    ln = xbc_ref.shape[1]

    @pl.when(c == 0)
    def _():
        prev[...] = hist_ref[...]
        ht[...] = h0_ref[...]

    for bi in range(xbc_ref.shape[0]):
        _ssd_chunk(bi, xbc_ref, z_ref, dt_ref, cw_ref, cb_ref, dtb_ref, alog_ref, dsk_ref, gs_ref,
                   ep_ref, el_ref, dmask_ref, causal_ref, tril_ref, y_ref, prev, ht)

    @pl.when(c == last)
    def _():
        hfin_ref[...] = ht[...]
        conv_ref[...] = xbc_ref[:, ln - (CONV_W - 1):, :]


def _ssd_chunk(bi, xbc_ref, z_ref, dt_ref, cw_ref, cb_ref, dtb_ref, alog_ref, dsk_ref, gs_ref,
               ep_ref, el_ref, dmask_ref, causal_ref, tril_ref, y_ref, prev, ht):
    ln = xbc_ref.shape[1]
    gw = SSM_HPG * SSM_HEAD_DIM
    xr = xbc_ref[bi]
    ext = jnp.concatenate([prev[bi], xr], axis=0)
    conv = cb_ref[...]
    for i in range(CONV_W):
        sh = CONV_W - 1 - i
        tap = xr if sh == 0 else pltpu.roll(ext, sh, 0)[SUBLANES:]
        conv = conv + tap * cw_ref[i:i + 1, :]
    prev[bi] = xr[ln - SUBLANES:]

    act = _silu(conv)
    xs = act[:, :SSM_DIM]
    bm = act[:, SSM_DIM:SSM_DIM + SSM_GROUPS * SSM_STATE]
    cm = act[:, SSM_DIM + SSM_GROUPS * SSM_STATE:]

    dtv = dt_ref[bi] + dtb_ref[...]
    dt = jnp.maximum(dtv, 0.0) + jnp.log1p(jnp.exp(-jnp.abs(dtv)))
    ad = dt * (-jnp.exp(alog_ref[...]))
    hi, mid, lo = _split3(ad)
    tril = tril_ref[...]
    d = functools.partial(jnp.dot, preferred_element_type=F32)
    a_cs = d(tril, hi) + d(tril, mid) + d(tril, lo)
    a_last = a_cs[ln - 1:ln, :]
    stacked = jnp.concatenate([dt, jnp.exp(a_cs), jnp.exp(a_last - a_cs)], axis=0)
    wide = _dot_exact_rhs(stacked, ep_ref[...])
    dt_x, ea_x, ds_x = wide[:ln], wide[ln:2 * ln], wide[2 * ln:]
    cd_x = ea_x[ln - 1:ln, :]

    a_l = _dot_exact_rhs(a_cs, el_ref[...])
    a_s = jnp.sum(a_l * dmask_ref[...], axis=0, keepdims=True)
    lmat = jnp.exp(jnp.where(causal_ref[...] > 0.0, a_l - a_s, -1e30))

    cdt = y_ref.dtype
    xd = xs * dt_x
    xds = (xd * ds_x).astype(cdt)
    xdb = xd.astype(cdt)
    ys = []
    for g in range(SSM_GROUPS):
        bg = bm[:, g * SSM_STATE:(g + 1) * SSM_STATE].astype(cdt)
        cg = cm[:, g * SSM_STATE:(g + 1) * SSM_STATE].astype(cdt)
        cbm = _dot(cg, bg, _NT)
        yd = []
        for r in range(SSM_HPG):
            hd = g * SSM_HPG + r
            w = (cbm * lmat[:, hd * ln:(hd + 1) * ln]).astype(cdt)
            yd.append(_dot(w, xdb[:, hd * SSM_HEAD_DIM:(hd + 1) * SSM_HEAD_DIM]))
        htg = ht[bi, g]
        y_off = _dot(cg, htg.astype(cdt)) * ea_x[:, g * gw:(g + 1) * gw]
        st = _dot(bg, xds[:, g * gw:(g + 1) * gw], _TN)
        ht[bi, g] = htg * cd_x[:, g * gw:(g + 1) * gw] + st
        ys.append(jnp.concatenate(yd, axis=1) + y_off)
    y = jnp.concatenate(ys, axis=1) + dsk_ref[...] * xs
    y = y * _silu(z_ref[bi])
    outs = [_rms(y[:, g * gw:(g + 1) * gw], gs_ref[:, g * gw:(g + 1) * gw]) for g in range(SSM_GROUPS)]
    y_ref[bi] = jnp.concatenate(outs, axis=1).astype(y_ref.dtype)


def _ssd_constants(ln):
    heads = SSM_HEADS
    ep = np.zeros((LANES, heads * SSM_HEAD_DIM), np.float32)
    el = np.zeros((LANES, heads * ln), np.float32)
    for r in range(heads):
        ep[r, r * SSM_HEAD_DIM:(r + 1) * SSM_HEAD_DIM] = 1.0
        el[r, r * ln:(r + 1) * ln] = 1.0
    eye = np.tile(np.eye(ln, dtype=np.float32), (1, heads))
    causal = np.tile(np.tril(np.ones((ln, ln), np.float32)), (1, heads))
    tril = np.tril(np.ones((ln, ln), np.float32))
    return (jnp.asarray(ep, BF16), jnp.asarray(el, BF16), jnp.asarray(eye), jnp.asarray(causal),
            jnp.asarray(tril, BF16))


def _ssd_mixer(xbc, z, dt_raw, hist8, h0t, conv_w8, conv_b, dt_bias, a_log, dskip_x, g_ssm, ln, out_dtype):
    b, s, _ = xbc.shape
    bb = 2 if b % 2 == 0 else 1
    consts = _ssd_constants(ln)
    row = lambda bi, c: (bi, c, 0)
    per_b3 = lambda bi, c: (bi, 0, 0)
    per_b4 = lambda bi, c: (bi, 0, 0, 0)
    const2 = lambda bi, c: (0, 0)
    full = lambda a: pl.BlockSpec(a.shape, const2)
    gw = SSM_HPG * SSM_HEAD_DIM
    return pl.pallas_call(
        _ssd_kernel,
        grid=(b // bb, s // ln),
        in_specs=[pl.BlockSpec((bb, ln, CONV_DIM), row),
                  pl.BlockSpec((bb, ln, SSM_DIM), row),
                  pl.BlockSpec((bb, ln, LANES), row),
                  pl.BlockSpec((bb, SUBLANES, CONV_DIM), per_b3),
                  pl.BlockSpec((bb, SSM_GROUPS, SSM_STATE, gw), per_b4),
                  full(conv_w8), full(conv_b), full(dt_bias), full(a_log), full(dskip_x), full(g_ssm)]
                 + [full(a) for a in consts],
        out_specs=[pl.BlockSpec((bb, ln, SSM_DIM), row),
                   pl.BlockSpec((bb, CONV_W - 1, CONV_DIM), per_b3),
                   pl.BlockSpec((bb, SSM_GROUPS, SSM_STATE, gw), per_b4)],
        out_shape=[jax.ShapeDtypeStruct((b, s, SSM_DIM), out_dtype),
                   jax.ShapeDtypeStruct((b, CONV_W - 1, CONV_DIM), F32),
                   jax.ShapeDtypeStruct((b, SSM_GROUPS, SSM_STATE, gw), F32)],
        scratch_shapes=[pltpu.VMEM((bb, SUBLANES, CONV_DIM), F32),
                        pltpu.VMEM((bb, SSM_GROUPS, SSM_STATE, gw), F32)],
        compiler_params=_params(("arbitrary", "arbitrary")),
        name="ssd_mixer",
    )(xbc, z, dt_raw, hist8, h0t, conv_w8, conv_b, dt_bias, a_log, dskip_x, g_ssm, *consts)


def _outproj_kernel(a_ref, s_ref, x_ref, w_ref, gpost_ref, gate_ref, gpre_ref, sc_ref, sh_ref, wr_ref, br_ref,
                    x1_ref, h2_ref, lg_ref):
    mix = _dot(a_ref[0], w_ref[:ATTN_DIM, :]) + _dot(s_ref[0], w_ref[ATTN_DIM:, :])
    x1 = x_ref[0] + gate_ref[0] * _rms(mix, gpost_ref[...])
    x1_ref[0] = x1
    h2 = _rms(x1, gpre_ref[...]) * sc_ref[0] + sh_ref[0]
    h2_ref[...] = h2
    lg_ref[...] = _dot(h2.astype(wr_ref.dtype), wr_ref[...]) + br_ref[...]


def _out_projection(attn, ssm, x, w_out, g_post, gate, g_pre, sc1p, sh, w_router, b_router, tm):
    b, s, d = x.shape
    nt = s // tm
    row = lambda bi, i: (bi, i, 0)
    const2 = lambda bi, i: (0, 0)
    flat = lambda bi, i: (bi * nt + i, 0)
    return pl.pallas_call(
        _outproj_kernel,
        grid=(b, nt),
        in_specs=[pl.BlockSpec((1, tm, ATTN_DIM), row),
                  pl.BlockSpec((1, tm, SSM_DIM), row),
                  pl.BlockSpec((1, tm, d), row),
                  pl.BlockSpec((ATTN_DIM + SSM_DIM, d), const2, pipeline_mode=pl.Buffered(1)),
                  pl.BlockSpec((1, d), const2),
                  _mod_spec(gate, tm),
                  pl.BlockSpec((1, d), const2),
                  _mod_spec(sc1p, tm),
                  _mod_spec(sh, tm),
                  pl.BlockSpec((d, LANES), const2),
                  pl.BlockSpec((1, LANES), const2)],
        out_specs=[pl.BlockSpec((1, tm, d), row),
                   pl.BlockSpec((tm, d), flat),
                   pl.BlockSpec((tm, LANES), flat)],
        out_shape=[jax.ShapeDtypeStruct((b, s, d), F32),
                   jax.ShapeDtypeStruct((b * s, d), F32),
                   jax.ShapeDtypeStruct((b * s, LANES), F32)],
        compiler_params=_params(("arbitrary", "arbitrary")),
        name="out_projection",
    )(attn, ssm, x, w_out, g_post, gate, g_pre, sc1p, sh, w_router, b_router)


LANE_IDX, LANE_RANK, LANE_GATE = 0, TOP_K, 2 * TOP_K


def _router_kernel(lp_ref, ls_ref, ltri_ref, meta_ref, cnt_ref, *, p_tiles):
    i = pl.program_id(0)
    tt = lp_ref.shape[0]

    @pl.when(i == 0)
    def _():
        cnt_ref[...] = jnp.zeros_like(cnt_ref)

    logits = jnp.where(i < p_tiles, lp_ref[...], ls_ref[...])
    lane = lax.broadcasted_iota(jnp.int32, (tt, LANES), 1)
    lane_f = lane.astype(F32)
    work = jnp.where(lane < N_EXPERTS, logits, -jnp.inf)
    vals, hots, idxs = [], [], []
    for _ in range(TOP_K):
        m = jnp.max(work, axis=-1, keepdims=True)
        idx = jnp.min(jnp.where(work == m, lane_f, float(LANES)), axis=-1, keepdims=True)
        hot = lane_f == idx
        vals.append(m)
        hots.append(hot)
        idxs.append(idx)
        work = jnp.where(hot, -jnp.inf, work)
    es = [jnp.exp(v - vals[0]) for v in vals]
    den = es[0] + es[1] + es[2] + es[3]
    onehot = jnp.zeros((tt, LANES), F32)
    for hot in hots:
        onehot = jnp.where(hot, 1.0, onehot)
    before = jnp.dot(ltri_ref[...], onehot.astype(BF16), preferred_element_type=F32) + cnt_ref[0:1, :]
    meta = jnp.zeros((tt, LANES), F32)
    for k in range(TOP_K):
        rank_k = jnp.sum(jnp.where(hots[k], before, 0.0), axis=-1, keepdims=True)
        meta = jnp.where(lane == LANE_IDX + k, idxs[k], meta)
        meta = jnp.where(lane == LANE_RANK + k, rank_k, meta)
        meta = jnp.where(lane == LANE_GATE + k, es[k] / den, meta)
    meta_ref[...] = meta
    cnt_ref[...] = cnt_ref[...] + jnp.sum(onehot, axis=0, keepdims=True)


def _token_specs(p_tiles, tt, trailing):
    zeros = (0,) * len(trailing)
    return [pl.BlockSpec((tt,) + trailing, lambda i, *_: (jnp.minimum(i, p_tiles - 1),) + zeros),
            pl.BlockSpec((tt,) + trailing, lambda i, *_: (jnp.maximum(i - p_tiles, 0),) + zeros)]


def _router(logits_p, logits_s, tt):
    n = logits_p.shape[0] + logits_s.shape[0]
    p_tiles = logits_p.shape[0] // tt
    ltri = jnp.asarray(np.tril(np.ones((tt, tt), np.float32), -1), BF16)
    return pl.pallas_call(
        functools.partial(_router_kernel, p_tiles=p_tiles),
        grid=(n // tt,),
        in_specs=_token_specs(p_tiles, tt, (LANES,)) + [pl.BlockSpec((tt, tt), lambda i: (0, 0))],
        out_specs=[pl.BlockSpec((tt, LANES), lambda i: (i, 0)),
                   pl.BlockSpec((SUBLANES, LANES), lambda i: (0, 0))],
        out_shape=[jax.ShapeDtypeStruct((n, LANES), F32),
                   jax.ShapeDtypeStruct((SUBLANES, LANES), F32)],
        compiler_params=_params(("arbitrary",)),
        name="router",
    )(logits_p, logits_s, ltri)


def _dispatch_kernel(zstart_ref, nu_ref, pos_ref, hp_ref, hs_ref, xs_ref, zeros, sem_z, sem_r, *, p_tiles):
    i = pl.program_id(0)
    tt = hp_ref.shape[0]
    tm = zeros.shape[0]
    n_blocks = xs_ref.shape[0] // tm

    def zero_copy(row):
        return pltpu.make_async_copy(zeros, xs_ref.at[pl.ds(pl.multiple_of(row, tm), tm)], sem_z)

    @pl.when(i == 0)
    def _():
        zeros[...] = jnp.zeros_like(zeros)

        def per_expert(act):
            def body(e, carry):
                @pl.when(zstart_ref[e] >= 0)
                def _():
                    act(zero_copy(zstart_ref[e]))
                return carry
            lax.fori_loop(0, N_EXPERTS, body, 0)

        def per_tail(act):
            def body(blk, carry):
                act(zero_copy(blk * tm))
                return carry
            lax.fori_loop(nu_ref[0], n_blocks, body, 0)

        per_expert(lambda cp: cp.start())
        per_tail(lambda cp: cp.start())
        per_expert(lambda cp: cp.wait())
        per_tail(lambda cp: cp.wait())

    def scatter_rows(h_ref):
        def row_copy(t, k):
            return pltpu.make_async_copy(h_ref.at[pl.ds(t, 1)],
                                         xs_ref.at[pl.ds(pos_ref[0, 0, t * TOP_K + k], 1)], sem_r)

        copies = [row_copy(t, k) for t in range(tt) for k in range(TOP_K)]
        for cp in copies:
            cp.start()
        for cp in copies:
            cp.wait()

    @pl.when(i < p_tiles)
    def _():
        scatter_rows(hp_ref)

    @pl.when(i >= p_tiles)
    def _():
        scatter_rows(hs_ref)


def _dispatch(h_p, h_s, pos3, zstart, n_used, n_slots, tt):
    row_shape = h_p.shape[1:]
    n = h_p.shape[0] + h_s.shape[0]
    p_tiles = h_p.shape[0] // tt
    return pl.pallas_call(
        functools.partial(_dispatch_kernel, p_tiles=p_tiles),
        grid_spec=pltpu.PrefetchScalarGridSpec(
            num_scalar_prefetch=2,
            grid=(n // tt,),
            in_specs=[pl.BlockSpec((1, 1, tt * TOP_K), lambda i, *_: (i, 0, 0), memory_space=pltpu.SMEM)]
                     + _token_specs(p_tiles, tt, row_shape),
            out_specs=pl.BlockSpec(memory_space=pl.ANY),
            scratch_shapes=[pltpu.VMEM((MOE_TM,) + row_shape, F32),
                            pltpu.SemaphoreType.DMA(()),
                            pltpu.SemaphoreType.DMA(())]),
        out_shape=jax.ShapeDtypeStruct((n_slots,) + row_shape, F32),
        compiler_params=_params(("arbitrary",)),
        name="dispatch",
    )(zstart, n_used, pos3, h_p, h_s)


W_CHUNK_ROWS = 256


def _stage_expert_weights(i, be_ref, nxt_ref, slot_ref, w_hbm, wbuf, stg, sem, done):
    rows = W_CHUNK_ROWS
    chunks = wbuf.shape[1] // rows
    e, e_next, slot = be_ref[i], nxt_ref[i], slot_ref[i]
    first = jnp.logical_or(i == 0, be_ref[jnp.maximum(i - 1, 0)] != e)

    def chunk_copy(expert, c, buf):
        return pltpu.make_async_copy(w_hbm.at[expert, pl.ds(pl.multiple_of(c * rows, rows), rows)],
                                     stg.at[buf], sem.at[buf])

    def convert_next_chunk(expert, into):
        c = done[0]
        buf = c % 2
        chunk_copy(expert, c, buf).wait()

        @pl.when(c + 1 < chunks)
        def _():
            chunk_copy(expert, c + 1, 1 - buf).start()

        wbuf[into, pl.ds(pl.multiple_of(c * rows, rows), rows), :] = stg[buf].astype(wbuf.dtype)
        done[0] = c + 1

    @pl.when(i == 0)
    def _():
        done[0] = 0
        chunk_copy(e, 0, 0).start()

    @pl.when(first)
    def _():
        def body(_, carry):
            convert_next_chunk(e, slot)
            return carry

        lax.fori_loop(done[0], chunks, body, 0)

        @pl.when(e_next >= 0)
        def _():
            done[0] = 0
            chunk_copy(e_next, 0, 0).start()

    @pl.when(jnp.logical_and(jnp.logical_not(first), jnp.logical_and(e_next >= 0, done[0] < chunks)))
    def _():
        convert_next_chunk(e_next, 1 - slot)

    return slot


def _up_kernel(be_ref, nxt_ref, slot_ref, nu_ref, x_ref, w_hbm, b_ref, act_ref, wbuf, stg, sem, done):
    i = pl.program_id(0)
    fc = 512

    @pl.when(i < nu_ref[0])
    def _():
        slot = _stage_expert_weights(i, be_ref, nxt_ref, slot_ref, w_hbm, wbuf, stg, sem, done)
        x = x_ref[...].astype(wbuf.dtype)
        for c in range(D_FF // fc):
            glu = _dot(x, wbuf[slot, :, c * fc:(c + 1) * fc]) + b_ref[:, c * fc:(c + 1) * fc]
            lin = (_dot(x, wbuf[slot, :, D_FF + c * fc:D_FF + (c + 1) * fc])
                   + b_ref[:, D_FF + c * fc:D_FF + (c + 1) * fc])
            glu = jnp.minimum(glu, SWIGLU_LIMIT)
            lin = jnp.clip(lin, -SWIGLU_LIMIT, SWIGLU_LIMIT)
            act_ref[:, c * fc:(c + 1) * fc] = (glu * _sigmoid(SWIGLU_ALPHA * glu) * (lin + 1.0)).astype(act_ref.dtype)

    @pl.when(i >= nu_ref[0])
    def _():
        act_ref[...] = jnp.zeros_like(act_ref)


def _down_kernel(be_ref, nxt_ref, slot_ref, nu_ref, a_ref, w_hbm, b_ref, y_ref, wbuf, stg, sem, done):
    i = pl.program_id(0)

    @pl.when(i < nu_ref[0])
    def _():
        slot = _stage_expert_weights(i, be_ref, nxt_ref, slot_ref, w_hbm, wbuf, stg, sem, done)
        y_ref[...] = _dot(a_ref[...], wbuf[slot]) + b_ref[...]

    @pl.when(i >= nu_ref[0])
    def _():
        y_ref[...] = jnp.zeros_like(y_ref)


def _grouped(kernel, name, x, w, b, plan, out_tail, out_dtype, extra_scratch=()):
    block_expert, next_expert, slot, n_used = plan
    n_slots = x.shape[0]
    _, kdim, ndim = w.shape
    nb = n_slots // MOE_TM
    zeros = lambda t: (0,) * len(t)
    x_tail, o_tail = x.shape[1:], tuple(out_tail)
    return pl.pallas_call(
        kernel,
        grid_spec=pltpu.PrefetchScalarGridSpec(
            num_scalar_prefetch=4,
            grid=(nb,),
            in_specs=[pl.BlockSpec((MOE_TM,) + x_tail, lambda i, be, nx, sl, nu: (jnp.minimum(i, nu[0] - 1),) + zeros(x_tail)),
                      pl.BlockSpec(memory_space=pl.ANY),
                      pl.BlockSpec((None, 1, ndim), lambda i, be, nx, sl, nu: (be[i], 0, 0))],
            out_specs=pl.BlockSpec((MOE_TM,) + o_tail, lambda i, be, nx, sl, nu: (i,) + zeros(o_tail)),
            scratch_shapes=[pltpu.VMEM((2, kdim, ndim), BF16),
                            pltpu.VMEM((2, W_CHUNK_ROWS, ndim), F32),
                            pltpu.SemaphoreType.DMA((2,)),
                            pltpu.SMEM((1,), jnp.int32)] + list(extra_scratch)),
        out_shape=jax.ShapeDtypeStruct((n_slots,) + o_tail, out_dtype),
        compiler_params=_params(("arbitrary",), 60 * 1024 * 1024),
        name=name,
    )(block_expert, next_expert, slot, n_used, x, w, b)


def _combine_kernel(pos_ref, y_ref, meta_ref, x1_ref, gate_ref, g_ref, o_ref, rows, sem):
    tt = x1_ref.shape[0]

    def row_copy(t, k):
        return pltpu.make_async_copy(y_ref.at[pl.ds(pos_ref[0, 0, t * TOP_K + k], 1)],
                                     rows.at[k, pl.ds(t, 1)], sem)

    copies = [row_copy(t, k) for t in range(tt) for k in range(TOP_K)]
    for cp in copies:
        cp.start()
    for cp in copies:
        cp.wait()
    meta = meta_ref[...]
    f = rows[0] * meta[:, LANE_GATE:LANE_GATE + 1]
    for k in range(1, TOP_K):
        f = f + rows[k] * meta[:, LANE_GATE + k:LANE_GATE + k + 1]
    o_ref[...] = x1_ref[...] + gate_ref[0] * _rms(f, g_ref[...])


def _combine(y, pos3, meta, x1, gate, g_post, tt):
    n, d = x1.shape
    nb = gate.shape[0]
    tiles_per_b = n // tt // nb
    return pl.pallas_call(
        _combine_kernel,
        grid=(n // tt,),
        in_specs=[pl.BlockSpec((1, 1, tt * TOP_K), lambda i: (i, 0, 0), memory_space=pltpu.SMEM),
                  pl.BlockSpec(memory_space=pl.ANY),
                  pl.BlockSpec((tt, LANES), lambda i: (i, 0)),
                  pl.BlockSpec((tt, d), lambda i: (i, 0)),
                  pl.BlockSpec((1, 1, d), lambda i: (i // tiles_per_b, 0, 0)),
                  pl.BlockSpec((1, d), lambda i: (0, 0))],
        out_specs=pl.BlockSpec((tt, d), lambda i: (i, 0)),
        out_shape=jax.ShapeDtypeStruct((n, d), F32),
        scratch_shapes=[pltpu.VMEM((TOP_K, tt) + y.shape[1:], F32), pltpu.SemaphoreType.DMA(())],
        compiler_params=_params(("arbitrary",)),
        name="combine",
    )(pos3, y, meta, x1, gate, g_post)


def _mixer(x, mod, pos, hist8, h0t, ln, tm, tq, wts, cache):
    b, s, d = x.shape
    mods = [m.reshape(b, 1, d) for m in jnp.split(mod, 6, axis=-1)]
    gt_f = mods[5]
    if cache is None:
        xf, fb, fs = x, b, s
    else:
        fb, fs = 1, b * s
        xf = x.reshape(fb, fs, d)
        pos = jnp.tile(pos, b)
        mods = [jnp.broadcast_to(m, (b, s, d)).reshape(fb, fs, d) for m in mods]
    sh_m, sc_m, gt_m, sh_f, sc_f, _ = mods
    proj = _in_projection(xf, 1.0 + sc_m, sh_m, wts["g_mix_pre"], wts["w_main"], wts["w_dt"], _rope_tables(pos), tm)
    q, k, v, z, xbc, dt_raw = [t.reshape(b, s, t.shape[-1]) for t in proj]
    if cache is None:
        attn = _attention_prompt(q, k, v, wts["sinks"], tq)
        k_win, v_win = k[:, s - WIN_CACHE:], v[:, s - WIN_CACHE:]
    else:
        attn, k_win, v_win = _attention_sample(q, k, v, wts["sinks"], *cache)
    ssm, conv_state, hfin = _ssd_mixer(xbc, z, dt_raw, hist8, h0t, wts["conv_w8"], wts["conv_b"], wts["dt_bias"],
                                       wts["a_log"], wts["dskip_x"], wts["g_ssm"], ln, q.dtype)
    x1, h2, logits = _out_projection(attn.reshape(fb, fs, -1), ssm.reshape(fb, fs, -1), xf, wts["w_out"],
                                     wts["g_mix_post"], gt_m, wts["g_ffn_pre"], 1.0 + sc_f, sh_f,
                                     wts["w_router"], wts["b_router"], tm)
    ssm_state = hfin.reshape(b, SSM_GROUPS, SSM_STATE, SSM_HPG, SSM_HEAD_DIM)
    ssm_state = ssm_state.transpose(0, 1, 3, 4, 2).reshape(b, SSM_HEADS, SSM_HEAD_DIM, SSM_STATE)
    states = (k_win.reshape(b, WIN_CACHE, N_KV, HEAD_DIM), v_win.reshape(b, WIN_CACHE, N_KV, HEAD_DIM),
              conv_state, ssm_state)
    return x1.reshape(b * s, d), h2, logits, gt_f, states


def _pad_lanes(a, width=LANES):
    return jnp.pad(a, [(0, 0)] * (a.ndim - 1) + [(0, width - a.shape[-1])])


def _largest_tile(n, cap):
    t = cap
    while n % t:
        t //= 2
    return t


def kernel(x_prompt, x_sample, c_prompt, c_sample, cache_k, cache_v, state_conv, state_ssm, w_mod, b_mod, g_mix_pre, g_mix_post, g_ffn_pre, g_ffn_post, w_in, conv_w, conv_b, dt_bias, a_log, d_skip, g_ssm, sinks, w_out, w_router, b_router, w_up, b_up, w_down, b_down):
    depth = w_mod.shape[0]
    assert depth == 1, "single-layer step"
    l = 0
    bp, sp, d = x_prompt.shape
    bs, ss, _ = x_sample.shape
    n_p, n_s = bp * sp, bs * ss
    n_tok = n_p + n_s

    c_all = jnp.concatenate([c_prompt, c_sample], axis=0)
    c_rows = -(-c_all.shape[0] // SUBLANES) * SUBLANES
    mod_all = _modulation(jnp.pad(c_all, ((0, c_rows - c_all.shape[0]), (0, 0))), w_mod[l], b_mod[l])

    row2 = lambda a: a.reshape(1, -1)
    paired = _paired_head_columns()
    wts = {
        "g_mix_pre": row2(g_mix_pre[l]), "g_mix_post": row2(g_mix_post[l]), "g_ffn_pre": row2(g_ffn_pre[l]),
        "w_main": jnp.concatenate([w_in[l][:, :ATTN_DIM][:, paired], w_in[l][:, ATTN_DIM:MAIN_DIM]],
                                  axis=1).astype(BF16),
        "w_dt": _pad_lanes(w_in[l][:, MAIN_DIM:]).astype(BF16),
        "sinks": sinks[l],
        "conv_w8": jnp.pad(conv_w[l], ((0, SUBLANES - CONV_W), (0, 0))), "conv_b": row2(conv_b[l]),
        "dt_bias": _pad_lanes(row2(dt_bias[l])), "a_log": _pad_lanes(row2(a_log[l])),
        "dskip_x": row2(jnp.repeat(d_skip[l], SSM_HEAD_DIM)), "g_ssm": row2(g_ssm[l]),
        "w_out": jnp.concatenate([w_out[l][:ATTN_DIM][paired], w_out[l][ATTN_DIM:]], axis=0).astype(BF16),
        "w_router": _pad_lanes(w_router[l]).astype(BF16), "b_router": _pad_lanes(row2(b_router[l])),
    }
    gw = SSM_HPG * SSM_HEAD_DIM

    hist_p = jnp.zeros((bp, SUBLANES, CONV_DIM), F32)
    h0_p = jnp.zeros((bp, SSM_GROUPS, SSM_STATE, gw), F32)
    x1_p, h2_p, lg_p, gtf_p, st_p = _mixer(x_prompt, mod_all[:bp], jnp.arange(sp, dtype=jnp.int32), hist_p, h0_p,
                                           SSD_CHUNK, _largest_tile(sp, 256), _largest_tile(sp, 256), wts, None)

    hist_s = jnp.pad(state_conv[l], ((0, 0), (SUBLANES - (CONV_W - 1), 0), (0, 0)))
    h0_s = state_ssm[l].astype(F32).reshape(bs, SSM_GROUPS, SSM_HPG, SSM_HEAD_DIM, SSM_STATE)
    h0_s = h0_s.transpose(0, 1, 4, 2, 3).reshape(bs, SSM_GROUPS, SSM_STATE, gw)
    cache = (cache_k[l].reshape(bs, WIN_CACHE, KV_DIM), cache_v[l].reshape(bs, WIN_CACHE, KV_DIM))
    x1_s, h2_s, lg_s, gtf_s, st_s = _mixer(x_sample, mod_all[bp:bp + bs], PAST_LEN + jnp.arange(ss, dtype=jnp.int32),
                                           hist_s, h0_s, ss, n_s, ss, wts, cache)

    tt = _largest_tile(n_s, 128)
    assert n_p % tt == 0
    meta, counts = _router(lg_p, lg_s, tt)
    top_idx = meta[:, LANE_IDX:LANE_IDX + TOP_K].astype(jnp.int32)
    rank = meta[:, LANE_RANK:LANE_RANK + TOP_K].astype(jnp.int32)
    cnt = counts[0, :N_EXPERTS].astype(jnp.int32)
    padded = (cnt + MOE_TM - 1) // MOE_TM * MOE_TM
    pend = jnp.cumsum(padded)
    offs = pend - padded
    pos = offs[top_idx] + rank
    n_blocks = -(-n_tok * TOP_K // MOE_TM) + N_EXPERTS
    n_slots = n_blocks * MOE_TM
    n_used = (pend[-1] // MOE_TM).astype(jnp.int32)
    blk = jnp.arange(n_blocks, dtype=jnp.int32)
    blk_row = jnp.minimum(blk, n_used - 1) * MOE_TM
    block_expert = jnp.minimum(jnp.sum(pend[None, :] <= blk_row[:, None], axis=1), N_EXPERTS - 1).astype(jnp.int32)
    zstart = jnp.where(cnt > 0, pend - MOE_TM, -1).astype(jnp.int32)
    pos3 = pos.reshape(n_tok // tt, 1, tt * TOP_K)
    nu = n_used.reshape(1)
    xs = _dispatch(h2_p, h2_s, pos3, zstart, nu, n_slots, tt)
    e_ids = jnp.arange(N_EXPERTS, dtype=jnp.int32)
    live = jnp.where(cnt > 0, e_ids, N_EXPERTS)
    later = jnp.concatenate([lax.cummin(live, reverse=True)[1:], jnp.full((1,), N_EXPERTS, jnp.int32)])
    next_live = jnp.where(later < N_EXPERTS, later, -1).astype(jnp.int32)
    visit = (jnp.cumsum((cnt > 0).astype(jnp.int32)) - 1) % 2
    plan = (block_expert, next_live[block_expert], visit[block_expert].astype(jnp.int32), nu)
    act = _grouped(_up_kernel, "expert_up", xs, w_up[l], b_up[l].reshape(N_EXPERTS, 1, -1), plan, (D_FF,), BF16)
    y = _grouped(_down_kernel, "expert_down", act, w_down[l], b_down[l].reshape(N_EXPERTS, 1, -1), plan,
                 xs.shape[1:], F32)

    g_post = row2(g_ffn_post[l])
    tt_p = _largest_tile(sp, 128)
    y_p = _combine(y, pos[:n_p].reshape(n_p // tt_p, 1, tt_p * TOP_K), meta[:n_p], x1_p,
                   gtf_p, g_post, tt_p).reshape(bp, sp, d)
    y_s = _combine(y, pos[n_p:].reshape(bs, 1, ss * TOP_K), meta[n_p:], x1_s,
                   gtf_s, g_post, ss).reshape(bs, ss, d)

    stack = lambda a: a[None]
    return (y_p, y_s, stack(st_p[0]), stack(st_p[1]), stack(st_p[2]), stack(st_p[3]),
            stack(st_s[0]), stack(st_s[1]), stack(st_s[2]), stack(st_s[3]))
```

```python
import functools

import numpy as np
import jax
import jax.numpy as jnp
from jax import lax
from jax.experimental import pallas as pl
from jax.experimental.pallas import tpu as pltpu

F32 = jnp.float32
BF16 = jnp.bfloat16

D_MODEL = 2048
CHUNK = 64
N_HEADS = 16
N_KV = 4
HEAD_DIM = 64
GQA = N_HEADS // N_KV
ATTN_DIM = N_HEADS * HEAD_DIM
KV_DIM = N_KV * HEAD_DIM
WINDOW = 128
WIN_CHUNKS = WINDOW // CHUNK
PAST_LEN = 4096
WIN_CACHE = min(WINDOW, PAST_LEN)
ROT_DIM = HEAD_DIM // 4
ROPE_THETA = 500000.0
SSM_HEADS = 16
SSM_HEAD_DIM = 64
SSM_DIM = SSM_HEADS * SSM_HEAD_DIM
SSM_GROUPS = 2
SSM_HPG = SSM_HEADS // SSM_GROUPS
SSM_STATE = 128
CONV_W = 4
CONV_DIM = SSM_DIM + 2 * SSM_GROUPS * SSM_STATE
SSD_CHUNK = 64
N_EXPERTS = 32
TOP_K = 4
D_FF = 2048
SWIGLU_ALPHA = 1.702
SWIGLU_LIMIT = 7.0
NORM_EPS = 1e-6

LANES = 128
SUBLANES = 8
MAIN_DIM = ATTN_DIM + 2 * KV_DIM + SSM_DIM + CONV_DIM
MOE_TM = 256
VMEM_LIMIT = 56 * 1024 * 1024


def _sigmoid(x):
    return 1.0 / (1.0 + jnp.exp(-x))


def _silu(x):
    return x * _sigmoid(x)


def _rms(x, g):
    return x * lax.rsqrt(jnp.mean(x * x, axis=-1, keepdims=True) + NORM_EPS) * g


def _split3(x):
    hi = x.astype(BF16)
    r1 = x - hi.astype(F32)
    mid = r1.astype(BF16)
    lo = (r1 - mid.astype(F32)).astype(BF16)
    return hi, mid, lo


def _dot_exact_rhs(x, m):
    hi, mid, lo = _split3(x)
    d = functools.partial(jnp.dot, preferred_element_type=F32)
    return d(hi, m) + d(mid, m) + d(lo, m)


_NN = (((1,), (0,)), ((), ()))
_NT = (((1,), (1,)), ((), ()))
_TN = (((0,), (0,)), ((), ()))


def _dot(a, b, dims=_NN):
    assert a.dtype == b.dtype, (a.dtype, b.dtype)
    prec = lax.Precision.HIGHEST if a.dtype == F32 else None
    return lax.dot_general(a, b, dims, preferred_element_type=F32, precision=prec)


def _params(sem, vmem=VMEM_LIMIT):
    return pltpu.CompilerParams(dimension_semantics=sem, vmem_limit_bytes=vmem)


def _mod_kernel(c_ref, w_ref, b_ref, o_ref):
    o_ref[...] = _dot(_silu(c_ref[...]).astype(BF16), w_ref[...].astype(BF16)) + b_ref[...]


def _modulation(c_all, w_mod, b_mod):
    rows, d = c_all.shape
    n = w_mod.shape[1]
    tn = 1536
    return pl.pallas_call(
        _mod_kernel,
        grid=(n // tn,),
        in_specs=[pl.BlockSpec((rows, d), lambda j: (0, 0)),
                  pl.BlockSpec((d, tn), lambda j: (0, j)),
                  pl.BlockSpec((1, tn), lambda j: (0, j))],
        out_specs=pl.BlockSpec((rows, tn), lambda j: (0, j)),
        out_shape=jax.ShapeDtypeStruct((rows, n), F32),
        compiler_params=_params(("arbitrary",)),
        name="modulation",
    )(c_all, w_mod, b_mod.reshape(1, n))


def _rope(t, cos, s1, s2):
    outs = []
    for j in range(t.shape[1] // LANES):
        tj = t[:, j * LANES:(j + 1) * LANES]
        up = pltpu.roll(tj, LANES - ROT_DIM // 2, 1)
        dn = pltpu.roll(tj, ROT_DIM // 2, 1)
        outs.append(tj * cos + up * s1 + dn * s2)
    return jnp.concatenate(outs, axis=1)


def _inproj_kernel(x_ref, sc_ref, sh_ref, g_ref, w_ref, wdt_ref, cos_ref, s1_ref, s2_ref,
                   q_ref, k_ref, v_ref, z_ref, xbc_ref, dt_ref):
    h = _rms(x_ref[0], g_ref[...]) * sc_ref[0] + sh_ref[0]
    hb = h.astype(w_ref.dtype)
    cos, s1, s2 = cos_ref[...], s1_ref[...], s2_ref[...]
    step = 512

    def mm(lo, hi):
        return jnp.concatenate([_dot(hb, w_ref[:, c:min(c + step, hi)]) for c in range(lo, hi, step)], axis=1)

    o = 0
    q_ref[0] = (_rope(mm(o, o + ATTN_DIM), cos, s1, s2) * (HEAD_DIM ** -0.5)).astype(q_ref.dtype)
    o += ATTN_DIM
    k_ref[0] = _rope(mm(o, o + KV_DIM), cos, s1, s2)
    o += KV_DIM
    v_ref[0] = mm(o, o + KV_DIM)
    o += KV_DIM
    z_ref[0] = mm(o, o + SSM_DIM)
    o += SSM_DIM
    xbc_ref[0] = mm(o, o + CONV_DIM)
    dt_ref[0] = _dot(hb, wdt_ref[...])


def _mod_spec(m, tm):
    if m.shape[1] == 1:
        return pl.BlockSpec((1, 1, m.shape[2]), lambda bi, i: (bi, 0, 0))
    return pl.BlockSpec((1, tm, m.shape[2]), lambda bi, i: (bi, i, 0))


def _in_projection(x, sc1p, sh, g, w_main, w_dt, rope_tabs, tm):
    b, s, d = x.shape
    cos, s1, s2 = rope_tabs
    row = lambda bi, i: (bi, i, 0)
    const2 = lambda bi, i: (0, 0)
    tab = lambda bi, i: (i, 0)
    widths = (ATTN_DIM, KV_DIM, KV_DIM, SSM_DIM, CONV_DIM, LANES)
    dtypes = (w_main.dtype, F32, F32, F32, F32, F32)
    return pl.pallas_call(
        _inproj_kernel,
        grid=(b, s // tm),
        in_specs=[pl.BlockSpec((1, tm, d), row),
                  _mod_spec(sc1p, tm),
                  _mod_spec(sh, tm),
                  pl.BlockSpec((1, d), const2),
                  pl.BlockSpec((d, MAIN_DIM), const2, pipeline_mode=pl.Buffered(1)),
                  pl.BlockSpec((d, LANES), const2, pipeline_mode=pl.Buffered(1)),
                  pl.BlockSpec((tm, LANES), tab),
                  pl.BlockSpec((tm, LANES), tab),
                  pl.BlockSpec((tm, LANES), tab)],
        out_specs=[pl.BlockSpec((1, tm, w), row) for w in widths],
        out_shape=[jax.ShapeDtypeStruct((b, s, w), dt) for w, dt in zip(widths, dtypes)],
        compiler_params=_params(("arbitrary", "arbitrary")),
        name="in_projection",
    )(x, sc1p, sh, g, w_main, w_dt, cos, s1, s2)


def _rope_tables(pos):
    inv_freq = ROPE_THETA ** (-jnp.arange(0, ROT_DIM, 2, dtype=F32) / ROT_DIM)
    ang = pos.astype(F32)[:, None] * inv_freq[None, :]
    cos, sin = jnp.cos(ang), jnp.sin(ang)
    half = ROT_DIM // 2
    n = pos.shape[0]
    ones = jnp.ones((n, HEAD_DIM - ROT_DIM), F32)
    zeros_h = jnp.zeros((n, half), F32)
    zeros_r = jnp.zeros((n, HEAD_DIM - ROT_DIM), F32)
    c = jnp.concatenate([cos, cos, ones], axis=1)
    a = jnp.concatenate([-sin, zeros_h, zeros_r], axis=1)
    b = jnp.concatenate([zeros_h, sin, zeros_r], axis=1)
    rep = LANES // HEAD_DIM
    return tuple(jnp.tile(t, (1, rep)) for t in (c, a, b))


KV_PAIRS = N_KV // 2


def _paired_head_columns():
    cols = []
    for j in range(KV_PAIRS):
        for a in range(GQA):
            for g in (2 * j, 2 * j + 1):
                h = g * GQA + a
                cols.extend(range(h * HEAD_DIM, (h + 1) * HEAD_DIM))
    return np.asarray(cols, np.int32)


def _attend_pair(qcols, k2, v2, sink_row, valid):
    rows = qcols[0].shape[0]
    lo = lax.broadcasted_iota(jnp.int32, (rows, LANES), 1) < HEAD_DIM
    zero = jnp.zeros_like(qcols[0])
    lhs = jnp.concatenate([jnp.where(lo, qc, zero) for qc in qcols] + [jnp.where(lo, zero, qc) for qc in qcols], axis=0)
    s = _dot(k2, lhs, _NT)
    if valid is not None:
        s = jnp.where(valid, s, -1e30)
    m = jnp.maximum(jnp.max(s, axis=0, keepdims=True), sink_row)
    p = jnp.exp(s - m)
    den = jnp.sum(p, axis=0, keepdims=True) + jnp.exp(sink_row - m)
    o = _dot((p * (1.0 / den)).astype(v2.dtype), v2, _TN)
    half = GQA * rows
    return [jnp.where(lo, o[a * rows:(a + 1) * rows], o[half + a * rows:half + (a + 1) * rows]) for a in range(GQA)]


def _sink_row(sinks_ref, j, rows):
    heads = [(2 * j) * GQA + a for a in range(GQA)] + [(2 * j + 1) * GQA + a for a in range(GQA)]
    col = lax.broadcasted_iota(jnp.int32, (1, len(heads) * rows), 1) // rows
    out = jnp.zeros((1, len(heads) * rows), F32)
    for n, h in enumerate(heads):
        out = jnp.where(col == n, sinks_ref[h], out)
    return out


def _attn_prompt_kernel(sinks_ref, q_ref, km_ref, kh_ref, vm_ref, vh_ref, o_ref):
    i = pl.program_id(1)
    tq = q_ref.shape[1]
    chunks = tq // CHUNK
    kfull = jnp.concatenate([kh_ref[0], km_ref[0]], axis=0).astype(q_ref.dtype)
    vfull = jnp.concatenate([vh_ref[0], vm_ref[0]], axis=0).astype(q_ref.dtype)
    span = (WIN_CHUNKS + 1) * CHUNK
    col_chunk = lax.broadcasted_iota(jnp.int32, (span, 2 * GQA * CHUNK), 0) // CHUNK
    for j in range(KV_PAIRS):
        k2 = kfull[:, j * LANES:(j + 1) * LANES]
        v2 = vfull[:, j * LANES:(j + 1) * LANES]
        sink = _sink_row(sinks_ref, j, CHUNK)
        for c in range(chunks):
            r0 = c * CHUNK
            slabs = [(j * GQA + a) * LANES for a in range(GQA)]
            qcols = [q_ref[0, r0:r0 + CHUNK, sl:sl + LANES] for sl in slabs]
            valid = None
            if c < WIN_CHUNKS:
                valid = (i * chunks + c - WIN_CHUNKS + col_chunk) >= 0
            outs = _attend_pair(qcols, k2[r0:r0 + span], v2[r0:r0 + span], sink, valid)
            for sl, o in zip(slabs, outs):
                o_ref[0, r0:r0 + CHUNK, sl:sl + LANES] = o.astype(o_ref.dtype)


def _attention_prompt(q, k, v, sinks, tq):
    b, s, _ = q.shape
    halo = WIN_CHUNKS * CHUNK
    ratio = tq // halo
    main = lambda bi, i: (bi, i, 0)
    prev = lambda bi, i: (bi, jnp.maximum(i * ratio - 1, 0), 0)
    return pl.pallas_call(
        _attn_prompt_kernel,
        grid=(b, s // tq),
        in_specs=[pl.BlockSpec(memory_space=pltpu.SMEM),
                  pl.BlockSpec((1, tq, ATTN_DIM), main),
                  pl.BlockSpec((1, tq, KV_DIM), main),
                  pl.BlockSpec((1, halo, KV_DIM), prev),
                  pl.BlockSpec((1, tq, KV_DIM), main),
                  pl.BlockSpec((1, halo, KV_DIM), prev)],
        out_specs=pl.BlockSpec((1, tq, ATTN_DIM), main),
        out_shape=jax.ShapeDtypeStruct((b, s, ATTN_DIM), q.dtype),
        compiler_params=_params(("arbitrary", "arbitrary")),
        name="attention_prompt",
    )(sinks, q, k, k, v, v)


def _attn_sample_kernel(sinks_ref, q_ref, k_ref, v_ref, ck_ref, cv_ref, o_ref, kw_ref, vw_ref):
    s = q_ref.shape[1]
    kf = jnp.concatenate([ck_ref[0], k_ref[0]], axis=0)
    vf = jnp.concatenate([cv_ref[0], v_ref[0]], axis=0)
    n = kf.shape[0]
    kw_ref[0] = kf[n - WIN_CACHE:]
    vw_ref[0] = vf[n - WIN_CACHE:]
    kb, vb = kf.astype(q_ref.dtype), vf.astype(q_ref.dtype)
    for j in range(KV_PAIRS):
        slabs = [(j * GQA + a) * LANES for a in range(GQA)]
        qcols = [q_ref[0, :, sl:sl + LANES] for sl in slabs]
        outs = _attend_pair(qcols, kb[:, j * LANES:(j + 1) * LANES], vb[:, j * LANES:(j + 1) * LANES],
                            _sink_row(sinks_ref, j, s), None)
        for sl, o in zip(slabs, outs):
            o_ref[0, :, sl:sl + LANES] = o.astype(o_ref.dtype)


def _attention_sample(q, k, v, sinks, cache_k, cache_v):
    b, s, _ = q.shape
    blk = lambda bi: (bi, 0, 0)
    return pl.pallas_call(
        _attn_sample_kernel,
        grid=(b,),
        in_specs=[pl.BlockSpec(memory_space=pltpu.SMEM),
                  pl.BlockSpec((1, s, ATTN_DIM), blk),
                  pl.BlockSpec((1, s, KV_DIM), blk),
                  pl.BlockSpec((1, s, KV_DIM), blk),
                  pl.BlockSpec((1, WIN_CACHE, KV_DIM), blk),
                  pl.BlockSpec((1, WIN_CACHE, KV_DIM), blk)],
        out_specs=[pl.BlockSpec((1, s, ATTN_DIM), blk),
                   pl.BlockSpec((1, WIN_CACHE, KV_DIM), blk),
                   pl.BlockSpec((1, WIN_CACHE, KV_DIM), blk)],
        out_shape=[jax.ShapeDtypeStruct((b, s, ATTN_DIM), q.dtype),
                   jax.ShapeDtypeStruct((b, WIN_CACHE, KV_DIM), F32),
                   jax.ShapeDtypeStruct((b, WIN_CACHE, KV_DIM), F32)],
        compiler_params=_params(("arbitrary",)),
        name="attention_sample",
    )(sinks, q, k, v, cache_k, cache_v)


def _ssd_kernel(xbc_ref, z_ref, dt_ref, hist_ref, h0_ref, cw_ref, cb_ref, dtb_ref, alog_ref, dsk_ref, gs_ref,
                ep_ref, el_ref, dmask_ref, causal_ref, tril_ref,
                y_ref, conv_ref, hfin_ref, prev, ht):
    c = pl.program_id(1)
    last = pl.num_programs(1) - 1
    ln = xbc_ref.shape[1]

    @pl.when(c == 0)
    def _():
        prev[...] = hist_ref[...]
        ht[...] = h0_ref[...]

    for bi in range(xbc_ref.shape[0]):
        _ssd_chunk(bi, xbc_ref, z_ref, dt_ref, cw_ref, cb_ref, dtb_ref, alog_ref, dsk_ref, gs_ref,
                   ep_ref, el_ref, dmask_ref, causal_ref, tril_ref, y_ref, prev, ht)

    @pl.when(c == last)
    def _():
        hfin_ref[...] = ht[...]
        conv_ref[...] = xbc_ref[:, ln - (CONV_W - 1):, :]


def _ssd_chunk(bi, xbc_ref, z_ref, dt_ref, cw_ref, cb_ref, dtb_ref, alog_ref, dsk_ref, gs_ref,
               ep_ref, el_ref, dmask_ref, causal_ref, tril_ref, y_ref, prev, ht):
    ln = xbc_ref.shape[1]
    gw = SSM_HPG * SSM_HEAD_DIM
    xr = xbc_ref[bi]
    ext = jnp.concatenate([prev[bi], xr], axis=0)
    conv = cb_ref[...]
    for i in range(CONV_W):
        sh = CONV_W - 1 - i
        tap = xr if sh == 0 else pltpu.roll(ext, sh, 0)[SUBLANES:]
        conv = conv + tap * cw_ref[i:i + 1, :]
    prev[bi] = xr[ln - SUBLANES:]

    act = _silu(conv)
    xs = act[:, :SSM_DIM]
    bm = act[:, SSM_DIM:SSM_DIM + SSM_GROUPS * SSM_STATE]
    cm = act[:, SSM_DIM + SSM_GROUPS * SSM_STATE:]

    dtv = dt_ref[bi] + dtb_ref[...]
    dt = jnp.maximum(dtv, 0.0) + jnp.log1p(jnp.exp(-jnp.abs(dtv)))
    ad = dt * (-jnp.exp(alog_ref[...]))
    hi, mid, lo = _split3(ad)
    tril = tril_ref[...]
    d = functools.partial(jnp.dot, preferred_element_type=F32)
    a_cs = d(tril, hi) + d(tril, mid) + d(tril, lo)
    a_last = a_cs[ln - 1:ln, :]
    stacked = jnp.concatenate([dt, jnp.exp(a_cs), jnp.exp(a_last - a_cs)], axis=0)
    wide = _dot_exact_rhs(stacked, ep_ref[...])
    dt_x, ea_x, ds_x = wide[:ln], wide[ln:2 * ln], wide[2 * ln:]
    cd_x = ea_x[ln - 1:ln, :]

    a_l = _dot_exact_rhs(a_cs, el_ref[...])
    a_s = jnp.sum(a_l * dmask_ref[...], axis=0, keepdims=True)
    lmat = jnp.exp(jnp.where(causal_ref[...] > 0.0, a_l - a_s, -1e30))

    cdt = y_ref.dtype
    xd = xs * dt_x
    xds = (xd * ds_x).astype(cdt)
    xdb = xd.astype(cdt)
    ys = []
    for g in range(SSM_GROUPS):
        bg = bm[:, g * SSM_STATE:(g + 1) * SSM_STATE].astype(cdt)
        cg = cm[:, g * SSM_STATE:(g + 1) * SSM_STATE].astype(cdt)
        cbm = _dot(cg, bg, _NT)
        yd = []
        for r in range(SSM_HPG):
            hd = g * SSM_HPG + r
            w = (cbm * lmat[:, hd * ln:(hd + 1) * ln]).astype(cdt)
            yd.append(_dot(w, xdb[:, hd * SSM_HEAD_DIM:(hd + 1) * SSM_HEAD_DIM]))
        htg = ht[bi, g]
        y_off = _dot(cg, htg.astype(cdt)) * ea_x[:, g * gw:(g + 1) * gw]
        st = _dot(bg, xds[:, g * gw:(g + 1) * gw], _TN)
        ht[bi, g] = htg * cd_x[:, g * gw:(g + 1) * gw] + st
        ys.append(jnp.concatenate(yd, axis=1) + y_off)
    y = jnp.concatenate(ys, axis=1) + dsk_ref[...] * xs
    y = y * _silu(z_ref[bi])
    outs = [_rms(y[:, g * gw:(g + 1) * gw], gs_ref[:, g * gw:(g + 1) * gw]) for g in range(SSM_GROUPS)]
    y_ref[bi] = jnp.concatenate(outs, axis=1).astype(y_ref.dtype)


def _ssd_constants(ln):
    heads = SSM_HEADS
    ep = np.zeros((LANES, heads * SSM_HEAD_DIM), np.float32)
    el = np.zeros((LANES, heads * ln), np.float32)
    for r in range(heads):
        ep[r, r * SSM_HEAD_DIM:(r + 1) * SSM_HEAD_DIM] = 1.0
        el[r, r * ln:(r + 1) * ln] = 1.0
    eye = np.tile(np.eye(ln, dtype=np.float32), (1, heads))
    causal = np.tile(np.tril(np.ones((ln, ln), np.float32)), (1, heads))
    tril = np.tril(np.ones((ln, ln), np.float32))
    return (jnp.asarray(ep, BF16), jnp.asarray(el, BF16), jnp.asarray(eye), jnp.asarray(causal),
            jnp.asarray(tril, BF16))


def _ssd_mixer(xbc, z, dt_raw, hist8, h0t, conv_w8, conv_b, dt_bias, a_log, dskip_x, g_ssm, ln, out_dtype):
    b, s, _ = xbc.shape
    bb = 2 if b % 2 == 0 else 1
    consts = _ssd_constants(ln)
    row = lambda bi, c: (bi, c, 0)
    per_b3 = lambda bi, c: (bi, 0, 0)
    per_b4 = lambda bi, c: (bi, 0, 0, 0)
    const2 = lambda bi, c: (0, 0)
    full = lambda a: pl.BlockSpec(a.shape, const2)
    gw = SSM_HPG * SSM_HEAD_DIM
    return pl.pallas_call(
        _ssd_kernel,
        grid=(b // bb, s // ln),
        in_specs=[pl.BlockSpec((bb, ln, CONV_DIM), row),
                  pl.BlockSpec((bb, ln, SSM_DIM), row),
                  pl.BlockSpec((bb, ln, LANES), row),
                  pl.BlockSpec((bb, SUBLANES, CONV_DIM), per_b3),
                  pl.BlockSpec((bb, SSM_GROUPS, SSM_STATE, gw), per_b4),
                  full(conv_w8), full(conv_b), full(dt_bias), full(a_log), full(dskip_x), full(g_ssm)]
                 + [full(a) for a in consts],
        out_specs=[pl.BlockSpec((bb, ln, SSM_DIM), row),
                   pl.BlockSpec((bb, CONV_W - 1, CONV_DIM), per_b3),
                   pl.BlockSpec((bb, SSM_GROUPS, SSM_STATE, gw), per_b4)],
        out_shape=[jax.ShapeDtypeStruct((b, s, SSM_DIM), out_dtype),
                   jax.ShapeDtypeStruct((b, CONV_W - 1, CONV_DIM), F32),
                   jax.ShapeDtypeStruct((b, SSM_GROUPS, SSM_STATE, gw), F32)],
        scratch_shapes=[pltpu.VMEM((bb, SUBLANES, CONV_DIM), F32),
                        pltpu.VMEM((bb, SSM_GROUPS, SSM_STATE, gw), F32)],
        compiler_params=_params(("arbitrary", "arbitrary")),
        name="ssd_mixer",
    )(xbc, z, dt_raw, hist8, h0t, conv_w8, conv_b, dt_bias, a_log, dskip_x, g_ssm, *consts)


def _outproj_kernel(a_ref, s_ref, x_ref, w_ref, gpost_ref, gate_ref, gpre_ref, sc_ref, sh_ref, wr_ref, br_ref,
                    x1_ref, h2_ref, lg_ref):
    mix = _dot(a_ref[0], w_ref[:ATTN_DIM, :]) + _dot(s_ref[0], w_ref[ATTN_DIM:, :])
    x1 = x_ref[0] + gate_ref[0] * _rms(mix, gpost_ref[...])
    x1_ref[0] = x1
    h2 = _rms(x1, gpre_ref[...]) * sc_ref[0] + sh_ref[0]
    h2_ref[...] = h2
    lg_ref[...] = _dot(h2.astype(wr_ref.dtype), wr_ref[...]) + br_ref[...]


def _out_projection(attn, ssm, x, w_out, g_post, gate, g_pre, sc1p, sh, w_router, b_router, tm):
    b, s, d = x.shape
    nt = s // tm
    row = lambda bi, i: (bi, i, 0)
    const2 = lambda bi, i: (0, 0)
    flat = lambda bi, i: (bi * nt + i, 0)
    return pl.pallas_call(
        _outproj_kernel,
        grid=(b, nt),
        in_specs=[pl.BlockSpec((1, tm, ATTN_DIM), row),
                  pl.BlockSpec((1, tm, SSM_DIM), row),
                  pl.BlockSpec((1, tm, d), row),
                  pl.BlockSpec((ATTN_DIM + SSM_DIM, d), const2, pipeline_mode=pl.Buffered(1)),
                  pl.BlockSpec((1, d), const2),
                  _mod_spec(gate, tm),
                  pl.BlockSpec((1, d), const2),
                  _mod_spec(sc1p, tm),
                  _mod_spec(sh, tm),
                  pl.BlockSpec((d, LANES), const2),
                  pl.BlockSpec((1, LANES), const2)],
        out_specs=[pl.BlockSpec((1, tm, d), row),
                   pl.BlockSpec((tm, d), flat),
                   pl.BlockSpec((tm, LANES), flat)],
        out_shape=[jax.ShapeDtypeStruct((b, s, d), F32),
                   jax.ShapeDtypeStruct((b * s, d), F32),
                   jax.ShapeDtypeStruct((b * s, LANES), F32)],
        compiler_params=_params(("arbitrary", "arbitrary")),
        name="out_projection",
    )(attn, ssm, x, w_out, g_post, gate, g_pre, sc1p, sh, w_router, b_router)


LANE_IDX, LANE_RANK, LANE_GATE = 0, TOP_K, 2 * TOP_K


def _router_kernel(lp_ref, ls_ref, ltri_ref, meta_ref, cnt_ref, *, p_tiles):
    i = pl.program_id(0)
    tt = lp_ref.shape[0]

    @pl.when(i == 0)
    def _():
        cnt_ref[...] = jnp.zeros_like(cnt_ref)

    logits = jnp.where(i < p_tiles, lp_ref[...], ls_ref[...])
    lane = lax.broadcasted_iota(jnp.int32, (tt, LANES), 1)
    lane_f = lane.astype(F32)
    work = jnp.where(lane < N_EXPERTS, logits, -jnp.inf)
    vals, hots, idxs = [], [], []
    for _ in range(TOP_K):
        m = jnp.max(work, axis=-1, keepdims=True)
        idx = jnp.min(jnp.where(work == m, lane_f, float(LANES)), axis=-1, keepdims=True)
        hot = lane_f == idx
        vals.append(m)
        hots.append(hot)
        idxs.append(idx)
        work = jnp.where(hot, -jnp.inf, work)
    es = [jnp.exp(v - vals[0]) for v in vals]
    den = es[0] + es[1] + es[2] + es[3]
    onehot = jnp.zeros((tt, LANES), F32)
    for hot in hots:
        onehot = jnp.where(hot, 1.0, onehot)
    before = jnp.dot(ltri_ref[...], onehot.astype(BF16), preferred_element_type=F32) + cnt_ref[0:1, :]
    meta = jnp.zeros((tt, LANES), F32)
    for k in range(TOP_K):
        rank_k = jnp.sum(jnp.where(hots[k], before, 0.0), axis=-1, keepdims=True)
        meta = jnp.where(lane == LANE_IDX + k, idxs[k], meta)
        meta = jnp.where(lane == LANE_RANK + k, rank_k, meta)
        meta = jnp.where(lane == LANE_GATE + k, es[k] / den, meta)
    meta_ref[...] = meta
    cnt_ref[...] = cnt_ref[...] + jnp.sum(onehot, axis=0, keepdims=True)


def _token_specs(p_tiles, tt, trailing):
    zeros = (0,) * len(trailing)
    return [pl.BlockSpec((tt,) + trailing, lambda i, *_: (jnp.minimum(i, p_tiles - 1),) + zeros),
            pl.BlockSpec((tt,) + trailing, lambda i, *_: (jnp.maximum(i - p_tiles, 0),) + zeros)]


def _router(logits_p, logits_s, tt):
    n = logits_p.shape[0] + logits_s.shape[0]
    p_tiles = logits_p.shape[0] // tt
    ltri = jnp.asarray(np.tril(np.ones((tt, tt), np.float32), -1), BF16)
    return pl.pallas_call(
        functools.partial(_router_kernel, p_tiles=p_tiles),
        grid=(n // tt,),
        in_specs=_token_specs(p_tiles, tt, (LANES,)) + [pl.BlockSpec((tt, tt), lambda i: (0, 0))],
        out_specs=[pl.BlockSpec((tt, LANES), lambda i: (i, 0)),
                   pl.BlockSpec((SUBLANES, LANES), lambda i: (0, 0))],
        out_shape=[jax.ShapeDtypeStruct((n, LANES), F32),
                   jax.ShapeDtypeStruct((SUBLANES, LANES), F32)],
        compiler_params=_params(("arbitrary",)),
        name="router",
    )(logits_p, logits_s, ltri)


def _dispatch_kernel(zstart_ref, nu_ref, pos_ref, hp_ref, hs_ref, xs_ref, zeros, sem_z, sem_r, *, p_tiles):
    i = pl.program_id(0)
    tt = hp_ref.shape[0]
    tm = zeros.shape[0]
    n_blocks = xs_ref.shape[0] // tm

    def zero_copy(row):
        return pltpu.make_async_copy(zeros, xs_ref.at[pl.ds(pl.multiple_of(row, tm), tm)], sem_z)

    @pl.when(i == 0)
    def _():
        zeros[...] = jnp.zeros_like(zeros)

        def per_expert(act):
            def body(e, carry):
                @pl.when(zstart_ref[e] >= 0)
                def _():
                    act(zero_copy(zstart_ref[e]))
                return carry
            lax.fori_loop(0, N_EXPERTS, body, 0)

        def per_tail(act):
            def body(blk, carry):
                act(zero_copy(blk * tm))
                return carry
            lax.fori_loop(nu_ref[0], n_blocks, body, 0)

        per_expert(lambda cp: cp.start())
        per_tail(lambda cp: cp.start())
        per_expert(lambda cp: cp.wait())
        per_tail(lambda cp: cp.wait())

    def scatter_rows(h_ref):
        def row_copy(t, k):
            return pltpu.make_async_copy(h_ref.at[pl.ds(t, 1)],
                                         xs_ref.at[pl.ds(pos_ref[0, 0, t * TOP_K + k], 1)], sem_r)

        copies = [row_copy(t, k) for t in range(tt) for k in range(TOP_K)]
        for cp in copies:
            cp.start()
        for cp in copies:
            cp.wait()

    @pl.when(i < p_tiles)
    def _():
        scatter_rows(hp_ref)

    @pl.when(i >= p_tiles)
    def _():
        scatter_rows(hs_ref)


def _dispatch(h_p, h_s, pos3, zstart, n_used, n_slots, tt):
    row_shape = h_p.shape[1:]
    n = h_p.shape[0] + h_s.shape[0]
    p_tiles = h_p.shape[0] // tt
    return pl.pallas_call(
        functools.partial(_dispatch_kernel, p_tiles=p_tiles),
        grid_spec=pltpu.PrefetchScalarGridSpec(
            num_scalar_prefetch=2,
            grid=(n // tt,),
            in_specs=[pl.BlockSpec((1, 1, tt * TOP_K), lambda i, *_: (i, 0, 0), memory_space=pltpu.SMEM)]
                     + _token_specs(p_tiles, tt, row_shape),
            out_specs=pl.BlockSpec(memory_space=pl.ANY),
            scratch_shapes=[pltpu.VMEM((MOE_TM,) + row_shape, F32),
                            pltpu.SemaphoreType.DMA(()),
                            pltpu.SemaphoreType.DMA(())]),
        out_shape=jax.ShapeDtypeStruct((n_slots,) + row_shape, F32),
        compiler_params=_params(("arbitrary",)),
        name="dispatch",
    )(zstart, n_used, pos3, h_p, h_s)


W_CHUNK_ROWS = 256


def _stage_expert_weights(i, be_ref, nxt_ref, slot_ref, w_hbm, wbuf, stg, sem, done):
    rows = W_CHUNK_ROWS
    chunks = wbuf.shape[1] // rows
    e, e_next, slot = be_ref[i], nxt_ref[i], slot_ref[i]
    first = jnp.logical_or(i == 0, be_ref[jnp.maximum(i - 1, 0)] != e)

    def chunk_copy(expert, c, buf):
        return pltpu.make_async_copy(w_hbm.at[expert, pl.ds(pl.multiple_of(c * rows, rows), rows)],
                                     stg.at[buf], sem.at[buf])

    def convert_next_chunk(expert, into):
        c = done[0]
        buf = c % 2
        chunk_copy(expert, c, buf).wait()

        @pl.when(c + 1 < chunks)
        def _():
            chunk_copy(expert, c + 1, 1 - buf).start()

        wbuf[into, pl.ds(pl.multiple_of(c * rows, rows), rows), :] = stg[buf].astype(wbuf.dtype)
        done[0] = c + 1

    @pl.when(i == 0)
    def _():
        done[0] = 0
        chunk_copy(e, 0, 0).start()

    @pl.when(first)
    def _():
        def body(_, carry):
            convert_next_chunk(e, slot)
            return carry

        lax.fori_loop(done[0], chunks, body, 0)

        @pl.when(e_next >= 0)
        def _():
            done[0] = 0
            chunk_copy(e_next, 0, 0).start()

    @pl.when(jnp.logical_and(jnp.logical_not(first), jnp.logical_and(e_next >= 0, done[0] < chunks)))
    def _():
        convert_next_chunk(e_next, 1 - slot)

    return slot


def _up_kernel(be_ref, nxt_ref, slot_ref, nu_ref, x_ref, w_hbm, b_ref, act_ref, wbuf, stg, sem, done):
    i = pl.program_id(0)
    fc = 512

    @pl.when(i < nu_ref[0])
    def _():
        slot = _stage_expert_weights(i, be_ref, nxt_ref, slot_ref, w_hbm, wbuf, stg, sem, done)
        x = x_ref[...].astype(wbuf.dtype)
        for c in range(D_FF // fc):
            glu = _dot(x, wbuf[slot, :, c * fc:(c + 1) * fc]) + b_ref[:, c * fc:(c + 1) * fc]
            lin = (_dot(x, wbuf[slot, :, D_FF + c * fc:D_FF + (c + 1) * fc])
                   + b_ref[:, D_FF + c * fc:D_FF + (c + 1) * fc])
            glu = jnp.minimum(glu, SWIGLU_LIMIT)
            lin = jnp.clip(lin, -SWIGLU_LIMIT, SWIGLU_LIMIT)
            act_ref[:, c * fc:(c + 1) * fc] = (glu * _sigmoid(SWIGLU_ALPHA * glu) * (lin + 1.0)).astype(act_ref.dtype)

    @pl.when(i >= nu_ref[0])
    def _():
        act_ref[...] = jnp.zeros_like(act_ref)


def _down_kernel(be_ref, nxt_ref, slot_ref, nu_ref, a_ref, w_hbm, b_ref, y_ref, wbuf, stg, sem, done):
    i = pl.program_id(0)

    @pl.when(i < nu_ref[0])
    def _():
        slot = _stage_expert_weights(i, be_ref, nxt_ref, slot_ref, w_hbm, wbuf, stg, sem, done)
        y_ref[...] = _dot(a_ref[...], wbuf[slot]) + b_ref[...]

    @pl.when(i >= nu_ref[0])
    def _():
        y_ref[...] = jnp.zeros_like(y_ref)


def _grouped(kernel, name, x, w, b, plan, out_tail, out_dtype, extra_scratch=()):
    block_expert, next_expert, slot, n_used = plan
    n_slots = x.shape[0]
    _, kdim, ndim = w.shape
    nb = n_slots // MOE_TM
    zeros = lambda t: (0,) * len(t)
    x_tail, o_tail = x.shape[1:], tuple(out_tail)
    return pl.pallas_call(
        kernel,
        grid_spec=pltpu.PrefetchScalarGridSpec(
            num_scalar_prefetch=4,
            grid=(nb,),
            in_specs=[pl.BlockSpec((MOE_TM,) + x_tail, lambda i, be, nx, sl, nu: (jnp.minimum(i, nu[0] - 1),) + zeros(x_tail)),
                      pl.BlockSpec(memory_space=pl.ANY),
                      pl.BlockSpec((None, 1, ndim), lambda i, be, nx, sl, nu: (be[i], 0, 0))],
            out_specs=pl.BlockSpec((MOE_TM,) + o_tail, lambda i, be, nx, sl, nu: (i,) + zeros(o_tail)),
            scratch_shapes=[pltpu.VMEM((2, kdim, ndim), BF16),
                            pltpu.VMEM((2, W_CHUNK_ROWS, ndim), F32),
                            pltpu.SemaphoreType.DMA((2,)),
                            pltpu.SMEM((1,), jnp.int32)] + list(extra_scratch)),
        out_shape=jax.ShapeDtypeStruct((n_slots,) + o_tail, out_dtype),
        compiler_params=_params(("arbitrary",), 60 * 1024 * 1024),
        name=name,
    )(block_expert, next_expert, slot, n_used, x, w, b)


def _combine_kernel(pos_ref, pos_next_ref, y_ref, meta_ref, x1_ref, gate_ref, g_ref, o_ref, rows, sem):
    i = pl.program_id(0)
    tt = x1_ref.shape[0]
    buf = i % 2

    def copies(p_ref, b):
        return [pltpu.make_async_copy(y_ref.at[pl.ds(p_ref[0, 0, t * TOP_K + k], 1)],
                                      rows.at[b, k, pl.ds(t, 1)], sem.at[b])
                for t in range(tt) for k in range(TOP_K)]

    @pl.when(i == 0)
    def _():
        for cp in copies(pos_ref, 0):
            cp.start()

    @pl.when(i + 1 < pl.num_programs(0))
    def _():
        for cp in copies(pos_next_ref, 1 - buf):
            cp.start()

    for cp in copies(pos_ref, buf):
        cp.wait()
    meta = meta_ref[...]
    f = rows[buf, 0] * meta[:, LANE_GATE:LANE_GATE + 1]
    for k in range(1, TOP_K):
        f = f + rows[buf, k] * meta[:, LANE_GATE + k:LANE_GATE + k + 1]
    o_ref[...] = x1_ref[...] + gate_ref[0] * _rms(f, g_ref[...])


def _combine(y, pos3, meta, x1, gate, g_post, tt):
    n, d = x1.shape
    nb = gate.shape[0]
    tiles_per_b = n // tt // nb
    tiles = n // tt
    return pl.pallas_call(
        _combine_kernel,
        grid=(tiles,),
        in_specs=[pl.BlockSpec((1, 1, tt * TOP_K), lambda i: (i, 0, 0), memory_space=pltpu.SMEM),
                  pl.BlockSpec((1, 1, tt * TOP_K), lambda i: (jnp.minimum(i + 1, tiles - 1), 0, 0),
                               memory_space=pltpu.SMEM),
                  pl.BlockSpec(memory_space=pl.ANY),
                  pl.BlockSpec((tt, LANES), lambda i: (i, 0)),
                  pl.BlockSpec((tt, d), lambda i: (i, 0)),
                  pl.BlockSpec((1, 1, d), lambda i: (i // tiles_per_b, 0, 0)),
                  pl.BlockSpec((1, d), lambda i: (0, 0))],
        out_specs=pl.BlockSpec((tt, d), lambda i: (i, 0)),
        out_shape=jax.ShapeDtypeStruct((n, d), F32),
        scratch_shapes=[pltpu.VMEM((2, TOP_K, tt) + y.shape[1:], F32), pltpu.SemaphoreType.DMA((2,))],
        compiler_params=_params(("arbitrary",)),
        name="combine",
    )(pos3, pos3, y, meta, x1, gate, g_post)


def _mixer(x, mod, pos, hist8, h0t, ln, tm, tq, wts, cache):
    b, s, d = x.shape
    mods = [m.reshape(b, 1, d) for m in jnp.split(mod, 6, axis=-1)]
    gt_f = mods[5]
    if cache is None:
        xf, fb, fs = x, b, s
    else:
        fb, fs = 1, b * s
        xf = x.reshape(fb, fs, d)
        pos = jnp.tile(pos, b)
        mods = [jnp.broadcast_to(m, (b, s, d)).reshape(fb, fs, d) for m in mods]
    sh_m, sc_m, gt_m, sh_f, sc_f, _ = mods
    proj = _in_projection(xf, 1.0 + sc_m, sh_m, wts["g_mix_pre"], wts["w_main"], wts["w_dt"], _rope_tables(pos), tm)
    q, k, v, z, xbc, dt_raw = [t.reshape(b, s, t.shape[-1]) for t in proj]
    if cache is None:
        attn = _attention_prompt(q, k, v, wts["sinks"], tq)
        k_win, v_win = k[:, s - WIN_CACHE:], v[:, s - WIN_CACHE:]
    else:
        attn, k_win, v_win = _attention_sample(q, k, v, wts["sinks"], *cache)
    ssm, conv_state, hfin = _ssd_mixer(xbc, z, dt_raw, hist8, h0t, wts["conv_w8"], wts["conv_b"], wts["dt_bias"],
                                       wts["a_log"], wts["dskip_x"], wts["g_ssm"], ln, q.dtype)
    x1, h2, logits = _out_projection(attn.reshape(fb, fs, -1), ssm.reshape(fb, fs, -1), xf, wts["w_out"],
                                     wts["g_mix_post"], gt_m, wts["g_ffn_pre"], 1.0 + sc_f, sh_f,
                                     wts["w_router"], wts["b_router"], tm)
    ssm_state = hfin.reshape(b, SSM_GROUPS, SSM_STATE, SSM_HPG, SSM_HEAD_DIM)
    ssm_state = ssm_state.transpose(0, 1, 3, 4, 2).reshape(b, SSM_HEADS, SSM_HEAD_DIM, SSM_STATE)
    states = (k_win.reshape(b, WIN_CACHE, N_KV, HEAD_DIM), v_win.reshape(b, WIN_CACHE, N_KV, HEAD_DIM),
              conv_state, ssm_state)
    return x1.reshape(b * s, d), h2, logits, gt_f, states


def _pad_lanes(a, width=LANES):
    return jnp.pad(a, [(0, 0)] * (a.ndim - 1) + [(0, width - a.shape[-1])])


def _largest_tile(n, cap):
    t = cap
    while n % t:
        t //= 2
    return t


def kernel(x_prompt, x_sample, c_prompt, c_sample, cache_k, cache_v, state_conv, state_ssm, w_mod, b_mod, g_mix_pre, g_mix_post, g_ffn_pre, g_ffn_post, w_in, conv_w, conv_b, dt_bias, a_log, d_skip, g_ssm, sinks, w_out, w_router, b_router, w_up, b_up, w_down, b_down):
    depth = w_mod.shape[0]
    assert depth == 1, "single-layer step"
    l = 0
    bp, sp, d = x_prompt.shape
    bs, ss, _ = x_sample.shape
    n_p, n_s = bp * sp, bs * ss
    n_tok = n_p + n_s

    c_all = jnp.concatenate([c_prompt, c_sample], axis=0)
    c_rows = -(-c_all.shape[0] // SUBLANES) * SUBLANES
    mod_all = _modulation(jnp.pad(c_all, ((0, c_rows - c_all.shape[0]), (0, 0))), w_mod[l], b_mod[l])

    row2 = lambda a: a.reshape(1, -1)
    paired = _paired_head_columns()
    wts = {
        "g_mix_pre": row2(g_mix_pre[l]), "g_mix_post": row2(g_mix_post[l]), "g_ffn_pre": row2(g_ffn_pre[l]),
        "w_main": jnp.concatenate([w_in[l][:, :ATTN_DIM][:, paired], w_in[l][:, ATTN_DIM:MAIN_DIM]],
                                  axis=1).astype(BF16),
        "w_dt": _pad_lanes(w_in[l][:, MAIN_DIM:]).astype(BF16),
        "sinks": sinks[l],
        "conv_w8": jnp.pad(conv_w[l], ((0, SUBLANES - CONV_W), (0, 0))), "conv_b": row2(conv_b[l]),
        "dt_bias": _pad_lanes(row2(dt_bias[l])), "a_log": _pad_lanes(row2(a_log[l])),
        "dskip_x": row2(jnp.repeat(d_skip[l], SSM_HEAD_DIM)), "g_ssm": row2(g_ssm[l]),
        "w_out": jnp.concatenate([w_out[l][:ATTN_DIM][paired], w_out[l][ATTN_DIM:]], axis=0).astype(BF16),
        "w_router": _pad_lanes(w_router[l]).astype(BF16), "b_router": _pad_lanes(row2(b_router[l])),
    }
    gw = SSM_HPG * SSM_HEAD_DIM

    hist_p = jnp.zeros((bp, SUBLANES, CONV_DIM), F32)
    h0_p = jnp.zeros((bp, SSM_GROUPS, SSM_STATE, gw), F32)
    x1_p, h2_p, lg_p, gtf_p, st_p = _mixer(x_prompt, mod_all[:bp], jnp.arange(sp, dtype=jnp.int32), hist_p, h0_p,
                                           SSD_CHUNK, _largest_tile(sp, 256), _largest_tile(sp, 256), wts, None)

    hist_s = jnp.pad(state_conv[l], ((0, 0), (SUBLANES - (CONV_W - 1), 0), (0, 0)))
    h0_s = state_ssm[l].astype(F32).reshape(bs, SSM_GROUPS, SSM_HPG, SSM_HEAD_DIM, SSM_STATE)
    h0_s = h0_s.transpose(0, 1, 4, 2, 3).reshape(bs, SSM_GROUPS, SSM_STATE, gw)
    cache = (cache_k[l].reshape(bs, WIN_CACHE, KV_DIM), cache_v[l].reshape(bs, WIN_CACHE, KV_DIM))
    x1_s, h2_s, lg_s, gtf_s, st_s = _mixer(x_sample, mod_all[bp:bp + bs], PAST_LEN + jnp.arange(ss, dtype=jnp.int32),
                                           hist_s, h0_s, ss, n_s, ss, wts, cache)

    tt = _largest_tile(n_s, 128)
    assert n_p % tt == 0
    meta, counts = _router(lg_p, lg_s, tt)
    top_idx = meta[:, LANE_IDX:LANE_IDX + TOP_K].astype(jnp.int32)
    rank = meta[:, LANE_RANK:LANE_RANK + TOP_K].astype(jnp.int32)
    cnt = counts[0, :N_EXPERTS].astype(jnp.int32)
    padded = (cnt + MOE_TM - 1) // MOE_TM * MOE_TM
    pend = jnp.cumsum(padded)
    offs = pend - padded
    pos = offs[top_idx] + rank
    n_blocks = -(-n_tok * TOP_K // MOE_TM) + N_EXPERTS
    n_slots = n_blocks * MOE_TM
    n_used = (pend[-1] // MOE_TM).astype(jnp.int32)
    blk = jnp.arange(n_blocks, dtype=jnp.int32)
    blk_row = jnp.minimum(blk, n_used - 1) * MOE_TM
    block_expert = jnp.minimum(jnp.sum(pend[None, :] <= blk_row[:, None], axis=1), N_EXPERTS - 1).astype(jnp.int32)
    zstart = jnp.where(cnt > 0, pend - MOE_TM, -1).astype(jnp.int32)
    pos3 = pos.reshape(n_tok // tt, 1, tt * TOP_K)
    nu = n_used.reshape(1)
    xs = _dispatch(h2_p, h2_s, pos3, zstart, nu, n_slots, tt)
    e_ids = jnp.arange(N_EXPERTS, dtype=jnp.int32)
    live = jnp.where(cnt > 0, e_ids, N_EXPERTS)
    later = jnp.concatenate([lax.cummin(live, reverse=True)[1:], jnp.full((1,), N_EXPERTS, jnp.int32)])
    next_live = jnp.where(later < N_EXPERTS, later, -1).astype(jnp.int32)
    visit = (jnp.cumsum((cnt > 0).astype(jnp.int32)) - 1) % 2
    plan = (block_expert, next_live[block_expert], visit[block_expert].astype(jnp.int32), nu)
    act = _grouped(_up_kernel, "expert_up", xs, w_up[l], b_up[l].reshape(N_EXPERTS, 1, -1), plan, (D_FF,), BF16)
    y = _grouped(_down_kernel, "expert_down", act, w_down[l], b_down[l].reshape(N_EXPERTS, 1, -1), plan,
                 xs.shape[1:], F32)

    g_post = row2(g_ffn_post[l])
    tt_p = _largest_tile(sp, 128)
    y_p = _combine(y, pos[:n_p].reshape(n_p // tt_p, 1, tt_p * TOP_K), meta[:n_p], x1_p,
                   gtf_p, g_post, tt_p).reshape(bp, sp, d)
    y_s = _combine(y, pos[n_p:].reshape(bs, 1, ss * TOP_K), meta[n_p:], x1_s,
                   gtf_s, g_post, ss).reshape(bs, ss, d)

    stack = lambda a: a[None]
    return (y_p, y_s, stack(st_p[0]), stack(st_p[1]), stack(st_p[2]), stack(st_p[3]),
            stack(st_s[0]), stack(st_s[1]), stack(st_s[2]), stack(st_s[3]))
```

```python
import functools

import numpy as np
import jax
import jax.numpy as jnp
from jax import lax
from jax.experimental import pallas as pl
from jax.experimental.pallas import tpu as pltpu

F32 = jnp.float32
BF16 = jnp.bfloat16

D_MODEL = 2048
CHUNK = 64
N_HEADS = 16
N_KV = 4
HEAD_DIM = 64
GQA = N_HEADS // N_KV
ATTN_DIM = N_HEADS * HEAD_DIM
KV_DIM = N_KV * HEAD_DIM
WINDOW = 128
WIN_CHUNKS = WINDOW // CHUNK
PAST_LEN = 4096
WIN_CACHE = min(WINDOW, PAST_LEN)
ROT_DIM = HEAD_DIM // 4
ROPE_THETA = 500000.0
SSM_HEADS = 16
SSM_HEAD_DIM = 64
SSM_DIM = SSM_HEADS * SSM_HEAD_DIM
SSM_GROUPS = 2
SSM_HPG = SSM_HEADS // SSM_GROUPS
SSM_STATE = 128
CONV_W = 4
CONV_DIM = SSM_DIM + 2 * SSM_GROUPS * SSM_STATE
SSD_CHUNK = 64
N_EXPERTS = 32
TOP_K = 4
D_FF = 2048
SWIGLU_ALPHA = 1.702
SWIGLU_LIMIT = 7.0
NORM_EPS = 1e-6

LANES = 128
SUBLANES = 8
MAIN_DIM = ATTN_DIM + 2 * KV_DIM + SSM_DIM + CONV_DIM
MOE_TM = 256
VMEM_LIMIT = 56 * 1024 * 1024


def _sigmoid(x):
    return 1.0 / (1.0 + jnp.exp(-x))


def _silu(x):
    return x * _sigmoid(x)


def _rms(x, g):
    return x * lax.rsqrt(jnp.mean(x * x, axis=-1, keepdims=True) + NORM_EPS) * g


def _split3(x):
    hi = x.astype(BF16)
    r1 = x - hi.astype(F32)
    mid = r1.astype(BF16)
    lo = (r1 - mid.astype(F32)).astype(BF16)
    return hi, mid, lo


def _dot_exact_rhs(x, m):
    hi, mid, lo = _split3(x)
    d = functools.partial(jnp.dot, preferred_element_type=F32)
    return d(hi, m) + d(mid, m) + d(lo, m)


_NN = (((1,), (0,)), ((), ()))
_NT = (((1,), (1,)), ((), ()))
_TN = (((0,), (0,)), ((), ()))


def _dot(a, b, dims=_NN):
    assert a.dtype == b.dtype, (a.dtype, b.dtype)
    prec = lax.Precision.HIGHEST if a.dtype == F32 else None
    return lax.dot_general(a, b, dims, preferred_element_type=F32, precision=prec)


def _params(sem, vmem=VMEM_LIMIT):
    return pltpu.CompilerParams(dimension_semantics=sem, vmem_limit_bytes=vmem)


def _mod_kernel(c_ref, w_ref, b_ref, o_ref):
    o_ref[...] = _dot(_silu(c_ref[...]).astype(BF16), w_ref[...].astype(BF16)) + b_ref[...]


def _modulation(c_all, w_mod, b_mod):
    rows, d = c_all.shape
    n = w_mod.shape[1]
    tn = 1536
    return pl.pallas_call(
        _mod_kernel,
        grid=(n // tn,),
        in_specs=[pl.BlockSpec((rows, d), lambda j: (0, 0)),
                  pl.BlockSpec((d, tn), lambda j: (0, j)),
                  pl.BlockSpec((1, tn), lambda j: (0, j))],
        out_specs=pl.BlockSpec((rows, tn), lambda j: (0, j)),
        out_shape=jax.ShapeDtypeStruct((rows, n), F32),
        compiler_params=_params(("arbitrary",)),
        name="modulation",
    )(c_all, w_mod, b_mod.reshape(1, n))


def _rope(t, cos, s1, s2):
    outs = []
    for j in range(t.shape[1] // LANES):
        tj = t[:, j * LANES:(j + 1) * LANES]
        up = pltpu.roll(tj, LANES - ROT_DIM // 2, 1)
        dn = pltpu.roll(tj, ROT_DIM // 2, 1)
        outs.append(tj * cos + up * s1 + dn * s2)
    return jnp.concatenate(outs, axis=1)


def _inproj_kernel(x_ref, sc_ref, sh_ref, g_ref, w_ref, wdt_ref, cos_ref, s1_ref, s2_ref,
                   q_ref, k_ref, v_ref, z_ref, xbc_ref, dt_ref):
    h = _rms(x_ref[0], g_ref[...]) * sc_ref[0] + sh_ref[0]
    hb = h.astype(w_ref.dtype)
    cos, s1, s2 = cos_ref[...], s1_ref[...], s2_ref[...]
    step = 512

    def mm(lo, hi):
        return jnp.concatenate([_dot(hb, w_ref[:, c:min(c + step, hi)]) for c in range(lo, hi, step)], axis=1)

    o = 0
    q_ref[0] = (_rope(mm(o, o + ATTN_DIM), cos, s1, s2) * (HEAD_DIM ** -0.5)).astype(q_ref.dtype)
    o += ATTN_DIM
    k_ref[0] = _rope(mm(o, o + KV_DIM), cos, s1, s2)
    o += KV_DIM
    v_ref[0] = mm(o, o + KV_DIM)
    o += KV_DIM
    z_ref[0] = mm(o, o + SSM_DIM)
    o += SSM_DIM
    xbc_ref[0] = mm(o, o + CONV_DIM)
    dt_ref[0] = _dot(hb, wdt_ref[...])


def _mod_spec(m, tm):
    if m.shape[1] == 1:
        return pl.BlockSpec((1, 1, m.shape[2]), lambda bi, i: (bi, 0, 0))
    return pl.BlockSpec((1, tm, m.shape[2]), lambda bi, i: (bi, i, 0))


def _in_projection(x, sc1p, sh, g, w_main, w_dt, rope_tabs, tm):
    b, s, d = x.shape
    cos, s1, s2 = rope_tabs
    row = lambda bi, i: (bi, i, 0)
    const2 = lambda bi, i: (0, 0)
    tab = lambda bi, i: (i, 0)
    widths = (ATTN_DIM, KV_DIM, KV_DIM, SSM_DIM, CONV_DIM, LANES)
    dtypes = (w_main.dtype, F32, F32, F32, F32, F32)
    return pl.pallas_call(
        _inproj_kernel,
        grid=(b, s // tm),
        in_specs=[pl.BlockSpec((1, tm, d), row),
                  _mod_spec(sc1p, tm),
                  _mod_spec(sh, tm),
                  pl.BlockSpec((1, d), const2),
                  pl.BlockSpec((d, MAIN_DIM), const2, pipeline_mode=pl.Buffered(1)),
                  pl.BlockSpec((d, LANES), const2, pipeline_mode=pl.Buffered(1)),
                  pl.BlockSpec((tm, LANES), tab),
                  pl.BlockSpec((tm, LANES), tab),
                  pl.BlockSpec((tm, LANES), tab)],
        out_specs=[pl.BlockSpec((1, tm, w), row) for w in widths],
        out_shape=[jax.ShapeDtypeStruct((b, s, w), dt) for w, dt in zip(widths, dtypes)],
        compiler_params=_params(("arbitrary", "arbitrary")),
        name="in_projection",
    )(x, sc1p, sh, g, w_main, w_dt, cos, s1, s2)


def _rope_tables(pos):
    inv_freq = ROPE_THETA ** (-jnp.arange(0, ROT_DIM, 2, dtype=F32) / ROT_DIM)
    ang = pos.astype(F32)[:, None] * inv_freq[None, :]
    cos, sin = jnp.cos(ang), jnp.sin(ang)
    half = ROT_DIM // 2
    n = pos.shape[0]
    ones = jnp.ones((n, HEAD_DIM - ROT_DIM), F32)
    zeros_h = jnp.zeros((n, half), F32)
    zeros_r = jnp.zeros((n, HEAD_DIM - ROT_DIM), F32)
    c = jnp.concatenate([cos, cos, ones], axis=1)
    a = jnp.concatenate([-sin, zeros_h, zeros_r], axis=1)
    b = jnp.concatenate([zeros_h, sin, zeros_r], axis=1)
    rep = LANES // HEAD_DIM
    return tuple(jnp.tile(t, (1, rep)) for t in (c, a, b))


KV_PAIRS = N_KV // 2


def _paired_head_columns():
    cols = []
    for j in range(KV_PAIRS):
        for a in range(GQA):
            for g in (2 * j, 2 * j + 1):
                h = g * GQA + a
                cols.extend(range(h * HEAD_DIM, (h + 1) * HEAD_DIM))
    return np.asarray(cols, np.int32)


def _attend_pair(qcols, k2, v2, sink_row, valid):
    rows = qcols[0].shape[0]
    lo = lax.broadcasted_iota(jnp.int32, (rows, LANES), 1) < HEAD_DIM
    zero = jnp.zeros_like(qcols[0])
    lhs = jnp.concatenate([jnp.where(lo, qc, zero) for qc in qcols] + [jnp.where(lo, zero, qc) for qc in qcols], axis=0)
    s = _dot(k2, lhs, _NT)
    if valid is not None:
        s = jnp.where(valid, s, -1e30)
    m = jnp.maximum(jnp.max(s, axis=0, keepdims=True), sink_row)
    p = jnp.exp(s - m)
    den = jnp.sum(p, axis=0, keepdims=True) + jnp.exp(sink_row - m)
    o = _dot((p * (1.0 / den)).astype(v2.dtype), v2, _TN)
    half = GQA * rows
    return [jnp.where(lo, o[a * rows:(a + 1) * rows], o[half + a * rows:half + (a + 1) * rows]) for a in range(GQA)]


def _sink_row(sinks_ref, j, rows):
    heads = [(2 * j) * GQA + a for a in range(GQA)] + [(2 * j + 1) * GQA + a for a in range(GQA)]
    col = lax.broadcasted_iota(jnp.int32, (1, len(heads) * rows), 1) // rows
    out = jnp.zeros((1, len(heads) * rows), F32)
    for n, h in enumerate(heads):
        out = jnp.where(col == n, sinks_ref[h], out)
    return out


def _attn_prompt_kernel(sinks_ref, q_ref, km_ref, kh_ref, vm_ref, vh_ref, o_ref):
    i = pl.program_id(1)
    tq = q_ref.shape[1]
    chunks = tq // CHUNK
    kfull = jnp.concatenate([kh_ref[0], km_ref[0]], axis=0).astype(q_ref.dtype)
    vfull = jnp.concatenate([vh_ref[0], vm_ref[0]], axis=0).astype(q_ref.dtype)
    span = (WIN_CHUNKS + 1) * CHUNK
    col_chunk = lax.broadcasted_iota(jnp.int32, (span, 2 * GQA * CHUNK), 0) // CHUNK
    for j in range(KV_PAIRS):
        k2 = kfull[:, j * LANES:(j + 1) * LANES]
        v2 = vfull[:, j * LANES:(j + 1) * LANES]
        sink = _sink_row(sinks_ref, j, CHUNK)
        for c in range(chunks):
            r0 = c * CHUNK
            slabs = [(j * GQA + a) * LANES for a in range(GQA)]
            qcols = [q_ref[0, r0:r0 + CHUNK, sl:sl + LANES] for sl in slabs]
            valid = None
            if c < WIN_CHUNKS:
                valid = (i * chunks + c - WIN_CHUNKS + col_chunk) >= 0
            outs = _attend_pair(qcols, k2[r0:r0 + span], v2[r0:r0 + span], sink, valid)
            for sl, o in zip(slabs, outs):
                o_ref[0, r0:r0 + CHUNK, sl:sl + LANES] = o.astype(o_ref.dtype)


def _attention_prompt(q, k, v, sinks, tq):
    b, s, _ = q.shape
    halo = WIN_CHUNKS * CHUNK
    ratio = tq // halo
    main = lambda bi, i: (bi, i, 0)
    prev = lambda bi, i: (bi, jnp.maximum(i * ratio - 1, 0), 0)
    return pl.pallas_call(
        _attn_prompt_kernel,
        grid=(b, s // tq),
        in_specs=[pl.BlockSpec(memory_space=pltpu.SMEM),
                  pl.BlockSpec((1, tq, ATTN_DIM), main),
                  pl.BlockSpec((1, tq, KV_DIM), main),
                  pl.BlockSpec((1, halo, KV_DIM), prev),
                  pl.BlockSpec((1, tq, KV_DIM), main),
                  pl.BlockSpec((1, halo, KV_DIM), prev)],
        out_specs=pl.BlockSpec((1, tq, ATTN_DIM), main),
        out_shape=jax.ShapeDtypeStruct((b, s, ATTN_DIM), q.dtype),
        compiler_params=_params(("arbitrary", "arbitrary")),
        name="attention_prompt",
    )(sinks, q, k, k, v, v)


def _attn_sample_kernel(sinks_ref, q_ref, k_ref, v_ref, ck_ref, cv_ref, o_ref, kw_ref, vw_ref):
    s = q_ref.shape[1]
    kf = jnp.concatenate([ck_ref[0], k_ref[0]], axis=0)
    vf = jnp.concatenate([cv_ref[0], v_ref[0]], axis=0)
    n = kf.shape[0]
    kw_ref[0] = kf[n - WIN_CACHE:]
    vw_ref[0] = vf[n - WIN_CACHE:]
    kb, vb = kf.astype(q_ref.dtype), vf.astype(q_ref.dtype)
    for j in range(KV_PAIRS):
        slabs = [(j * GQA + a) * LANES for a in range(GQA)]
        qcols = [q_ref[0, :, sl:sl + LANES] for sl in slabs]
        outs = _attend_pair(qcols, kb[:, j * LANES:(j + 1) * LANES], vb[:, j * LANES:(j + 1) * LANES],
                            _sink_row(sinks_ref, j, s), None)
        for sl, o in zip(slabs, outs):
            o_ref[0, :, sl:sl + LANES] = o.astype(o_ref.dtype)


def _attention_sample(q, k, v, sinks, cache_k, cache_v):
    b, s, _ = q.shape
    blk = lambda bi: (bi, 0, 0)
    return pl.pallas_call(
        _attn_sample_kernel,
        grid=(b,),
        in_specs=[pl.BlockSpec(memory_space=pltpu.SMEM),
                  pl.BlockSpec((1, s, ATTN_DIM), blk),
                  pl.BlockSpec((1, s, KV_DIM), blk),
                  pl.BlockSpec((1, s, KV_DIM), blk),
                  pl.BlockSpec((1, WIN_CACHE, KV_DIM), blk),
                  pl.BlockSpec((1, WIN_CACHE, KV_DIM), blk)],
        out_specs=[pl.BlockSpec((1, s, ATTN_DIM), blk),
                   pl.BlockSpec((1, WIN_CACHE, KV_DIM), blk),
                   pl.BlockSpec((1, WIN_CACHE, KV_DIM), blk)],
        out_shape=[jax.ShapeDtypeStruct((b, s, ATTN_DIM), q.dtype),
                   jax.ShapeDtypeStruct((b, WIN_CACHE, KV_DIM), F32),
                   jax.ShapeDtypeStruct((b, WIN_CACHE, KV_DIM), F32)],
        compiler_params=_params(("arbitrary",)),
        name="attention_sample",
    )(sinks, q, k, v, cache_k, cache_v)


def _ssd_kernel(xbc_ref, z_ref, dt_ref, hist_ref, h0_ref, cw_ref, cb_ref, dtb_ref, alog_ref, dsk_ref, gs_ref,
                ep_ref, el_ref, dmask_ref, causal_ref, tril_ref,
                y_ref, conv_ref, hfin_ref, prev, ht):
    c = pl.program_id(1)
    last = pl.num_programs(1) - 1
    ln = xbc_ref.shape[1]

    @pl.when(c == 0)
    def _():
        prev[...] = hist_ref[...]
        ht[...] = h0_ref[...]

    for bi in range(xbc_ref.shape[0]):
        _ssd_chunk(bi, xbc_ref, z_ref, dt_ref, cw_ref, cb_ref, dtb_ref, alog_ref, dsk_ref, gs_ref,
                   ep_ref, el_ref, dmask_ref, causal_ref, tril_ref, y_ref, prev, ht)

    @pl.when(c == last)
    def _():
        hfin_ref[...] = ht[...]
        conv_ref[...] = xbc_ref[:, ln - (CONV_W - 1):, :]


def _ssd_chunk(bi, xbc_ref, z_ref, dt_ref, cw_ref, cb_ref, dtb_ref, alog_ref, dsk_ref, gs_ref,
               ep_ref, el_ref, dmask_ref, causal_ref, tril_ref, y_ref, prev, ht):
    ln = xbc_ref.shape[1]
    gw = SSM_HPG * SSM_HEAD_DIM
    xr = xbc_ref[bi]
    ext = jnp.concatenate([prev[bi], xr], axis=0)
    conv = cb_ref[...]
    for i in range(CONV_W):
        sh = CONV_W - 1 - i
        tap = xr if sh == 0 else pltpu.roll(ext, sh, 0)[SUBLANES:]
        conv = conv + tap * cw_ref[i:i + 1, :]
    prev[bi] = xr[ln - SUBLANES:]

    act = _silu(conv)
    xs = act[:, :SSM_DIM]
    bm = act[:, SSM_DIM:SSM_DIM + SSM_GROUPS * SSM_STATE]
    cm = act[:, SSM_DIM + SSM_GROUPS * SSM_STATE:]

    dtv = dt_ref[bi] + dtb_ref[...]
    dt = jnp.maximum(dtv, 0.0) + jnp.log1p(jnp.exp(-jnp.abs(dtv)))
    ad = dt * (-jnp.exp(alog_ref[...]))
    hi, mid, lo = _split3(ad)
    tril = tril_ref[...]
    d = functools.partial(jnp.dot, preferred_element_type=F32)
    a_cs = d(tril, hi) + d(tril, mid) + d(tril, lo)
    a_last = a_cs[ln - 1:ln, :]
    stacked = jnp.concatenate([dt, jnp.exp(a_cs), jnp.exp(a_last - a_cs)], axis=0)
    wide = _dot_exact_rhs(stacked, ep_ref[...])
    dt_x, ea_x, ds_x = wide[:ln], wide[ln:2 * ln], wide[2 * ln:]
    cd_x = ea_x[ln - 1:ln, :]

    a_l = _dot_exact_rhs(a_cs, el_ref[...])
    a_s = jnp.sum(a_l * dmask_ref[...], axis=0, keepdims=True)
    lmat = jnp.exp(jnp.where(causal_ref[...] > 0.0, a_l - a_s, -1e30))

    cdt = y_ref.dtype
    xd = xs * dt_x
    xds = (xd * ds_x).astype(cdt)
    xdb = xd.astype(cdt)
    ys = []
    for g in range(SSM_GROUPS):
        bg = bm[:, g * SSM_STATE:(g + 1) * SSM_STATE].astype(cdt)
        cg = cm[:, g * SSM_STATE:(g + 1) * SSM_STATE].astype(cdt)
        cbm = _dot(cg, bg, _NT)
        yd = []
        for r in range(SSM_HPG):
            hd = g * SSM_HPG + r
            w = (cbm * lmat[:, hd * ln:(hd + 1) * ln]).astype(cdt)
            yd.append(_dot(w, xdb[:, hd * SSM_HEAD_DIM:(hd + 1) * SSM_HEAD_DIM]))
        htg = ht[bi, g]
        y_off = _dot(cg, htg.astype(cdt)) * ea_x[:, g * gw:(g + 1) * gw]
        st = _dot(bg, xds[:, g * gw:(g + 1) * gw], _TN)
        ht[bi, g] = htg * cd_x[:, g * gw:(g + 1) * gw] + st
        ys.append(jnp.concatenate(yd, axis=1) + y_off)
    y = jnp.concatenate(ys, axis=1) + dsk_ref[...] * xs
    y = y * _silu(z_ref[bi])
    outs = [_rms(y[:, g * gw:(g + 1) * gw], gs_ref[:, g * gw:(g + 1) * gw]) for g in range(SSM_GROUPS)]
    y_ref[bi] = jnp.concatenate(outs, axis=1).astype(y_ref.dtype)


def _ssd_constants(ln):
    heads = SSM_HEADS
    ep = np.zeros((LANES, heads * SSM_HEAD_DIM), np.float32)
    el = np.zeros((LANES, heads * ln), np.float32)
    for r in range(heads):
        ep[r, r * SSM_HEAD_DIM:(r + 1) * SSM_HEAD_DIM] = 1.0
        el[r, r * ln:(r + 1) * ln] = 1.0
    eye = np.tile(np.eye(ln, dtype=np.float32), (1, heads))
    causal = np.tile(np.tril(np.ones((ln, ln), np.float32)), (1, heads))
    tril = np.tril(np.ones((ln, ln), np.float32))
    return (jnp.asarray(ep, BF16), jnp.asarray(el, BF16), jnp.asarray(eye), jnp.asarray(causal),
            jnp.asarray(tril, BF16))


def _ssd_mixer(xbc, z, dt_raw, hist8, h0t, conv_w8, conv_b, dt_bias, a_log, dskip_x, g_ssm, ln, out_dtype):
    b, s, _ = xbc.shape
    bb = 2 if b % 2 == 0 else 1
    consts = _ssd_constants(ln)
    row = lambda bi, c: (bi, c, 0)
    per_b3 = lambda bi, c: (bi, 0, 0)
    per_b4 = lambda bi, c: (bi, 0, 0, 0)
    const2 = lambda bi, c: (0, 0)
    full = lambda a: pl.BlockSpec(a.shape, const2)
    gw = SSM_HPG * SSM_HEAD_DIM
    return pl.pallas_call(
        _ssd_kernel,
        grid=(b // bb, s // ln),
        in_specs=[pl.BlockSpec((bb, ln, CONV_DIM), row),
                  pl.BlockSpec((bb, ln, SSM_DIM), row),
                  pl.BlockSpec((bb, ln, LANES), row),
                  pl.BlockSpec((bb, SUBLANES, CONV_DIM), per_b3),
                  pl.BlockSpec((bb, SSM_GROUPS, SSM_STATE, gw), per_b4),
                  full(conv_w8), full(conv_b), full(dt_bias), full(a_log), full(dskip_x), full(g_ssm)]
                 + [full(a) for a in consts],
        out_specs=[pl.BlockSpec((bb, ln, SSM_DIM), row),
                   pl.BlockSpec((bb, CONV_W - 1, CONV_DIM), per_b3),
                   pl.BlockSpec((bb, SSM_GROUPS, SSM_STATE, gw), per_b4)],
        out_shape=[jax.ShapeDtypeStruct((b, s, SSM_DIM), out_dtype),
                   jax.ShapeDtypeStruct((b, CONV_W - 1, CONV_DIM), F32),
                   jax.ShapeDtypeStruct((b, SSM_GROUPS, SSM_STATE, gw), F32)],
        scratch_shapes=[pltpu.VMEM((bb, SUBLANES, CONV_DIM), F32),
                        pltpu.VMEM((bb, SSM_GROUPS, SSM_STATE, gw), F32)],
        compiler_params=_params(("arbitrary", "arbitrary")),
        name="ssd_mixer",
    )(xbc, z, dt_raw, hist8, h0t, conv_w8, conv_b, dt_bias, a_log, dskip_x, g_ssm, *consts)


def _outproj_kernel(a_ref, s_ref, x_ref, w_ref, gpost_ref, gate_ref, gpre_ref, sc_ref, sh_ref, wr_ref, br_ref,
                    x1_ref, h2_ref, lg_ref):
    mix = _dot(a_ref[0], w_ref[:ATTN_DIM, :]) + _dot(s_ref[0], w_ref[ATTN_DIM:, :])
    x1 = x_ref[0] + gate_ref[0] * _rms(mix, gpost_ref[...])
    x1_ref[0] = x1
    h2 = _rms(x1, gpre_ref[...]) * sc_ref[0] + sh_ref[0]
    h2_ref[...] = h2
    lg_ref[...] = _dot(h2.astype(wr_ref.dtype), wr_ref[...]) + br_ref[...]


def _out_projection(attn, ssm, x, w_out, g_post, gate, g_pre, sc1p, sh, w_router, b_router, tm):
    b, s, d = x.shape
    nt = s // tm
    row = lambda bi, i: (bi, i, 0)
    const2 = lambda bi, i: (0, 0)
    flat = lambda bi, i: (bi * nt + i, 0)
    return pl.pallas_call(
        _outproj_kernel,
        grid=(b, nt),
        in_specs=[pl.BlockSpec((1, tm, ATTN_DIM), row),
                  pl.BlockSpec((1, tm, SSM_DIM), row),
                  pl.BlockSpec((1, tm, d), row),
                  pl.BlockSpec((ATTN_DIM + SSM_DIM, d), const2, pipeline_mode=pl.Buffered(1)),
                  pl.BlockSpec((1, d), const2),
                  _mod_spec(gate, tm),
                  pl.BlockSpec((1, d), const2),
                  _mod_spec(sc1p, tm),
                  _mod_spec(sh, tm),
                  pl.BlockSpec((d, LANES), const2),
                  pl.BlockSpec((1, LANES), const2)],
        out_specs=[pl.BlockSpec((1, tm, d), row),
                   pl.BlockSpec((tm, d), flat),
                   pl.BlockSpec((tm, LANES), flat)],
        out_shape=[jax.ShapeDtypeStruct((b, s, d), F32),
                   jax.ShapeDtypeStruct((b * s, d), F32),
                   jax.ShapeDtypeStruct((b * s, LANES), F32)],
        compiler_params=_params(("arbitrary", "arbitrary")),
        name="out_projection",
    )(attn, ssm, x, w_out, g_post, gate, g_pre, sc1p, sh, w_router, b_router)


LANE_IDX, LANE_RANK, LANE_GATE = 0, TOP_K, 2 * TOP_K


def _router_kernel(lp_ref, ls_ref, ltri_ref, meta_ref, cnt_ref, *, p_tiles):
    i = pl.program_id(0)
    tt = lp_ref.shape[0]

    @pl.when(i == 0)
    def _():
        cnt_ref[...] = jnp.zeros_like(cnt_ref)

    logits = jnp.where(i < p_tiles, lp_ref[...], ls_ref[...])
    lane = lax.broadcasted_iota(jnp.int32, (tt, LANES), 1)
    lane_f = lane.astype(F32)
    work = jnp.where(lane < N_EXPERTS, logits, -jnp.inf)
    vals, hots, idxs = [], [], []
    for _ in range(TOP_K):
        m = jnp.max(work, axis=-1, keepdims=True)
        idx = jnp.min(jnp.where(work == m, lane_f, float(LANES)), axis=-1, keepdims=True)
        hot = lane_f == idx
        vals.append(m)
        hots.append(hot)
        idxs.append(idx)
        work = jnp.where(hot, -jnp.inf, work)
    es = [jnp.exp(v - vals[0]) for v in vals]
    den = es[0] + es[1] + es[2] + es[3]
    onehot = jnp.zeros((tt, LANES), F32)
    for hot in hots:
        onehot = jnp.where(hot, 1.0, onehot)
    before = jnp.dot(ltri_ref[...], onehot.astype(BF16), preferred_element_type=F32) + cnt_ref[0:1, :]
    meta = jnp.zeros((tt, LANES), F32)
    for k in range(TOP_K):
        rank_k = jnp.sum(jnp.where(hots[k], before, 0.0), axis=-1, keepdims=True)
        meta = jnp.where(lane == LANE_IDX + k, idxs[k], meta)
        meta = jnp.where(lane == LANE_RANK + k, rank_k, meta)
        meta = jnp.where(lane == LANE_GATE + k, es[k] / den, meta)
    meta_ref[...] = meta
    cnt_ref[...] = cnt_ref[...] + jnp.sum(onehot, axis=0, keepdims=True)


def _token_specs(p_tiles, tt, trailing):
    zeros = (0,) * len(trailing)
    return [pl.BlockSpec((tt,) + trailing, lambda i, *_: (jnp.minimum(i, p_tiles - 1),) + zeros),
            pl.BlockSpec((tt,) + trailing, lambda i, *_: (jnp.maximum(i - p_tiles, 0),) + zeros)]


def _router(logits_p, logits_s, tt):
    n = logits_p.shape[0] + logits_s.shape[0]
    p_tiles = logits_p.shape[0] // tt
    ltri = jnp.asarray(np.tril(np.ones((tt, tt), np.float32), -1), BF16)
    return pl.pallas_call(
        functools.partial(_router_kernel, p_tiles=p_tiles),
        grid=(n // tt,),
        in_specs=_token_specs(p_tiles, tt, (LANES,)) + [pl.BlockSpec((tt, tt), lambda i: (0, 0))],
        out_specs=[pl.BlockSpec((tt, LANES), lambda i: (i, 0)),
                   pl.BlockSpec((SUBLANES, LANES), lambda i: (0, 0))],
        out_shape=[jax.ShapeDtypeStruct((n, LANES), F32),
                   jax.ShapeDtypeStruct((SUBLANES, LANES), F32)],
        compiler_params=_params(("arbitrary",)),
        name="router",
    )(logits_p, logits_s, ltri)


def _dispatch_kernel(zstart_ref, nu_ref, pos_ref, pos_prev_ref, hp_ref, hs_ref, xs_ref, zeros, src, sem_z, sem_l, sem_r,
                     *, p_tiles):
    i = pl.program_id(0)
    n_tiles = pl.num_programs(0)
    tt = src.shape[1]
    tm = zeros.shape[0]
    n_blocks = xs_ref.shape[0] // tm

    def zero_copy(row):
        return pltpu.make_async_copy(zeros, xs_ref.at[pl.ds(pl.multiple_of(row, tm), tm)], sem_z)

    @pl.when(i == 0)
    def _():
        zeros[...] = jnp.zeros_like(zeros)

        def per_expert(act):
            def body(e, carry):
                @pl.when(zstart_ref[e] >= 0)
                def _():
                    act(zero_copy(zstart_ref[e]))
                return carry
            lax.fori_loop(0, N_EXPERTS, body, 0)

        def per_tail(act):
            def body(blk, carry):
                act(zero_copy(blk * tm))
                return carry
            lax.fori_loop(nu_ref[0], n_blocks, body, 0)

        per_expert(lambda cp: cp.start())
        per_tail(lambda cp: cp.start())
        per_expert(lambda cp: cp.wait())
        per_tail(lambda cp: cp.wait())

    def tile_load(j, act):
        b = j % 3

        @pl.when(j < p_tiles)
        def _():
            act(pltpu.make_async_copy(hp_ref.at[pl.ds(pl.multiple_of(j * tt, tt), tt)], src.at[b], sem_l.at[b]))

        @pl.when(j >= p_tiles)
        def _():
            act(pltpu.make_async_copy(hs_ref.at[pl.ds(pl.multiple_of((j - p_tiles) * tt, tt), tt)], src.at[b],
                                      sem_l.at[b]))

    def row_copies(j, p_ref):
        return [pltpu.make_async_copy(src.at[j % 3, pl.ds(t, 1)],
                                      xs_ref.at[pl.ds(p_ref[0, 0, t * TOP_K + k], 1)], sem_r.at[j % 2])
                for t in range(tt) for k in range(TOP_K)]

    @pl.when(i == 0)
    def _():
        tile_load(i, lambda cp: cp.start())

    @pl.when(i + 1 < n_tiles)
    def _():
        tile_load(i + 1, lambda cp: cp.start())

    tile_load(i, lambda cp: cp.wait())
    for cp in row_copies(i, pos_ref):
        cp.start()

    @pl.when(i > 0)
    def _():
        for cp in row_copies(i - 1, pos_prev_ref):
            cp.wait()

    @pl.when(i == n_tiles - 1)
    def _():
        for cp in row_copies(i, pos_ref):
            cp.wait()


def _dispatch(h_p, h_s, pos3, zstart, n_used, n_slots, tt):
    row_shape = h_p.shape[1:]
    n = h_p.shape[0] + h_s.shape[0]
    p_tiles = h_p.shape[0] // tt
    return pl.pallas_call(
        functools.partial(_dispatch_kernel, p_tiles=p_tiles),
        grid_spec=pltpu.PrefetchScalarGridSpec(
            num_scalar_prefetch=2,
            grid=(n // tt,),
            in_specs=[pl.BlockSpec((1, 1, tt * TOP_K), lambda i, *_: (i, 0, 0), memory_space=pltpu.SMEM),
                      pl.BlockSpec((1, 1, tt * TOP_K), lambda i, *_: (jnp.maximum(i - 1, 0), 0, 0),
                                   memory_space=pltpu.SMEM),
                      pl.BlockSpec(memory_space=pl.ANY),
                      pl.BlockSpec(memory_space=pl.ANY)],
            out_specs=pl.BlockSpec(memory_space=pl.ANY),
            scratch_shapes=[pltpu.VMEM((MOE_TM,) + row_shape, F32),
                            pltpu.VMEM((3, tt) + row_shape, F32),
                            pltpu.SemaphoreType.DMA(()),
                            pltpu.SemaphoreType.DMA((3,)),
                            pltpu.SemaphoreType.DMA((2,))]),
        out_shape=jax.ShapeDtypeStruct((n_slots,) + row_shape, F32),
        compiler_params=_params(("arbitrary",)),
        name="dispatch",
    )(zstart, n_used, pos3, pos3, h_p, h_s)


W_CHUNK_ROWS = 256


def _stage_expert_weights(i, be_ref, nxt_ref, slot_ref, w_hbm, wbuf, stg, sem, done):
    rows = W_CHUNK_ROWS
    chunks = wbuf.shape[1] // rows
    e, e_next, slot = be_ref[i], nxt_ref[i], slot_ref[i]
    first = jnp.logical_or(i == 0, be_ref[jnp.maximum(i - 1, 0)] != e)

    def chunk_copy(expert, c, buf):
        return pltpu.make_async_copy(w_hbm.at[expert, pl.ds(pl.multiple_of(c * rows, rows), rows)],
                                     stg.at[buf], sem.at[buf])

    def convert_next_chunk(expert, into):
        c = done[0]
        buf = c % 2
        chunk_copy(expert, c, buf).wait()

        @pl.when(c + 1 < chunks)
        def _():
            chunk_copy(expert, c + 1, 1 - buf).start()

        wbuf[into, pl.ds(pl.multiple_of(c * rows, rows), rows), :] = stg[buf].astype(wbuf.dtype)
        done[0] = c + 1

    @pl.when(i == 0)
    def _():
        done[0] = 0
        chunk_copy(e, 0, 0).start()

    @pl.when(first)
    def _():
        def body(_, carry):
            convert_next_chunk(e, slot)
            return carry

        lax.fori_loop(done[0], chunks, body, 0)

        @pl.when(e_next >= 0)
        def _():
            done[0] = 0
            chunk_copy(e_next, 0, 0).start()

    @pl.when(jnp.logical_and(jnp.logical_not(first), jnp.logical_and(e_next >= 0, done[0] < chunks)))
    def _():
        convert_next_chunk(e_next, 1 - slot)

    return slot


def _up_kernel(be_ref, nxt_ref, slot_ref, nu_ref, x_ref, w_hbm, b_ref, act_ref, wbuf, stg, sem, done):
    i = pl.program_id(0)
    fc = 512

    @pl.when(i < nu_ref[0])
    def _():
        slot = _stage_expert_weights(i, be_ref, nxt_ref, slot_ref, w_hbm, wbuf, stg, sem, done)
        x = x_ref[...].astype(wbuf.dtype)
        for c in range(D_FF // fc):
            glu = _dot(x, wbuf[slot, :, c * fc:(c + 1) * fc]) + b_ref[:, c * fc:(c + 1) * fc]
            lin = (_dot(x, wbuf[slot, :, D_FF + c * fc:D_FF + (c + 1) * fc])
                   + b_ref[:, D_FF + c * fc:D_FF + (c + 1) * fc])
            glu = jnp.minimum(glu, SWIGLU_LIMIT)
            lin = jnp.clip(lin, -SWIGLU_LIMIT, SWIGLU_LIMIT)
            act_ref[:, c * fc:(c + 1) * fc] = (glu * _sigmoid(SWIGLU_ALPHA * glu) * (lin + 1.0)).astype(act_ref.dtype)

    @pl.when(i >= nu_ref[0])
    def _():
        act_ref[...] = jnp.zeros_like(act_ref)


def _down_kernel(be_ref, nxt_ref, slot_ref, nu_ref, a_ref, w_hbm, b_ref, y_ref, wbuf, stg, sem, done):
    i = pl.program_id(0)

    @pl.when(i < nu_ref[0])
    def _():
        slot = _stage_expert_weights(i, be_ref, nxt_ref, slot_ref, w_hbm, wbuf, stg, sem, done)
        y_ref[...] = _dot(a_ref[...], wbuf[slot]) + b_ref[...]

    @pl.when(i >= nu_ref[0])
    def _():
        y_ref[...] = jnp.zeros_like(y_ref)


def _grouped(kernel, name, x, w, b, plan, out_tail, out_dtype, extra_scratch=()):
    block_expert, next_expert, slot, n_used = plan
    n_slots = x.shape[0]
    _, kdim, ndim = w.shape
    nb = n_slots // MOE_TM
    zeros = lambda t: (0,) * len(t)
    x_tail, o_tail = x.shape[1:], tuple(out_tail)
    return pl.pallas_call(
        kernel,
        grid_spec=pltpu.PrefetchScalarGridSpec(
            num_scalar_prefetch=4,
            grid=(nb,),
            in_specs=[pl.BlockSpec((MOE_TM,) + x_tail, lambda i, be, nx, sl, nu: (jnp.minimum(i, nu[0] - 1),) + zeros(x_tail)),
                      pl.BlockSpec(memory_space=pl.ANY),
                      pl.BlockSpec((None, 1, ndim), lambda i, be, nx, sl, nu: (be[i], 0, 0))],
            out_specs=pl.BlockSpec((MOE_TM,) + o_tail, lambda i, be, nx, sl, nu: (i,) + zeros(o_tail)),
            scratch_shapes=[pltpu.VMEM((2, kdim, ndim), BF16),
                            pltpu.VMEM((2, W_CHUNK_ROWS, ndim), F32),
                            pltpu.SemaphoreType.DMA((2,)),
                            pltpu.SMEM((1,), jnp.int32)] + list(extra_scratch)),
        out_shape=jax.ShapeDtypeStruct((n_slots,) + o_tail, out_dtype),
        compiler_params=_params(("arbitrary",), 60 * 1024 * 1024),
        name=name,
    )(block_expert, next_expert, slot, n_used, x, w, b)


def _combine_kernel(pos_ref, pos_next_ref, y_ref, meta_ref, x1_ref, gate_ref, g_ref, o_ref, rows, sem):
    i = pl.program_id(0)
    tt = x1_ref.shape[0]
    buf = i % 2

    def copies(p_ref, b):
        return [pltpu.make_async_copy(y_ref.at[pl.ds(p_ref[0, 0, t * TOP_K + k], 1)],
                                      rows.at[b, k, pl.ds(t, 1)], sem.at[b])
                for t in range(tt) for k in range(TOP_K)]

    @pl.when(i == 0)
    def _():
        for cp in copies(pos_ref, 0):
            cp.start()

    @pl.when(i + 1 < pl.num_programs(0))
    def _():
        for cp in copies(pos_next_ref, 1 - buf):
            cp.start()

    for cp in copies(pos_ref, buf):
        cp.wait()
    meta = meta_ref[...]
    f = rows[buf, 0] * meta[:, LANE_GATE:LANE_GATE + 1]
    for k in range(1, TOP_K):
        f = f + rows[buf, k] * meta[:, LANE_GATE + k:LANE_GATE + k + 1]
    o_ref[...] = x1_ref[...] + gate_ref[0] * _rms(f, g_ref[...])


def _combine(y, pos3, meta, x1, gate, g_post, tt):
    n, d = x1.shape
    nb = gate.shape[0]
    tiles_per_b = n // tt // nb
    tiles = n // tt
    return pl.pallas_call(
        _combine_kernel,
        grid=(tiles,),
        in_specs=[pl.BlockSpec((1, 1, tt * TOP_K), lambda i: (i, 0, 0), memory_space=pltpu.SMEM),
                  pl.BlockSpec((1, 1, tt * TOP_K), lambda i: (jnp.minimum(i + 1, tiles - 1), 0, 0),
                               memory_space=pltpu.SMEM),
                  pl.BlockSpec(memory_space=pl.ANY),
                  pl.BlockSpec((tt, LANES), lambda i: (i, 0)),
                  pl.BlockSpec((tt, d), lambda i: (i, 0)),
                  pl.BlockSpec((1, 1, d), lambda i: (i // tiles_per_b, 0, 0)),
                  pl.BlockSpec((1, d), lambda i: (0, 0))],
        out_specs=pl.BlockSpec((tt, d), lambda i: (i, 0)),
        out_shape=jax.ShapeDtypeStruct((n, d), F32),
        scratch_shapes=[pltpu.VMEM((2, TOP_K, tt) + y.shape[1:], F32), pltpu.SemaphoreType.DMA((2,))],
        compiler_params=_params(("arbitrary",)),
        name="combine",
    )(pos3, pos3, y, meta, x1, gate, g_post)


def _mixer(x, mod, pos, hist8, h0t, ln, tm, tq, wts, cache):
    b, s, d = x.shape
    mods = [m.reshape(b, 1, d) for m in jnp.split(mod, 6, axis=-1)]
    gt_f = mods[5]
    if cache is None:
        xf, fb, fs = x, b, s
    else:
        fb, fs = 1, b * s
        xf = x.reshape(fb, fs, d)
        pos = jnp.tile(pos, b)
        mods = [jnp.broadcast_to(m, (b, s, d)).reshape(fb, fs, d) for m in mods]
    sh_m, sc_m, gt_m, sh_f, sc_f, _ = mods
    proj = _in_projection(xf, 1.0 + sc_m, sh_m, wts["g_mix_pre"], wts["w_main"], wts["w_dt"], _rope_tables(pos), tm)
    q, k, v, z, xbc, dt_raw = [t.reshape(b, s, t.shape[-1]) for t in proj]
    if cache is None:
        attn = _attention_prompt(q, k, v, wts["sinks"], tq)
        k_win, v_win = k[:, s - WIN_CACHE:], v[:, s - WIN_CACHE:]
    else:
        attn, k_win, v_win = _attention_sample(q, k, v, wts["sinks"], *cache)
    ssm, conv_state, hfin = _ssd_mixer(xbc, z, dt_raw, hist8, h0t, wts["conv_w8"], wts["conv_b"], wts["dt_bias"],
                                       wts["a_log"], wts["dskip_x"], wts["g_ssm"], ln, q.dtype)
    x1, h2, logits = _out_projection(attn.reshape(fb, fs, -1), ssm.reshape(fb, fs, -1), xf, wts["w_out"],
                                     wts["g_mix_post"], gt_m, wts["g_ffn_pre"], 1.0 + sc_f, sh_f,
                                     wts["w_router"], wts["b_router"], tm)
    ssm_state = hfin.reshape(b, SSM_GROUPS, SSM_STATE, SSM_HPG, SSM_HEAD_DIM)
    ssm_state = ssm_state.transpose(0, 1, 3, 4, 2).reshape(b, SSM_HEADS, SSM_HEAD_DIM, SSM_STATE)
    states = (k_win.reshape(b, WIN_CACHE, N_KV, HEAD_DIM), v_win.reshape(b, WIN_CACHE, N_KV, HEAD_DIM),
              conv_state, ssm_state)
    return x1.reshape(b * s, d), h2, logits, gt_f, states


def _pad_lanes(a, width=LANES):
    return jnp.pad(a, [(0, 0)] * (a.ndim - 1) + [(0, width - a.shape[-1])])


def _largest_tile(n, cap):
    t = cap
    while n % t:
        t //= 2
    return t


def kernel(x_prompt, x_sample, c_prompt, c_sample, cache_k, cache_v, state_conv, state_ssm, w_mod, b_mod, g_mix_pre, g_mix_post, g_ffn_pre, g_ffn_post, w_in, conv_w, conv_b, dt_bias, a_log, d_skip, g_ssm, sinks, w_out, w_router, b_router, w_up, b_up, w_down, b_down):
    depth = w_mod.shape[0]
    assert depth == 1, "single-layer step"
    l = 0
    bp, sp, d = x_prompt.shape
    bs, ss, _ = x_sample.shape
    n_p, n_s = bp * sp, bs * ss
    n_tok = n_p + n_s

    c_all = jnp.concatenate([c_prompt, c_sample], axis=0)
    c_rows = -(-c_all.shape[0] // SUBLANES) * SUBLANES
    mod_all = _modulation(jnp.pad(c_all, ((0, c_rows - c_all.shape[0]), (0, 0))), w_mod[l], b_mod[l])

    row2 = lambda a: a.reshape(1, -1)
    paired = _paired_head_columns()
    wts = {
        "g_mix_pre": row2(g_mix_pre[l]), "g_mix_post": row2(g_mix_post[l]), "g_ffn_pre": row2(g_ffn_pre[l]),
        "w_main": jnp.concatenate([w_in[l][:, :ATTN_DIM][:, paired], w_in[l][:, ATTN_DIM:MAIN_DIM]],
                                  axis=1).astype(BF16),
        "w_dt": _pad_lanes(w_in[l][:, MAIN_DIM:]).astype(BF16),
        "sinks": sinks[l],
        "conv_w8": jnp.pad(conv_w[l], ((0, SUBLANES - CONV_W), (0, 0))), "conv_b": row2(conv_b[l]),
        "dt_bias": _pad_lanes(row2(dt_bias[l])), "a_log": _pad_lanes(row2(a_log[l])),
        "dskip_x": row2(jnp.repeat(d_skip[l], SSM_HEAD_DIM)), "g_ssm": row2(g_ssm[l]),
        "w_out": jnp.concatenate([w_out[l][:ATTN_DIM][paired], w_out[l][ATTN_DIM:]], axis=0).astype(BF16),
        "w_router": _pad_lanes(w_router[l]).astype(BF16), "b_router": _pad_lanes(row2(b_router[l])),
    }
    gw = SSM_HPG * SSM_HEAD_DIM

    hist_p = jnp.zeros((bp, SUBLANES, CONV_DIM), F32)
    h0_p = jnp.zeros((bp, SSM_GROUPS, SSM_STATE, gw), F32)
    x1_p, h2_p, lg_p, gtf_p, st_p = _mixer(x_prompt, mod_all[:bp], jnp.arange(sp, dtype=jnp.int32), hist_p, h0_p,
                                           SSD_CHUNK, _largest_tile(sp, 256), _largest_tile(sp, 256), wts, None)

    hist_s = jnp.pad(state_conv[l], ((0, 0), (SUBLANES - (CONV_W - 1), 0), (0, 0)))
    h0_s = state_ssm[l].astype(F32).reshape(bs, SSM_GROUPS, SSM_HPG, SSM_HEAD_DIM, SSM_STATE)
    h0_s = h0_s.transpose(0, 1, 4, 2, 3).reshape(bs, SSM_GROUPS, SSM_STATE, gw)
    cache = (cache_k[l].reshape(bs, WIN_CACHE, KV_DIM), cache_v[l].reshape(bs, WIN_CACHE, KV_DIM))
    x1_s, h2_s, lg_s, gtf_s, st_s = _mixer(x_sample, mod_all[bp:bp + bs], PAST_LEN + jnp.arange(ss, dtype=jnp.int32),
                                           hist_s, h0_s, ss, n_s, ss, wts, cache)

    tt = _largest_tile(n_s, 128)
    assert n_p % tt == 0
    meta, counts = _router(lg_p, lg_s, tt)
    top_idx = meta[:, LANE_IDX:LANE_IDX + TOP_K].astype(jnp.int32)
    rank = meta[:, LANE_RANK:LANE_RANK + TOP_K].astype(jnp.int32)
    cnt = counts[0, :N_EXPERTS].astype(jnp.int32)
    padded = (cnt + MOE_TM - 1) // MOE_TM * MOE_TM
    pend = jnp.cumsum(padded)
    offs = pend - padded
    pos = offs[top_idx] + rank
    n_blocks = -(-n_tok * TOP_K // MOE_TM) + N_EXPERTS
    n_slots = n_blocks * MOE_TM
    n_used = (pend[-1] // MOE_TM).astype(jnp.int32)
    blk = jnp.arange(n_blocks, dtype=jnp.int32)
    blk_row = jnp.minimum(blk, n_used - 1) * MOE_TM
    block_expert = jnp.minimum(jnp.sum(pend[None, :] <= blk_row[:, None], axis=1), N_EXPERTS - 1).astype(jnp.int32)
    zstart = jnp.where(cnt > 0, pend - MOE_TM, -1).astype(jnp.int32)
    pos3 = pos.reshape(n_tok // tt, 1, tt * TOP_K)
    nu = n_used.reshape(1)
    xs = _dispatch(h2_p, h2_s, pos3, zstart, nu, n_slots, tt)
    e_ids = jnp.arange(N_EXPERTS, dtype=jnp.int32)
    live = jnp.where(cnt > 0, e_ids, N_EXPERTS)
    later = jnp.concatenate([lax.cummin(live, reverse=True)[1:], jnp.full((1,), N_EXPERTS, jnp.int32)])
    next_live = jnp.where(later < N_EXPERTS, later, -1).astype(jnp.int32)
    visit = (jnp.cumsum((cnt > 0).astype(jnp.int32)) - 1) % 2
    plan = (block_expert, next_live[block_expert], visit[block_expert].astype(jnp.int32), nu)
    act = _grouped(_up_kernel, "expert_up", xs, w_up[l], b_up[l].reshape(N_EXPERTS, 1, -1), plan, (D_FF,), BF16)
    y = _grouped(_down_kernel, "expert_down", act, w_down[l], b_down[l].reshape(N_EXPERTS, 1, -1), plan,
                 xs.shape[1:], F32)

    g_post = row2(g_ffn_post[l])
    tt_p = _largest_tile(sp, 128)
    y_p = _combine(y, pos[:n_p].reshape(n_p // tt_p, 1, tt_p * TOP_K), meta[:n_p], x1_p,
                   gtf_p, g_post, tt_p).reshape(bp, sp, d)
    y_s = _combine(y, pos[n_p:].reshape(bs, 1, ss * TOP_K), meta[n_p:], x1_s,
                   gtf_s, g_post, ss).reshape(bs, ss, d)

    stack = lambda a: a[None]
    return (y_p, y_s, stack(st_p[0]), stack(st_p[1]), stack(st_p[2]), stack(st_p[3]),
            stack(st_s[0]), stack(st_s[1]), stack(st_s[2]), stack(st_s[3]))
```

```python
import functools

import numpy as np
import jax
import jax.numpy as jnp
from jax import lax
from jax.experimental import pallas as pl
from jax.experimental.pallas import tpu as pltpu

F32 = jnp.float32
BF16 = jnp.bfloat16

D_MODEL = 2048
CHUNK = 64
N_HEADS = 16
N_KV = 4
HEAD_DIM = 64
GQA = N_HEADS // N_KV
ATTN_DIM = N_HEADS * HEAD_DIM
KV_DIM = N_KV * HEAD_DIM
WINDOW = 128
WIN_CHUNKS = WINDOW // CHUNK
PAST_LEN = 4096
WIN_CACHE = min(WINDOW, PAST_LEN)
ROT_DIM = HEAD_DIM // 4
ROPE_THETA = 500000.0
SSM_HEADS = 16
SSM_HEAD_DIM = 64
SSM_DIM = SSM_HEADS * SSM_HEAD_DIM
SSM_GROUPS = 2
SSM_HPG = SSM_HEADS // SSM_GROUPS
SSM_STATE = 128
CONV_W = 4
CONV_DIM = SSM_DIM + 2 * SSM_GROUPS * SSM_STATE
SSD_CHUNK = 64
N_EXPERTS = 32
TOP_K = 4
D_FF = 2048
SWIGLU_ALPHA = 1.702
SWIGLU_LIMIT = 7.0
NORM_EPS = 1e-6

LANES = 128
SUBLANES = 8
MAIN_DIM = ATTN_DIM + 2 * KV_DIM + SSM_DIM + CONV_DIM
MOE_TM = 256
VMEM_LIMIT = 56 * 1024 * 1024


def _sigmoid(x):
    return 1.0 / (1.0 + jnp.exp(-x))


def _silu(x):
    return x * _sigmoid(x)


def _rms(x, g):
    return x * lax.rsqrt(jnp.mean(x * x, axis=-1, keepdims=True) + NORM_EPS) * g


def _split3(x):
    hi = x.astype(BF16)
    r1 = x - hi.astype(F32)
    mid = r1.astype(BF16)
    lo = (r1 - mid.astype(F32)).astype(BF16)
    return hi, mid, lo


def _dot_exact_rhs(x, m):
    hi, mid, lo = _split3(x)
    d = functools.partial(jnp.dot, preferred_element_type=F32)
    return d(hi, m) + d(mid, m) + d(lo, m)


_NN = (((1,), (0,)), ((), ()))
_NT = (((1,), (1,)), ((), ()))
_TN = (((0,), (0,)), ((), ()))


def _dot(a, b, dims=_NN):
    assert a.dtype == b.dtype, (a.dtype, b.dtype)
    prec = lax.Precision.HIGHEST if a.dtype == F32 else None
    return lax.dot_general(a, b, dims, preferred_element_type=F32, precision=prec)


def _params(sem, vmem=VMEM_LIMIT):
    return pltpu.CompilerParams(dimension_semantics=sem, vmem_limit_bytes=vmem)


def _mod_kernel(c_ref, w_ref, b_ref, o_ref):
    o_ref[...] = _dot(_silu(c_ref[...]).astype(BF16), w_ref[...].astype(BF16)) + b_ref[...]


def _modulation(c_all, w_mod, b_mod):
    rows, d = c_all.shape
    n = w_mod.shape[1]
    tn = 1536
    return pl.pallas_call(
        _mod_kernel,
        grid=(n // tn,),
        in_specs=[pl.BlockSpec((rows, d), lambda j: (0, 0)),
                  pl.BlockSpec((d, tn), lambda j: (0, j)),
                  pl.BlockSpec((1, tn), lambda j: (0, j))],
        out_specs=pl.BlockSpec((rows, tn), lambda j: (0, j)),
        out_shape=jax.ShapeDtypeStruct((rows, n), F32),
        compiler_params=_params(("arbitrary",)),
        name="modulation",
    )(c_all, w_mod, b_mod.reshape(1, n))


def _rope(t, cos, s1, s2):
    outs = []
    for j in range(t.shape[1] // LANES):
        tj = t[:, j * LANES:(j + 1) * LANES]
        up = pltpu.roll(tj, LANES - ROT_DIM // 2, 1)
        dn = pltpu.roll(tj, ROT_DIM // 2, 1)
        outs.append(tj * cos + up * s1 + dn * s2)
    return jnp.concatenate(outs, axis=1)


def _inproj_kernel(x_ref, sc_ref, sh_ref, g_ref, w_ref, wdt_ref, cos_ref, s1_ref, s2_ref,
                   q_ref, k_ref, v_ref, z_ref, xbc_ref, dt_ref):
    h = _rms(x_ref[0], g_ref[...]) * sc_ref[0] + sh_ref[0]
    hb = h.astype(w_ref.dtype)
    cos, s1, s2 = cos_ref[...], s1_ref[...], s2_ref[...]
    step = 512

    def mm(lo, hi):
        return jnp.concatenate([_dot(hb, w_ref[:, c:min(c + step, hi)]) for c in range(lo, hi, step)], axis=1)

    o = 0
    q_ref[0] = (_rope(mm(o, o + ATTN_DIM), cos, s1, s2) * (HEAD_DIM ** -0.5)).astype(q_ref.dtype)
    o += ATTN_DIM
    k_ref[0] = _rope(mm(o, o + KV_DIM), cos, s1, s2)
    o += KV_DIM
    v_ref[0] = mm(o, o + KV_DIM)
    o += KV_DIM
    z_ref[0] = mm(o, o + SSM_DIM)
    o += SSM_DIM
    xbc_ref[0] = mm(o, o + CONV_DIM)
    dt_ref[0] = _dot(hb, wdt_ref[...])


def _mod_spec(m, tm):
    if m.shape[1] == 1:
        return pl.BlockSpec((1, 1, m.shape[2]), lambda bi, i: (bi, 0, 0))
    return pl.BlockSpec((1, tm, m.shape[2]), lambda bi, i: (bi, i, 0))


def _in_projection(x, sc1p, sh, g, w_main, w_dt, rope_tabs, tm):
    b, s, d = x.shape
    cos, s1, s2 = rope_tabs
    row = lambda bi, i: (bi, i, 0)
    const2 = lambda bi, i: (0, 0)
    tab = lambda bi, i: (i, 0)
    widths = (ATTN_DIM, KV_DIM, KV_DIM, SSM_DIM, CONV_DIM, LANES)
    dtypes = (w_main.dtype, F32, F32, F32, F32, F32)
    return pl.pallas_call(
        _inproj_kernel,
        grid=(b, s // tm),
        in_specs=[pl.BlockSpec((1, tm, d), row),
                  _mod_spec(sc1p, tm),
                  _mod_spec(sh, tm),
                  pl.BlockSpec((1, d), const2),
                  pl.BlockSpec((d, MAIN_DIM), const2, pipeline_mode=pl.Buffered(1)),
                  pl.BlockSpec((d, LANES), const2, pipeline_mode=pl.Buffered(1)),
                  pl.BlockSpec((tm, LANES), tab),
                  pl.BlockSpec((tm, LANES), tab),
                  pl.BlockSpec((tm, LANES), tab)],
        out_specs=[pl.BlockSpec((1, tm, w), row) for w in widths],
        out_shape=[jax.ShapeDtypeStruct((b, s, w), dt) for w, dt in zip(widths, dtypes)],
        compiler_params=_params(("arbitrary", "arbitrary")),
        name="in_projection",
    )(x, sc1p, sh, g, w_main, w_dt, cos, s1, s2)


def _rope_tables(pos):
    inv_freq = ROPE_THETA ** (-jnp.arange(0, ROT_DIM, 2, dtype=F32) / ROT_DIM)
    ang = pos.astype(F32)[:, None] * inv_freq[None, :]
    cos, sin = jnp.cos(ang), jnp.sin(ang)
    half = ROT_DIM // 2
    n = pos.shape[0]
    ones = jnp.ones((n, HEAD_DIM - ROT_DIM), F32)
    zeros_h = jnp.zeros((n, half), F32)
    zeros_r = jnp.zeros((n, HEAD_DIM - ROT_DIM), F32)
    c = jnp.concatenate([cos, cos, ones], axis=1)
    a = jnp.concatenate([-sin, zeros_h, zeros_r], axis=1)
    b = jnp.concatenate([zeros_h, sin, zeros_r], axis=1)
    rep = LANES // HEAD_DIM
    return tuple(jnp.tile(t, (1, rep)) for t in (c, a, b))


KV_PAIRS = N_KV // 2


def _paired_head_columns():
    cols = []
    for j in range(KV_PAIRS):
        for a in range(GQA):
            for g in (2 * j, 2 * j + 1):
                h = g * GQA + a
                cols.extend(range(h * HEAD_DIM, (h + 1) * HEAD_DIM))
    return np.asarray(cols, np.int32)


def _attend_pair(qcols, k2, v2, sink_row, valid):
    rows = qcols[0].shape[0]
    lo = lax.broadcasted_iota(jnp.int32, (rows, LANES), 1) < HEAD_DIM
    zero = jnp.zeros_like(qcols[0])
    lhs = jnp.concatenate([jnp.where(lo, qc, zero) for qc in qcols] + [jnp.where(lo, zero, qc) for qc in qcols], axis=0)
    s = _dot(k2, lhs, _NT)
    if valid is not None:
        s = jnp.where(valid, s, -1e30)
    m = jnp.maximum(jnp.max(s, axis=0, keepdims=True), sink_row)
    p = jnp.exp(s - m)
    den = jnp.sum(p, axis=0, keepdims=True) + jnp.exp(sink_row - m)
    o = _dot((p * (1.0 / den)).astype(v2.dtype), v2, _TN)
    half = GQA * rows
    return [jnp.where(lo, o[a * rows:(a + 1) * rows], o[half + a * rows:half + (a + 1) * rows]) for a in range(GQA)]


def _sink_row(sinks_ref, j, rows):
    heads = [(2 * j) * GQA + a for a in range(GQA)] + [(2 * j + 1) * GQA + a for a in range(GQA)]
    col = lax.broadcasted_iota(jnp.int32, (1, len(heads) * rows), 1) // rows
    out = jnp.zeros((1, len(heads) * rows), F32)
    for n, h in enumerate(heads):
        out = jnp.where(col == n, sinks_ref[h], out)
    return out


def _attn_prompt_kernel(sinks_ref, q_ref, km_ref, kh_ref, vm_ref, vh_ref, o_ref):
    i = pl.program_id(1)
    tq = q_ref.shape[1]
    chunks = tq // CHUNK
    kfull = jnp.concatenate([kh_ref[0], km_ref[0]], axis=0).astype(q_ref.dtype)
    vfull = jnp.concatenate([vh_ref[0], vm_ref[0]], axis=0).astype(q_ref.dtype)
    span = (WIN_CHUNKS + 1) * CHUNK
    col_chunk = lax.broadcasted_iota(jnp.int32, (span, 2 * GQA * CHUNK), 0) // CHUNK
    for j in range(KV_PAIRS):
        k2 = kfull[:, j * LANES:(j + 1) * LANES]
        v2 = vfull[:, j * LANES:(j + 1) * LANES]
        sink = _sink_row(sinks_ref, j, CHUNK)
        for c in range(chunks):
            r0 = c * CHUNK
            slabs = [(j * GQA + a) * LANES for a in range(GQA)]
            qcols = [q_ref[0, r0:r0 + CHUNK, sl:sl + LANES] for sl in slabs]
            valid = None
            if c < WIN_CHUNKS:
                valid = (i * chunks + c - WIN_CHUNKS + col_chunk) >= 0
            outs = _attend_pair(qcols, k2[r0:r0 + span], v2[r0:r0 + span], sink, valid)
            for sl, o in zip(slabs, outs):
                o_ref[0, r0:r0 + CHUNK, sl:sl + LANES] = o.astype(o_ref.dtype)


def _attention_prompt(q, k, v, sinks, tq):
    b, s, _ = q.shape
    halo = WIN_CHUNKS * CHUNK
    ratio = tq // halo
    main = lambda bi, i: (bi, i, 0)
    prev = lambda bi, i: (bi, jnp.maximum(i * ratio - 1, 0), 0)
    return pl.pallas_call(
        _attn_prompt_kernel,
        grid=(b, s // tq),
        in_specs=[pl.BlockSpec(memory_space=pltpu.SMEM),
                  pl.BlockSpec((1, tq, ATTN_DIM), main),
                  pl.BlockSpec((1, tq, KV_DIM), main),
                  pl.BlockSpec((1, halo, KV_DIM), prev),
                  pl.BlockSpec((1, tq, KV_DIM), main),
                  pl.BlockSpec((1, halo, KV_DIM), prev)],
        out_specs=pl.BlockSpec((1, tq, ATTN_DIM), main),
        out_shape=jax.ShapeDtypeStruct((b, s, ATTN_DIM), q.dtype),
        compiler_params=_params(("arbitrary", "arbitrary")),
        name="attention_prompt",
    )(sinks, q, k, k, v, v)


def _attn_sample_kernel(sinks_ref, q_ref, k_ref, v_ref, ck_ref, cv_ref, o_ref, kw_ref, vw_ref):
    s = q_ref.shape[1]
    kf = jnp.concatenate([ck_ref[0], k_ref[0]], axis=0)
    vf = jnp.concatenate([cv_ref[0], v_ref[0]], axis=0)
    n = kf.shape[0]
    kw_ref[0] = kf[n - WIN_CACHE:]
    vw_ref[0] = vf[n - WIN_CACHE:]
    kb, vb = kf.astype(q_ref.dtype), vf.astype(q_ref.dtype)
    for j in range(KV_PAIRS):
        slabs = [(j * GQA + a) * LANES for a in range(GQA)]
        qcols = [q_ref[0, :, sl:sl + LANES] for sl in slabs]
        outs = _attend_pair(qcols, kb[:, j * LANES:(j + 1) * LANES], vb[:, j * LANES:(j + 1) * LANES],
                            _sink_row(sinks_ref, j, s), None)
        for sl, o in zip(slabs, outs):
            o_ref[0, :, sl:sl + LANES] = o.astype(o_ref.dtype)


def _attention_sample(q, k, v, sinks, cache_k, cache_v):
    b, s, _ = q.shape
    blk = lambda bi: (bi, 0, 0)
    return pl.pallas_call(
        _attn_sample_kernel,
        grid=(b,),
        in_specs=[pl.BlockSpec(memory_space=pltpu.SMEM),
                  pl.BlockSpec((1, s, ATTN_DIM), blk),
                  pl.BlockSpec((1, s, KV_DIM), blk),
                  pl.BlockSpec((1, s, KV_DIM), blk),
                  pl.BlockSpec((1, WIN_CACHE, KV_DIM), blk),
                  pl.BlockSpec((1, WIN_CACHE, KV_DIM), blk)],
        out_specs=[pl.BlockSpec((1, s, ATTN_DIM), blk),
                   pl.BlockSpec((1, WIN_CACHE, KV_DIM), blk),
                   pl.BlockSpec((1, WIN_CACHE, KV_DIM), blk)],
        out_shape=[jax.ShapeDtypeStruct((b, s, ATTN_DIM), q.dtype),
                   jax.ShapeDtypeStruct((b, WIN_CACHE, KV_DIM), F32),
                   jax.ShapeDtypeStruct((b, WIN_CACHE, KV_DIM), F32)],
        compiler_params=_params(("arbitrary",)),
        name="attention_sample",
    )(sinks, q, k, v, cache_k, cache_v)


def _ssd_kernel(xbc_ref, z_ref, dt_ref, hist_ref, h0_ref, cw_ref, cb_ref, dtb_ref, alog_ref, dsk_ref, gs_ref,
                ep_ref, el_ref, dmask_ref, causal_ref, tril_ref,
                y_ref, conv_ref, hfin_ref, prev, ht):
    c = pl.program_id(1)
    last = pl.num_programs(1) - 1
    ln = xbc_ref.shape[1]

    @pl.when(c == 0)
    def _():
        prev[...] = hist_ref[...]
        ht[...] = h0_ref[...]

    for bi in range(xbc_ref.shape[0]):
        _ssd_chunk(bi, xbc_ref, z_ref, dt_ref, cw_ref, cb_ref, dtb_ref, alog_ref, dsk_ref, gs_ref,
                   ep_ref, el_ref, dmask_ref, causal_ref, tril_ref, y_ref, prev, ht)

    @pl.when(c == last)
    def _():
        hfin_ref[...] = ht[...]
        conv_ref[...] = xbc_ref[:, ln - (CONV_W - 1):, :]


def _ssd_chunk(bi, xbc_ref, z_ref, dt_ref, cw_ref, cb_ref, dtb_ref, alog_ref, dsk_ref, gs_ref,
               ep_ref, el_ref, dmask_ref, causal_ref, tril_ref, y_ref, prev, ht):
    ln = xbc_ref.shape[1]
    gw = SSM_HPG * SSM_HEAD_DIM
    xr = xbc_ref[bi]
    ext = jnp.concatenate([prev[bi], xr], axis=0)
    conv = cb_ref[...]
    for i in range(CONV_W):
        sh = CONV_W - 1 - i
        tap = xr if sh == 0 else pltpu.roll(ext, sh, 0)[SUBLANES:]
        conv = conv + tap * cw_ref[i:i + 1, :]
    prev[bi] = xr[ln - SUBLANES:]

    act = _silu(conv)
    xs = act[:, :SSM_DIM]
    bm = act[:, SSM_DIM:SSM_DIM + SSM_GROUPS * SSM_STATE]
    cm = act[:, SSM_DIM + SSM_GROUPS * SSM_STATE:]

    dtv = dt_ref[bi] + dtb_ref[...]
    dt = jnp.maximum(dtv, 0.0) + jnp.log1p(jnp.exp(-jnp.abs(dtv)))
    ad = dt * (-jnp.exp(alog_ref[...]))
    hi, mid, lo = _split3(ad)
    tril = tril_ref[...]
    d = functools.partial(jnp.dot, preferred_element_type=F32)
    a_cs = d(tril, hi) + d(tril, mid) + d(tril, lo)
    a_last = a_cs[ln - 1:ln, :]
    stacked = jnp.concatenate([dt, jnp.exp(a_cs), jnp.exp(a_last - a_cs)], axis=0)
    wide = _dot_exact_rhs(stacked, ep_ref[...])
    dt_x, ea_x, ds_x = wide[:ln], wide[ln:2 * ln], wide[2 * ln:]
    cd_x = ea_x[ln - 1:ln, :]

    a_l = _dot_exact_rhs(a_cs, el_ref[...])
    a_s = jnp.sum(a_l * dmask_ref[...], axis=0, keepdims=True)
    lmat = jnp.exp(jnp.where(causal_ref[...] > 0.0, a_l - a_s, -1e30))

    cdt = y_ref.dtype
    xd = xs * dt_x
    xds = (xd * ds_x).astype(cdt)
    xdb = xd.astype(cdt)
    ys = []
    for g in range(SSM_GROUPS):
        bg = bm[:, g * SSM_STATE:(g + 1) * SSM_STATE].astype(cdt)
        cg = cm[:, g * SSM_STATE:(g + 1) * SSM_STATE].astype(cdt)
        cbm = _dot(cg, bg, _NT)
        yd = []
        for r in range(SSM_HPG):
            hd = g * SSM_HPG + r
            w = (cbm * lmat[:, hd * ln:(hd + 1) * ln]).astype(cdt)
            yd.append(_dot(w, xdb[:, hd * SSM_HEAD_DIM:(hd + 1) * SSM_HEAD_DIM]))
        htg = ht[bi, g]
        y_off = _dot(cg, htg.astype(cdt)) * ea_x[:, g * gw:(g + 1) * gw]
        st = _dot(bg, xds[:, g * gw:(g + 1) * gw], _TN)
        ht[bi, g] = htg * cd_x[:, g * gw:(g + 1) * gw] + st
        ys.append(jnp.concatenate(yd, axis=1) + y_off)
    y = jnp.concatenate(ys, axis=1) + dsk_ref[...] * xs
    y = y * _silu(z_ref[bi])
    outs = [_rms(y[:, g * gw:(g + 1) * gw], gs_ref[:, g * gw:(g + 1) * gw]) for g in range(SSM_GROUPS)]
    y_ref[bi] = jnp.concatenate(outs, axis=1).astype(y_ref.dtype)


def _ssd_constants(ln):
    heads = SSM_HEADS
    ep = np.zeros((LANES, heads * SSM_HEAD_DIM), np.float32)
    el = np.zeros((LANES, heads * ln), np.float32)
    for r in range(heads):
        ep[r, r * SSM_HEAD_DIM:(r + 1) * SSM_HEAD_DIM] = 1.0
        el[r, r * ln:(r + 1) * ln] = 1.0
    eye = np.tile(np.eye(ln, dtype=np.float32), (1, heads))
    causal = np.tile(np.tril(np.ones((ln, ln), np.float32)), (1, heads))
    tril = np.tril(np.ones((ln, ln), np.float32))
    return (jnp.asarray(ep, BF16), jnp.asarray(el, BF16), jnp.asarray(eye), jnp.asarray(causal),
            jnp.asarray(tril, BF16))


def _ssd_mixer(xbc, z, dt_raw, hist8, h0t, conv_w8, conv_b, dt_bias, a_log, dskip_x, g_ssm, ln, out_dtype):
    b, s, _ = xbc.shape
    bb = 2 if b % 2 == 0 else 1
    consts = _ssd_constants(ln)
    row = lambda bi, c: (bi, c, 0)
    per_b3 = lambda bi, c: (bi, 0, 0)
    per_b4 = lambda bi, c: (bi, 0, 0, 0)
    const2 = lambda bi, c: (0, 0)
    full = lambda a: pl.BlockSpec(a.shape, const2)
    gw = SSM_HPG * SSM_HEAD_DIM
    return pl.pallas_call(
        _ssd_kernel,
        grid=(b // bb, s // ln),
        in_specs=[pl.BlockSpec((bb, ln, CONV_DIM), row),
                  pl.BlockSpec((bb, ln, SSM_DIM), row),
                  pl.BlockSpec((bb, ln, LANES), row),
                  pl.BlockSpec((bb, SUBLANES, CONV_DIM), per_b3),
                  pl.BlockSpec((bb, SSM_GROUPS, SSM_STATE, gw), per_b4),
                  full(conv_w8), full(conv_b), full(dt_bias), full(a_log), full(dskip_x), full(g_ssm)]
                 + [full(a) for a in consts],
        out_specs=[pl.BlockSpec((bb, ln, SSM_DIM), row),
                   pl.BlockSpec((bb, CONV_W - 1, CONV_DIM), per_b3),
                   pl.BlockSpec((bb, SSM_GROUPS, SSM_STATE, gw), per_b4)],
        out_shape=[jax.ShapeDtypeStruct((b, s, SSM_DIM), out_dtype),
                   jax.ShapeDtypeStruct((b, CONV_W - 1, CONV_DIM), F32),
                   jax.ShapeDtypeStruct((b, SSM_GROUPS, SSM_STATE, gw), F32)],
        scratch_shapes=[pltpu.VMEM((bb, SUBLANES, CONV_DIM), F32),
                        pltpu.VMEM((bb, SSM_GROUPS, SSM_STATE, gw), F32)],
        compiler_params=_params(("arbitrary", "arbitrary")),
        name="ssd_mixer",
    )(xbc, z, dt_raw, hist8, h0t, conv_w8, conv_b, dt_bias, a_log, dskip_x, g_ssm, *consts)


def _outproj_kernel(a_ref, s_ref, x_ref, w_ref, gpost_ref, gate_ref, gpre_ref, sc_ref, sh_ref, wr_ref, br_ref,
                    x1_ref, h2_ref, lg_ref):
    mix = _dot(a_ref[0], w_ref[:ATTN_DIM, :]) + _dot(s_ref[0], w_ref[ATTN_DIM:, :])
    x1 = x_ref[0] + gate_ref[0] * _rms(mix, gpost_ref[...])
    x1_ref[0] = x1
    h2 = _rms(x1, gpre_ref[...]) * sc_ref[0] + sh_ref[0]
    h2_ref[...] = h2
    lg_ref[...] = _dot(wr_ref[...], h2.astype(wr_ref.dtype), _NT) + br_ref[...]


def _out_projection(attn, ssm, x, w_out, g_post, gate, g_pre, sc1p, sh, w_router, b_router, tm):
    b, s, d = x.shape
    nt = s // tm
    row = lambda bi, i: (bi, i, 0)
    const2 = lambda bi, i: (0, 0)
    flat = lambda bi, i: (bi * nt + i, 0)
    b_router = jnp.broadcast_to(b_router, (w_router.shape[0], tm))
    return pl.pallas_call(
        _outproj_kernel,
        grid=(b, nt),
        in_specs=[pl.BlockSpec((1, tm, ATTN_DIM), row),
                  pl.BlockSpec((1, tm, SSM_DIM), row),
                  pl.BlockSpec((1, tm, d), row),
                  pl.BlockSpec((ATTN_DIM + SSM_DIM, d), const2, pipeline_mode=pl.Buffered(1)),
                  pl.BlockSpec((1, d), const2),
                  _mod_spec(gate, tm),
                  pl.BlockSpec((1, d), const2),
                  _mod_spec(sc1p, tm),
                  _mod_spec(sh, tm),
                  pl.BlockSpec(w_router.shape, const2),
                  pl.BlockSpec((w_router.shape[0], tm), const2)],
        out_specs=[pl.BlockSpec((1, tm, d), row),
                   pl.BlockSpec((tm, d), flat),
                   pl.BlockSpec((w_router.shape[0], tm), lambda bi, i: (0, bi * nt + i))],
        out_shape=[jax.ShapeDtypeStruct((b, s, d), F32),
                   jax.ShapeDtypeStruct((b * s, d), F32),
                   jax.ShapeDtypeStruct((w_router.shape[0], b * s), F32)],
        compiler_params=_params(("arbitrary", "arbitrary")),
        name="out_projection",
    )(attn, ssm, x, w_out, g_post, gate, g_pre, sc1p, sh, w_router, b_router)


ROW_IDX, ROW_RANK, ROW_GATE, ROUTE_ROWS = 0, TOP_K, 2 * TOP_K, 4 * TOP_K


def _router_kernel(lp_ref, ls_ref, upper_ref, ones_ref, meta_ref, cnt_ref, *, p_tiles):
    i = pl.program_id(0)
    ne, tt = lp_ref.shape

    @pl.when(i == 0)
    def _():
        cnt_ref[...] = jnp.zeros_like(cnt_ref)

    work = jnp.where(i < p_tiles, lp_ref[...], ls_ref[...])
    row = lax.broadcasted_iota(jnp.int32, (ne, tt), 0).astype(F32)
    vals, hots, idxs = [], [], []
    for _ in range(TOP_K):
        m = jnp.max(work, axis=0, keepdims=True)
        idx = jnp.min(jnp.where(work == m, row, float(ne)), axis=0, keepdims=True)
        hot = row == idx
        vals.append(m)
        hots.append(hot)
        idxs.append(idx)
        work = jnp.where(hot, -jnp.inf, work)
    es = [jnp.exp(v - vals[0]) for v in vals]
    den = es[0] + es[1] + es[2] + es[3]
    onehot = jnp.zeros((ne, tt), F32)
    for hot in hots:
        onehot = jnp.where(hot, 1.0, onehot)
    ohb = onehot.astype(BF16)
    before = jnp.dot(ohb, upper_ref[...], preferred_element_type=F32) + cnt_ref[...]
    ranks = [jnp.sum(jnp.where(hot, before, 0.0), axis=0, keepdims=True) for hot in hots]
    gates = [e / den for e in es]
    pad = [jnp.zeros((ROUTE_ROWS - 3 * TOP_K, tt), F32)]
    meta_ref[...] = jnp.concatenate(idxs + ranks + gates + pad, axis=0)
    cnt_ref[...] = cnt_ref[...] + jnp.dot(ohb, ones_ref[...], preferred_element_type=F32)


def _router(logits_p, logits_s, tt):
    ne = logits_p.shape[0]
    n = logits_p.shape[1] + logits_s.shape[1]
    p_tiles = logits_p.shape[1] // tt
    assert tt == LANES, "running counts are kept lane-replicated at one vreg width"
    upper = jnp.asarray(np.triu(np.ones((tt, tt), np.float32), 1), BF16)
    ones = jnp.ones((tt, tt), BF16)
    return pl.pallas_call(
        functools.partial(_router_kernel, p_tiles=p_tiles),
        grid=(n // tt,),
        in_specs=[pl.BlockSpec((ne, tt), lambda i: (0, jnp.minimum(i, p_tiles - 1))),
                  pl.BlockSpec((ne, tt), lambda i: (0, jnp.maximum(i - p_tiles, 0))),
                  pl.BlockSpec((tt, tt), lambda i: (0, 0)),
                  pl.BlockSpec((tt, tt), lambda i: (0, 0))],
        out_specs=[pl.BlockSpec((ROUTE_ROWS, tt), lambda i: (0, i)),
                   pl.BlockSpec((ne, LANES), lambda i: (0, 0))],
        out_shape=[jax.ShapeDtypeStruct((ROUTE_ROWS, n), F32),
                   jax.ShapeDtypeStruct((ne, LANES), F32)],
        compiler_params=_params(("arbitrary",)),
        name="router",
    )(logits_p, logits_s, upper, ones)


def _dispatch_kernel(zstart_ref, nu_ref, pos_ref, pos_prev_ref, hp_ref, hs_ref, xs_ref, zeros, src, sem_z, sem_l, sem_r,
                     *, p_tiles):
    i = pl.program_id(0)
    n_tiles = pl.num_programs(0)
    tt = src.shape[1]
    tm = zeros.shape[0]
    n_blocks = xs_ref.shape[0] // tm

    def zero_copy(row):
        return pltpu.make_async_copy(zeros, xs_ref.at[pl.ds(pl.multiple_of(row, tm), tm)], sem_z)

    @pl.when(i == 0)
    def _():
        zeros[...] = jnp.zeros_like(zeros)

        def per_expert(act):
            def body(e, carry):
                @pl.when(zstart_ref[e] >= 0)
                def _():
                    act(zero_copy(zstart_ref[e]))
                return carry
            lax.fori_loop(0, N_EXPERTS, body, 0)

        def per_tail(act):
            def body(blk, carry):
                act(zero_copy(blk * tm))
                return carry
            lax.fori_loop(nu_ref[0], n_blocks, body, 0)

        per_expert(lambda cp: cp.start())
        per_tail(lambda cp: cp.start())
        per_expert(lambda cp: cp.wait())
        per_tail(lambda cp: cp.wait())

    def tile_load(j, act):
        b = j % 3

        @pl.when(j < p_tiles)
        def _():
            act(pltpu.make_async_copy(hp_ref.at[pl.ds(pl.multiple_of(j * tt, tt), tt)], src.at[b], sem_l.at[b]))

        @pl.when(j >= p_tiles)
        def _():
            act(pltpu.make_async_copy(hs_ref.at[pl.ds(pl.multiple_of((j - p_tiles) * tt, tt), tt)], src.at[b],
                                      sem_l.at[b]))

    def row_copies(j, p_ref, b):
        return [pltpu.make_async_copy(src.at[b, pl.ds(t, 1)],
                                      xs_ref.at[pl.ds(p_ref[0, 0, t * TOP_K + k], 1)], sem_r.at[j % 2])
                for t in range(tt) for k in range(TOP_K)]

    @pl.when(i == 0)
    def _():
        tile_load(i, lambda cp: cp.start())

    @pl.when(i + 1 < n_tiles)
    def _():
        tile_load(i + 1, lambda cp: cp.start())

    tile_load(i, lambda cp: cp.wait())
    for b in range(3):
        @pl.when(i % 3 == b)
        def _():
            for cp in row_copies(i, pos_ref, b):
                cp.start()

    for b in range(3):
        @pl.when(jnp.logical_and(i > 0, (i + 2) % 3 == b))
        def _():
            for cp in row_copies(i - 1, pos_prev_ref, b):
                cp.wait()

    for b in range(3):
        @pl.when(jnp.logical_and(i == n_tiles - 1, i % 3 == b))
        def _():
            for cp in row_copies(i, pos_ref, b):
                cp.wait()


def _dispatch(h_p, h_s, pos3, zstart, n_used, n_slots, tt):
    row_shape = h_p.shape[1:]
    n = h_p.shape[0] + h_s.shape[0]
    p_tiles = h_p.shape[0] // tt
    return pl.pallas_call(
        functools.partial(_dispatch_kernel, p_tiles=p_tiles),
        grid_spec=pltpu.PrefetchScalarGridSpec(
            num_scalar_prefetch=2,
            grid=(n // tt,),
            in_specs=[pl.BlockSpec((1, 1, tt * TOP_K), lambda i, *_: (i, 0, 0), memory_space=pltpu.SMEM),
                      pl.BlockSpec((1, 1, tt * TOP_K), lambda i, *_: (jnp.maximum(i - 1, 0), 0, 0),
                                   memory_space=pltpu.SMEM),
                      pl.BlockSpec(memory_space=pl.ANY),
                      pl.BlockSpec(memory_space=pl.ANY)],
            out_specs=pl.BlockSpec(memory_space=pl.ANY),
            scratch_shapes=[pltpu.VMEM((MOE_TM,) + row_shape, F32),
                            pltpu.VMEM((3, tt) + row_shape, F32),
                            pltpu.SemaphoreType.DMA(()),
                            pltpu.SemaphoreType.DMA((3,)),
                            pltpu.SemaphoreType.DMA((2,))]),
        out_shape=jax.ShapeDtypeStruct((n_slots,) + row_shape, F32),
        compiler_params=_params(("arbitrary",)),
        name="dispatch",
    )(zstart, n_used, pos3, pos3, h_p, h_s)


W_CHUNK_ROWS = 256


def _stage_expert_weights(i, be_ref, nxt_ref, slot_ref, w_hbm, wbuf, stg, sem, done):
    rows = W_CHUNK_ROWS
    chunks = wbuf.shape[1] // rows
    e, e_next, slot = be_ref[i], nxt_ref[i], slot_ref[i]
    first = jnp.logical_or(i == 0, be_ref[jnp.maximum(i - 1, 0)] != e)

    def chunk_copy(expert, c, buf):
        return pltpu.make_async_copy(w_hbm.at[expert, pl.ds(pl.multiple_of(c * rows, rows), rows)],
                                     stg.at[buf], sem.at[buf])

    def convert_next_chunk(expert, into):
        c = done[0]
        buf = c % 2
        chunk_copy(expert, c, buf).wait()

        @pl.when(c + 1 < chunks)
        def _():
            chunk_copy(expert, c + 1, 1 - buf).start()

        wbuf[into, pl.ds(pl.multiple_of(c * rows, rows), rows), :] = stg[buf].astype(wbuf.dtype)
        done[0] = c + 1

    @pl.when(i == 0)
    def _():
        done[0] = 0
        chunk_copy(e, 0, 0).start()

    @pl.when(first)
    def _():
        def body(_, carry):
            convert_next_chunk(e, slot)
            return carry

        lax.fori_loop(done[0], chunks, body, 0)

        @pl.when(e_next >= 0)
        def _():
            done[0] = 0
            chunk_copy(e_next, 0, 0).start()

    @pl.when(jnp.logical_and(jnp.logical_not(first), jnp.logical_and(e_next >= 0, done[0] < chunks)))
    def _():
        convert_next_chunk(e_next, 1 - slot)

    return slot


def _up_kernel(be_ref, nxt_ref, slot_ref, nu_ref, x_ref, w_hbm, b_ref, act_ref, wbuf, stg, sem, done):
    i = pl.program_id(0)
    fc = 512

    @pl.when(i < nu_ref[0])
    def _():
        slot = _stage_expert_weights(i, be_ref, nxt_ref, slot_ref, w_hbm, wbuf, stg, sem, done)
        x = x_ref[...].astype(wbuf.dtype)
        for c in range(D_FF // fc):
            glu = _dot(x, wbuf[slot, :, c * fc:(c + 1) * fc]) + b_ref[:, c * fc:(c + 1) * fc]
            lin = (_dot(x, wbuf[slot, :, D_FF + c * fc:D_FF + (c + 1) * fc])
                   + b_ref[:, D_FF + c * fc:D_FF + (c + 1) * fc])
            glu = jnp.minimum(glu, SWIGLU_LIMIT)
            lin = jnp.clip(lin, -SWIGLU_LIMIT, SWIGLU_LIMIT)
            act_ref[:, c * fc:(c + 1) * fc] = (glu * _sigmoid(SWIGLU_ALPHA * glu) * (lin + 1.0)).astype(act_ref.dtype)

    @pl.when(i >= nu_ref[0])
    def _():
        act_ref[...] = jnp.zeros_like(act_ref)


def _down_kernel(be_ref, nxt_ref, slot_ref, nu_ref, a_ref, w_hbm, b_ref, y_ref, wbuf, stg, sem, done):
    i = pl.program_id(0)

    @pl.when(i < nu_ref[0])
    def _():
        slot = _stage_expert_weights(i, be_ref, nxt_ref, slot_ref, w_hbm, wbuf, stg, sem, done)
        y_ref[...] = _dot(a_ref[...], wbuf[slot]) + b_ref[...]

    @pl.when(i >= nu_ref[0])
    def _():
        y_ref[...] = jnp.zeros_like(y_ref)


def _grouped(kernel, name, x, w, b, plan, out_tail, out_dtype, extra_scratch=()):
    block_expert, next_expert, slot, n_used = plan
    n_slots = x.shape[0]
    _, kdim, ndim = w.shape
    nb = n_slots // MOE_TM
    zeros = lambda t: (0,) * len(t)
    x_tail, o_tail = x.shape[1:], tuple(out_tail)
    return pl.pallas_call(
        kernel,
        grid_spec=pltpu.PrefetchScalarGridSpec(
            num_scalar_prefetch=4,
            grid=(nb,),
            in_specs=[pl.BlockSpec((MOE_TM,) + x_tail, lambda i, be, nx, sl, nu: (jnp.minimum(i, nu[0] - 1),) + zeros(x_tail)),
                      pl.BlockSpec(memory_space=pl.ANY),
                      pl.BlockSpec((None, 1, ndim), lambda i, be, nx, sl, nu: (be[i], 0, 0))],
            out_specs=pl.BlockSpec((MOE_TM,) + o_tail, lambda i, be, nx, sl, nu: (i,) + zeros(o_tail)),
            scratch_shapes=[pltpu.VMEM((2, kdim, ndim), BF16),
                            pltpu.VMEM((2, W_CHUNK_ROWS, ndim), F32),
                            pltpu.SemaphoreType.DMA((2,)),
                            pltpu.SMEM((1,), jnp.int32)] + list(extra_scratch)),
        out_shape=jax.ShapeDtypeStruct((n_slots,) + o_tail, out_dtype),
        compiler_params=_params(("arbitrary",), 60 * 1024 * 1024),
        name=name,
    )(block_expert, next_expert, slot, n_used, x, w, b)


def _combine_kernel(pos_ref, pos_next_ref, y_ref, gates_ref, x1_ref, gate_ref, g_ref, o_ref, rows, sem):
    i = pl.program_id(0)
    tt = x1_ref.shape[0]
    buf = i % 2

    def copies(p_ref, b):
        return [pltpu.make_async_copy(y_ref.at[pl.ds(p_ref[0, 0, t * TOP_K + k], 1)],
                                      rows.at[b, k, pl.ds(t, 1)], sem.at[b])
                for t in range(tt) for k in range(TOP_K)]

    @pl.when(i == 0)
    def _():
        for cp in copies(pos_ref, 0):
            cp.start()

    for b in range(2):
        @pl.when(jnp.logical_and(i + 1 < pl.num_programs(0), buf != b))
        def _():
            for cp in copies(pos_next_ref, b):
                cp.start()

    for b in range(2):
        @pl.when(buf == b)
        def _():
            for cp in copies(pos_ref, b):
                cp.wait()

    gates = gates_ref[...]
    f = rows[buf, 0] * gates[:, 0:1]
    for k in range(1, TOP_K):
        f = f + rows[buf, k] * gates[:, k:k + 1]
    o_ref[...] = x1_ref[...] + gate_ref[0] * _rms(f, g_ref[...])


def _combine(y, pos3, gates, x1, gate, g_post, tt):
    n, d = x1.shape
    nb = gate.shape[0]
    tiles_per_b = n // tt // nb
    tiles = n // tt
    return pl.pallas_call(
        _combine_kernel,
        grid=(tiles,),
        in_specs=[pl.BlockSpec((1, 1, tt * TOP_K), lambda i: (i, 0, 0), memory_space=pltpu.SMEM),
                  pl.BlockSpec((1, 1, tt * TOP_K), lambda i: (jnp.minimum(i + 1, tiles - 1), 0, 0),
                               memory_space=pltpu.SMEM),
                  pl.BlockSpec(memory_space=pl.ANY),
                  pl.BlockSpec((tt, TOP_K), lambda i: (i, 0)),
                  pl.BlockSpec((tt, d), lambda i: (i, 0)),
                  pl.BlockSpec((1, 1, d), lambda i: (i // tiles_per_b, 0, 0)),
                  pl.BlockSpec((1, d), lambda i: (0, 0))],
        out_specs=pl.BlockSpec((tt, d), lambda i: (i, 0)),
        out_shape=jax.ShapeDtypeStruct((n, d), F32),
        scratch_shapes=[pltpu.VMEM((2, TOP_K, tt) + y.shape[1:], F32), pltpu.SemaphoreType.DMA((2,))],
        compiler_params=_params(("arbitrary",)),
        name="combine",
    )(pos3, pos3, y, gates, x1, gate, g_post)


def _mixer(x, mod, pos, hist8, h0t, ln, tm, tq, wts, cache):
    b, s, d = x.shape
    mods = [m.reshape(b, 1, d) for m in jnp.split(mod, 6, axis=-1)]
    gt_f = mods[5]
    if cache is None:
        xf, fb, fs = x, b, s
    else:
        fb, fs = 1, b * s
        xf = x.reshape(fb, fs, d)
        pos = jnp.tile(pos, b)
        mods = [jnp.broadcast_to(m, (b, s, d)).reshape(fb, fs, d) for m in mods]
    sh_m, sc_m, gt_m, sh_f, sc_f, _ = mods
    proj = _in_projection(xf, 1.0 + sc_m, sh_m, wts["g_mix_pre"], wts["w_main"], wts["w_dt"], _rope_tables(pos), tm)
    q, k, v, z, xbc, dt_raw = [t.reshape(b, s, t.shape[-1]) for t in proj]
    if cache is None:
        attn = _attention_prompt(q, k, v, wts["sinks"], tq)
        k_win, v_win = k[:, s - WIN_CACHE:], v[:, s - WIN_CACHE:]
    else:
        attn, k_win, v_win = _attention_sample(q, k, v, wts["sinks"], *cache)
    ssm, conv_state, hfin = _ssd_mixer(xbc, z, dt_raw, hist8, h0t, wts["conv_w8"], wts["conv_b"], wts["dt_bias"],
                                       wts["a_log"], wts["dskip_x"], wts["g_ssm"], ln, q.dtype)
    x1, h2, logits = _out_projection(attn.reshape(fb, fs, -1), ssm.reshape(fb, fs, -1), xf, wts["w_out"],
                                     wts["g_mix_post"], gt_m, wts["g_ffn_pre"], 1.0 + sc_f, sh_f,
                                     wts["w_router"], wts["b_router"], tm)
    ssm_state = hfin.reshape(b, SSM_GROUPS, SSM_STATE, SSM_HPG, SSM_HEAD_DIM)
    ssm_state = ssm_state.transpose(0, 1, 3, 4, 2).reshape(b, SSM_HEADS, SSM_HEAD_DIM, SSM_STATE)
    states = (k_win.reshape(b, WIN_CACHE, N_KV, HEAD_DIM), v_win.reshape(b, WIN_CACHE, N_KV, HEAD_DIM),
              conv_state, ssm_state)
    return x1.reshape(b * s, d), h2, logits, gt_f, states


def _pad_lanes(a, width=LANES):
    return jnp.pad(a, [(0, 0)] * (a.ndim - 1) + [(0, width - a.shape[-1])])


def _largest_tile(n, cap):
    t = cap
    while n % t:
        t //= 2
    return t


def kernel(x_prompt, x_sample, c_prompt, c_sample, cache_k, cache_v, state_conv, state_ssm, w_mod, b_mod, g_mix_pre, g_mix_post, g_ffn_pre, g_ffn_post, w_in, conv_w, conv_b, dt_bias, a_log, d_skip, g_ssm, sinks, w_out, w_router, b_router, w_up, b_up, w_down, b_down):
    depth = w_mod.shape[0]
    assert depth == 1, "single-layer step"
    l = 0
    bp, sp, d = x_prompt.shape
    bs, ss, _ = x_sample.shape
    n_p, n_s = bp * sp, bs * ss
    n_tok = n_p + n_s

    c_all = jnp.concatenate([c_prompt, c_sample], axis=0)
    c_rows = -(-c_all.shape[0] // SUBLANES) * SUBLANES
    mod_all = _modulation(jnp.pad(c_all, ((0, c_rows - c_all.shape[0]), (0, 0))), w_mod[l], b_mod[l])

    row2 = lambda a: a.reshape(1, -1)
    paired = _paired_head_columns()
    wts = {
        "g_mix_pre": row2(g_mix_pre[l]), "g_mix_post": row2(g_mix_post[l]), "g_ffn_pre": row2(g_ffn_pre[l]),
        "w_main": jnp.concatenate([w_in[l][:, :ATTN_DIM][:, paired], w_in[l][:, ATTN_DIM:MAIN_DIM]],
                                  axis=1).astype(BF16),
        "w_dt": _pad_lanes(w_in[l][:, MAIN_DIM:]).astype(BF16),
        "sinks": sinks[l],
        "conv_w8": jnp.pad(conv_w[l], ((0, SUBLANES - CONV_W), (0, 0))), "conv_b": row2(conv_b[l]),
        "dt_bias": _pad_lanes(row2(dt_bias[l])), "a_log": _pad_lanes(row2(a_log[l])),
        "dskip_x": row2(jnp.repeat(d_skip[l], SSM_HEAD_DIM)), "g_ssm": row2(g_ssm[l]),
        "w_out": jnp.concatenate([w_out[l][:ATTN_DIM][paired], w_out[l][ATTN_DIM:]], axis=0).astype(BF16),
        "w_router": w_router[l].T.astype(BF16), "b_router": b_router[l].reshape(-1, 1),
    }
    gw = SSM_HPG * SSM_HEAD_DIM

    hist_p = jnp.zeros((bp, SUBLANES, CONV_DIM), F32)
    h0_p = jnp.zeros((bp, SSM_GROUPS, SSM_STATE, gw), F32)
    x1_p, h2_p, lg_p, gtf_p, st_p = _mixer(x_prompt, mod_all[:bp], jnp.arange(sp, dtype=jnp.int32), hist_p, h0_p,
                                           SSD_CHUNK, _largest_tile(sp, 256), _largest_tile(sp, 256), wts, None)

    hist_s = jnp.pad(state_conv[l], ((0, 0), (SUBLANES - (CONV_W - 1), 0), (0, 0)))
    h0_s = state_ssm[l].astype(F32).reshape(bs, SSM_GROUPS, SSM_HPG, SSM_HEAD_DIM, SSM_STATE)
    h0_s = h0_s.transpose(0, 1, 4, 2, 3).reshape(bs, SSM_GROUPS, SSM_STATE, gw)
    cache = (cache_k[l].reshape(bs, WIN_CACHE, KV_DIM), cache_v[l].reshape(bs, WIN_CACHE, KV_DIM))
    x1_s, h2_s, lg_s, gtf_s, st_s = _mixer(x_sample, mod_all[bp:bp + bs], PAST_LEN + jnp.arange(ss, dtype=jnp.int32),
                                           hist_s, h0_s, ss, n_s, ss, wts, cache)

    tt = _largest_tile(n_s, 128)
    assert n_p % tt == 0
    route, counts = _router(lg_p, lg_s, tt)
    top_idx = route[ROW_IDX:ROW_IDX + TOP_K].T.astype(jnp.int32)
    rank = route[ROW_RANK:ROW_RANK + TOP_K].T.astype(jnp.int32)
    gates = route[ROW_GATE:ROW_GATE + TOP_K].T
    cnt = counts[:, 0].astype(jnp.int32)
    padded = (cnt + MOE_TM - 1) // MOE_TM * MOE_TM
    pend = jnp.cumsum(padded)
    offs = pend - padded
    pos = offs[top_idx] + rank
    n_blocks = -(-n_tok * TOP_K // MOE_TM) + N_EXPERTS
    n_slots = n_blocks * MOE_TM
    n_used = (pend[-1] // MOE_TM).astype(jnp.int32)
    blk = jnp.arange(n_blocks, dtype=jnp.int32)
    blk_row = jnp.minimum(blk, n_used - 1) * MOE_TM
    block_expert = jnp.minimum(jnp.sum(pend[None, :] <= blk_row[:, None], axis=1), N_EXPERTS - 1).astype(jnp.int32)
    zstart = jnp.where(cnt > 0, pend - MOE_TM, -1).astype(jnp.int32)
    pos3 = pos.reshape(n_tok // tt, 1, tt * TOP_K)
    nu = n_used.reshape(1)
    xs = _dispatch(h2_p, h2_s, pos3, zstart, nu, n_slots, tt)
    e_ids = jnp.arange(N_EXPERTS, dtype=jnp.int32)
    live = jnp.where(cnt > 0, e_ids, N_EXPERTS)
    later = jnp.concatenate([lax.cummin(live, reverse=True)[1:], jnp.full((1,), N_EXPERTS, jnp.int32)])
    next_live = jnp.where(later < N_EXPERTS, later, -1).astype(jnp.int32)
    visit = (jnp.cumsum((cnt > 0).astype(jnp.int32)) - 1) % 2
    plan = (block_expert, next_live[block_expert], visit[block_expert].astype(jnp.int32), nu)
    act = _grouped(_up_kernel, "expert_up", xs, w_up[l], b_up[l].reshape(N_EXPERTS, 1, -1), plan, (D_FF,), BF16)
    y = _grouped(_down_kernel, "expert_down", act, w_down[l], b_down[l].reshape(N_EXPERTS, 1, -1), plan,
                 xs.shape[1:], F32)

    g_post = row2(g_ffn_post[l])
    tt_p = _largest_tile(sp, 128)
    y_p = _combine(y, pos[:n_p].reshape(n_p // tt_p, 1, tt_p * TOP_K), gates[:n_p], x1_p,
                   gtf_p, g_post, tt_p).reshape(bp, sp, d)
    y_s = _combine(y, pos[n_p:].reshape(bs, 1, ss * TOP_K), gates[n_p:], x1_s,
                   gtf_s, g_post, ss).reshape(bs, ss, d)

    stack = lambda a: a[None]
    return (y_p, y_s, stack(st_p[0]), stack(st_p[1]), stack(st_p[2]), stack(st_p[3]),
            stack(st_s[0]), stack(st_s[1]), stack(st_s[2]), stack(st_s[3]))
```

```python
import functools

import numpy as np
import jax
import jax.numpy as jnp
from jax import lax
from jax.experimental import pallas as pl
from jax.experimental.pallas import tpu as pltpu

F32 = jnp.float32
BF16 = jnp.bfloat16

D_MODEL = 2048
CHUNK = 64
N_HEADS = 16
N_KV = 4
HEAD_DIM = 64
GQA = N_HEADS // N_KV
ATTN_DIM = N_HEADS * HEAD_DIM
KV_DIM = N_KV * HEAD_DIM
WINDOW = 128
WIN_CHUNKS = WINDOW // CHUNK
PAST_LEN = 4096
WIN_CACHE = min(WINDOW, PAST_LEN)
ROT_DIM = HEAD_DIM // 4
ROPE_THETA = 500000.0
SSM_HEADS = 16
SSM_HEAD_DIM = 64
SSM_DIM = SSM_HEADS * SSM_HEAD_DIM
SSM_GROUPS = 2
SSM_HPG = SSM_HEADS // SSM_GROUPS
SSM_STATE = 128
CONV_W = 4
CONV_DIM = SSM_DIM + 2 * SSM_GROUPS * SSM_STATE
SSD_CHUNK = 64
N_EXPERTS = 32
TOP_K = 4
D_FF = 2048
SWIGLU_ALPHA = 1.702
SWIGLU_LIMIT = 7.0
NORM_EPS = 1e-6

LANES = 128
SUBLANES = 8
MAIN_DIM = ATTN_DIM + 2 * KV_DIM + SSM_DIM + CONV_DIM
MOE_TM = 256
VMEM_LIMIT = 56 * 1024 * 1024


def _sigmoid(x):
    return 1.0 / (1.0 + jnp.exp(-x))


def _silu(x):
    return x * _sigmoid(x)


def _rms(x, g):
    return x * lax.rsqrt(jnp.mean(x * x, axis=-1, keepdims=True) + NORM_EPS) * g


def _pack_bf16_pairs(x):
    n = x.shape[1] // 2
    hi = pltpu.bitcast(x[:, :n].astype(BF16).astype(F32), jnp.uint32)
    lo = pltpu.bitcast(x[:, n:].astype(BF16).astype(F32), jnp.uint32)
    return hi | (lo >> 16)


def _unpack_bf16_pairs(u):
    hi = pltpu.bitcast(u & jnp.uint32(0xFFFF0000), F32).astype(BF16)
    lo = pltpu.bitcast(u << 16, F32).astype(BF16)
    return jnp.concatenate([hi, lo], axis=1)


def _split3(x):
    hi = x.astype(BF16)
    r1 = x - hi.astype(F32)
    mid = r1.astype(BF16)
    lo = (r1 - mid.astype(F32)).astype(BF16)
    return hi, mid, lo


def _dot_exact_rhs(x, m):
    hi, mid, lo = _split3(x)
    d = functools.partial(jnp.dot, preferred_element_type=F32)
    return d(hi, m) + d(mid, m) + d(lo, m)


_NN = (((1,), (0,)), ((), ()))
_NT = (((1,), (1,)), ((), ()))
_TN = (((0,), (0,)), ((), ()))


def _dot(a, b, dims=_NN):
    assert a.dtype == b.dtype, (a.dtype, b.dtype)
    prec = lax.Precision.HIGHEST if a.dtype == F32 else None
    return lax.dot_general(a, b, dims, preferred_element_type=F32, precision=prec)


def _params(sem, vmem=VMEM_LIMIT):
    return pltpu.CompilerParams(dimension_semantics=sem, vmem_limit_bytes=vmem)


def _mod_kernel(c_ref, w_ref, b_ref, o_ref):
    o_ref[...] = _dot(_silu(c_ref[...]).astype(BF16), w_ref[...].astype(BF16)) + b_ref[...]


def _modulation(c_all, w_mod, b_mod):
    rows, d = c_all.shape
    n = w_mod.shape[1]
    tn = 1536
    return pl.pallas_call(
        _mod_kernel,
        grid=(n // tn,),
        in_specs=[pl.BlockSpec((rows, d), lambda j: (0, 0)),
                  pl.BlockSpec((d, tn), lambda j: (0, j)),
                  pl.BlockSpec((1, tn), lambda j: (0, j))],
        out_specs=pl.BlockSpec((rows, tn), lambda j: (0, j)),
        out_shape=jax.ShapeDtypeStruct((rows, n), F32),
        compiler_params=_params(("arbitrary",)),
        name="modulation",
    )(c_all, w_mod, b_mod.reshape(1, n))


def _rope(t, cos, s1, s2):
    outs = []
    for j in range(t.shape[1] // LANES):
        tj = t[:, j * LANES:(j + 1) * LANES]
        up = pltpu.roll(tj, LANES - ROT_DIM // 2, 1)
        dn = pltpu.roll(tj, ROT_DIM // 2, 1)
        outs.append(tj * cos + up * s1 + dn * s2)
    return jnp.concatenate(outs, axis=1)


def _inproj_kernel(x_ref, sc_ref, sh_ref, g_ref, w_ref, wdt_ref, cos_ref, s1_ref, s2_ref,
                   q_ref, k_ref, v_ref, z_ref, xbc_ref, dt_ref):
    h = _rms(x_ref[0], g_ref[...]) * sc_ref[0] + sh_ref[0]
    hb = h.astype(w_ref.dtype)
    cos, s1, s2 = cos_ref[...], s1_ref[...], s2_ref[...]
    step = 512

    def mm(lo, hi):
        return jnp.concatenate([_dot(hb, w_ref[:, c:min(c + step, hi)]) for c in range(lo, hi, step)], axis=1)

    o = 0
    q_ref[0] = (_rope(mm(o, o + ATTN_DIM), cos, s1, s2) * (HEAD_DIM ** -0.5)).astype(q_ref.dtype)
    o += ATTN_DIM
    k_ref[0] = _rope(mm(o, o + KV_DIM), cos, s1, s2)
    o += KV_DIM
    v_ref[0] = mm(o, o + KV_DIM)
    o += KV_DIM
    z_ref[0] = mm(o, o + SSM_DIM)
    o += SSM_DIM
    xbc_ref[0] = mm(o, o + CONV_DIM)
    dt_ref[0] = _dot(hb, wdt_ref[...])


def _mod_spec(m, tm):
    if m.shape[1] == 1:
        return pl.BlockSpec((1, 1, m.shape[2]), lambda bi, i: (bi, 0, 0))
    return pl.BlockSpec((1, tm, m.shape[2]), lambda bi, i: (bi, i, 0))


def _in_projection(x, sc1p, sh, g, w_main, w_dt, rope_tabs, tm):
    b, s, d = x.shape
    cos, s1, s2 = rope_tabs
    row = lambda bi, i: (bi, i, 0)
    const2 = lambda bi, i: (0, 0)
    tab = lambda bi, i: (i, 0)
    widths = (ATTN_DIM, KV_DIM, KV_DIM, SSM_DIM, CONV_DIM, LANES)
    dtypes = (w_main.dtype, F32, F32, F32, F32, F32)
    return pl.pallas_call(
        _inproj_kernel,
        grid=(b, s // tm),
        in_specs=[pl.BlockSpec((1, tm, d), row),
                  _mod_spec(sc1p, tm),
                  _mod_spec(sh, tm),
                  pl.BlockSpec((1, d), const2),
                  pl.BlockSpec((d, MAIN_DIM), const2, pipeline_mode=pl.Buffered(1)),
                  pl.BlockSpec((d, LANES), const2, pipeline_mode=pl.Buffered(1)),
                  pl.BlockSpec((tm, LANES), tab),
                  pl.BlockSpec((tm, LANES), tab),
                  pl.BlockSpec((tm, LANES), tab)],
        out_specs=[pl.BlockSpec((1, tm, w), row) for w in widths],
        out_shape=[jax.ShapeDtypeStruct((b, s, w), dt) for w, dt in zip(widths, dtypes)],
        compiler_params=_params(("arbitrary", "arbitrary")),
        name="in_projection",
    )(x, sc1p, sh, g, w_main, w_dt, cos, s1, s2)


def _rope_tables(pos):
    inv_freq = ROPE_THETA ** (-jnp.arange(0, ROT_DIM, 2, dtype=F32) / ROT_DIM)
    ang = pos.astype(F32)[:, None] * inv_freq[None, :]
    cos, sin = jnp.cos(ang), jnp.sin(ang)
    half = ROT_DIM // 2
    n = pos.shape[0]
    ones = jnp.ones((n, HEAD_DIM - ROT_DIM), F32)
    zeros_h = jnp.zeros((n, half), F32)
    zeros_r = jnp.zeros((n, HEAD_DIM - ROT_DIM), F32)
    c = jnp.concatenate([cos, cos, ones], axis=1)
    a = jnp.concatenate([-sin, zeros_h, zeros_r], axis=1)
    b = jnp.concatenate([zeros_h, sin, zeros_r], axis=1)
    rep = LANES // HEAD_DIM
    return tuple(jnp.tile(t, (1, rep)) for t in (c, a, b))


KV_PAIRS = N_KV // 2


def _paired_head_columns():
    cols = []
    for j in range(KV_PAIRS):
        for a in range(GQA):
            for g in (2 * j, 2 * j + 1):
                h = g * GQA + a
                cols.extend(range(h * HEAD_DIM, (h + 1) * HEAD_DIM))
    return np.asarray(cols, np.int32)


def _attend_pair(qcols, k2, v2, sink_row, valid):
    rows = qcols[0].shape[0]
    lo = lax.broadcasted_iota(jnp.int32, (rows, LANES), 1) < HEAD_DIM
    zero = jnp.zeros_like(qcols[0])
    lhs = jnp.concatenate([jnp.where(lo, qc, zero) for qc in qcols] + [jnp.where(lo, zero, qc) for qc in qcols], axis=0)
    s = _dot(k2, lhs, _NT)
    if valid is not None:
        s = jnp.where(valid, s, -1e30)
    m = jnp.maximum(jnp.max(s, axis=0, keepdims=True), sink_row)
    p = jnp.exp(s - m)
    den = jnp.sum(p, axis=0, keepdims=True) + jnp.exp(sink_row - m)
    o = _dot((p * (1.0 / den)).astype(v2.dtype), v2, _TN)
    half = GQA * rows
    return [jnp.where(lo, o[a * rows:(a + 1) * rows], o[half + a * rows:half + (a + 1) * rows]) for a in range(GQA)]


def _sink_row(sinks_ref, j, rows):
    heads = [(2 * j) * GQA + a for a in range(GQA)] + [(2 * j + 1) * GQA + a for a in range(GQA)]
    col = lax.broadcasted_iota(jnp.int32, (1, len(heads) * rows), 1) // rows
    out = jnp.zeros((1, len(heads) * rows), F32)
    for n, h in enumerate(heads):
        out = jnp.where(col == n, sinks_ref[h], out)
    return out


def _attn_prompt_kernel(sinks_ref, q_ref, km_ref, kh_ref, vm_ref, vh_ref, o_ref):
    i = pl.program_id(1)
    tq = q_ref.shape[1]
    chunks = tq // CHUNK
    kfull = jnp.concatenate([kh_ref[0], km_ref[0]], axis=0).astype(q_ref.dtype)
    vfull = jnp.concatenate([vh_ref[0], vm_ref[0]], axis=0).astype(q_ref.dtype)
    span = (WIN_CHUNKS + 1) * CHUNK
    col_chunk = lax.broadcasted_iota(jnp.int32, (span, 2 * GQA * CHUNK), 0) // CHUNK
    for j in range(KV_PAIRS):
        k2 = kfull[:, j * LANES:(j + 1) * LANES]
        v2 = vfull[:, j * LANES:(j + 1) * LANES]
        sink = _sink_row(sinks_ref, j, CHUNK)
        for c in range(chunks):
            r0 = c * CHUNK
            slabs = [(j * GQA + a) * LANES for a in range(GQA)]
            qcols = [q_ref[0, r0:r0 + CHUNK, sl:sl + LANES] for sl in slabs]
            valid = None
            if c < WIN_CHUNKS:
                valid = (i * chunks + c - WIN_CHUNKS + col_chunk) >= 0
            outs = _attend_pair(qcols, k2[r0:r0 + span], v2[r0:r0 + span], sink, valid)
            for sl, o in zip(slabs, outs):
                o_ref[0, r0:r0 + CHUNK, sl:sl + LANES] = o.astype(o_ref.dtype)


def _attention_prompt(q, k, v, sinks, tq):
    b, s, _ = q.shape
    halo = WIN_CHUNKS * CHUNK
    ratio = tq // halo
    main = lambda bi, i: (bi, i, 0)
    prev = lambda bi, i: (bi, jnp.maximum(i * ratio - 1, 0), 0)
    return pl.pallas_call(
        _attn_prompt_kernel,
        grid=(b, s // tq),
        in_specs=[pl.BlockSpec(memory_space=pltpu.SMEM),
                  pl.BlockSpec((1, tq, ATTN_DIM), main),
                  pl.BlockSpec((1, tq, KV_DIM), main),
                  pl.BlockSpec((1, halo, KV_DIM), prev),
                  pl.BlockSpec((1, tq, KV_DIM), main),
                  pl.BlockSpec((1, halo, KV_DIM), prev)],
        out_specs=pl.BlockSpec((1, tq, ATTN_DIM), main),
        out_shape=jax.ShapeDtypeStruct((b, s, ATTN_DIM), q.dtype),
        compiler_params=_params(("arbitrary", "arbitrary")),
        name="attention_prompt",
    )(sinks, q, k, k, v, v)


def _attn_sample_kernel(sinks_ref, q_ref, k_ref, v_ref, ck_ref, cv_ref, o_ref, kw_ref, vw_ref):
    s = q_ref.shape[1]
    kf = jnp.concatenate([ck_ref[0], k_ref[0]], axis=0)
    vf = jnp.concatenate([cv_ref[0], v_ref[0]], axis=0)
    n = kf.shape[0]
    kw_ref[0] = kf[n - WIN_CACHE:]
    vw_ref[0] = vf[n - WIN_CACHE:]
    kb, vb = kf.astype(q_ref.dtype), vf.astype(q_ref.dtype)
    for j in range(KV_PAIRS):
        slabs = [(j * GQA + a) * LANES for a in range(GQA)]
        qcols = [q_ref[0, :, sl:sl + LANES] for sl in slabs]
        outs = _attend_pair(qcols, kb[:, j * LANES:(j + 1) * LANES], vb[:, j * LANES:(j + 1) * LANES],
                            _sink_row(sinks_ref, j, s), None)
        for sl, o in zip(slabs, outs):
            o_ref[0, :, sl:sl + LANES] = o.astype(o_ref.dtype)


def _attention_sample(q, k, v, sinks, cache_k, cache_v):
    b, s, _ = q.shape
    blk = lambda bi: (bi, 0, 0)
    return pl.pallas_call(
        _attn_sample_kernel,
        grid=(b,),
        in_specs=[pl.BlockSpec(memory_space=pltpu.SMEM),
                  pl.BlockSpec((1, s, ATTN_DIM), blk),
                  pl.BlockSpec((1, s, KV_DIM), blk),
                  pl.BlockSpec((1, s, KV_DIM), blk),
                  pl.BlockSpec((1, WIN_CACHE, KV_DIM), blk),
                  pl.BlockSpec((1, WIN_CACHE, KV_DIM), blk)],
        out_specs=[pl.BlockSpec((1, s, ATTN_DIM), blk),
                   pl.BlockSpec((1, WIN_CACHE, KV_DIM), blk),
                   pl.BlockSpec((1, WIN_CACHE, KV_DIM), blk)],
        out_shape=[jax.ShapeDtypeStruct((b, s, ATTN_DIM), q.dtype),
                   jax.ShapeDtypeStruct((b, WIN_CACHE, KV_DIM), F32),
                   jax.ShapeDtypeStruct((b, WIN_CACHE, KV_DIM), F32)],
        compiler_params=_params(("arbitrary",)),
        name="attention_sample",
    )(sinks, q, k, v, cache_k, cache_v)


def _ssd_kernel(xbc_ref, z_ref, dt_ref, hist_ref, h0_ref, cw_ref, cb_ref, dtb_ref, alog_ref, dsk_ref, gs_ref,
                ep_ref, el_ref, dmask_ref, causal_ref, tril_ref,
                y_ref, conv_ref, hfin_ref, prev, ht):
    c = pl.program_id(1)
    last = pl.num_programs(1) - 1
    ln = xbc_ref.shape[1]

    @pl.when(c == 0)
    def _():
        prev[...] = hist_ref[...]
        ht[...] = h0_ref[...]

    for bi in range(xbc_ref.shape[0]):
        _ssd_chunk(bi, xbc_ref, z_ref, dt_ref, cw_ref, cb_ref, dtb_ref, alog_ref, dsk_ref, gs_ref,
                   ep_ref, el_ref, dmask_ref, causal_ref, tril_ref, y_ref, prev, ht)

    @pl.when(c == last)
    def _():
        hfin_ref[...] = ht[...]
        conv_ref[...] = xbc_ref[:, ln - (CONV_W - 1):, :]


def _ssd_chunk(bi, xbc_ref, z_ref, dt_ref, cw_ref, cb_ref, dtb_ref, alog_ref, dsk_ref, gs_ref,
               ep_ref, el_ref, dmask_ref, causal_ref, tril_ref, y_ref, prev, ht):
    ln = xbc_ref.shape[1]
    gw = SSM_HPG * SSM_HEAD_DIM
    xr = xbc_ref[bi]
    ext = jnp.concatenate([prev[bi], xr], axis=0)
    conv = cb_ref[...]
    for i in range(CONV_W):
        sh = CONV_W - 1 - i
        tap = xr if sh == 0 else pltpu.roll(ext, sh, 0)[SUBLANES:]
        conv = conv + tap * cw_ref[i:i + 1, :]
    prev[bi] = xr[ln - SUBLANES:]

    act = _silu(conv)
    xs = act[:, :SSM_DIM]
    bm = act[:, SSM_DIM:SSM_DIM + SSM_GROUPS * SSM_STATE]
    cm = act[:, SSM_DIM + SSM_GROUPS * SSM_STATE:]

    dtv = dt_ref[bi] + dtb_ref[...]
    dt = jnp.maximum(dtv, 0.0) + jnp.log1p(jnp.exp(-jnp.abs(dtv)))
    ad = dt * (-jnp.exp(alog_ref[...]))
    hi, mid, lo = _split3(ad)
    tril = tril_ref[...]
    d = functools.partial(jnp.dot, preferred_element_type=F32)
    a_cs = d(tril, hi) + d(tril, mid) + d(tril, lo)
    a_last = a_cs[ln - 1:ln, :]
    stacked = jnp.concatenate([dt, jnp.exp(a_cs), jnp.exp(a_last - a_cs)], axis=0)
    wide = _dot_exact_rhs(stacked, ep_ref[...])
    dt_x, ea_x, ds_x = wide[:ln], wide[ln:2 * ln], wide[2 * ln:]
    cd_x = ea_x[ln - 1:ln, :]

    a_l = _dot_exact_rhs(a_cs, el_ref[...])
    a_s = jnp.sum(a_l * dmask_ref[...], axis=0, keepdims=True)
    lmat = jnp.exp(jnp.where(causal_ref[...] > 0.0, a_l - a_s, -1e30))

    cdt = y_ref.dtype
    xd = xs * dt_x
    xds = (xd * ds_x).astype(cdt)
    xdb = xd.astype(cdt)
    ys = []
    for g in range(SSM_GROUPS):
        bg = bm[:, g * SSM_STATE:(g + 1) * SSM_STATE].astype(cdt)
        cg = cm[:, g * SSM_STATE:(g + 1) * SSM_STATE].astype(cdt)
        cbm = _dot(cg, bg, _NT)
        yd = []
        for r in range(SSM_HPG):
            hd = g * SSM_HPG + r
            w = (cbm * lmat[:, hd * ln:(hd + 1) * ln]).astype(cdt)
            yd.append(_dot(w, xdb[:, hd * SSM_HEAD_DIM:(hd + 1) * SSM_HEAD_DIM]))
        htg = ht[bi, g]
        y_off = _dot(cg, htg.astype(cdt)) * ea_x[:, g * gw:(g + 1) * gw]
        st = _dot(bg, xds[:, g * gw:(g + 1) * gw], _TN)
        ht[bi, g] = htg * cd_x[:, g * gw:(g + 1) * gw] + st
        ys.append(jnp.concatenate(yd, axis=1) + y_off)
    y = jnp.concatenate(ys, axis=1) + dsk_ref[...] * xs
    y = y * _silu(z_ref[bi])
    outs = [_rms(y[:, g * gw:(g + 1) * gw], gs_ref[:, g * gw:(g + 1) * gw]) for g in range(SSM_GROUPS)]
    y_ref[bi] = jnp.concatenate(outs, axis=1).astype(y_ref.dtype)


def _ssd_constants(ln):
    heads = SSM_HEADS
    ep = np.zeros((LANES, heads * SSM_HEAD_DIM), np.float32)
    el = np.zeros((LANES, heads * ln), np.float32)
    for r in range(heads):
        ep[r, r * SSM_HEAD_DIM:(r + 1) * SSM_HEAD_DIM] = 1.0
        el[r, r * ln:(r + 1) * ln] = 1.0
    eye = np.tile(np.eye(ln, dtype=np.float32), (1, heads))
    causal = np.tile(np.tril(np.ones((ln, ln), np.float32)), (1, heads))
    tril = np.tril(np.ones((ln, ln), np.float32))
    return (jnp.asarray(ep, BF16), jnp.asarray(el, BF16), jnp.asarray(eye), jnp.asarray(causal),
            jnp.asarray(tril, BF16))


def _ssd_mixer(xbc, z, dt_raw, hist8, h0t, conv_w8, conv_b, dt_bias, a_log, dskip_x, g_ssm, ln, out_dtype):
    b, s, _ = xbc.shape
    bb = 2 if b % 2 == 0 else 1
    consts = _ssd_constants(ln)
    row = lambda bi, c: (bi, c, 0)
    per_b3 = lambda bi, c: (bi, 0, 0)
    per_b4 = lambda bi, c: (bi, 0, 0, 0)
    const2 = lambda bi, c: (0, 0)
    full = lambda a: pl.BlockSpec(a.shape, const2)
    gw = SSM_HPG * SSM_HEAD_DIM
    return pl.pallas_call(
        _ssd_kernel,
        grid=(b // bb, s // ln),
        in_specs=[pl.BlockSpec((bb, ln, CONV_DIM), row),
                  pl.BlockSpec((bb, ln, SSM_DIM), row),
                  pl.BlockSpec((bb, ln, LANES), row),
                  pl.BlockSpec((bb, SUBLANES, CONV_DIM), per_b3),
                  pl.BlockSpec((bb, SSM_GROUPS, SSM_STATE, gw), per_b4),
                  full(conv_w8), full(conv_b), full(dt_bias), full(a_log), full(dskip_x), full(g_ssm)]
                 + [full(a) for a in consts],
        out_specs=[pl.BlockSpec((bb, ln, SSM_DIM), row),
                   pl.BlockSpec((bb, CONV_W - 1, CONV_DIM), per_b3),
                   pl.BlockSpec((bb, SSM_GROUPS, SSM_STATE, gw), per_b4)],
        out_shape=[jax.ShapeDtypeStruct((b, s, SSM_DIM), out_dtype),
                   jax.ShapeDtypeStruct((b, CONV_W - 1, CONV_DIM), F32),
                   jax.ShapeDtypeStruct((b, SSM_GROUPS, SSM_STATE, gw), F32)],
        scratch_shapes=[pltpu.VMEM((bb, SUBLANES, CONV_DIM), F32),
                        pltpu.VMEM((bb, SSM_GROUPS, SSM_STATE, gw), F32)],
        compiler_params=_params(("arbitrary", "arbitrary")),
        name="ssd_mixer",
    )(xbc, z, dt_raw, hist8, h0t, conv_w8, conv_b, dt_bias, a_log, dskip_x, g_ssm, *consts)


def _outproj_kernel(a_ref, s_ref, x_ref, w_ref, gpost_ref, gate_ref, gpre_ref, sc_ref, sh_ref, wr_ref, br_ref,
                    x1_ref, h2_ref, lg_ref):
    mix = _dot(a_ref[0], w_ref[:ATTN_DIM, :]) + _dot(s_ref[0], w_ref[ATTN_DIM:, :])
    x1 = x_ref[0] + gate_ref[0] * _rms(mix, gpost_ref[...])
    x1_ref[0] = x1
    h2 = _rms(x1, gpre_ref[...]) * sc_ref[0] + sh_ref[0]
    h2_ref[...] = _pack_bf16_pairs(h2)
    lg_ref[...] = _dot(wr_ref[...], h2.astype(wr_ref.dtype), _NT) + br_ref[...]


def _out_projection(attn, ssm, x, w_out, g_post, gate, g_pre, sc1p, sh, w_router, b_router, tm):
    b, s, d = x.shape
    nt = s // tm
    row = lambda bi, i: (bi, i, 0)
    const2 = lambda bi, i: (0, 0)
    flat = lambda bi, i: (bi * nt + i, 0)
    b_router = jnp.broadcast_to(b_router, (w_router.shape[0], tm))
    return pl.pallas_call(
        _outproj_kernel,
        grid=(b, nt),
        in_specs=[pl.BlockSpec((1, tm, ATTN_DIM), row),
                  pl.BlockSpec((1, tm, SSM_DIM), row),
                  pl.BlockSpec((1, tm, d), row),
                  pl.BlockSpec((ATTN_DIM + SSM_DIM, d), const2, pipeline_mode=pl.Buffered(1)),
                  pl.BlockSpec((1, d), const2),
                  _mod_spec(gate, tm),
                  pl.BlockSpec((1, d), const2),
                  _mod_spec(sc1p, tm),
                  _mod_spec(sh, tm),
                  pl.BlockSpec(w_router.shape, const2),
                  pl.BlockSpec((w_router.shape[0], tm), const2)],
        out_specs=[pl.BlockSpec((1, tm, d), row),
                   pl.BlockSpec((tm, d // 2), flat),
                   pl.BlockSpec((w_router.shape[0], tm), lambda bi, i: (0, bi * nt + i))],
        out_shape=[jax.ShapeDtypeStruct((b, s, d), F32),
                   jax.ShapeDtypeStruct((b * s, d // 2), jnp.uint32),
                   jax.ShapeDtypeStruct((w_router.shape[0], b * s), F32)],
        compiler_params=_params(("arbitrary", "arbitrary")),
        name="out_projection",
    )(attn, ssm, x, w_out, g_post, gate, g_pre, sc1p, sh, w_router, b_router)


ROW_IDX, ROW_RANK, ROW_GATE, ROUTE_ROWS = 0, TOP_K, 2 * TOP_K, 4 * TOP_K


def _router_kernel(lp_ref, ls_ref, upper_ref, ones_ref, meta_ref, cnt_ref, *, p_tiles):
    i = pl.program_id(0)
    ne, tt = lp_ref.shape

    @pl.when(i == 0)
    def _():
        cnt_ref[...] = jnp.zeros_like(cnt_ref)

    work = jnp.where(i < p_tiles, lp_ref[...], ls_ref[...])
    row = lax.broadcasted_iota(jnp.int32, (ne, tt), 0).astype(F32)
    vals, hots, idxs = [], [], []
    for _ in range(TOP_K):
        m = jnp.max(work, axis=0, keepdims=True)
        idx = jnp.min(jnp.where(work == m, row, float(ne)), axis=0, keepdims=True)
        hot = row == idx
        vals.append(m)
        hots.append(hot)
        idxs.append(idx)
        work = jnp.where(hot, -jnp.inf, work)
    es = [jnp.exp(v - vals[0]) for v in vals]
    den = es[0] + es[1] + es[2] + es[3]
    onehot = jnp.zeros((ne, tt), F32)
    for hot in hots:
        onehot = jnp.where(hot, 1.0, onehot)
    ohb = onehot.astype(BF16)
    before = jnp.dot(ohb, upper_ref[...], preferred_element_type=F32) + cnt_ref[...]
    ranks = [jnp.sum(jnp.where(hot, before, 0.0), axis=0, keepdims=True) for hot in hots]
    gates = [e / den for e in es]
    pad = [jnp.zeros((ROUTE_ROWS - 3 * TOP_K, tt), F32)]
    meta_ref[...] = jnp.concatenate(idxs + ranks + gates + pad, axis=0)
    cnt_ref[...] = cnt_ref[...] + jnp.dot(ohb, ones_ref[...], preferred_element_type=F32)


def _router(logits_p, logits_s, tt):
    ne = logits_p.shape[0]
    n = logits_p.shape[1] + logits_s.shape[1]
    p_tiles = logits_p.shape[1] // tt
    assert tt == LANES, "running counts are kept lane-replicated at one vreg width"
    upper = jnp.asarray(np.triu(np.ones((tt, tt), np.float32), 1), BF16)
    ones = jnp.ones((tt, tt), BF16)
    return pl.pallas_call(
        functools.partial(_router_kernel, p_tiles=p_tiles),
        grid=(n // tt,),
        in_specs=[pl.BlockSpec((ne, tt), lambda i: (0, jnp.minimum(i, p_tiles - 1))),
                  pl.BlockSpec((ne, tt), lambda i: (0, jnp.maximum(i - p_tiles, 0))),
                  pl.BlockSpec((tt, tt), lambda i: (0, 0)),
                  pl.BlockSpec((tt, tt), lambda i: (0, 0))],
        out_specs=[pl.BlockSpec((ROUTE_ROWS, tt), lambda i: (0, i)),
                   pl.BlockSpec((ne, LANES), lambda i: (0, 0))],
        out_shape=[jax.ShapeDtypeStruct((ROUTE_ROWS, n), F32),
                   jax.ShapeDtypeStruct((ne, LANES), F32)],
        compiler_params=_params(("arbitrary",)),
        name="router",
    )(logits_p, logits_s, upper, ones)


def _dispatch_kernel(zstart_ref, nu_ref, pos_ref, pos_prev_ref, hp_ref, hs_ref, xs_ref, zeros, src, sem_z, sem_l, sem_r,
                     *, p_tiles):
    i = pl.program_id(0)
    n_tiles = pl.num_programs(0)
    tt = src.shape[1]
    tm = zeros.shape[0]
    n_blocks = xs_ref.shape[0] // tm

    def zero_copy(row):
        return pltpu.make_async_copy(zeros, xs_ref.at[pl.ds(pl.multiple_of(row, tm), tm)], sem_z)

    @pl.when(i == 0)
    def _():
        zeros[...] = jnp.zeros_like(zeros)

        def per_expert(act):
            def body(e, carry):
                @pl.when(zstart_ref[e] >= 0)
                def _():
                    act(zero_copy(zstart_ref[e]))
                return carry
            lax.fori_loop(0, N_EXPERTS, body, 0)

        def per_tail(act):
            def body(blk, carry):
                act(zero_copy(blk * tm))
                return carry
            lax.fori_loop(nu_ref[0], n_blocks, body, 0)

        per_expert(lambda cp: cp.start())
        per_tail(lambda cp: cp.start())
        per_expert(lambda cp: cp.wait())
        per_tail(lambda cp: cp.wait())

    def tile_load(j, act):
        b = j % 3

        @pl.when(j < p_tiles)
        def _():
            act(pltpu.make_async_copy(hp_ref.at[pl.ds(pl.multiple_of(j * tt, tt), tt)], src.at[b], sem_l.at[b]))

        @pl.when(j >= p_tiles)
        def _():
            act(pltpu.make_async_copy(hs_ref.at[pl.ds(pl.multiple_of((j - p_tiles) * tt, tt), tt)], src.at[b],
                                      sem_l.at[b]))

    def row_copies(j, p_ref, b):
        return [pltpu.make_async_copy(src.at[b, pl.ds(t, 1)],
                                      xs_ref.at[pl.ds(p_ref[0, 0, t * TOP_K + k], 1)], sem_r.at[j % 2])
                for t in range(tt) for k in range(TOP_K)]

    @pl.when(i == 0)
    def _():
        tile_load(i, lambda cp: cp.start())

    @pl.when(i + 1 < n_tiles)
    def _():
        tile_load(i + 1, lambda cp: cp.start())

    tile_load(i, lambda cp: cp.wait())
    for b in range(3):
        @pl.when(i % 3 == b)
        def _():
            for cp in row_copies(i, pos_ref, b):
                cp.start()

    for b in range(3):
        @pl.when(jnp.logical_and(i > 0, (i + 2) % 3 == b))
        def _():
            for cp in row_copies(i - 1, pos_prev_ref, b):
                cp.wait()

    for b in range(3):
        @pl.when(jnp.logical_and(i == n_tiles - 1, i % 3 == b))
        def _():
            for cp in row_copies(i, pos_ref, b):
                cp.wait()


def _dispatch(h_p, h_s, pos3, zstart, n_used, n_slots, tt):
    row_shape = h_p.shape[1:]
    n = h_p.shape[0] + h_s.shape[0]
    p_tiles = h_p.shape[0] // tt
    return pl.pallas_call(
        functools.partial(_dispatch_kernel, p_tiles=p_tiles),
        grid_spec=pltpu.PrefetchScalarGridSpec(
            num_scalar_prefetch=2,
            grid=(n // tt,),
            in_specs=[pl.BlockSpec((1, 1, tt * TOP_K), lambda i, *_: (i, 0, 0), memory_space=pltpu.SMEM),
                      pl.BlockSpec((1, 1, tt * TOP_K), lambda i, *_: (jnp.maximum(i - 1, 0), 0, 0),
                                   memory_space=pltpu.SMEM),
                      pl.BlockSpec(memory_space=pl.ANY),
                      pl.BlockSpec(memory_space=pl.ANY)],
            out_specs=pl.BlockSpec(memory_space=pl.ANY),
            scratch_shapes=[pltpu.VMEM((MOE_TM,) + row_shape, h_p.dtype),
                            pltpu.VMEM((3, tt) + row_shape, h_p.dtype),
                            pltpu.SemaphoreType.DMA(()),
                            pltpu.SemaphoreType.DMA((3,)),
                            pltpu.SemaphoreType.DMA((2,))]),
        out_shape=jax.ShapeDtypeStruct((n_slots,) + row_shape, h_p.dtype),
        compiler_params=_params(("arbitrary",)),
        name="dispatch",
    )(zstart, n_used, pos3, pos3, h_p, h_s)


W_CHUNK_ROWS = 256


def _stage_expert_weights(i, be_ref, nxt_ref, slot_ref, w_hbm, wbuf, stg, sem, done):
    rows = W_CHUNK_ROWS
    chunks = wbuf.shape[1] // rows
    e, e_next, slot = be_ref[i], nxt_ref[i], slot_ref[i]
    first = jnp.logical_or(i == 0, be_ref[jnp.maximum(i - 1, 0)] != e)

    def chunk_copy(expert, c, buf):
        return pltpu.make_async_copy(w_hbm.at[expert, pl.ds(pl.multiple_of(c * rows, rows), rows)],
                                     stg.at[buf], sem.at[buf])

    def convert_next_chunk(expert, into):
        c = done[0]
        buf = c % 2
        chunk_copy(expert, c, buf).wait()

        @pl.when(c + 1 < chunks)
        def _():
            chunk_copy(expert, c + 1, 1 - buf).start()

        wbuf[into, pl.ds(pl.multiple_of(c * rows, rows), rows), :] = stg[buf].astype(wbuf.dtype)
        done[0] = c + 1

    @pl.when(i == 0)
    def _():
        done[0] = 0
        chunk_copy(e, 0, 0).start()

    @pl.when(first)
    def _():
        def body(_, carry):
            convert_next_chunk(e, slot)
            return carry

        lax.fori_loop(done[0], chunks, body, 0)

        @pl.when(e_next >= 0)
        def _():
            done[0] = 0
            chunk_copy(e_next, 0, 0).start()

    @pl.when(jnp.logical_and(jnp.logical_not(first), jnp.logical_and(e_next >= 0, done[0] < chunks)))
    def _():
        convert_next_chunk(e_next, 1 - slot)

    return slot


def _up_kernel(be_ref, nxt_ref, slot_ref, nu_ref, x_ref, w_hbm, b_ref, act_ref, wbuf, stg, sem, done):
    i = pl.program_id(0)
    fc = 512

    @pl.when(i < nu_ref[0])
    def _():
        slot = _stage_expert_weights(i, be_ref, nxt_ref, slot_ref, w_hbm, wbuf, stg, sem, done)
        x = _unpack_bf16_pairs(x_ref[...])
        for c in range(D_FF // fc):
            glu = _dot(x, wbuf[slot, :, c * fc:(c + 1) * fc]) + b_ref[:, c * fc:(c + 1) * fc]
            lin = (_dot(x, wbuf[slot, :, D_FF + c * fc:D_FF + (c + 1) * fc])
                   + b_ref[:, D_FF + c * fc:D_FF + (c + 1) * fc])
            glu = jnp.minimum(glu, SWIGLU_LIMIT)
            lin = jnp.clip(lin, -SWIGLU_LIMIT, SWIGLU_LIMIT)
            act_ref[:, c * fc:(c + 1) * fc] = (glu * _sigmoid(SWIGLU_ALPHA * glu) * (lin + 1.0)).astype(act_ref.dtype)

    @pl.when(i >= nu_ref[0])
    def _():
        act_ref[...] = jnp.zeros_like(act_ref)


def _down_kernel(be_ref, nxt_ref, slot_ref, nu_ref, a_ref, w_hbm, b_ref, y_ref, wbuf, stg, sem, done):
    i = pl.program_id(0)

    @pl.when(i < nu_ref[0])
    def _():
        slot = _stage_expert_weights(i, be_ref, nxt_ref, slot_ref, w_hbm, wbuf, stg, sem, done)
        y_ref[...] = _dot(a_ref[...], wbuf[slot]) + b_ref[...]

    @pl.when(i >= nu_ref[0])
    def _():
        y_ref[...] = jnp.zeros_like(y_ref)


def _grouped(kernel, name, x, w, b, plan, out_tail, out_dtype, extra_scratch=()):
    block_expert, next_expert, slot, n_used = plan
    n_slots = x.shape[0]
    _, kdim, ndim = w.shape
    nb = n_slots // MOE_TM
    zeros = lambda t: (0,) * len(t)
    x_tail, o_tail = x.shape[1:], tuple(out_tail)
    return pl.pallas_call(
        kernel,
        grid_spec=pltpu.PrefetchScalarGridSpec(
            num_scalar_prefetch=4,
            grid=(nb,),
            in_specs=[pl.BlockSpec((MOE_TM,) + x_tail, lambda i, be, nx, sl, nu: (jnp.minimum(i, nu[0] - 1),) + zeros(x_tail)),
                      pl.BlockSpec(memory_space=pl.ANY),
                      pl.BlockSpec((None, 1, ndim), lambda i, be, nx, sl, nu: (be[i], 0, 0))],
            out_specs=pl.BlockSpec((MOE_TM,) + o_tail, lambda i, be, nx, sl, nu: (i,) + zeros(o_tail)),
            scratch_shapes=[pltpu.VMEM((2, kdim, ndim), BF16),
                            pltpu.VMEM((2, W_CHUNK_ROWS, ndim), F32),
                            pltpu.SemaphoreType.DMA((2,)),
                            pltpu.SMEM((1,), jnp.int32)] + list(extra_scratch)),
        out_shape=jax.ShapeDtypeStruct((n_slots,) + o_tail, out_dtype),
        compiler_params=_params(("arbitrary",), 60 * 1024 * 1024),
        name=name,
    )(block_expert, next_expert, slot, n_used, x, w, b)


def _combine_kernel(pos_ref, pos_next_ref, y_ref, gates_ref, x1_ref, gate_ref, g_ref, o_ref, rows, sem):
    i = pl.program_id(0)
    tt = x1_ref.shape[0]
    buf = i % 2

    def copies(p_ref, b):
        return [pltpu.make_async_copy(y_ref.at[pl.ds(p_ref[0, 0, t * TOP_K + k], 1)],
                                      rows.at[b, k, pl.ds(t, 1)], sem.at[b])
                for t in range(tt) for k in range(TOP_K)]

    @pl.when(i == 0)
    def _():
        for cp in copies(pos_ref, 0):
            cp.start()

    for b in range(2):
        @pl.when(jnp.logical_and(i + 1 < pl.num_programs(0), buf != b))
        def _():
            for cp in copies(pos_next_ref, b):
                cp.start()

    for b in range(2):
        @pl.when(buf == b)
        def _():
            for cp in copies(pos_ref, b):
                cp.wait()

    gates = gates_ref[...]
    f = rows[buf, 0] * gates[:, 0:1]
    for k in range(1, TOP_K):
        f = f + rows[buf, k] * gates[:, k:k + 1]
    o_ref[...] = x1_ref[...] + gate_ref[0] * _rms(f, g_ref[...])


def _combine(y, pos3, gates, x1, gate, g_post, tt):
    n, d = x1.shape
    nb = gate.shape[0]
    tiles_per_b = n // tt // nb
    tiles = n // tt
    return pl.pallas_call(
        _combine_kernel,
        grid=(tiles,),
        in_specs=[pl.BlockSpec((1, 1, tt * TOP_K), lambda i: (i, 0, 0), memory_space=pltpu.SMEM),
                  pl.BlockSpec((1, 1, tt * TOP_K), lambda i: (jnp.minimum(i + 1, tiles - 1), 0, 0),
                               memory_space=pltpu.SMEM),
                  pl.BlockSpec(memory_space=pl.ANY),
                  pl.BlockSpec((tt, TOP_K), lambda i: (i, 0)),
                  pl.BlockSpec((tt, d), lambda i: (i, 0)),
                  pl.BlockSpec((1, 1, d), lambda i: (i // tiles_per_b, 0, 0)),
                  pl.BlockSpec((1, d), lambda i: (0, 0))],
        out_specs=pl.BlockSpec((tt, d), lambda i: (i, 0)),
        out_shape=jax.ShapeDtypeStruct((n, d), F32),
        scratch_shapes=[pltpu.VMEM((2, TOP_K, tt) + y.shape[1:], F32), pltpu.SemaphoreType.DMA((2,))],
        compiler_params=_params(("arbitrary",)),
        name="combine",
    )(pos3, pos3, y, gates, x1, gate, g_post)


def _mixer(x, mod, pos, hist8, h0t, ln, tm, tq, wts, cache):
    b, s, d = x.shape
    mods = [m.reshape(b, 1, d) for m in jnp.split(mod, 6, axis=-1)]
    gt_f = mods[5]
    if cache is None:
        xf, fb, fs = x, b, s
    else:
        fb, fs = 1, b * s
        xf = x.reshape(fb, fs, d)
        pos = jnp.tile(pos, b)
        mods = [jnp.broadcast_to(m, (b, s, d)).reshape(fb, fs, d) for m in mods]
    sh_m, sc_m, gt_m, sh_f, sc_f, _ = mods
    proj = _in_projection(xf, 1.0 + sc_m, sh_m, wts["g_mix_pre"], wts["w_main"], wts["w_dt"], _rope_tables(pos), tm)
    q, k, v, z, xbc, dt_raw = [t.reshape(b, s, t.shape[-1]) for t in proj]
    if cache is None:
        attn = _attention_prompt(q, k, v, wts["sinks"], tq)
        k_win, v_win = k[:, s - WIN_CACHE:], v[:, s - WIN_CACHE:]
    else:
        attn, k_win, v_win = _attention_sample(q, k, v, wts["sinks"], *cache)
    ssm, conv_state, hfin = _ssd_mixer(xbc, z, dt_raw, hist8, h0t, wts["conv_w8"], wts["conv_b"], wts["dt_bias"],
                                       wts["a_log"], wts["dskip_x"], wts["g_ssm"], ln, q.dtype)
    x1, h2, logits = _out_projection(attn.reshape(fb, fs, -1), ssm.reshape(fb, fs, -1), xf, wts["w_out"],
                                     wts["g_mix_post"], gt_m, wts["g_ffn_pre"], 1.0 + sc_f, sh_f,
                                     wts["w_router"], wts["b_router"], tm)
    ssm_state = hfin.reshape(b, SSM_GROUPS, SSM_STATE, SSM_HPG, SSM_HEAD_DIM)
    ssm_state = ssm_state.transpose(0, 1, 3, 4, 2).reshape(b, SSM_HEADS, SSM_HEAD_DIM, SSM_STATE)
    states = (k_win.reshape(b, WIN_CACHE, N_KV, HEAD_DIM), v_win.reshape(b, WIN_CACHE, N_KV, HEAD_DIM),
              conv_state, ssm_state)
    return x1.reshape(b * s, d), h2, logits, gt_f, states


def _pad_lanes(a, width=LANES):
    return jnp.pad(a, [(0, 0)] * (a.ndim - 1) + [(0, width - a.shape[-1])])


def _largest_tile(n, cap):
    t = cap
    while n % t:
        t //= 2
    return t


def kernel(x_prompt, x_sample, c_prompt, c_sample, cache_k, cache_v, state_conv, state_ssm, w_mod, b_mod, g_mix_pre, g_mix_post, g_ffn_pre, g_ffn_post, w_in, conv_w, conv_b, dt_bias, a_log, d_skip, g_ssm, sinks, w_out, w_router, b_router, w_up, b_up, w_down, b_down):
    depth = w_mod.shape[0]
    assert depth == 1, "single-layer step"
    l = 0
    bp, sp, d = x_prompt.shape
    bs, ss, _ = x_sample.shape
    n_p, n_s = bp * sp, bs * ss
    n_tok = n_p + n_s

    c_all = jnp.concatenate([c_prompt, c_sample], axis=0)
    c_rows = -(-c_all.shape[0] // SUBLANES) * SUBLANES
    mod_all = _modulation(jnp.pad(c_all, ((0, c_rows - c_all.shape[0]), (0, 0))), w_mod[l], b_mod[l])

    row2 = lambda a: a.reshape(1, -1)
    paired = _paired_head_columns()
    wts = {
        "g_mix_pre": row2(g_mix_pre[l]), "g_mix_post": row2(g_mix_post[l]), "g_ffn_pre": row2(g_ffn_pre[l]),
        "w_main": jnp.concatenate([w_in[l][:, :ATTN_DIM][:, paired], w_in[l][:, ATTN_DIM:MAIN_DIM]],
                                  axis=1).astype(BF16),
        "w_dt": _pad_lanes(w_in[l][:, MAIN_DIM:]).astype(BF16),
        "sinks": sinks[l],
        "conv_w8": jnp.pad(conv_w[l], ((0, SUBLANES - CONV_W), (0, 0))), "conv_b": row2(conv_b[l]),
        "dt_bias": _pad_lanes(row2(dt_bias[l])), "a_log": _pad_lanes(row2(a_log[l])),
        "dskip_x": row2(jnp.repeat(d_skip[l], SSM_HEAD_DIM)), "g_ssm": row2(g_ssm[l]),
        "w_out": jnp.concatenate([w_out[l][:ATTN_DIM][paired], w_out[l][ATTN_DIM:]], axis=0).astype(BF16),
        "w_router": w_router[l].T.astype(BF16), "b_router": b_router[l].reshape(-1, 1),
    }
    gw = SSM_HPG * SSM_HEAD_DIM

    hist_p = jnp.zeros((bp, SUBLANES, CONV_DIM), F32)
    h0_p = jnp.zeros((bp, SSM_GROUPS, SSM_STATE, gw), F32)
    x1_p, h2_p, lg_p, gtf_p, st_p = _mixer(x_prompt, mod_all[:bp], jnp.arange(sp, dtype=jnp.int32), hist_p, h0_p,
                                           SSD_CHUNK, _largest_tile(sp, 256), _largest_tile(sp, 256), wts, None)

    hist_s = jnp.pad(state_conv[l], ((0, 0), (SUBLANES - (CONV_W - 1), 0), (0, 0)))
    h0_s = state_ssm[l].astype(F32).reshape(bs, SSM_GROUPS, SSM_HPG, SSM_HEAD_DIM, SSM_STATE)
    h0_s = h0_s.transpose(0, 1, 4, 2, 3).reshape(bs, SSM_GROUPS, SSM_STATE, gw)
    cache = (cache_k[l].reshape(bs, WIN_CACHE, KV_DIM), cache_v[l].reshape(bs, WIN_CACHE, KV_DIM))
    x1_s, h2_s, lg_s, gtf_s, st_s = _mixer(x_sample, mod_all[bp:bp + bs], PAST_LEN + jnp.arange(ss, dtype=jnp.int32),
                                           hist_s, h0_s, ss, n_s, ss, wts, cache)

    tt = _largest_tile(n_s, 128)
    assert n_p % tt == 0
    route, counts = _router(lg_p, lg_s, tt)
    top_idx = route[ROW_IDX:ROW_IDX + TOP_K].T.astype(jnp.int32)
    rank = route[ROW_RANK:ROW_RANK + TOP_K].T.astype(jnp.int32)
    gates = route[ROW_GATE:ROW_GATE + TOP_K].T
    cnt = counts[:, 0].astype(jnp.int32)
    padded = (cnt + MOE_TM - 1) // MOE_TM * MOE_TM
    pend = jnp.cumsum(padded)
    offs = pend - padded
    pos = offs[top_idx] + rank
    n_blocks = -(-n_tok * TOP_K // MOE_TM) + N_EXPERTS
    n_slots = n_blocks * MOE_TM
    n_used = (pend[-1] // MOE_TM).astype(jnp.int32)
    blk = jnp.arange(n_blocks, dtype=jnp.int32)
    blk_row = jnp.minimum(blk, n_used - 1) * MOE_TM
    block_expert = jnp.minimum(jnp.sum(pend[None, :] <= blk_row[:, None], axis=1), N_EXPERTS - 1).astype(jnp.int32)
    zstart = jnp.where(cnt > 0, pend - MOE_TM, -1).astype(jnp.int32)
    pos3 = pos.reshape(n_tok // tt, 1, tt * TOP_K)
    nu = n_used.reshape(1)
    xs = _dispatch(h2_p, h2_s, pos3, zstart, nu, n_slots, tt)
    e_ids = jnp.arange(N_EXPERTS, dtype=jnp.int32)
    live = jnp.where(cnt > 0, e_ids, N_EXPERTS)
    later = jnp.concatenate([lax.cummin(live, reverse=True)[1:], jnp.full((1,), N_EXPERTS, jnp.int32)])
    next_live = jnp.where(later < N_EXPERTS, later, -1).astype(jnp.int32)
    visit = (jnp.cumsum((cnt > 0).astype(jnp.int32)) - 1) % 2
    plan = (block_expert, next_live[block_expert], visit[block_expert].astype(jnp.int32), nu)
    act = _grouped(_up_kernel, "expert_up", xs, w_up[l], b_up[l].reshape(N_EXPERTS, 1, -1), plan, (D_FF,), BF16)
    y = _grouped(_down_kernel, "expert_down", act, w_down[l], b_down[l].reshape(N_EXPERTS, 1, -1), plan,
                 (d,), F32)

    g_post = row2(g_ffn_post[l])
    tt_p = _largest_tile(sp, 128)
    y_p = _combine(y, pos[:n_p].reshape(n_p // tt_p, 1, tt_p * TOP_K), gates[:n_p], x1_p,
                   gtf_p, g_post, tt_p).reshape(bp, sp, d)
    y_s = _combine(y, pos[n_p:].reshape(bs, 1, ss * TOP_K), gates[n_p:], x1_s,
                   gtf_s, g_post, ss).reshape(bs, ss, d)

    stack = lambda a: a[None]
    return (y_p, y_s, stack(st_p[0]), stack(st_p[1]), stack(st_p[2]), stack(st_p[3]),
            stack(st_s[0]), stack(st_s[1]), stack(st_s[2]), stack(st_s[3]))
```

```python
import functools

import numpy as np
import jax
import jax.numpy as jnp
from jax import lax
from jax.experimental import pallas as pl
from jax.experimental.pallas import tpu as pltpu

F32 = jnp.float32
BF16 = jnp.bfloat16

D_MODEL = 2048
CHUNK = 64
N_HEADS = 16
N_KV = 4
HEAD_DIM = 64
GQA = N_HEADS // N_KV
ATTN_DIM = N_HEADS * HEAD_DIM
KV_DIM = N_KV * HEAD_DIM
WINDOW = 128
WIN_CHUNKS = WINDOW // CHUNK
PAST_LEN = 4096
WIN_CACHE = min(WINDOW, PAST_LEN)
ROT_DIM = HEAD_DIM // 4
ROPE_THETA = 500000.0
SSM_HEADS = 16
SSM_HEAD_DIM = 64
SSM_DIM = SSM_HEADS * SSM_HEAD_DIM
SSM_GROUPS = 2
SSM_HPG = SSM_HEADS // SSM_GROUPS
SSM_STATE = 128
CONV_W = 4
CONV_DIM = SSM_DIM + 2 * SSM_GROUPS * SSM_STATE
SSD_CHUNK = 64
N_EXPERTS = 32
TOP_K = 4
D_FF = 2048
SWIGLU_ALPHA = 1.702
SWIGLU_LIMIT = 7.0
NORM_EPS = 1e-6

LANES = 128
SUBLANES = 8
MAIN_DIM = ATTN_DIM + 2 * KV_DIM + SSM_DIM + CONV_DIM
MOE_TM = 256
VMEM_LIMIT = 56 * 1024 * 1024


def _sigmoid(x):
    return 1.0 / (1.0 + jnp.exp(-x))


def _silu(x):
    return x * _sigmoid(x)


def _rms(x, g):
    return x * lax.rsqrt(jnp.mean(x * x, axis=-1, keepdims=True) + NORM_EPS) * g


def _pack_bf16_pairs(x):
    n = x.shape[1] // 2
    hi = pltpu.bitcast(x[:, :n].astype(BF16).astype(F32), jnp.uint32)
    lo = pltpu.bitcast(x[:, n:].astype(BF16).astype(F32), jnp.uint32)
    return hi | (lo >> 16)


def _unpack_bf16_pairs(u):
    hi = pltpu.bitcast(u & jnp.uint32(0xFFFF0000), F32).astype(BF16)
    lo = pltpu.bitcast(u << 16, F32).astype(BF16)
    return jnp.concatenate([hi, lo], axis=1)


def _split3(x):
    hi = x.astype(BF16)
    r1 = x - hi.astype(F32)
    mid = r1.astype(BF16)
    lo = (r1 - mid.astype(F32)).astype(BF16)
    return hi, mid, lo


def _dot_exact_rhs(x, m):
    hi, mid, lo = _split3(x)
    d = functools.partial(jnp.dot, preferred_element_type=F32)
    return d(hi, m) + d(mid, m) + d(lo, m)


_NN = (((1,), (0,)), ((), ()))
_NT = (((1,), (1,)), ((), ()))
_TN = (((0,), (0,)), ((), ()))


def _dot(a, b, dims=_NN):
    assert a.dtype == b.dtype, (a.dtype, b.dtype)
    prec = lax.Precision.HIGHEST if a.dtype == F32 else None
    return lax.dot_general(a, b, dims, preferred_element_type=F32, precision=prec)


def _params(sem, vmem=VMEM_LIMIT):
    return pltpu.CompilerParams(dimension_semantics=sem, vmem_limit_bytes=vmem)


def _mod_kernel(c_ref, w_ref, b_ref, o_ref):
    o_ref[...] = _dot(_silu(c_ref[...]).astype(BF16), w_ref[...].astype(BF16)) + b_ref[...]


def _modulation(c_all, w_mod, b_mod):
    rows, d = c_all.shape
    n = w_mod.shape[1]
    tn = 1536
    return pl.pallas_call(
        _mod_kernel,
        grid=(n // tn,),
        in_specs=[pl.BlockSpec((rows, d), lambda j: (0, 0)),
                  pl.BlockSpec((d, tn), lambda j: (0, j)),
                  pl.BlockSpec((1, tn), lambda j: (0, j))],
        out_specs=pl.BlockSpec((rows, tn), lambda j: (0, j)),
        out_shape=jax.ShapeDtypeStruct((rows, n), F32),
        compiler_params=_params(("arbitrary",)),
        name="modulation",
    )(c_all, w_mod, b_mod.reshape(1, n))


def _rope(t, cos, s1, s2):
    outs = []
    for j in range(t.shape[1] // LANES):
        tj = t[:, j * LANES:(j + 1) * LANES]
        up = pltpu.roll(tj, LANES - ROT_DIM // 2, 1)
        dn = pltpu.roll(tj, ROT_DIM // 2, 1)
        outs.append(tj * cos + up * s1 + dn * s2)
    return jnp.concatenate(outs, axis=1)


def _inproj_kernel(x_ref, sc_ref, sh_ref, g_ref, w_ref, wdt_ref, cos_ref, s1_ref, s2_ref,
                   q_ref, k_ref, v_ref, z_ref, xbc_ref, dt_ref):
    h = _rms(x_ref[0], g_ref[...]) * sc_ref[0] + sh_ref[0]
    hb = h.astype(w_ref.dtype)
    cos, s1, s2 = cos_ref[...], s1_ref[...], s2_ref[...]
    step = 512

    def mm(lo, hi):
        return jnp.concatenate([_dot(hb, w_ref[:, c:min(c + step, hi)]) for c in range(lo, hi, step)], axis=1)

    o = 0
    q_ref[0] = (_rope(mm(o, o + ATTN_DIM), cos, s1, s2) * (HEAD_DIM ** -0.5)).astype(q_ref.dtype)
    o += ATTN_DIM
    k_ref[0] = _rope(mm(o, o + KV_DIM), cos, s1, s2)
    o += KV_DIM
    v_ref[0] = mm(o, o + KV_DIM)
    o += KV_DIM
    z_ref[0] = mm(o, o + SSM_DIM)
    o += SSM_DIM
    xbc_ref[0] = mm(o, o + CONV_DIM)
    dt_ref[0] = _dot(hb, wdt_ref[...])


def _mod_spec(m, tm):
    if m.shape[1] == 1:
        return pl.BlockSpec((1, 1, m.shape[2]), lambda bi, i: (bi, 0, 0))
    return pl.BlockSpec((1, tm, m.shape[2]), lambda bi, i: (bi, i, 0))


def _in_projection(x, sc1p, sh, g, w_main, w_dt, rope_tabs, tm):
    b, s, d = x.shape
    cos, s1, s2 = rope_tabs
    row = lambda bi, i: (bi, i, 0)
    const2 = lambda bi, i: (0, 0)
    tab = lambda bi, i: (i, 0)
    widths = (ATTN_DIM, KV_DIM, KV_DIM, SSM_DIM, CONV_DIM, LANES)
    dtypes = (w_main.dtype, F32, F32, F32, F32, F32)
    return pl.pallas_call(
        _inproj_kernel,
        grid=(b, s // tm),
        in_specs=[pl.BlockSpec((1, tm, d), row),
                  _mod_spec(sc1p, tm),
                  _mod_spec(sh, tm),
                  pl.BlockSpec((1, d), const2),
                  pl.BlockSpec((d, MAIN_DIM), const2, pipeline_mode=pl.Buffered(1)),
                  pl.BlockSpec((d, LANES), const2, pipeline_mode=pl.Buffered(1)),
                  pl.BlockSpec((tm, LANES), tab),
                  pl.BlockSpec((tm, LANES), tab),
                  pl.BlockSpec((tm, LANES), tab)],
        out_specs=[pl.BlockSpec((1, tm, w), row) for w in widths],
        out_shape=[jax.ShapeDtypeStruct((b, s, w), dt) for w, dt in zip(widths, dtypes)],
        compiler_params=_params(("arbitrary", "arbitrary")),
        name="in_projection",
    )(x, sc1p, sh, g, w_main, w_dt, cos, s1, s2)


def _rope_tables(pos):
    inv_freq = ROPE_THETA ** (-jnp.arange(0, ROT_DIM, 2, dtype=F32) / ROT_DIM)
    ang = pos.astype(F32)[:, None] * inv_freq[None, :]
    cos, sin = jnp.cos(ang), jnp.sin(ang)
    half = ROT_DIM // 2
    n = pos.shape[0]
    ones = jnp.ones((n, HEAD_DIM - ROT_DIM), F32)
    zeros_h = jnp.zeros((n, half), F32)
    zeros_r = jnp.zeros((n, HEAD_DIM - ROT_DIM), F32)
    c = jnp.concatenate([cos, cos, ones], axis=1)
    a = jnp.concatenate([-sin, zeros_h, zeros_r], axis=1)
    b = jnp.concatenate([zeros_h, sin, zeros_r], axis=1)
    rep = LANES // HEAD_DIM
    return tuple(jnp.tile(t, (1, rep)) for t in (c, a, b))


KV_PAIRS = N_KV // 2


def _paired_head_columns():
    cols = []
    for j in range(KV_PAIRS):
        for a in range(GQA):
            for g in (2 * j, 2 * j + 1):
                h = g * GQA + a
                cols.extend(range(h * HEAD_DIM, (h + 1) * HEAD_DIM))
    return np.asarray(cols, np.int32)


def _attend_pair(qcols, k2, v2, sink_row, valid):
    rows = qcols[0].shape[0]
    lo = lax.broadcasted_iota(jnp.int32, (rows, LANES), 1) < HEAD_DIM
    zero = jnp.zeros_like(qcols[0])
    lhs = jnp.concatenate([jnp.where(lo, qc, zero) for qc in qcols] + [jnp.where(lo, zero, qc) for qc in qcols], axis=0)
    s = _dot(k2, lhs, _NT)
    if valid is not None:
        s = jnp.where(valid, s, -1e30)
    m = jnp.maximum(jnp.max(s, axis=0, keepdims=True), sink_row)
    p = jnp.exp(s - m)
    den = jnp.sum(p, axis=0, keepdims=True) + jnp.exp(sink_row - m)
    o = _dot((p * (1.0 / den)).astype(v2.dtype), v2, _TN)
    half = GQA * rows
    return [jnp.where(lo, o[a * rows:(a + 1) * rows], o[half + a * rows:half + (a + 1) * rows]) for a in range(GQA)]


def _sink_row(sinks_ref, j, rows):
    heads = [(2 * j) * GQA + a for a in range(GQA)] + [(2 * j + 1) * GQA + a for a in range(GQA)]
    col = lax.broadcasted_iota(jnp.int32, (1, len(heads) * rows), 1) // rows
    out = jnp.zeros((1, len(heads) * rows), F32)
    for n, h in enumerate(heads):
        out = jnp.where(col == n, sinks_ref[h], out)
    return out


def _attn_prompt_kernel(sinks_ref, q_ref, km_ref, kh_ref, vm_ref, vh_ref, o_ref):
    i = pl.program_id(1)
    tq = q_ref.shape[1]
    chunks = tq // CHUNK
    kfull = jnp.concatenate([kh_ref[0], km_ref[0]], axis=0).astype(q_ref.dtype)
    vfull = jnp.concatenate([vh_ref[0], vm_ref[0]], axis=0).astype(q_ref.dtype)
    span = (WIN_CHUNKS + 1) * CHUNK
    col_chunk = lax.broadcasted_iota(jnp.int32, (span, 2 * GQA * CHUNK), 0) // CHUNK
    for j in range(KV_PAIRS):
        k2 = kfull[:, j * LANES:(j + 1) * LANES]
        v2 = vfull[:, j * LANES:(j + 1) * LANES]
        sink = _sink_row(sinks_ref, j, CHUNK)
        for c in range(chunks):
            r0 = c * CHUNK
            slabs = [(j * GQA + a) * LANES for a in range(GQA)]
            qcols = [q_ref[0, r0:r0 + CHUNK, sl:sl + LANES] for sl in slabs]
            valid = None
            if c < WIN_CHUNKS:
                valid = (i * chunks + c - WIN_CHUNKS + col_chunk) >= 0
            outs = _attend_pair(qcols, k2[r0:r0 + span], v2[r0:r0 + span], sink, valid)
            for sl, o in zip(slabs, outs):
                o_ref[0, r0:r0 + CHUNK, sl:sl + LANES] = o.astype(o_ref.dtype)


def _attention_prompt(q, k, v, sinks, tq):
    b, s, _ = q.shape
    halo = WIN_CHUNKS * CHUNK
    ratio = tq // halo
    main = lambda bi, i: (bi, i, 0)
    prev = lambda bi, i: (bi, jnp.maximum(i * ratio - 1, 0), 0)
    return pl.pallas_call(
        _attn_prompt_kernel,
        grid=(b, s // tq),
        in_specs=[pl.BlockSpec(memory_space=pltpu.SMEM),
                  pl.BlockSpec((1, tq, ATTN_DIM), main),
                  pl.BlockSpec((1, tq, KV_DIM), main),
                  pl.BlockSpec((1, halo, KV_DIM), prev),
                  pl.BlockSpec((1, tq, KV_DIM), main),
                  pl.BlockSpec((1, halo, KV_DIM), prev)],
        out_specs=pl.BlockSpec((1, tq, ATTN_DIM), main),
        out_shape=jax.ShapeDtypeStruct((b, s, ATTN_DIM), q.dtype),
        compiler_params=_params(("arbitrary", "arbitrary")),
        name="attention_prompt",
    )(sinks, q, k, k, v, v)


def _attn_sample_kernel(sinks_ref, q_ref, k_ref, v_ref, ck_ref, cv_ref, o_ref, kw_ref, vw_ref):
    s = q_ref.shape[1]
    kf = jnp.concatenate([ck_ref[0], k_ref[0]], axis=0)
    vf = jnp.concatenate([cv_ref[0], v_ref[0]], axis=0)
    n = kf.shape[0]
    kw_ref[0] = kf[n - WIN_CACHE:]
    vw_ref[0] = vf[n - WIN_CACHE:]
    kb, vb = kf.astype(q_ref.dtype), vf.astype(q_ref.dtype)
    for j in range(KV_PAIRS):
        slabs = [(j * GQA + a) * LANES for a in range(GQA)]
        qcols = [q_ref[0, :, sl:sl + LANES] for sl in slabs]
        outs = _attend_pair(qcols, kb[:, j * LANES:(j + 1) * LANES], vb[:, j * LANES:(j + 1) * LANES],
                            _sink_row(sinks_ref, j, s), None)
        for sl, o in zip(slabs, outs):
            o_ref[0, :, sl:sl + LANES] = o.astype(o_ref.dtype)


def _attention_sample(q, k, v, sinks, cache_k, cache_v):
    b, s, _ = q.shape
    blk = lambda bi: (bi, 0, 0)
    return pl.pallas_call(
        _attn_sample_kernel,
        grid=(b,),
        in_specs=[pl.BlockSpec(memory_space=pltpu.SMEM),
                  pl.BlockSpec((1, s, ATTN_DIM), blk),
                  pl.BlockSpec((1, s, KV_DIM), blk),
                  pl.BlockSpec((1, s, KV_DIM), blk),
                  pl.BlockSpec((1, WIN_CACHE, KV_DIM), blk),
                  pl.BlockSpec((1, WIN_CACHE, KV_DIM), blk)],
        out_specs=[pl.BlockSpec((1, s, ATTN_DIM), blk),
                   pl.BlockSpec((1, WIN_CACHE, KV_DIM), blk),
                   pl.BlockSpec((1, WIN_CACHE, KV_DIM), blk)],
        out_shape=[jax.ShapeDtypeStruct((b, s, ATTN_DIM), q.dtype),
                   jax.ShapeDtypeStruct((b, WIN_CACHE, KV_DIM), F32),
                   jax.ShapeDtypeStruct((b, WIN_CACHE, KV_DIM), F32)],
        compiler_params=_params(("arbitrary",)),
        name="attention_sample",
    )(sinks, q, k, v, cache_k, cache_v)


def _ssd_kernel(xbc_ref, z_ref, dt_ref, hist_ref, h0_ref, cw_ref, cb_ref, dtb_ref, alog_ref, dsk_ref, gs_ref,
                ep_ref, el_ref, dmask_ref, causal_ref, tril_ref,
                y_ref, conv_ref, hfin_ref, prev, ht, *, ln):
    c = pl.program_id(1)
    last = pl.num_programs(1) - 1
    rows = xbc_ref.shape[1]

    @pl.when(c == 0)
    def _():
        prev[...] = hist_ref[...]
        ht[...] = h0_ref[...]

    for bi in range(xbc_ref.shape[0]):
        for r0 in range(0, rows, ln):
            _ssd_chunk(bi, r0, ln, xbc_ref, z_ref, dt_ref, cw_ref, cb_ref, dtb_ref, alog_ref, dsk_ref, gs_ref,
                       ep_ref, el_ref, dmask_ref, causal_ref, tril_ref, y_ref, prev, ht)
        prev[bi] = xbc_ref[bi, rows - SUBLANES:, :]

    @pl.when(c == last)
    def _():
        hfin_ref[...] = ht[...]
        conv_ref[...] = xbc_ref[:, rows - (CONV_W - 1):, :]


def _ssd_chunk(bi, r0, ln, xbc_ref, z_ref, dt_ref, cw_ref, cb_ref, dtb_ref, alog_ref, dsk_ref, gs_ref,
               ep_ref, el_ref, dmask_ref, causal_ref, tril_ref, y_ref, prev, ht):
    gw = SSM_HPG * SSM_HEAD_DIM
    xr = xbc_ref[bi, r0:r0 + ln, :]
    before = prev[bi] if r0 == 0 else xbc_ref[bi, r0 - SUBLANES:r0, :]
    ext = jnp.concatenate([before, xr], axis=0)
    conv = cb_ref[...]
    for i in range(CONV_W):
        sh = CONV_W - 1 - i
        tap = xr if sh == 0 else pltpu.roll(ext, sh, 0)[SUBLANES:]
        conv = conv + tap * cw_ref[i:i + 1, :]

    act = _silu(conv)
    xs = act[:, :SSM_DIM]
    bm = act[:, SSM_DIM:SSM_DIM + SSM_GROUPS * SSM_STATE]
    cm = act[:, SSM_DIM + SSM_GROUPS * SSM_STATE:]

    dtv = dt_ref[bi, r0:r0 + ln, :] + dtb_ref[...]
    dt = jnp.maximum(dtv, 0.0) + jnp.log1p(jnp.exp(-jnp.abs(dtv)))
    ad = dt * (-jnp.exp(alog_ref[...]))
    hi, mid, lo = _split3(ad)
    tril = tril_ref[...]
    d = functools.partial(jnp.dot, preferred_element_type=F32)
    a_cs = d(tril, hi) + d(tril, mid) + d(tril, lo)
    a_last = a_cs[ln - 1:ln, :]
    stacked = jnp.concatenate([dt, jnp.exp(a_cs), jnp.exp(a_last - a_cs)], axis=0)
    wide = _dot_exact_rhs(stacked, ep_ref[...])
    dt_x, ea_x, ds_x = wide[:ln], wide[ln:2 * ln], wide[2 * ln:]
    cd_x = ea_x[ln - 1:ln, :]

    a_l = _dot_exact_rhs(a_cs, el_ref[...])
    a_s = jnp.sum(a_l * dmask_ref[...], axis=0, keepdims=True)
    lmat = jnp.exp(jnp.where(causal_ref[...] > 0.0, a_l - a_s, -1e30))

    cdt = y_ref.dtype
    xd = xs * dt_x
    xds = (xd * ds_x).astype(cdt)
    xdb = xd.astype(cdt)
    ys = []
    for g in range(SSM_GROUPS):
        bg = bm[:, g * SSM_STATE:(g + 1) * SSM_STATE].astype(cdt)
        cg = cm[:, g * SSM_STATE:(g + 1) * SSM_STATE].astype(cdt)
        cbm = _dot(cg, bg, _NT)
        yd = []
        for r in range(SSM_HPG):
            hd = g * SSM_HPG + r
            w = (cbm * lmat[:, hd * ln:(hd + 1) * ln]).astype(cdt)
            yd.append(_dot(w, xdb[:, hd * SSM_HEAD_DIM:(hd + 1) * SSM_HEAD_DIM]))
        htg = ht[bi, g]
        y_off = _dot(cg, htg.astype(cdt)) * ea_x[:, g * gw:(g + 1) * gw]
        st = _dot(bg, xds[:, g * gw:(g + 1) * gw], _TN)
        ht[bi, g] = htg * cd_x[:, g * gw:(g + 1) * gw] + st
        ys.append(jnp.concatenate(yd, axis=1) + y_off)
    y = jnp.concatenate(ys, axis=1) + dsk_ref[...] * xs
    y = y * _silu(z_ref[bi, r0:r0 + ln, :])
    outs = [_rms(y[:, g * gw:(g + 1) * gw], gs_ref[:, g * gw:(g + 1) * gw]) for g in range(SSM_GROUPS)]
    y_ref[bi, r0:r0 + ln, :] = jnp.concatenate(outs, axis=1).astype(y_ref.dtype)


def _ssd_constants(ln):
    heads = SSM_HEADS
    ep = np.zeros((LANES, heads * SSM_HEAD_DIM), np.float32)
    el = np.zeros((LANES, heads * ln), np.float32)
    for r in range(heads):
        ep[r, r * SSM_HEAD_DIM:(r + 1) * SSM_HEAD_DIM] = 1.0
        el[r, r * ln:(r + 1) * ln] = 1.0
    eye = np.tile(np.eye(ln, dtype=np.float32), (1, heads))
    causal = np.tile(np.tril(np.ones((ln, ln), np.float32)), (1, heads))
    tril = np.tril(np.ones((ln, ln), np.float32))
    return (jnp.asarray(ep, BF16), jnp.asarray(el, BF16), jnp.asarray(eye), jnp.asarray(causal),
            jnp.asarray(tril, BF16))


def _ssd_mixer(xbc, z, dt_raw, hist8, h0t, conv_w8, conv_b, dt_bias, a_log, dskip_x, g_ssm, ln, out_dtype):
    b, s, _ = xbc.shape
    bb = 2 if b % 2 == 0 else 1
    rows = 2 * ln if (s // ln) % 2 == 0 else ln
    consts = _ssd_constants(ln)
    row = lambda bi, c: (bi, c, 0)
    per_b3 = lambda bi, c: (bi, 0, 0)
    per_b4 = lambda bi, c: (bi, 0, 0, 0)
    const2 = lambda bi, c: (0, 0)
    full = lambda a: pl.BlockSpec(a.shape, const2)
    gw = SSM_HPG * SSM_HEAD_DIM
    return pl.pallas_call(
        functools.partial(_ssd_kernel, ln=ln),
        grid=(b // bb, s // rows),
        in_specs=[pl.BlockSpec((bb, rows, CONV_DIM), row),
                  pl.BlockSpec((bb, rows, SSM_DIM), row),
                  pl.BlockSpec((bb, rows, LANES), row),
                  pl.BlockSpec((bb, SUBLANES, CONV_DIM), per_b3),
                  pl.BlockSpec((bb, SSM_GROUPS, SSM_STATE, gw), per_b4),
                  full(conv_w8), full(conv_b), full(dt_bias), full(a_log), full(dskip_x), full(g_ssm)]
                 + [full(a) for a in consts],
        out_specs=[pl.BlockSpec((bb, rows, SSM_DIM), row),
                   pl.BlockSpec((bb, CONV_W - 1, CONV_DIM), per_b3),
                   pl.BlockSpec((bb, SSM_GROUPS, SSM_STATE, gw), per_b4)],
        out_shape=[jax.ShapeDtypeStruct((b, s, SSM_DIM), out_dtype),
                   jax.ShapeDtypeStruct((b, CONV_W - 1, CONV_DIM), F32),
                   jax.ShapeDtypeStruct((b, SSM_GROUPS, SSM_STATE, gw), F32)],
        scratch_shapes=[pltpu.VMEM((bb, SUBLANES, CONV_DIM), F32),
                        pltpu.VMEM((bb, SSM_GROUPS, SSM_STATE, gw), F32)],
        compiler_params=_params(("arbitrary", "arbitrary")),
        name="ssd_mixer",
    )(xbc, z, dt_raw, hist8, h0t, conv_w8, conv_b, dt_bias, a_log, dskip_x, g_ssm, *consts)


def _outproj_kernel(a_ref, s_ref, x_ref, w_ref, gpost_ref, gate_ref, gpre_ref, sc_ref, sh_ref, wr_ref, br_ref,
                    x1_ref, h2_ref, lg_ref):
    mix = _dot(a_ref[0], w_ref[:ATTN_DIM, :]) + _dot(s_ref[0], w_ref[ATTN_DIM:, :])
    x1 = x_ref[0] + gate_ref[0] * _rms(mix, gpost_ref[...])
    x1_ref[0] = x1
    h2 = _rms(x1, gpre_ref[...]) * sc_ref[0] + sh_ref[0]
    h2_ref[...] = _pack_bf16_pairs(h2)
    lg_ref[...] = _dot(wr_ref[...], h2.astype(wr_ref.dtype), _NT) + br_ref[...]


def _out_projection(attn, ssm, x, w_out, g_post, gate, g_pre, sc1p, sh, w_router, b_router, tm):
    b, s, d = x.shape
    nt = s // tm
    row = lambda bi, i: (bi, i, 0)
    const2 = lambda bi, i: (0, 0)
    flat = lambda bi, i: (bi * nt + i, 0)
    b_router = jnp.broadcast_to(b_router, (w_router.shape[0], tm))
    return pl.pallas_call(
        _outproj_kernel,
        grid=(b, nt),
        in_specs=[pl.BlockSpec((1, tm, ATTN_DIM), row),
                  pl.BlockSpec((1, tm, SSM_DIM), row),
                  pl.BlockSpec((1, tm, d), row),
                  pl.BlockSpec((ATTN_DIM + SSM_DIM, d), const2, pipeline_mode=pl.Buffered(1)),
                  pl.BlockSpec((1, d), const2),
                  _mod_spec(gate, tm),
                  pl.BlockSpec((1, d), const2),
                  _mod_spec(sc1p, tm),
                  _mod_spec(sh, tm),
                  pl.BlockSpec(w_router.shape, const2),
                  pl.BlockSpec((w_router.shape[0], tm), const2)],
        out_specs=[pl.BlockSpec((1, tm, d), row),
                   pl.BlockSpec((tm, d // 2), flat),
                   pl.BlockSpec((w_router.shape[0], tm), lambda bi, i: (0, bi * nt + i))],
        out_shape=[jax.ShapeDtypeStruct((b, s, d), F32),
                   jax.ShapeDtypeStruct((b * s, d // 2), jnp.uint32),
                   jax.ShapeDtypeStruct((w_router.shape[0], b * s), F32)],
        compiler_params=_params(("arbitrary", "arbitrary")),
        name="out_projection",
    )(attn, ssm, x, w_out, g_post, gate, g_pre, sc1p, sh, w_router, b_router)


ROW_IDX, ROW_RANK, ROW_GATE, ROUTE_ROWS = 0, TOP_K, 2 * TOP_K, 4 * TOP_K


def _router_kernel(lp_ref, ls_ref, upper_ref, ones_ref, meta_ref, cnt_ref, *, p_tiles):
    i = pl.program_id(0)
    ne, tt = lp_ref.shape

    @pl.when(i == 0)
    def _():
        cnt_ref[...] = jnp.zeros_like(cnt_ref)

    work = jnp.where(i < p_tiles, lp_ref[...], ls_ref[...])
    row = lax.broadcasted_iota(jnp.int32, (ne, tt), 0).astype(F32)
    vals, hots, idxs = [], [], []
    for _ in range(TOP_K):
        m = jnp.max(work, axis=0, keepdims=True)
        idx = jnp.min(jnp.where(work == m, row, float(ne)), axis=0, keepdims=True)
        hot = row == idx
        vals.append(m)
        hots.append(hot)
        idxs.append(idx)
        work = jnp.where(hot, -jnp.inf, work)
    es = [jnp.exp(v - vals[0]) for v in vals]
    den = es[0] + es[1] + es[2] + es[3]
    onehot = jnp.zeros((ne, tt), F32)
    for hot in hots:
        onehot = jnp.where(hot, 1.0, onehot)
    ohb = onehot.astype(BF16)
    before = jnp.dot(ohb, upper_ref[...], preferred_element_type=F32) + cnt_ref[...]
    ranks = [jnp.sum(jnp.where(hot, before, 0.0), axis=0, keepdims=True) for hot in hots]
    gates = [e / den for e in es]
    pad = [jnp.zeros((ROUTE_ROWS - 3 * TOP_K, tt), F32)]
    meta_ref[...] = jnp.concatenate(idxs + ranks + gates + pad, axis=0)
    cnt_ref[...] = cnt_ref[...] + jnp.dot(ohb, ones_ref[...], preferred_element_type=F32)


def _router(logits_p, logits_s, tt):
    ne = logits_p.shape[0]
    n = logits_p.shape[1] + logits_s.shape[1]
    p_tiles = logits_p.shape[1] // tt
    assert tt == LANES, "running counts are kept lane-replicated at one vreg width"
    upper = jnp.asarray(np.triu(np.ones((tt, tt), np.float32), 1), BF16)
    ones = jnp.ones((tt, tt), BF16)
    return pl.pallas_call(
        functools.partial(_router_kernel, p_tiles=p_tiles),
        grid=(n // tt,),
        in_specs=[pl.BlockSpec((ne, tt), lambda i: (0, jnp.minimum(i, p_tiles - 1))),
                  pl.BlockSpec((ne, tt), lambda i: (0, jnp.maximum(i - p_tiles, 0))),
                  pl.BlockSpec((tt, tt), lambda i: (0, 0)),
                  pl.BlockSpec((tt, tt), lambda i: (0, 0))],
        out_specs=[pl.BlockSpec((ROUTE_ROWS, tt), lambda i: (0, i)),
                   pl.BlockSpec((ne, LANES), lambda i: (0, 0))],
        out_shape=[jax.ShapeDtypeStruct((ROUTE_ROWS, n), F32),
                   jax.ShapeDtypeStruct((ne, LANES), F32)],
        compiler_params=_params(("arbitrary",)),
        name="router",
    )(logits_p, logits_s, upper, ones)


def _dispatch_kernel(zstart_ref, nu_ref, pos_ref, pos_prev_ref, hp_ref, hs_ref, xs_ref, zeros, src, sem_z, sem_l, sem_r,
                     *, p_tiles):
    i = pl.program_id(0)
    n_tiles = pl.num_programs(0)
    tt = src.shape[1]
    tm = zeros.shape[0]
    n_blocks = xs_ref.shape[0] // tm

    def zero_copy(row):
        return pltpu.make_async_copy(zeros, xs_ref.at[pl.ds(pl.multiple_of(row, tm), tm)], sem_z)

    @pl.when(i == 0)
    def _():
        zeros[...] = jnp.zeros_like(zeros)

        def per_expert(act):
            def body(e, carry):
                @pl.when(zstart_ref[e] >= 0)
                def _():
                    act(zero_copy(zstart_ref[e]))
                return carry
            lax.fori_loop(0, N_EXPERTS, body, 0)

        def per_tail(act):
            def body(blk, carry):
                act(zero_copy(blk * tm))
                return carry
            lax.fori_loop(nu_ref[0], n_blocks, body, 0)

        per_expert(lambda cp: cp.start())
        per_tail(lambda cp: cp.start())
        per_expert(lambda cp: cp.wait())
        per_tail(lambda cp: cp.wait())

    def tile_load(j, act):
        b = j % 3

        @pl.when(j < p_tiles)
        def _():
            act(pltpu.make_async_copy(hp_ref.at[pl.ds(pl.multiple_of(j * tt, tt), tt)], src.at[b], sem_l.at[b]))

        @pl.when(j >= p_tiles)
        def _():
            act(pltpu.make_async_copy(hs_ref.at[pl.ds(pl.multiple_of((j - p_tiles) * tt, tt), tt)], src.at[b],
                                      sem_l.at[b]))

    def row_copies(j, p_ref, b):
        return [pltpu.make_async_copy(src.at[b, pl.ds(t, 1)],
                                      xs_ref.at[pl.ds(p_ref[0, 0, t * TOP_K + k], 1)], sem_r.at[j % 2])
                for t in range(tt) for k in range(TOP_K)]

    @pl.when(i == 0)
    def _():
        tile_load(i, lambda cp: cp.start())

    @pl.when(i + 1 < n_tiles)
    def _():
        tile_load(i + 1, lambda cp: cp.start())

    tile_load(i, lambda cp: cp.wait())
    for b in range(3):
        @pl.when(i % 3 == b)
        def _():
            for cp in row_copies(i, pos_ref, b):
                cp.start()

    for b in range(3):
        @pl.when(jnp.logical_and(i > 0, (i + 2) % 3 == b))
        def _():
            for cp in row_copies(i - 1, pos_prev_ref, b):
                cp.wait()

    for b in range(3):
        @pl.when(jnp.logical_and(i == n_tiles - 1, i % 3 == b))
        def _():
            for cp in row_copies(i, pos_ref, b):
                cp.wait()


def _dispatch(h_p, h_s, pos3, zstart, n_used, n_slots, tt):
    row_shape = h_p.shape[1:]
    n = h_p.shape[0] + h_s.shape[0]
    p_tiles = h_p.shape[0] // tt
    return pl.pallas_call(
        functools.partial(_dispatch_kernel, p_tiles=p_tiles),
        grid_spec=pltpu.PrefetchScalarGridSpec(
            num_scalar_prefetch=2,
            grid=(n // tt,),
            in_specs=[pl.BlockSpec((1, 1, tt * TOP_K), lambda i, *_: (i, 0, 0), memory_space=pltpu.SMEM),
                      pl.BlockSpec((1, 1, tt * TOP_K), lambda i, *_: (jnp.maximum(i - 1, 0), 0, 0),
                                   memory_space=pltpu.SMEM),
                      pl.BlockSpec(memory_space=pl.ANY),
                      pl.BlockSpec(memory_space=pl.ANY)],
            out_specs=pl.BlockSpec(memory_space=pl.ANY),
            scratch_shapes=[pltpu.VMEM((MOE_TM,) + row_shape, h_p.dtype),
                            pltpu.VMEM((3, tt) + row_shape, h_p.dtype),
                            pltpu.SemaphoreType.DMA(()),
                            pltpu.SemaphoreType.DMA((3,)),
                            pltpu.SemaphoreType.DMA((2,))]),
        out_shape=jax.ShapeDtypeStruct((n_slots,) + row_shape, h_p.dtype),
        compiler_params=_params(("arbitrary",)),
        name="dispatch",
    )(zstart, n_used, pos3, pos3, h_p, h_s)


W_CHUNK_ROWS = 256


def _stage_expert_weights(i, be_ref, nxt_ref, slot_ref, w_hbm, wbuf, stg, sem, done):
    rows = W_CHUNK_ROWS
    chunks = wbuf.shape[1] // rows
    e, e_next, slot = be_ref[i], nxt_ref[i], slot_ref[i]
    first = jnp.logical_or(i == 0, be_ref[jnp.maximum(i - 1, 0)] != e)

    def chunk_copy(expert, c, buf):
        return pltpu.make_async_copy(w_hbm.at[expert, pl.ds(pl.multiple_of(c * rows, rows), rows)],
                                     stg.at[buf], sem.at[buf])

    def convert_next_chunk(expert, into):
        c = done[0]
        buf = c % 2
        chunk_copy(expert, c, buf).wait()

        @pl.when(c + 1 < chunks)
        def _():
            chunk_copy(expert, c + 1, 1 - buf).start()

        wbuf[into, pl.ds(pl.multiple_of(c * rows, rows), rows), :] = stg[buf].astype(wbuf.dtype)
        done[0] = c + 1

    @pl.when(i == 0)
    def _():
        done[0] = 0
        chunk_copy(e, 0, 0).start()

    @pl.when(first)
    def _():
        def body(_, carry):
            convert_next_chunk(e, slot)
            return carry

        lax.fori_loop(done[0], chunks, body, 0)

        @pl.when(e_next >= 0)
        def _():
            done[0] = 0
            chunk_copy(e_next, 0, 0).start()

    @pl.when(jnp.logical_and(jnp.logical_not(first), jnp.logical_and(e_next >= 0, done[0] < chunks)))
    def _():
        convert_next_chunk(e_next, 1 - slot)

    return slot


def _up_kernel(be_ref, nxt_ref, slot_ref, nu_ref, x_ref, w_hbm, b_ref, act_ref, wbuf, stg, sem, done):
    i = pl.program_id(0)
    fc = 512

    @pl.when(i < nu_ref[0])
    def _():
        slot = _stage_expert_weights(i, be_ref, nxt_ref, slot_ref, w_hbm, wbuf, stg, sem, done)
        x = _unpack_bf16_pairs(x_ref[...])
        for c in range(D_FF // fc):
            glu = _dot(x, wbuf[slot, :, c * fc:(c + 1) * fc]) + b_ref[:, c * fc:(c + 1) * fc]
            lin = (_dot(x, wbuf[slot, :, D_FF + c * fc:D_FF + (c + 1) * fc])
                   + b_ref[:, D_FF + c * fc:D_FF + (c + 1) * fc])
            glu = jnp.minimum(glu, SWIGLU_LIMIT)
            lin = jnp.clip(lin, -SWIGLU_LIMIT, SWIGLU_LIMIT)
            act_ref[:, c * fc:(c + 1) * fc] = (glu * _sigmoid(SWIGLU_ALPHA * glu) * (lin + 1.0)).astype(act_ref.dtype)

    @pl.when(i >= nu_ref[0])
    def _():
        act_ref[...] = jnp.zeros_like(act_ref)


def _down_kernel(be_ref, nxt_ref, slot_ref, nu_ref, a_ref, w_hbm, b_ref, y_ref, wbuf, stg, sem, done):
    i = pl.program_id(0)

    @pl.when(i < nu_ref[0])
    def _():
        slot = _stage_expert_weights(i, be_ref, nxt_ref, slot_ref, w_hbm, wbuf, stg, sem, done)
        y_ref[...] = _dot(a_ref[...], wbuf[slot]) + b_ref[...]

    @pl.when(i >= nu_ref[0])
    def _():
        y_ref[...] = jnp.zeros_like(y_ref)


def _grouped(kernel, name, x, w, b, plan, out_tail, out_dtype, extra_scratch=()):
    block_expert, next_expert, slot, n_used = plan
    n_slots = x.shape[0]
    _, kdim, ndim = w.shape
    nb = n_slots // MOE_TM
    zeros = lambda t: (0,) * len(t)
    x_tail, o_tail = x.shape[1:], tuple(out_tail)
    return pl.pallas_call(
        kernel,
        grid_spec=pltpu.PrefetchScalarGridSpec(
            num_scalar_prefetch=4,
            grid=(nb,),
            in_specs=[pl.BlockSpec((MOE_TM,) + x_tail, lambda i, be, nx, sl, nu: (jnp.minimum(i, nu[0] - 1),) + zeros(x_tail)),
                      pl.BlockSpec(memory_space=pl.ANY),
                      pl.BlockSpec((None, 1, ndim), lambda i, be, nx, sl, nu: (be[i], 0, 0))],
            out_specs=pl.BlockSpec((MOE_TM,) + o_tail, lambda i, be, nx, sl, nu: (i,) + zeros(o_tail)),
            scratch_shapes=[pltpu.VMEM((2, kdim, ndim), BF16),
                            pltpu.VMEM((2, W_CHUNK_ROWS, ndim), F32),
                            pltpu.SemaphoreType.DMA((2,)),
                            pltpu.SMEM((1,), jnp.int32)] + list(extra_scratch)),
        out_shape=jax.ShapeDtypeStruct((n_slots,) + o_tail, out_dtype),
        compiler_params=_params(("arbitrary",), 60 * 1024 * 1024),
        name=name,
    )(block_expert, next_expert, slot, n_used, x, w, b)


def _combine_kernel(pos_ref, pos_next_ref, y_ref, gates_ref, x1_ref, gate_ref, g_ref, o_ref, rows, sem):
    i = pl.program_id(0)
    tt = x1_ref.shape[0]
    buf = i % 2

    def copies(p_ref, b):
        return [pltpu.make_async_copy(y_ref.at[pl.ds(p_ref[0, 0, t * TOP_K + k], 1)],
                                      rows.at[b, k, pl.ds(t, 1)], sem.at[b])
                for t in range(tt) for k in range(TOP_K)]

    @pl.when(i == 0)
    def _():
        for cp in copies(pos_ref, 0):
            cp.start()

    for b in range(2):
        @pl.when(jnp.logical_and(i + 1 < pl.num_programs(0), buf != b))
        def _():
            for cp in copies(pos_next_ref, b):
                cp.start()

    for b in range(2):
        @pl.when(buf == b)
        def _():
            for cp in copies(pos_ref, b):
                cp.wait()

    gates = gates_ref[...]
    f = rows[buf, 0] * gates[:, 0:1]
    for k in range(1, TOP_K):
        f = f + rows[buf, k] * gates[:, k:k + 1]
    o_ref[...] = x1_ref[...] + gate_ref[0] * _rms(f, g_ref[...])


def _combine(y, pos3, gates, x1, gate, g_post, tt):
    n, d = x1.shape
    nb = gate.shape[0]
    tiles_per_b = n // tt // nb
    tiles = n // tt
    return pl.pallas_call(
        _combine_kernel,
        grid=(tiles,),
        in_specs=[pl.BlockSpec((1, 1, tt * TOP_K), lambda i: (i, 0, 0), memory_space=pltpu.SMEM),
                  pl.BlockSpec((1, 1, tt * TOP_K), lambda i: (jnp.minimum(i + 1, tiles - 1), 0, 0),
                               memory_space=pltpu.SMEM),
                  pl.BlockSpec(memory_space=pl.ANY),
                  pl.BlockSpec((tt, TOP_K), lambda i: (i, 0)),
                  pl.BlockSpec((tt, d), lambda i: (i, 0)),
                  pl.BlockSpec((1, 1, d), lambda i: (i // tiles_per_b, 0, 0)),
                  pl.BlockSpec((1, d), lambda i: (0, 0))],
        out_specs=pl.BlockSpec((tt, d), lambda i: (i, 0)),
        out_shape=jax.ShapeDtypeStruct((n, d), F32),
        scratch_shapes=[pltpu.VMEM((2, TOP_K, tt) + y.shape[1:], F32), pltpu.SemaphoreType.DMA((2,))],
        compiler_params=_params(("arbitrary",)),
        name="combine",
    )(pos3, pos3, y, gates, x1, gate, g_post)


def _mixer(x, mod, pos, hist8, h0t, ln, tm, tq, wts, cache):
    b, s, d = x.shape
    mods = [m.reshape(b, 1, d) for m in jnp.split(mod, 6, axis=-1)]
    gt_f = mods[5]
    if cache is None:
        xf, fb, fs = x, b, s
    else:
        fb, fs = 1, b * s
        xf = x.reshape(fb, fs, d)
        pos = jnp.tile(pos, b)
        mods = [jnp.broadcast_to(m, (b, s, d)).reshape(fb, fs, d) for m in mods]
    sh_m, sc_m, gt_m, sh_f, sc_f, _ = mods
    proj = _in_projection(xf, 1.0 + sc_m, sh_m, wts["g_mix_pre"], wts["w_main"], wts["w_dt"], _rope_tables(pos), tm)
    q, k, v, z, xbc, dt_raw = [t.reshape(b, s, t.shape[-1]) for t in proj]
    if cache is None:
        attn = _attention_prompt(q, k, v, wts["sinks"], tq)
        k_win, v_win = k[:, s - WIN_CACHE:], v[:, s - WIN_CACHE:]
    else:
        attn, k_win, v_win = _attention_sample(q, k, v, wts["sinks"], *cache)
    ssm, conv_state, hfin = _ssd_mixer(xbc, z, dt_raw, hist8, h0t, wts["conv_w8"], wts["conv_b"], wts["dt_bias"],
                                       wts["a_log"], wts["dskip_x"], wts["g_ssm"], ln, q.dtype)
    x1, h2, logits = _out_projection(attn.reshape(fb, fs, -1), ssm.reshape(fb, fs, -1), xf, wts["w_out"],
                                     wts["g_mix_post"], gt_m, wts["g_ffn_pre"], 1.0 + sc_f, sh_f,
                                     wts["w_router"], wts["b_router"], tm)
    ssm_state = hfin.reshape(b, SSM_GROUPS, SSM_STATE, SSM_HPG, SSM_HEAD_DIM)
    ssm_state = ssm_state.transpose(0, 1, 3, 4, 2).reshape(b, SSM_HEADS, SSM_HEAD_DIM, SSM_STATE)
    states = (k_win.reshape(b, WIN_CACHE, N_KV, HEAD_DIM), v_win.reshape(b, WIN_CACHE, N_KV, HEAD_DIM),
              conv_state, ssm_state)
    return x1.reshape(b * s, d), h2, logits, gt_f, states


def _pad_lanes(a, width=LANES):
    return jnp.pad(a, [(0, 0)] * (a.ndim - 1) + [(0, width - a.shape[-1])])


def _largest_tile(n, cap):
    t = cap
    while n % t:
        t //= 2
    return t


def kernel(x_prompt, x_sample, c_prompt, c_sample, cache_k, cache_v, state_conv, state_ssm, w_mod, b_mod, g_mix_pre, g_mix_post, g_ffn_pre, g_ffn_post, w_in, conv_w, conv_b, dt_bias, a_log, d_skip, g_ssm, sinks, w_out, w_router, b_router, w_up, b_up, w_down, b_down):
    depth = w_mod.shape[0]
    assert depth == 1, "single-layer step"
    l = 0
    bp, sp, d = x_prompt.shape
    bs, ss, _ = x_sample.shape
    n_p, n_s = bp * sp, bs * ss
    n_tok = n_p + n_s

    c_all = jnp.concatenate([c_prompt, c_sample], axis=0)
    c_rows = -(-c_all.shape[0] // SUBLANES) * SUBLANES
    mod_all = _modulation(jnp.pad(c_all, ((0, c_rows - c_all.shape[0]), (0, 0))), w_mod[l], b_mod[l])

    row2 = lambda a: a.reshape(1, -1)
    paired = _paired_head_columns()
    wts = {
        "g_mix_pre": row2(g_mix_pre[l]), "g_mix_post": row2(g_mix_post[l]), "g_ffn_pre": row2(g_ffn_pre[l]),
        "w_main": jnp.concatenate([w_in[l][:, :ATTN_DIM][:, paired], w_in[l][:, ATTN_DIM:MAIN_DIM]],
                                  axis=1).astype(BF16),
        "w_dt": _pad_lanes(w_in[l][:, MAIN_DIM:]).astype(BF16),
        "sinks": sinks[l],
        "conv_w8": jnp.pad(conv_w[l], ((0, SUBLANES - CONV_W), (0, 0))), "conv_b": row2(conv_b[l]),
        "dt_bias": _pad_lanes(row2(dt_bias[l])), "a_log": _pad_lanes(row2(a_log[l])),
        "dskip_x": row2(jnp.repeat(d_skip[l], SSM_HEAD_DIM)), "g_ssm": row2(g_ssm[l]),
        "w_out": jnp.concatenate([w_out[l][:ATTN_DIM][paired], w_out[l][ATTN_DIM:]], axis=0).astype(BF16),
        "w_router": w_router[l].T.astype(BF16), "b_router": b_router[l].reshape(-1, 1),
    }
    gw = SSM_HPG * SSM_HEAD_DIM

    hist_p = jnp.zeros((bp, SUBLANES, CONV_DIM), F32)
    h0_p = jnp.zeros((bp, SSM_GROUPS, SSM_STATE, gw), F32)
    x1_p, h2_p, lg_p, gtf_p, st_p = _mixer(x_prompt, mod_all[:bp], jnp.arange(sp, dtype=jnp.int32), hist_p, h0_p,
                                           SSD_CHUNK, _largest_tile(sp, 256), _largest_tile(sp, 256), wts, None)

    hist_s = jnp.pad(state_conv[l], ((0, 0), (SUBLANES - (CONV_W - 1), 0), (0, 0)))
    h0_s = state_ssm[l].astype(F32).reshape(bs, SSM_GROUPS, SSM_HPG, SSM_HEAD_DIM, SSM_STATE)
    h0_s = h0_s.transpose(0, 1, 4, 2, 3).reshape(bs, SSM_GROUPS, SSM_STATE, gw)
    cache = (cache_k[l].reshape(bs, WIN_CACHE, KV_DIM), cache_v[l].reshape(bs, WIN_CACHE, KV_DIM))
    x1_s, h2_s, lg_s, gtf_s, st_s = _mixer(x_sample, mod_all[bp:bp + bs], PAST_LEN + jnp.arange(ss, dtype=jnp.int32),
                                           hist_s, h0_s, ss, n_s, ss, wts, cache)

    tt = _largest_tile(n_s, 128)
    assert n_p % tt == 0
    route, counts = _router(lg_p, lg_s, tt)
    top_idx = route[ROW_IDX:ROW_IDX + TOP_K].T.astype(jnp.int32)
    rank = route[ROW_RANK:ROW_RANK + TOP_K].T.astype(jnp.int32)
    gates = route[ROW_GATE:ROW_GATE + TOP_K].T
    cnt = counts[:, 0].astype(jnp.int32)
    padded = (cnt + MOE_TM - 1) // MOE_TM * MOE_TM
    pend = jnp.cumsum(padded)
    offs = pend - padded
    pos = offs[top_idx] + rank
    n_blocks = -(-n_tok * TOP_K // MOE_TM) + N_EXPERTS
    n_slots = n_blocks * MOE_TM
    n_used = (pend[-1] // MOE_TM).astype(jnp.int32)
    blk = jnp.arange(n_blocks, dtype=jnp.int32)
    blk_row = jnp.minimum(blk, n_used - 1) * MOE_TM
    block_expert = jnp.minimum(jnp.sum(pend[None, :] <= blk_row[:, None], axis=1), N_EXPERTS - 1).astype(jnp.int32)
    zstart = jnp.where(cnt > 0, pend - MOE_TM, -1).astype(jnp.int32)
    pos3 = pos.reshape(n_tok // tt, 1, tt * TOP_K)
    nu = n_used.reshape(1)
    xs = _dispatch(h2_p, h2_s, pos3, zstart, nu, n_slots, tt)
    e_ids = jnp.arange(N_EXPERTS, dtype=jnp.int32)
    live = jnp.where(cnt > 0, e_ids, N_EXPERTS)
    later = jnp.concatenate([lax.cummin(live, reverse=True)[1:], jnp.full((1,), N_EXPERTS, jnp.int32)])
    next_live = jnp.where(later < N_EXPERTS, later, -1).astype(jnp.int32)
    visit = (jnp.cumsum((cnt > 0).astype(jnp.int32)) - 1) % 2
    plan = (block_expert, next_live[block_expert], visit[block_expert].astype(jnp.int32), nu)
    act = _grouped(_up_kernel, "expert_up", xs, w_up[l], b_up[l].reshape(N_EXPERTS, 1, -1), plan, (D_FF,), BF16)
    y = _grouped(_down_kernel, "expert_down", act, w_down[l], b_down[l].reshape(N_EXPERTS, 1, -1), plan,
                 (d,), F32)

    g_post = row2(g_ffn_post[l])
    tt_p = _largest_tile(sp, 128)
    y_p = _combine(y, pos[:n_p].reshape(n_p // tt_p, 1, tt_p * TOP_K), gates[:n_p], x1_p,
                   gtf_p, g_post, tt_p).reshape(bp, sp, d)
    y_s = _combine(y, pos[n_p:].reshape(bs, 1, ss * TOP_K), gates[n_p:], x1_s,
                   gtf_s, g_post, ss).reshape(bs, ss, d)

    stack = lambda a: a[None]
    return (y_p, y_s, stack(st_p[0]), stack(st_p[1]), stack(st_p[2]), stack(st_p[3]),
            stack(st_s[0]), stack(st_s[1]), stack(st_s[2]), stack(st_s[3]))
```

```python
import functools

import numpy as np
import jax
import jax.numpy as jnp
from jax import lax
from jax.experimental import pallas as pl
from jax.experimental.pallas import tpu as pltpu

F32 = jnp.float32
BF16 = jnp.bfloat16

D_MODEL = 2048
CHUNK = 64
N_HEADS = 16
N_KV = 4
HEAD_DIM = 64
GQA = N_HEADS // N_KV
ATTN_DIM = N_HEADS * HEAD_DIM
KV_DIM = N_KV * HEAD_DIM
WINDOW = 128
WIN_CHUNKS = WINDOW // CHUNK
PAST_LEN = 4096
WIN_CACHE = min(WINDOW, PAST_LEN)
ROT_DIM = HEAD_DIM // 4
ROPE_THETA = 500000.0
SSM_HEADS = 16
SSM_HEAD_DIM = 64
SSM_DIM = SSM_HEADS * SSM_HEAD_DIM
SSM_GROUPS = 2
SSM_HPG = SSM_HEADS // SSM_GROUPS
SSM_STATE = 128
CONV_W = 4
CONV_DIM = SSM_DIM + 2 * SSM_GROUPS * SSM_STATE
SSD_CHUNK = 64
N_EXPERTS = 32
TOP_K = 4
D_FF = 2048
SWIGLU_ALPHA = 1.702
SWIGLU_LIMIT = 7.0
NORM_EPS = 1e-6

LANES = 128
SUBLANES = 8
MAIN_DIM = ATTN_DIM + 2 * KV_DIM + SSM_DIM + CONV_DIM
MOE_TM = 256
VMEM_LIMIT = 56 * 1024 * 1024


def _sigmoid(x):
    return 1.0 / (1.0 + jnp.exp(-x))


def _silu(x):
    return x * _sigmoid(x)


def _rms(x, g):
    return x * lax.rsqrt(jnp.mean(x * x, axis=-1, keepdims=True) + NORM_EPS) * g


def _pack_bf16_pairs(x):
    n = x.shape[1] // 2
    hi = pltpu.bitcast(x[:, :n].astype(BF16).astype(F32), jnp.uint32)
    lo = pltpu.bitcast(x[:, n:].astype(BF16).astype(F32), jnp.uint32)
    return hi | (lo >> 16)


def _unpack_bf16_pairs(u):
    hi = pltpu.bitcast(u & jnp.uint32(0xFFFF0000), F32).astype(BF16)
    lo = pltpu.bitcast(u << 16, F32).astype(BF16)
    return jnp.concatenate([hi, lo], axis=1)


def _split3(x):
    hi = x.astype(BF16)
    r1 = x - hi.astype(F32)
    mid = r1.astype(BF16)
    lo = (r1 - mid.astype(F32)).astype(BF16)
    return hi, mid, lo


def _dot_exact_rhs(x, m):
    hi, mid, lo = _split3(x)
    d = functools.partial(jnp.dot, preferred_element_type=F32)
    return d(hi, m) + d(mid, m) + d(lo, m)


_NN = (((1,), (0,)), ((), ()))
_NT = (((1,), (1,)), ((), ()))
_TN = (((0,), (0,)), ((), ()))


def _dot(a, b, dims=_NN):
    assert a.dtype == b.dtype, (a.dtype, b.dtype)
    prec = lax.Precision.HIGHEST if a.dtype == F32 else None
    return lax.dot_general(a, b, dims, preferred_element_type=F32, precision=prec)


def _params(sem, vmem=VMEM_LIMIT):
    return pltpu.CompilerParams(dimension_semantics=sem, vmem_limit_bytes=vmem)


def _mod_kernel(c_ref, w_ref, b_ref, o_ref):
    o_ref[...] = _dot(_silu(c_ref[...]).astype(BF16), w_ref[...].astype(BF16)) + b_ref[...]


def _modulation(c_all, w_mod, b_mod):
    rows, d = c_all.shape
    n = w_mod.shape[1]
    tn = 1536
    return pl.pallas_call(
        _mod_kernel,
        grid=(n // tn,),
        in_specs=[pl.BlockSpec((rows, d), lambda j: (0, 0)),
                  pl.BlockSpec((d, tn), lambda j: (0, j)),
                  pl.BlockSpec((1, tn), lambda j: (0, j))],
        out_specs=pl.BlockSpec((rows, tn), lambda j: (0, j)),
        out_shape=jax.ShapeDtypeStruct((rows, n), F32),
        compiler_params=_params(("arbitrary",)),
        name="modulation",
    )(c_all, w_mod, b_mod.reshape(1, n))


def _rope(t, cos, s1, s2):
    outs = []
    for j in range(t.shape[1] // LANES):
        tj = t[:, j * LANES:(j + 1) * LANES]
        up = pltpu.roll(tj, LANES - ROT_DIM // 2, 1)
        dn = pltpu.roll(tj, ROT_DIM // 2, 1)
        outs.append(tj * cos + up * s1 + dn * s2)
    return jnp.concatenate(outs, axis=1)


def _inproj_kernel(x_ref, sc_ref, sh_ref, g_ref, w_ref, wdt_ref, cos_ref, s1_ref, s2_ref,
                   q_ref, k_ref, v_ref, z_ref, xbc_ref, dt_ref):
    h = _rms(x_ref[0], g_ref[...]) * sc_ref[0] + sh_ref[0]
    hb = h.astype(w_ref.dtype)
    cos, s1, s2 = cos_ref[...], s1_ref[...], s2_ref[...]
    step = 512

    def mm(lo, hi):
        return jnp.concatenate([_dot(hb, w_ref[:, c:min(c + step, hi)]) for c in range(lo, hi, step)], axis=1)

    o = 0
    q_ref[0] = (_rope(mm(o, o + ATTN_DIM), cos, s1, s2) * (HEAD_DIM ** -0.5)).astype(q_ref.dtype)
    o += ATTN_DIM
    k_ref[0] = _rope(mm(o, o + KV_DIM), cos, s1, s2)
    o += KV_DIM
    v_ref[0] = mm(o, o + KV_DIM)
    o += KV_DIM
    z_ref[0] = mm(o, o + SSM_DIM)
    o += SSM_DIM
    xbc_ref[0] = mm(o, o + CONV_DIM)
    dt_ref[0] = _dot(hb, wdt_ref[...])


def _mod_spec(m, tm):
    if m.shape[1] == 1:
        return pl.BlockSpec((1, 1, m.shape[2]), lambda bi, i: (bi, 0, 0))
    return pl.BlockSpec((1, tm, m.shape[2]), lambda bi, i: (bi, i, 0))


def _in_projection(x, sc1p, sh, g, w_main, w_dt, rope_tabs, tm):
    b, s, d = x.shape
    cos, s1, s2 = rope_tabs
    row = lambda bi, i: (bi, i, 0)
    const2 = lambda bi, i: (0, 0)
    tab = lambda bi, i: (i, 0)
    widths = (ATTN_DIM, KV_DIM, KV_DIM, SSM_DIM, CONV_DIM, LANES)
    dtypes = (w_main.dtype, F32, F32, F32, F32, F32)
    return pl.pallas_call(
        _inproj_kernel,
        grid=(b, s // tm),
        in_specs=[pl.BlockSpec((1, tm, d), row),
                  _mod_spec(sc1p, tm),
                  _mod_spec(sh, tm),
                  pl.BlockSpec((1, d), const2),
                  pl.BlockSpec((d, MAIN_DIM), const2, pipeline_mode=pl.Buffered(1)),
                  pl.BlockSpec((d, LANES), const2, pipeline_mode=pl.Buffered(1)),
                  pl.BlockSpec((tm, LANES), tab),
                  pl.BlockSpec((tm, LANES), tab),
                  pl.BlockSpec((tm, LANES), tab)],
        out_specs=[pl.BlockSpec((1, tm, w), row) for w in widths],
        out_shape=[jax.ShapeDtypeStruct((b, s, w), dt) for w, dt in zip(widths, dtypes)],
        compiler_params=_params(("arbitrary", "arbitrary")),
        name="in_projection",
    )(x, sc1p, sh, g, w_main, w_dt, cos, s1, s2)


def _rope_tables(pos):
    inv_freq = ROPE_THETA ** (-jnp.arange(0, ROT_DIM, 2, dtype=F32) / ROT_DIM)
    ang = pos.astype(F32)[:, None] * inv_freq[None, :]
    cos, sin = jnp.cos(ang), jnp.sin(ang)
    half = ROT_DIM // 2
    n = pos.shape[0]
    ones = jnp.ones((n, HEAD_DIM - ROT_DIM), F32)
    zeros_h = jnp.zeros((n, half), F32)
    zeros_r = jnp.zeros((n, HEAD_DIM - ROT_DIM), F32)
    c = jnp.concatenate([cos, cos, ones], axis=1)
    a = jnp.concatenate([-sin, zeros_h, zeros_r], axis=1)
    b = jnp.concatenate([zeros_h, sin, zeros_r], axis=1)
    rep = LANES // HEAD_DIM
    return tuple(jnp.tile(t, (1, rep)) for t in (c, a, b))


KV_PAIRS = N_KV // 2


def _paired_head_columns():
    cols = []
    for j in range(KV_PAIRS):
        for a in range(GQA):
            for g in (2 * j, 2 * j + 1):
                h = g * GQA + a
                cols.extend(range(h * HEAD_DIM, (h + 1) * HEAD_DIM))
    return np.asarray(cols, np.int32)


def _attend_pair(qcols, k2, v2, sink_row, valid):
    rows = qcols[0].shape[0]
    lo = lax.broadcasted_iota(jnp.int32, (rows, LANES), 1) < HEAD_DIM
    zero = jnp.zeros_like(qcols[0])
    lhs = jnp.concatenate([jnp.where(lo, qc, zero) for qc in qcols] + [jnp.where(lo, zero, qc) for qc in qcols], axis=0)
    s = _dot(k2, lhs, _NT)
    if valid is not None:
        s = jnp.where(valid, s, -1e30)
    m = jnp.maximum(jnp.max(s, axis=0, keepdims=True), sink_row)
    p = jnp.exp(s - m)
    den = jnp.sum(p, axis=0, keepdims=True) + jnp.exp(sink_row - m)
    o = _dot((p * (1.0 / den)).astype(v2.dtype), v2, _TN)
    half = GQA * rows
    return [jnp.where(lo, o[a * rows:(a + 1) * rows], o[half + a * rows:half + (a + 1) * rows]) for a in range(GQA)]


def _sink_row(sinks_ref, j, rows):
    heads = [(2 * j) * GQA + a for a in range(GQA)] + [(2 * j + 1) * GQA + a for a in range(GQA)]
    col = lax.broadcasted_iota(jnp.int32, (1, len(heads) * rows), 1) // rows
    out = jnp.zeros((1, len(heads) * rows), F32)
    for n, h in enumerate(heads):
        out = jnp.where(col == n, sinks_ref[h], out)
    return out


def _attn_prompt_kernel(sinks_ref, q_ref, km_ref, kh_ref, vm_ref, vh_ref, o_ref):
    i = pl.program_id(1)
    tq = q_ref.shape[1]
    chunks = tq // CHUNK
    kfull = jnp.concatenate([kh_ref[0], km_ref[0]], axis=0).astype(q_ref.dtype)
    vfull = jnp.concatenate([vh_ref[0], vm_ref[0]], axis=0).astype(q_ref.dtype)
    span = (WIN_CHUNKS + 1) * CHUNK
    col_chunk = lax.broadcasted_iota(jnp.int32, (span, 2 * GQA * CHUNK), 0) // CHUNK
    for j in range(KV_PAIRS):
        k2 = kfull[:, j * LANES:(j + 1) * LANES]
        v2 = vfull[:, j * LANES:(j + 1) * LANES]
        sink = _sink_row(sinks_ref, j, CHUNK)
        for c in range(chunks):
            r0 = c * CHUNK
            slabs = [(j * GQA + a) * LANES for a in range(GQA)]
            qcols = [q_ref[0, r0:r0 + CHUNK, sl:sl + LANES] for sl in slabs]
            valid = None
            if c < WIN_CHUNKS:
                valid = (i * chunks + c - WIN_CHUNKS + col_chunk) >= 0
            outs = _attend_pair(qcols, k2[r0:r0 + span], v2[r0:r0 + span], sink, valid)
            for sl, o in zip(slabs, outs):
                o_ref[0, r0:r0 + CHUNK, sl:sl + LANES] = o.astype(o_ref.dtype)


def _attention_prompt(q, k, v, sinks, tq):
    b, s, _ = q.shape
    halo = WIN_CHUNKS * CHUNK
    ratio = tq // halo
    main = lambda bi, i: (bi, i, 0)
    prev = lambda bi, i: (bi, jnp.maximum(i * ratio - 1, 0), 0)
    return pl.pallas_call(
        _attn_prompt_kernel,
        grid=(b, s // tq),
        in_specs=[pl.BlockSpec(memory_space=pltpu.SMEM),
                  pl.BlockSpec((1, tq, ATTN_DIM), main),
                  pl.BlockSpec((1, tq, KV_DIM), main),
                  pl.BlockSpec((1, halo, KV_DIM), prev),
                  pl.BlockSpec((1, tq, KV_DIM), main),
                  pl.BlockSpec((1, halo, KV_DIM), prev)],
        out_specs=pl.BlockSpec((1, tq, ATTN_DIM), main),
        out_shape=jax.ShapeDtypeStruct((b, s, ATTN_DIM), q.dtype),
        compiler_params=_params(("arbitrary", "arbitrary")),
        name="attention_prompt",
    )(sinks, q, k, k, v, v)


def _attn_sample_kernel(sinks_ref, q_ref, k_ref, v_ref, ck_ref, cv_ref, o_ref, kw_ref, vw_ref):
    s = q_ref.shape[1]
    kf = jnp.concatenate([ck_ref[0], k_ref[0]], axis=0)
    vf = jnp.concatenate([cv_ref[0], v_ref[0]], axis=0)
    n = kf.shape[0]
    kw_ref[0] = kf[n - WIN_CACHE:]
    vw_ref[0] = vf[n - WIN_CACHE:]
    kb, vb = kf.astype(q_ref.dtype), vf.astype(q_ref.dtype)
    for j in range(KV_PAIRS):
        slabs = [(j * GQA + a) * LANES for a in range(GQA)]
        qcols = [q_ref[0, :, sl:sl + LANES] for sl in slabs]
        outs = _attend_pair(qcols, kb[:, j * LANES:(j + 1) * LANES], vb[:, j * LANES:(j + 1) * LANES],
                            _sink_row(sinks_ref, j, s), None)
        for sl, o in zip(slabs, outs):
            o_ref[0, :, sl:sl + LANES] = o.astype(o_ref.dtype)


def _attention_sample(q, k, v, sinks, cache_k, cache_v):
    b, s, _ = q.shape
    blk = lambda bi: (bi, 0, 0)
    return pl.pallas_call(
        _attn_sample_kernel,
        grid=(b,),
        in_specs=[pl.BlockSpec(memory_space=pltpu.SMEM),
                  pl.BlockSpec((1, s, ATTN_DIM), blk),
                  pl.BlockSpec((1, s, KV_DIM), blk),
                  pl.BlockSpec((1, s, KV_DIM), blk),
                  pl.BlockSpec((1, WIN_CACHE, KV_DIM), blk),
                  pl.BlockSpec((1, WIN_CACHE, KV_DIM), blk)],
        out_specs=[pl.BlockSpec((1, s, ATTN_DIM), blk),
                   pl.BlockSpec((1, WIN_CACHE, KV_DIM), blk),
                   pl.BlockSpec((1, WIN_CACHE, KV_DIM), blk)],
        out_shape=[jax.ShapeDtypeStruct((b, s, ATTN_DIM), q.dtype),
                   jax.ShapeDtypeStruct((b, WIN_CACHE, KV_DIM), F32),
                   jax.ShapeDtypeStruct((b, WIN_CACHE, KV_DIM), F32)],
        compiler_params=_params(("arbitrary",)),
        name="attention_sample",
    )(sinks, q, k, v, cache_k, cache_v)


def _ssd_kernel(xbc_ref, z_ref, dt_ref, hist_ref, h0_ref, cw_ref, cb_ref, dtb_ref, alog_ref, dsk_ref, gs_ref,
                ep_ref, el_ref, dmask_ref, causal_ref, tril_ref,
                y_ref, conv_ref, hfin_ref, prev, ht, *, ln):
    c = pl.program_id(1)
    last = pl.num_programs(1) - 1
    rows = xbc_ref.shape[1]

    @pl.when(c == 0)
    def _():
        prev[...] = hist_ref[...]
        ht[...] = h0_ref[...]

    for bi in range(xbc_ref.shape[0]):
        for r0 in range(0, rows, ln):
            _ssd_chunk(bi, r0, ln, xbc_ref, z_ref, dt_ref, cw_ref, cb_ref, dtb_ref, alog_ref, dsk_ref, gs_ref,
                       ep_ref, el_ref, dmask_ref, causal_ref, tril_ref, y_ref, prev, ht)
        prev[bi] = xbc_ref[bi, rows - SUBLANES:, :]

    @pl.when(c == last)
    def _():
        hfin_ref[...] = ht[...]
        conv_ref[...] = xbc_ref[:, rows - (CONV_W - 1):, :]


def _ssd_chunk(bi, r0, ln, xbc_ref, z_ref, dt_ref, cw_ref, cb_ref, dtb_ref, alog_ref, dsk_ref, gs_ref,
               ep_ref, el_ref, dmask_ref, causal_ref, tril_ref, y_ref, prev, ht):
    gw = SSM_HPG * SSM_HEAD_DIM
    xr = xbc_ref[bi, r0:r0 + ln, :]
    before = prev[bi] if r0 == 0 else xbc_ref[bi, r0 - SUBLANES:r0, :]
    ext = jnp.concatenate([before, xr], axis=0)
    conv = cb_ref[...]
    for i in range(CONV_W):
        sh = CONV_W - 1 - i
        tap = xr if sh == 0 else pltpu.roll(ext, sh, 0)[SUBLANES:]
        conv = conv + tap * cw_ref[i:i + 1, :]

    act = _silu(conv)
    xs = act[:, :SSM_DIM]
    bm = act[:, SSM_DIM:SSM_DIM + SSM_GROUPS * SSM_STATE]
    cm = act[:, SSM_DIM + SSM_GROUPS * SSM_STATE:]

    dtv = dt_ref[bi, r0:r0 + ln, :] + dtb_ref[...]
    dt = jnp.maximum(dtv, 0.0) + jnp.log1p(jnp.exp(-jnp.abs(dtv)))
    ad = dt * (-jnp.exp(alog_ref[...]))
    hi, mid, lo = _split3(ad)
    tril = tril_ref[...]
    d = functools.partial(jnp.dot, preferred_element_type=F32)
    a_cs = d(tril, hi) + d(tril, mid) + d(tril, lo)
    a_last = a_cs[ln - 1:ln, :]
    stacked = jnp.concatenate([dt, jnp.exp(a_cs), jnp.exp(a_last - a_cs)], axis=0)
    wide = _dot_exact_rhs(stacked, ep_ref[...])
    dt_x, ea_x, ds_x = wide[:ln], wide[ln:2 * ln], wide[2 * ln:]
    cd_x = ea_x[ln - 1:ln, :]

    a_l = _dot_exact_rhs(a_cs, el_ref[...])
    a_s = jnp.sum(a_l * dmask_ref[...], axis=0, keepdims=True)
    lmat = jnp.exp(jnp.where(causal_ref[...] > 0.0, a_l - a_s, -1e30))

    cdt = y_ref.dtype
    xd = xs * dt_x
    xds = (xd * ds_x).astype(cdt)
    xdb = xd.astype(cdt)
    ys = []
    for g in range(SSM_GROUPS):
        bg = bm[:, g * SSM_STATE:(g + 1) * SSM_STATE].astype(cdt)
        cg = cm[:, g * SSM_STATE:(g + 1) * SSM_STATE].astype(cdt)
        cbm = _dot(cg, bg, _NT)
        yd = []
        for r in range(SSM_HPG):
            hd = g * SSM_HPG + r
            w = (cbm * lmat[:, hd * ln:(hd + 1) * ln]).astype(cdt)
            yd.append(_dot(w, xdb[:, hd * SSM_HEAD_DIM:(hd + 1) * SSM_HEAD_DIM]))
        htg = ht[bi, g]
        y_off = _dot(cg, htg.astype(cdt)) * ea_x[:, g * gw:(g + 1) * gw]
        st = _dot(bg, xds[:, g * gw:(g + 1) * gw], _TN)
        ht[bi, g] = htg * cd_x[:, g * gw:(g + 1) * gw] + st
        ys.append(jnp.concatenate(yd, axis=1) + y_off)
    y = jnp.concatenate(ys, axis=1) + dsk_ref[...] * xs
    y = y * _silu(z_ref[bi, r0:r0 + ln, :])
    outs = [_rms(y[:, g * gw:(g + 1) * gw], gs_ref[:, g * gw:(g + 1) * gw]) for g in range(SSM_GROUPS)]
    y_ref[bi, r0:r0 + ln, :] = jnp.concatenate(outs, axis=1).astype(y_ref.dtype)


def _ssd_constants(ln):
    heads = SSM_HEADS
    ep = np.zeros((LANES, heads * SSM_HEAD_DIM), np.float32)
    el = np.zeros((LANES, heads * ln), np.float32)
    for r in range(heads):
        ep[r, r * SSM_HEAD_DIM:(r + 1) * SSM_HEAD_DIM] = 1.0
        el[r, r * ln:(r + 1) * ln] = 1.0
    eye = np.tile(np.eye(ln, dtype=np.float32), (1, heads))
    causal = np.tile(np.tril(np.ones((ln, ln), np.float32)), (1, heads))
    tril = np.tril(np.ones((ln, ln), np.float32))
    return (jnp.asarray(ep, BF16), jnp.asarray(el, BF16), jnp.asarray(eye), jnp.asarray(causal),
            jnp.asarray(tril, BF16))


def _ssd_mixer(xbc, z, dt_raw, hist8, h0t, conv_w8, conv_b, dt_bias, a_log, dskip_x, g_ssm, ln, out_dtype):
    b, s, _ = xbc.shape
    bb = 2 if b % 2 == 0 else 1
    rows = 2 * ln if (s // ln) % 2 == 0 else ln
    consts = _ssd_constants(ln)
    row = lambda bi, c: (bi, c, 0)
    per_b3 = lambda bi, c: (bi, 0, 0)
    per_b4 = lambda bi, c: (bi, 0, 0, 0)
    const2 = lambda bi, c: (0, 0)
    full = lambda a: pl.BlockSpec(a.shape, const2)
    gw = SSM_HPG * SSM_HEAD_DIM
    return pl.pallas_call(
        functools.partial(_ssd_kernel, ln=ln),
        grid=(b // bb, s // rows),
        in_specs=[pl.BlockSpec((bb, rows, CONV_DIM), row),
                  pl.BlockSpec((bb, rows, SSM_DIM), row),
                  pl.BlockSpec((bb, rows, LANES), row),
                  pl.BlockSpec((bb, SUBLANES, CONV_DIM), per_b3),
                  pl.BlockSpec((bb, SSM_GROUPS, SSM_STATE, gw), per_b4),
                  full(conv_w8), full(conv_b), full(dt_bias), full(a_log), full(dskip_x), full(g_ssm)]
                 + [full(a) for a in consts],
        out_specs=[pl.BlockSpec((bb, rows, SSM_DIM), row),
                   pl.BlockSpec((bb, CONV_W - 1, CONV_DIM), per_b3),
                   pl.BlockSpec((bb, SSM_GROUPS, SSM_STATE, gw), per_b4)],
        out_shape=[jax.ShapeDtypeStruct((b, s, SSM_DIM), out_dtype),
                   jax.ShapeDtypeStruct((b, CONV_W - 1, CONV_DIM), F32),
                   jax.ShapeDtypeStruct((b, SSM_GROUPS, SSM_STATE, gw), F32)],
        scratch_shapes=[pltpu.VMEM((bb, SUBLANES, CONV_DIM), F32),
                        pltpu.VMEM((bb, SSM_GROUPS, SSM_STATE, gw), F32)],
        compiler_params=_params(("arbitrary", "arbitrary")),
        name="ssd_mixer",
    )(xbc, z, dt_raw, hist8, h0t, conv_w8, conv_b, dt_bias, a_log, dskip_x, g_ssm, *consts)


def _outproj_kernel(a_ref, s_ref, x_ref, w_ref, gpost_ref, gate_ref, gpre_ref, sc_ref, sh_ref, wr_ref, br_ref,
                    x1_ref, h2_ref, lg_ref):
    mix = _dot(a_ref[0], w_ref[:ATTN_DIM, :]) + _dot(s_ref[0], w_ref[ATTN_DIM:, :])
    x1 = x_ref[0] + gate_ref[0] * _rms(mix, gpost_ref[...])
    x1_ref[0] = x1
    h2 = _rms(x1, gpre_ref[...]) * sc_ref[0] + sh_ref[0]
    h2_ref[...] = _pack_bf16_pairs(h2)
    lg_ref[...] = _dot(wr_ref[...], h2.astype(wr_ref.dtype), _NT) + br_ref[...]


def _out_projection(attn, ssm, x, w_out, g_post, gate, g_pre, sc1p, sh, w_router, b_router, tm):
    b, s, d = x.shape
    nt = s // tm
    row = lambda bi, i: (bi, i, 0)
    const2 = lambda bi, i: (0, 0)
    flat = lambda bi, i: (bi * nt + i, 0)
    b_router = jnp.broadcast_to(b_router, (w_router.shape[0], tm))
    return pl.pallas_call(
        _outproj_kernel,
        grid=(b, nt),
        in_specs=[pl.BlockSpec((1, tm, ATTN_DIM), row),
                  pl.BlockSpec((1, tm, SSM_DIM), row),
                  pl.BlockSpec((1, tm, d), row),
                  pl.BlockSpec((ATTN_DIM + SSM_DIM, d), const2, pipeline_mode=pl.Buffered(1)),
                  pl.BlockSpec((1, d), const2),
                  _mod_spec(gate, tm),
                  pl.BlockSpec((1, d), const2),
                  _mod_spec(sc1p, tm),
                  _mod_spec(sh, tm),
                  pl.BlockSpec(w_router.shape, const2),
                  pl.BlockSpec((w_router.shape[0], tm), const2)],
        out_specs=[pl.BlockSpec((1, tm, d), row),
                   pl.BlockSpec((tm, d // 2), flat),
                   pl.BlockSpec((w_router.shape[0], tm), lambda bi, i: (0, bi * nt + i))],
        out_shape=[jax.ShapeDtypeStruct((b, s, d), F32),
                   jax.ShapeDtypeStruct((b * s, d // 2), jnp.uint32),
                   jax.ShapeDtypeStruct((w_router.shape[0], b * s), F32)],
        compiler_params=_params(("arbitrary", "arbitrary")),
        name="out_projection",
    )(attn, ssm, x, w_out, g_post, gate, g_pre, sc1p, sh, w_router, b_router)


ROW_IDX, ROW_RANK, ROW_GATE, ROUTE_ROWS = 0, TOP_K, 2 * TOP_K, 4 * TOP_K


def _router_kernel(lp_ref, ls_ref, upper_ref, ones_ref, meta_ref, cnt_ref, *, p_tiles):
    i = pl.program_id(0)
    ne, tt = lp_ref.shape

    @pl.when(i == 0)
    def _():
        cnt_ref[...] = jnp.zeros_like(cnt_ref)

    work = jnp.where(i < p_tiles, lp_ref[...], ls_ref[...])
    row = lax.broadcasted_iota(jnp.int32, (ne, tt), 0).astype(F32)
    vals, hots, idxs = [], [], []
    for _ in range(TOP_K):
        m = jnp.max(work, axis=0, keepdims=True)
        idx = jnp.min(jnp.where(work == m, row, float(ne)), axis=0, keepdims=True)
        hot = row == idx
        vals.append(m)
        hots.append(hot)
        idxs.append(idx)
        work = jnp.where(hot, -jnp.inf, work)
    es = [jnp.exp(v - vals[0]) for v in vals]
    den = es[0] + es[1] + es[2] + es[3]
    onehot = jnp.zeros((ne, tt), F32)
    for hot in hots:
        onehot = jnp.where(hot, 1.0, onehot)
    ohb = onehot.astype(BF16)
    before = jnp.dot(ohb, upper_ref[...], preferred_element_type=F32) + cnt_ref[...]
    ranks = [jnp.sum(jnp.where(hot, before, 0.0), axis=0, keepdims=True) for hot in hots]
    gates = [e / den for e in es]
    pad = [jnp.zeros((ROUTE_ROWS - 3 * TOP_K, tt), F32)]
    meta_ref[...] = jnp.concatenate(idxs + ranks + gates + pad, axis=0)
    cnt_ref[...] = cnt_ref[...] + jnp.dot(ohb, ones_ref[...], preferred_element_type=F32)


def _router(logits_p, logits_s, tt):
    ne = logits_p.shape[0]
    n = logits_p.shape[1] + logits_s.shape[1]
    p_tiles = logits_p.shape[1] // tt
    assert tt == LANES, "running counts are kept lane-replicated at one vreg width"
    upper = jnp.asarray(np.triu(np.ones((tt, tt), np.float32), 1), BF16)
    ones = jnp.ones((tt, tt), BF16)
    return pl.pallas_call(
        functools.partial(_router_kernel, p_tiles=p_tiles),
        grid=(n // tt,),
        in_specs=[pl.BlockSpec((ne, tt), lambda i: (0, jnp.minimum(i, p_tiles - 1))),
                  pl.BlockSpec((ne, tt), lambda i: (0, jnp.maximum(i - p_tiles, 0))),
                  pl.BlockSpec((tt, tt), lambda i: (0, 0)),
                  pl.BlockSpec((tt, tt), lambda i: (0, 0))],
        out_specs=[pl.BlockSpec((ROUTE_ROWS, tt), lambda i: (0, i)),
                   pl.BlockSpec((ne, LANES), lambda i: (0, 0))],
        out_shape=[jax.ShapeDtypeStruct((ROUTE_ROWS, n), F32),
                   jax.ShapeDtypeStruct((ne, LANES), F32)],
        compiler_params=_params(("arbitrary",)),
        name="router",
    )(logits_p, logits_s, upper, ones)


def _dispatch_kernel(zstart_ref, nu_ref, pos_ref, pos_prev_ref, hp_ref, hs_ref, xs_ref, zeros, src, sem_z, sem_l, sem_r,
                     *, p_tiles):
    i = pl.program_id(0)
    n_tiles = pl.num_programs(0)
    tt = src.shape[1]
    tm = zeros.shape[0]
    n_blocks = xs_ref.shape[0] // tm

    def zero_copy(row):
        return pltpu.make_async_copy(zeros, xs_ref.at[pl.ds(pl.multiple_of(row, tm), tm)], sem_z)

    @pl.when(i == 0)
    def _():
        zeros[...] = jnp.zeros_like(zeros)

        def per_expert(act):
            def body(e, carry):
                @pl.when(zstart_ref[e] >= 0)
                def _():
                    act(zero_copy(zstart_ref[e]))
                return carry
            lax.fori_loop(0, N_EXPERTS, body, 0)

        def per_tail(act):
            def body(blk, carry):
                act(zero_copy(blk * tm))
                return carry
            lax.fori_loop(nu_ref[0], n_blocks, body, 0)

        per_expert(lambda cp: cp.start())
        per_tail(lambda cp: cp.start())
        per_expert(lambda cp: cp.wait())
        per_tail(lambda cp: cp.wait())

    def tile_load(j, act):
        b = j % 3

        @pl.when(j < p_tiles)
        def _():
            act(pltpu.make_async_copy(hp_ref.at[pl.ds(pl.multiple_of(j * tt, tt), tt)], src.at[b], sem_l.at[b]))

        @pl.when(j >= p_tiles)
        def _():
            act(pltpu.make_async_copy(hs_ref.at[pl.ds(pl.multiple_of((j - p_tiles) * tt, tt), tt)], src.at[b],
                                      sem_l.at[b]))

    def row_copies(j, p_ref, b):
        return [pltpu.make_async_copy(src.at[b, pl.ds(t, 1)],
                                      xs_ref.at[pl.ds(p_ref[0, 0, t * TOP_K + k], 1)], sem_r.at[j % 2])
                for t in range(tt) for k in range(TOP_K)]

    @pl.when(i == 0)
    def _():
        tile_load(i, lambda cp: cp.start())

    @pl.when(i + 1 < n_tiles)
    def _():
        tile_load(i + 1, lambda cp: cp.start())

    tile_load(i, lambda cp: cp.wait())
    for b in range(3):
        @pl.when(i % 3 == b)
        def _():
            for n, cp in enumerate(row_copies(i, pos_ref, b)):
                cp.start(priority=n % 2)

    for b in range(3):
        @pl.when(jnp.logical_and(i > 0, (i + 2) % 3 == b))
        def _():
            for cp in row_copies(i - 1, pos_prev_ref, b):
                cp.wait()

    for b in range(3):
        @pl.when(jnp.logical_and(i == n_tiles - 1, i % 3 == b))
        def _():
            for cp in row_copies(i, pos_ref, b):
                cp.wait()


def _dispatch(h_p, h_s, pos3, zstart, n_used, n_slots, tt):
    row_shape = h_p.shape[1:]
    n = h_p.shape[0] + h_s.shape[0]
    p_tiles = h_p.shape[0] // tt
    return pl.pallas_call(
        functools.partial(_dispatch_kernel, p_tiles=p_tiles),
        grid_spec=pltpu.PrefetchScalarGridSpec(
            num_scalar_prefetch=2,
            grid=(n // tt,),
            in_specs=[pl.BlockSpec((1, 1, tt * TOP_K), lambda i, *_: (i, 0, 0), memory_space=pltpu.SMEM),
                      pl.BlockSpec((1, 1, tt * TOP_K), lambda i, *_: (jnp.maximum(i - 1, 0), 0, 0),
                                   memory_space=pltpu.SMEM),
                      pl.BlockSpec(memory_space=pl.ANY),
                      pl.BlockSpec(memory_space=pl.ANY)],
            out_specs=pl.BlockSpec(memory_space=pl.ANY),
            scratch_shapes=[pltpu.VMEM((MOE_TM,) + row_shape, h_p.dtype),
                            pltpu.VMEM((3, tt) + row_shape, h_p.dtype),
                            pltpu.SemaphoreType.DMA(()),
                            pltpu.SemaphoreType.DMA((3,)),
                            pltpu.SemaphoreType.DMA((2,))]),
        out_shape=jax.ShapeDtypeStruct((n_slots,) + row_shape, h_p.dtype),
        compiler_params=_params(("arbitrary",)),
        name="dispatch",
    )(zstart, n_used, pos3, pos3, h_p, h_s)


W_CHUNK_ROWS = 256


def _stage_expert_weights(i, be_ref, nxt_ref, slot_ref, w_hbm, wbuf, stg, sem, done):
    rows = W_CHUNK_ROWS
    chunks = wbuf.shape[1] // rows
    e, e_next, slot = be_ref[i], nxt_ref[i], slot_ref[i]
    first = jnp.logical_or(i == 0, be_ref[jnp.maximum(i - 1, 0)] != e)

    def chunk_copy(expert, c, buf):
        return pltpu.make_async_copy(w_hbm.at[expert, pl.ds(pl.multiple_of(c * rows, rows), rows)],
                                     stg.at[buf], sem.at[buf])

    def convert_next_chunk(expert, into):
        c = done[0]
        buf = c % 2
        chunk_copy(expert, c, buf).wait()

        @pl.when(c + 1 < chunks)
        def _():
            chunk_copy(expert, c + 1, 1 - buf).start()

        wbuf[into, pl.ds(pl.multiple_of(c * rows, rows), rows), :] = stg[buf].astype(wbuf.dtype)
        done[0] = c + 1

    @pl.when(i == 0)
    def _():
        done[0] = 0
        chunk_copy(e, 0, 0).start()

    @pl.when(first)
    def _():
        def body(_, carry):
            convert_next_chunk(e, slot)
            return carry

        lax.fori_loop(done[0], chunks, body, 0)

        @pl.when(e_next >= 0)
        def _():
            done[0] = 0
            chunk_copy(e_next, 0, 0).start()

    @pl.when(jnp.logical_and(jnp.logical_not(first), jnp.logical_and(e_next >= 0, done[0] < chunks)))
    def _():
        convert_next_chunk(e_next, 1 - slot)

    return slot


def _up_kernel(be_ref, nxt_ref, slot_ref, nu_ref, x_ref, w_hbm, b_ref, act_ref, wbuf, stg, sem, done):
    i = pl.program_id(0)
    fc = 512

    @pl.when(i < nu_ref[0])
    def _():
        slot = _stage_expert_weights(i, be_ref, nxt_ref, slot_ref, w_hbm, wbuf, stg, sem, done)
        x = _unpack_bf16_pairs(x_ref[...])
        for c in range(D_FF // fc):
            glu = _dot(x, wbuf[slot, :, c * fc:(c + 1) * fc]) + b_ref[:, c * fc:(c + 1) * fc]
            lin = (_dot(x, wbuf[slot, :, D_FF + c * fc:D_FF + (c + 1) * fc])
                   + b_ref[:, D_FF + c * fc:D_FF + (c + 1) * fc])
            glu = jnp.minimum(glu, SWIGLU_LIMIT)
            lin = jnp.clip(lin, -SWIGLU_LIMIT, SWIGLU_LIMIT)
            act_ref[:, c * fc:(c + 1) * fc] = (glu * _sigmoid(SWIGLU_ALPHA * glu) * (lin + 1.0)).astype(act_ref.dtype)

    @pl.when(i >= nu_ref[0])
    def _():
        act_ref[...] = jnp.zeros_like(act_ref)


def _down_kernel(be_ref, nxt_ref, slot_ref, nu_ref, a_ref, w_hbm, b_ref, y_ref, wbuf, stg, sem, done):
    i = pl.program_id(0)

    @pl.when(i < nu_ref[0])
    def _():
        slot = _stage_expert_weights(i, be_ref, nxt_ref, slot_ref, w_hbm, wbuf, stg, sem, done)
        y_ref[...] = _dot(a_ref[...], wbuf[slot]) + b_ref[...]

    @pl.when(i >= nu_ref[0])
    def _():
        y_ref[...] = jnp.zeros_like(y_ref)


def _grouped(kernel, name, x, w, b, plan, out_tail, out_dtype, extra_scratch=()):
    block_expert, next_expert, slot, n_used = plan
    n_slots = x.shape[0]
    _, kdim, ndim = w.shape
    nb = n_slots // MOE_TM
    zeros = lambda t: (0,) * len(t)
    x_tail, o_tail = x.shape[1:], tuple(out_tail)
    return pl.pallas_call(
        kernel,
        grid_spec=pltpu.PrefetchScalarGridSpec(
            num_scalar_prefetch=4,
            grid=(nb,),
            in_specs=[pl.BlockSpec((MOE_TM,) + x_tail, lambda i, be, nx, sl, nu: (jnp.minimum(i, nu[0] - 1),) + zeros(x_tail)),
                      pl.BlockSpec(memory_space=pl.ANY),
                      pl.BlockSpec((None, 1, ndim), lambda i, be, nx, sl, nu: (be[i], 0, 0))],
            out_specs=pl.BlockSpec((MOE_TM,) + o_tail, lambda i, be, nx, sl, nu: (i,) + zeros(o_tail)),
            scratch_shapes=[pltpu.VMEM((2, kdim, ndim), BF16),
                            pltpu.VMEM((2, W_CHUNK_ROWS, ndim), F32),
                            pltpu.SemaphoreType.DMA((2,)),
                            pltpu.SMEM((1,), jnp.int32)] + list(extra_scratch)),
        out_shape=jax.ShapeDtypeStruct((n_slots,) + o_tail, out_dtype),
        compiler_params=_params(("arbitrary",), 60 * 1024 * 1024),
        name=name,
    )(block_expert, next_expert, slot, n_used, x, w, b)


def _combine_kernel(pos_ref, pos_next_ref, y_ref, gates_ref, x1_ref, gate_ref, g_ref, o_ref, rows, sem):
    i = pl.program_id(0)
    tt = x1_ref.shape[0]
    buf = i % 2

    def copies(p_ref, b):
        return [pltpu.make_async_copy(y_ref.at[pl.ds(p_ref[0, 0, t * TOP_K + k], 1)],
                                      rows.at[b, k, pl.ds(t, 1)], sem.at[b])
                for t in range(tt) for k in range(TOP_K)]

    @pl.when(i == 0)
    def _():
        for cp in copies(pos_ref, 0):
            cp.start()

    for b in range(2):
        @pl.when(jnp.logical_and(i + 1 < pl.num_programs(0), buf != b))
        def _():
            for n, cp in enumerate(copies(pos_next_ref, b)):
                cp.start(priority=n % 2)

    for b in range(2):
        @pl.when(buf == b)
        def _():
            for cp in copies(pos_ref, b):
                cp.wait()

    gates = gates_ref[...]
    f = rows[buf, 0] * gates[:, 0:1]
    for k in range(1, TOP_K):
        f = f + rows[buf, k] * gates[:, k:k + 1]
    o_ref[...] = x1_ref[...] + gate_ref[0] * _rms(f, g_ref[...])


def _combine(y, pos3, gates, x1, gate, g_post, tt):
    n, d = x1.shape
    nb = gate.shape[0]
    tiles_per_b = n // tt // nb
    tiles = n // tt
    return pl.pallas_call(
        _combine_kernel,
        grid=(tiles,),
        in_specs=[pl.BlockSpec((1, 1, tt * TOP_K), lambda i: (i, 0, 0), memory_space=pltpu.SMEM),
                  pl.BlockSpec((1, 1, tt * TOP_K), lambda i: (jnp.minimum(i + 1, tiles - 1), 0, 0),
                               memory_space=pltpu.SMEM),
                  pl.BlockSpec(memory_space=pl.ANY),
                  pl.BlockSpec((tt, TOP_K), lambda i: (i, 0)),
                  pl.BlockSpec((tt, d), lambda i: (i, 0)),
                  pl.BlockSpec((1, 1, d), lambda i: (i // tiles_per_b, 0, 0)),
                  pl.BlockSpec((1, d), lambda i: (0, 0))],
        out_specs=pl.BlockSpec((tt, d), lambda i: (i, 0)),
        out_shape=jax.ShapeDtypeStruct((n, d), F32),
        scratch_shapes=[pltpu.VMEM((2, TOP_K, tt) + y.shape[1:], F32), pltpu.SemaphoreType.DMA((2,))],
        compiler_params=_params(("arbitrary",)),
        name="combine",
    )(pos3, pos3, y, gates, x1, gate, g_post)


def _mixer(x, mod, pos, hist8, h0t, ln, tm, tq, wts, cache):
    b, s, d = x.shape
    mods = [m.reshape(b, 1, d) for m in jnp.split(mod, 6, axis=-1)]
    gt_f = mods[5]
    if cache is None:
        xf, fb, fs = x, b, s
    else:
        fb, fs = 1, b * s
        xf = x.reshape(fb, fs, d)
        pos = jnp.tile(pos, b)
        mods = [jnp.broadcast_to(m, (b, s, d)).reshape(fb, fs, d) for m in mods]
    sh_m, sc_m, gt_m, sh_f, sc_f, _ = mods
    proj = _in_projection(xf, 1.0 + sc_m, sh_m, wts["g_mix_pre"], wts["w_main"], wts["w_dt"], _rope_tables(pos), tm)
    q, k, v, z, xbc, dt_raw = [t.reshape(b, s, t.shape[-1]) for t in proj]
    if cache is None:
        attn = _attention_prompt(q, k, v, wts["sinks"], tq)
        k_win, v_win = k[:, s - WIN_CACHE:], v[:, s - WIN_CACHE:]
    else:
        attn, k_win, v_win = _attention_sample(q, k, v, wts["sinks"], *cache)
    ssm, conv_state, hfin = _ssd_mixer(xbc, z, dt_raw, hist8, h0t, wts["conv_w8"], wts["conv_b"], wts["dt_bias"],
                                       wts["a_log"], wts["dskip_x"], wts["g_ssm"], ln, q.dtype)
    x1, h2, logits = _out_projection(attn.reshape(fb, fs, -1), ssm.reshape(fb, fs, -1), xf, wts["w_out"],
                                     wts["g_mix_post"], gt_m, wts["g_ffn_pre"], 1.0 + sc_f, sh_f,
                                     wts["w_router"], wts["b_router"], tm)
    ssm_state = hfin.reshape(b, SSM_GROUPS, SSM_STATE, SSM_HPG, SSM_HEAD_DIM)
    ssm_state = ssm_state.transpose(0, 1, 3, 4, 2).reshape(b, SSM_HEADS, SSM_HEAD_DIM, SSM_STATE)
    states = (k_win.reshape(b, WIN_CACHE, N_KV, HEAD_DIM), v_win.reshape(b, WIN_CACHE, N_KV, HEAD_DIM),
              conv_state, ssm_state)
    return x1.reshape(b * s, d), h2, logits, gt_f, states


def _pad_lanes(a, width=LANES):
    return jnp.pad(a, [(0, 0)] * (a.ndim - 1) + [(0, width - a.shape[-1])])


def _largest_tile(n, cap):
    t = cap
    while n % t:
        t //= 2
    return t


def kernel(x_prompt, x_sample, c_prompt, c_sample, cache_k, cache_v, state_conv, state_ssm, w_mod, b_mod, g_mix_pre, g_mix_post, g_ffn_pre, g_ffn_post, w_in, conv_w, conv_b, dt_bias, a_log, d_skip, g_ssm, sinks, w_out, w_router, b_router, w_up, b_up, w_down, b_down):
    depth = w_mod.shape[0]
    assert depth == 1, "single-layer step"
    l = 0
    bp, sp, d = x_prompt.shape
    bs, ss, _ = x_sample.shape
    n_p, n_s = bp * sp, bs * ss
    n_tok = n_p + n_s

    c_all = jnp.concatenate([c_prompt, c_sample], axis=0)
    c_rows = -(-c_all.shape[0] // SUBLANES) * SUBLANES
    mod_all = _modulation(jnp.pad(c_all, ((0, c_rows - c_all.shape[0]), (0, 0))), w_mod[l], b_mod[l])

    row2 = lambda a: a.reshape(1, -1)
    paired = _paired_head_columns()
    wts = {
        "g_mix_pre": row2(g_mix_pre[l]), "g_mix_post": row2(g_mix_post[l]), "g_ffn_pre": row2(g_ffn_pre[l]),
        "w_main": jnp.concatenate([w_in[l][:, :ATTN_DIM][:, paired], w_in[l][:, ATTN_DIM:MAIN_DIM]],
                                  axis=1).astype(BF16),
        "w_dt": _pad_lanes(w_in[l][:, MAIN_DIM:]).astype(BF16),
        "sinks": sinks[l],
        "conv_w8": jnp.pad(conv_w[l], ((0, SUBLANES - CONV_W), (0, 0))), "conv_b": row2(conv_b[l]),
        "dt_bias": _pad_lanes(row2(dt_bias[l])), "a_log": _pad_lanes(row2(a_log[l])),
        "dskip_x": row2(jnp.repeat(d_skip[l], SSM_HEAD_DIM)), "g_ssm": row2(g_ssm[l]),
        "w_out": jnp.concatenate([w_out[l][:ATTN_DIM][paired], w_out[l][ATTN_DIM:]], axis=0).astype(BF16),
        "w_router": w_router[l].T.astype(BF16), "b_router": b_router[l].reshape(-1, 1),
    }
    gw = SSM_HPG * SSM_HEAD_DIM

    hist_p = jnp.zeros((bp, SUBLANES, CONV_DIM), F32)
    h0_p = jnp.zeros((bp, SSM_GROUPS, SSM_STATE, gw), F32)
    x1_p, h2_p, lg_p, gtf_p, st_p = _mixer(x_prompt, mod_all[:bp], jnp.arange(sp, dtype=jnp.int32), hist_p, h0_p,
                                           SSD_CHUNK, _largest_tile(sp, 512), _largest_tile(sp, 512), wts, None)

    hist_s = jnp.pad(state_conv[l], ((0, 0), (SUBLANES - (CONV_W - 1), 0), (0, 0)))
    h0_s = state_ssm[l].astype(F32).reshape(bs, SSM_GROUPS, SSM_HPG, SSM_HEAD_DIM, SSM_STATE)
    h0_s = h0_s.transpose(0, 1, 4, 2, 3).reshape(bs, SSM_GROUPS, SSM_STATE, gw)
    cache = (cache_k[l].reshape(bs, WIN_CACHE, KV_DIM), cache_v[l].reshape(bs, WIN_CACHE, KV_DIM))
    x1_s, h2_s, lg_s, gtf_s, st_s = _mixer(x_sample, mod_all[bp:bp + bs], PAST_LEN + jnp.arange(ss, dtype=jnp.int32),
                                           hist_s, h0_s, ss, n_s, ss, wts, cache)

    tt = _largest_tile(n_s, 128)
    assert n_p % tt == 0
    route, counts = _router(lg_p, lg_s, tt)
    top_idx = route[ROW_IDX:ROW_IDX + TOP_K].T.astype(jnp.int32)
    rank = route[ROW_RANK:ROW_RANK + TOP_K].T.astype(jnp.int32)
    gates = route[ROW_GATE:ROW_GATE + TOP_K].T
    cnt = counts[:, 0].astype(jnp.int32)
    padded = (cnt + MOE_TM - 1) // MOE_TM * MOE_TM
    pend = jnp.cumsum(padded)
    offs = pend - padded
    pos = offs[top_idx] + rank
    n_blocks = -(-n_tok * TOP_K // MOE_TM) + N_EXPERTS
    n_slots = n_blocks * MOE_TM
    n_used = (pend[-1] // MOE_TM).astype(jnp.int32)
    blk = jnp.arange(n_blocks, dtype=jnp.int32)
    blk_row = jnp.minimum(blk, n_used - 1) * MOE_TM
    block_expert = jnp.minimum(jnp.sum(pend[None, :] <= blk_row[:, None], axis=1), N_EXPERTS - 1).astype(jnp.int32)
    zstart = jnp.where(cnt > 0, pend - MOE_TM, -1).astype(jnp.int32)
    pos3 = pos.reshape(n_tok // tt, 1, tt * TOP_K)
    nu = n_used.reshape(1)
    xs = _dispatch(h2_p, h2_s, pos3, zstart, nu, n_slots, tt)
    e_ids = jnp.arange(N_EXPERTS, dtype=jnp.int32)
    live = jnp.where(cnt > 0, e_ids, N_EXPERTS)
    later = jnp.concatenate([lax.cummin(live, reverse=True)[1:], jnp.full((1,), N_EXPERTS, jnp.int32)])
    next_live = jnp.where(later < N_EXPERTS, later, -1).astype(jnp.int32)
    visit = (jnp.cumsum((cnt > 0).astype(jnp.int32)) - 1) % 2
    plan = (block_expert, next_live[block_expert], visit[block_expert].astype(jnp.int32), nu)
    act = _grouped(_up_kernel, "expert_up", xs, w_up[l], b_up[l].reshape(N_EXPERTS, 1, -1), plan, (D_FF,), BF16)
    y = _grouped(_down_kernel, "expert_down", act, w_down[l], b_down[l].reshape(N_EXPERTS, 1, -1), plan,
                 (d,), F32)

    g_post = row2(g_ffn_post[l])
    tt_p = _largest_tile(sp, 128)
    y_p = _combine(y, pos[:n_p].reshape(n_p // tt_p, 1, tt_p * TOP_K), gates[:n_p], x1_p,
                   gtf_p, g_post, tt_p).reshape(bp, sp, d)
    y_s = _combine(y, pos[n_p:].reshape(bs, 1, ss * TOP_K), gates[n_p:], x1_s,
                   gtf_s, g_post, ss).reshape(bs, ss, d)

    stack = lambda a: a[None]
    return (y_p, y_s, stack(st_p[0]), stack(st_p[1]), stack(st_p[2]), stack(st_p[3]),
            stack(st_s[0]), stack(st_s[1]), stack(st_s[2]), stack(st_s[3]))
```
